```python
import jax, jax.numpy as jnp
from jax import lax
import numpy as np

D_MODEL = 1024
BATCH = 32
SEQ = 2048
DEPTH = 1
DEC_BATCH = 32
DEC_SEQ = 64
PAST_LEN = 4096

CHUNK = 64
RET_HEADS = 4
RET_DK = 128
RET_DV = 128
RET_QK = RET_HEADS * RET_DK
RET_VW = RET_HEADS * RET_DV
ROPE_BASE = 10000.0
POOL_WINDOWS = (2, 4, 8, 16)
POOL_GROUPS = 4
POOL_WIDTH = D_MODEL // 2
POOL_C = POOL_WIDTH // POOL_GROUPS
POOL_HIST = max(POOL_WINDOWS) - 1
MIX_WIDTH = RET_VW + POOL_WIDTH
IN_WIDTH = 2 * RET_QK + 2 * RET_VW + POOL_WIDTH
N_MEM = 256
MEM_HEADS = 4
MEM_HD = D_MODEL // MEM_HEADS
N_EXPERTS = 32
TOP_K = 4
D_FF = D_MODEL
SWIGLU_LIMIT = 7.0
SWIGLU_ALPHA = 1.702
EXPERT_BLOCK = 256
EPS = 1e-5

kernel_name = 'hybrid_retention_pool_moe_stream_step'


def _rmsnorm(x, w):
    xf = x.astype(jnp.float32)
    y = xf * lax.rsqrt(jnp.mean(xf * xf, axis=-1, keepdims=True) + EPS) * w.astype(jnp.float32)
    return y.astype(x.dtype)


def _rotary(t, pos):
    half = t.shape[-1] // 2
    inv_freq = jnp.power(ROPE_BASE, -jnp.arange(half, dtype=jnp.float32) / half)
    ang = pos.astype(jnp.float32)[:, None] * inv_freq[None, :]
    cos = jnp.cos(ang)[None, :, None, :]
    sin = jnp.sin(ang)[None, :, None, :]
    tf = t.astype(jnp.float32)
    t1, t2 = tf[..., :half], tf[..., half:]
    return jnp.concatenate([t1 * cos - t2 * sin, t1 * sin + t2 * cos], axis=-1).astype(t.dtype)


def _retention(q, k, v, state):
    B, L, H, DK = q.shape
    DV = v.shape[-1]
    lc = min(L, CHUNK)
    nc = L // lc
    log_g = jnp.log1p(-jnp.exp2(-5.0 - jnp.arange(H, dtype=jnp.float32)))
    idx = jnp.arange(lc, dtype=jnp.float32)
    intra = jnp.exp(log_g[:, None, None] * jnp.abs(idx[:, None] - idx[None, :]))
    q_dec = jnp.exp((idx[:, None] + 1.0) * log_g[None, :])
    k_dec = jnp.exp((lc - 1.0 - idx[:, None]) * log_g[None, :])
    c_dec = jnp.exp(lc * log_g)[:, None, None]
    qc = q.astype(jnp.float32).reshape(B, nc, lc, H, DK)
    kc = k.astype(jnp.float32).reshape(B, nc, lc, H, DK)
    vc = v.astype(jnp.float32).reshape(B, nc, lc, H, DV)
    s = jnp.einsum('bnihd,bnjhd->bnhij', qc, kc) * intra
    o_intra = jnp.einsum('bnhij,bnjhe->bnihe', s, vc)
    contrib = jnp.einsum('bnjhd,bnjhe->nbhde', kc * k_dec[:, :, None], vc)

    def step(S, c):
        return c_dec * S + c, S

    s_fin, s_before = lax.scan(step, state.astype(jnp.float32), contrib)
    o_inter = jnp.einsum('bnihd,nbhde->bnihe', qc * q_dec[:, :, None], s_before)
    return (o_intra + o_inter).reshape(B, L, H, DV), s_fin


def _pool_mixer(p_ext, pos, w_pool, pool_scale):
    B, T, W = p_ext.shape
    L = pos.shape[0]
    P = T - L
    cs = jnp.concatenate([jnp.zeros((B, 1, W), jnp.float32), jnp.cumsum(p_ext.astype(jnp.float32), axis=1)], axis=1)
    hi = P + 1 + jnp.arange(L, dtype=jnp.int32)
    means = []
    for gi, w in enumerate(POOL_WINDOWS):
        sl = slice(gi * POOL_C, (gi + 1) * POOL_C)
        lo = jnp.maximum(hi - w, 0)
        cnt = jnp.minimum(w, pos + 1).astype(jnp.float32)
        means.append((cs[:, hi, sl] - cs[:, lo, sl]) / cnt[None, :, None])
    d = (jnp.concatenate(means, axis=-1) - p_ext[:, P:].astype(jnp.float32)).reshape(B, L, POOL_GROUPS, POOL_C)
    y = jnp.einsum('blgc,gcd->blgd', d, w_pool.astype(jnp.float32)).reshape(B, L, W)
    return y * pool_scale.astype(jnp.float32)


def _mem_kv(mem, mem_norm_w, w_kv_mem):
    B, M, _ = mem.shape
    kv = _rmsnorm(mem, mem_norm_w) @ w_kv_mem
    k, v = jnp.split(kv, 2, axis=-1)
    return k.reshape(B, M, MEM_HEADS, MEM_HD), v.reshape(B, M, MEM_HEADS, MEM_HD)


def _mem_attn(q, mk, mv):
    B, L = q.shape[:2]
    s = jnp.einsum('blhd,bmhd->bhlm', q, mk).astype(jnp.float32) * (MEM_HD ** -0.5)
    p = jax.nn.softmax(s, axis=-1)
    o = jnp.einsum('bhlm,bmhd->blhd', p.astype(mv.dtype), mv)
    return o.reshape(B, L, MEM_HEADS * MEM_HD)


def _expert_ffn(xe, wgu, bgu, wd, bd):
    gu = (xe @ wgu + bgu).astype(jnp.float32)
    gate = jnp.minimum(gu[..., 0::2], SWIGLU_LIMIT)
    up = jnp.clip(gu[..., 1::2], -SWIGLU_LIMIT, SWIGLU_LIMIT)
    h = (up + 1.0) * gate * jax.nn.sigmoid(SWIGLU_ALPHA * gate)
    return h.astype(xe.dtype) @ wd + bd


def _moe(x2d, router_w, router_b, w_gate_up, b_gate_up, w_down, b_down):
    T, D = x2d.shape
    logits = (x2d @ router_w).astype(jnp.float32) + router_b.astype(jnp.float32)
    top_v, top_e = lax.top_k(logits, TOP_K)
    gates = jax.nn.softmax(top_v, axis=-1)
    n_slots = T * TOP_K
    flat_e = top_e.reshape(-1)
    flat_tok = jnp.arange(n_slots, dtype=jnp.int32) // TOP_K
    flat_g = gates.reshape(-1)
    order = jnp.argsort(flat_e)
    se = flat_e[order]
    counts = jnp.zeros((N_EXPERTS,), jnp.int32).at[flat_e].add(1)
    start = jnp.cumsum(counts) - counts
    pcounts = (counts + EXPERT_BLOCK - 1) // EXPERT_BLOCK * EXPERT_BLOCK
    pend = jnp.cumsum(pcounts)
    pstart = pend - pcounts
    dest = pstart[se] + jnp.arange(n_slots, dtype=jnp.int32) - start[se]
    n_blocks = -(-(n_slots + N_EXPERTS * (EXPERT_BLOCK - 1)) // EXPERT_BLOCK)
    m_pad = n_blocks * EXPERT_BLOCK
    row_tok = jnp.full((m_pad,), T, jnp.int32).at[dest].set(flat_tok[order])
    row_gate = jnp.zeros((m_pad,), jnp.float32).at[dest].set(flat_g[order])
    block_e = jnp.minimum(jnp.searchsorted(pend, jnp.arange(n_blocks, dtype=jnp.int32) * EXPERT_BLOCK, side='right'), N_EXPERTS - 1)
    x_pad = jnp.concatenate([x2d, jnp.zeros((1, D), x2d.dtype)], axis=0)
    xb = x_pad[row_tok].reshape(n_blocks, EXPERT_BLOCK, D)

    def run(args):
        xe, e = args
        return _expert_ffn(xe, w_gate_up[e], b_gate_up[e], w_down[e], b_down[e])

    yb = lax.map(run, (xb, block_e)).reshape(m_pad, D)
    out = jnp.zeros((T + 1, D), jnp.float32).at[row_tok].add(yb.astype(jnp.float32) * row_gate[:, None])
    return out[:T].astype(x2d.dtype)


def _layer(x, pos0, ret_state, pool_hist, mem_k, mem_v, norm_mix_w, w_in, ret_gn_w, w_pool, pool_scale, w_out,
           norm_mem_w, w_q_mem, w_o_mem, norm_ffn_w, router_w, router_b, w_gate_up, b_gate_up, w_down, b_down):
    B, L, D = x.shape
    pos = pos0 + jnp.arange(L, dtype=jnp.int32)
    proj = _rmsnorm(x, norm_mix_w) @ w_in
    q, k, v, g, pin = jnp.split(proj, [RET_QK, 2 * RET_QK, 2 * RET_QK + RET_VW, 2 * RET_QK + 2 * RET_VW], axis=-1)
    q = _rotary(q.reshape(B, L, RET_HEADS, RET_DK), pos) * (RET_DK ** -0.5)
    k = _rotary(k.reshape(B, L, RET_HEADS, RET_DK), pos)
    o, new_ret = _retention(q, k, v.reshape(B, L, RET_HEADS, RET_DV), ret_state)
    mu = jnp.mean(o, axis=-1, keepdims=True)
    var = jnp.mean(jnp.square(o - mu), axis=-1, keepdims=True)
    o = ((o - mu) * lax.rsqrt(var + EPS)).reshape(B, L, RET_VW)
    o = o * ret_gn_w.astype(jnp.float32) * jax.nn.silu(g.astype(jnp.float32))
    p_ext = jnp.concatenate([pool_hist.astype(pin.dtype), pin], axis=1)
    o_pool = _pool_mixer(p_ext, pos, w_pool, pool_scale)
    x = x + jnp.concatenate([o, o_pool], axis=-1).astype(x.dtype) @ w_out
    qm = (_rmsnorm(x, norm_mem_w) @ w_q_mem).reshape(B, L, MEM_HEADS, MEM_HD)
    x = x + _mem_attn(qm, mem_k, mem_v).astype(x.dtype) @ w_o_mem
    h = _rmsnorm(x, norm_ffn_w).reshape(B * L, D)
    x = x + _moe(h, router_w, router_b, w_gate_up, b_gate_up, w_down, b_down).reshape(B, L, D)
    return x, new_ret.astype(ret_state.dtype), p_ext[:, -POOL_HIST:]


def setup_inputs(seed: int = 0) -> dict:
    key = jax.random.key(seed)
    ks = iter(jax.random.split(key, 32))

    def nrm(shape, scale):
        return jax.random.normal(next(ks), shape, jnp.float32) * scale

    def gain(shape):
        return 1.0 + nrm(shape, 0.02)

    return {
        'x_prompt': nrm((BATCH, SEQ, D_MODEL), 1.0),
        'x_sample': nrm((DEC_BATCH, DEC_SEQ, D_MODEL), 1.0),
        'cache_mem_k': nrm((DEPTH, DEC_BATCH, N_MEM, MEM_HEADS, MEM_HD), 1.0),
        'cache_mem_v': nrm((DEPTH, DEC_BATCH, N_MEM, MEM_HEADS, MEM_HD), 1.0),
        'state_ret': nrm((DEPTH, DEC_BATCH, RET_HEADS, RET_DK, RET_DV), 0.5),
        'state_pool': nrm((DEPTH, DEC_BATCH, POOL_HIST, POOL_WIDTH), 1.0),
        'mem_prompt': nrm((BATCH, N_MEM, D_MODEL), 1.0),
        'norm_mix_w': gain((DEPTH, D_MODEL)),
        'w_in': nrm((DEPTH, D_MODEL, IN_WIDTH), D_MODEL ** -0.5),
        'ret_gn_w': gain((DEPTH, RET_VW)),
        'w_pool': nrm((DEPTH, POOL_GROUPS, POOL_C, POOL_C), POOL_C ** -0.5),
        'pool_scale': gain((DEPTH, POOL_WIDTH)),
        'w_out': nrm((DEPTH, MIX_WIDTH, D_MODEL), MIX_WIDTH ** -0.5),
        'norm_mem_w': gain((DEPTH, D_MODEL)),
        'mem_norm_w': gain((DEPTH, D_MODEL)),
        'w_q_mem': nrm((DEPTH, D_MODEL, MEM_HEADS * MEM_HD), D_MODEL ** -0.5),
        'w_kv_mem': nrm((DEPTH, D_MODEL, 2 * MEM_HEADS * MEM_HD), D_MODEL ** -0.5),
        'w_o_mem': nrm((DEPTH, MEM_HEADS * MEM_HD, D_MODEL), (MEM_HEADS * MEM_HD) ** -0.5),
        'norm_ffn_w': gain((DEPTH, D_MODEL)),
        'router_w': nrm((DEPTH, D_MODEL, N_EXPERTS), D_MODEL ** -0.5),
        'router_b': nrm((DEPTH, N_EXPERTS), 0.01),
        'w_gate_up': nrm((DEPTH, N_EXPERTS, D_MODEL, 2 * D_FF), D_MODEL ** -0.5),
        'b_gate_up': nrm((DEPTH, N_EXPERTS, 2 * D_FF), 0.01),
        'w_down': nrm((DEPTH, N_EXPERTS, D_FF, D_MODEL), D_FF ** -0.5),
        'b_down': nrm((DEPTH, N_EXPERTS, D_MODEL), 0.01),
        'final_norm_w': gain((D_MODEL,)),
    }


def reference(x_prompt, x_sample, cache_mem_k, cache_mem_v, state_ret, state_pool, mem_prompt,
              norm_mix_w, w_in, ret_gn_w, w_pool, pool_scale, w_out, norm_mem_w, mem_norm_w,
              w_q_mem, w_kv_mem, w_o_mem, norm_ffn_w, router_w, router_b, w_gate_up, b_gate_up,
              w_down, b_down, final_norm_w):
    B = x_prompt.shape[0]
    hp, hs = x_prompt, x_sample
    mk_p_l, mv_p_l, ret_p_l, pool_p_l, ret_s_l, pool_s_l = [], [], [], [], [], []
    for l in range(DEPTH):
        lw = (norm_mix_w[l], w_in[l], ret_gn_w[l], w_pool[l], pool_scale[l], w_out[l], norm_mem_w[l],
              w_q_mem[l], w_o_mem[l], norm_ffn_w[l], router_w[l], router_b[l], w_gate_up[l],
              b_gate_up[l], w_down[l], b_down[l])
        mk_p, mv_p = _mem_kv(mem_prompt, mem_norm_w[l], w_kv_mem[l])
        ret0 = jnp.zeros((B, RET_HEADS, RET_DK, RET_DV), x_prompt.dtype)
        pool0 = jnp.zeros((B, POOL_HIST, POOL_WIDTH), x_prompt.dtype)
        hp, ret_p, pool_p = _layer(hp, 0, ret0, pool0, mk_p, mv_p, *lw)
        hs, ret_s, pool_s = _layer(hs, PAST_LEN, state_ret[l], state_pool[l], cache_mem_k[l], cache_mem_v[l], *lw)
        mk_p_l.append(mk_p)
        mv_p_l.append(mv_p)
        ret_p_l.append(ret_p)
        pool_p_l.append(pool_p)
        ret_s_l.append(ret_s)
        pool_s_l.append(pool_s)
    y_prompt = _rmsnorm(hp, final_norm_w)
    y_sample = _rmsnorm(hs, final_norm_w)
    return (y_prompt, y_sample, jnp.stack(mk_p_l), jnp.stack(mv_p_l), jnp.stack(ret_p_l), jnp.stack(pool_p_l), jnp.stack(ret_s_l), jnp.stack(pool_s_l))
```

```python
import functools

import numpy as np
import jax
import jax.numpy as jnp
from jax import lax
from jax.experimental import pallas as pl
from jax.experimental.pallas import tpu as pltpu

D_MODEL = 1024
CHUNK = 64
PAST_LEN = 4096
RET_HEADS = 4
RET_DK = 128
RET_DV = 128
RET_QK = RET_HEADS * RET_DK
RET_VW = RET_HEADS * RET_DV
ROPE_BASE = 10000.0
POOL_WINDOWS = (2, 4, 8, 16)
POOL_GROUPS = 4
POOL_WIDTH = D_MODEL // 2
POOL_C = POOL_WIDTH // POOL_GROUPS
POOL_HIST = max(POOL_WINDOWS) - 1
HIST_ROWS = POOL_HIST + 1
IN_WIDTH = 2 * RET_QK + 2 * RET_VW + POOL_WIDTH
N_MEM = 256
MEM_HEADS = 4
MEM_HD = D_MODEL // MEM_HEADS
N_EXPERTS = 32
TOP_K = 4
D_FF = D_MODEL
SWIGLU_LIMIT = 7.0
SWIGLU_ALPHA = 1.702
EPS = 1e-5

LANES = 128
PROMPT_TILE = 256
EXPERT_ROWS = 512
KV_ROWS = 512
COMBINE_ROWS = 512
VMEM_LIMIT = 56 * 1024 * 1024

BF16 = jnp.bfloat16
F32 = jnp.float32


def _rms(x, w):
    return x * lax.rsqrt(jnp.mean(x * x, axis=-1, keepdims=True) + EPS) * w


def _dot(a, b):
    return jnp.dot(a, b, preferred_element_type=F32)


def _dot_nt(a, b):
    return lax.dot_general(a, b, (((1,), (1,)), ((), ())), preferred_element_type=F32)


def _const_spec(shape):
    nd = len(shape)
    return pl.BlockSpec(shape, lambda *_: (0,) * nd, pipeline_mode=pl.Buffered(1))


def _mem_kv_kernel(mem_ref, nw_ref, w_ref, k_ref, v_ref):
    xn = _rms(mem_ref[...], nw_ref[...]).astype(BF16)
    kv = _dot(xn, w_ref[...])
    k_ref[...] = kv[:, :D_MODEL]
    v_ref[...] = kv[:, D_MODEL:]


def _mem_kv(mem2d, mem_norm_w, w_kv_bf):
    rows = mem2d.shape[0]
    return pl.pallas_call(
        _mem_kv_kernel,
        grid=(rows // KV_ROWS,),
        in_specs=[
            pl.BlockSpec((KV_ROWS, D_MODEL), lambda i: (i, 0)),
            _const_spec((1, D_MODEL)),
            _const_spec((D_MODEL, 2 * D_MODEL)),
        ],
        out_specs=[
            pl.BlockSpec((KV_ROWS, D_MODEL), lambda i: (i, 0)),
            pl.BlockSpec((KV_ROWS, D_MODEL), lambda i: (i, 0)),
        ],
        out_shape=[jax.ShapeDtypeStruct((rows, D_MODEL), F32)] * 2,
        compiler_params=pltpu.CompilerParams(
            dimension_semantics=("arbitrary",), vmem_limit_bytes=VMEM_LIMIT),
        name="mem_kv",
    )(mem2d, mem_norm_w, w_kv_bf)


def _layer_kernel(x_ref, cos_ref, sin_ref, dmat_ref, qdec_ref, kdec_ref, state0_ref, hist0_ref,
                  mk_ref, mv_ref, nmix_ref, win_ref, gnw_ref, wpool_ref, pscale_ref, wout_ref,
                  nmem_ref, wq_ref, wo_ref, nffn_ref, rw_ref, rb_ref,
                  x2_ref, h_ref, route_ref, rstate_ref, pstate_ref,
                  s_scr, ext_scr, mk_scr, mv_scr, *, tl, pos0, cdec):
    t = pl.program_id(1)

    @pl.when(t == 0)
    def _():
        s_scr[...] = state0_ref[0]
        ext_scr[0:HIST_ROWS, :] = hist0_ref[0]
        mk_scr[...] = mk_ref[0].astype(BF16)
        mv_scr[...] = mv_ref[0].astype(BF16)

    x = x_ref[0]
    proj = _dot(_rms(x, nmix_ref[...]).astype(BF16), win_ref[...])
    cos = cos_ref[...]
    sin = sin_ref[...]

    outs = []
    for hd in range(RET_HEADS):
        lo = hd * RET_DK
        q = proj[:, lo:lo + RET_DK]
        k = proj[:, RET_QK + lo:RET_QK + lo + RET_DK]
        v = proj[:, 2 * RET_QK + lo:2 * RET_QK + lo + RET_DV]
        g = proj[:, 2 * RET_QK + RET_VW + lo:2 * RET_QK + RET_VW + lo + RET_DV]
        qr = (q * cos + pltpu.roll(q, RET_DK // 2, 1) * sin) * (RET_DK ** -0.5)
        kr = k * cos + pltpu.roll(k, RET_DK // 2, 1) * sin
        vb = v.astype(BF16)
        s = _dot_nt(qr.astype(BF16), kr.astype(BF16)) * dmat_ref[hd]
        o = _dot(s.astype(BF16), vb)
        state = s_scr[hd]
        o = o + _dot((qr * qdec_ref[hd]).astype(BF16), state.astype(BF16))
        kd_t = jnp.transpose(kr * kdec_ref[hd]).astype(BF16)
        s_scr[hd] = cdec[hd] * state + _dot(kd_t, vb)
        mu = jnp.mean(o, axis=-1, keepdims=True)
        oc = o - mu
        var = jnp.mean(oc * oc, axis=-1, keepdims=True)
        on = oc * lax.rsqrt(var + EPS)
        outs.append(on * gnw_ref[:, lo:lo + RET_DV] * (g * jax.nn.sigmoid(g)))
    rstate_ref[0] = s_scr[...]

    pin = proj[:, 2 * RET_QK + 2 * RET_VW:]
    ext_scr[HIST_ROWS:HIST_ROWS + tl, :] = pin
    pstate_ref[0] = pin[tl - POOL_HIST:, :]
    pos = (pos0 + t * tl + lax.broadcasted_iota(jnp.int32, (tl, POOL_C), 0)).astype(F32)
    for gi, w in enumerate(POOL_WINDOWS):
        lo = gi * POOL_C
        wsum = ext_scr[HIST_ROWS:HIST_ROWS + tl, lo:lo + POOL_C]
        for j in range(1, w):
            wsum = wsum + ext_scr[HIST_ROWS - j:HIST_ROWS - j + tl, lo:lo + POOL_C]
        cnt = jnp.minimum(float(w), pos + 1.0)
        d = wsum / cnt - pin[:, lo:lo + POOL_C]
        y = _dot(d.astype(BF16), wpool_ref[gi])
        outs.append(y * pscale_ref[:, lo:lo + POOL_C])
    ext_scr[0:HIST_ROWS, :] = ext_scr[tl:tl + HIST_ROWS, :]

    mix = jnp.concatenate(outs, axis=-1).astype(BF16)
    x1 = x + _dot(mix, wout_ref[...])

    qm = _dot(_rms(x1, nmem_ref[...]).astype(BF16), wq_ref[...])
    aouts = []
    for hd in range(MEM_HEADS):
        lo = hd * MEM_HD
        s = _dot_nt(qm[:, lo:lo + MEM_HD].astype(BF16), mk_scr[:, lo:lo + MEM_HD]) * (MEM_HD ** -0.5)
        e = jnp.exp(s - jnp.max(s, axis=-1, keepdims=True))
        p = e / jnp.sum(e, axis=-1, keepdims=True)
        aouts.append(_dot(p.astype(BF16), mv_scr[:, lo:lo + MEM_HD]))
    att = jnp.concatenate(aouts, axis=-1).astype(BF16)
    x2 = x1 + _dot(att, wo_ref[...])
    x2_ref[0] = x2

    hn = _rms(x2, nffn_ref[...])
    hb = hn.astype(BF16)
    h_ref[0] = hb
    logits = _dot(hb, rw_ref[...]) + rb_ref[...]
    eiota = lax.broadcasted_iota(jnp.int32, (tl, N_EXPERTS), 1)
    neg = jnp.finfo(F32).min
    vals, idxs = [], []
    for _k in range(TOP_K):
        m = jnp.max(logits, axis=-1, keepdims=True)
        idx = jnp.min(jnp.where(logits == m, eiota, N_EXPERTS), axis=-1, keepdims=True)
        vals.append(m)
        idxs.append(idx)
        logits = jnp.where(eiota == idx, neg, logits)
    exps = [jnp.exp(vk - vals[0]) for vk in vals]
    den = exps[0] + exps[1] + exps[2] + exps[3]
    lane = lax.broadcasted_iota(jnp.int32, (tl, LANES), 1)
    route = jnp.zeros((tl, LANES), F32)
    for kk in range(TOP_K):
        route = jnp.where(lane == kk, exps[kk] / den, route)
        route = jnp.where(lane == TOP_K + kk, idxs[kk].astype(F32), route)
    route_ref[0] = route


def _decay_tables(tl):
    hh = np.arange(RET_HEADS, dtype=np.float64)
    log_g = np.log1p(-np.exp2(-5.0 - hh))
    idx = np.arange(tl, dtype=np.float64)
    dist = np.abs(idx[:, None] - idx[None, :])
    visible = (idx[None, :] // CHUNK) <= (idx[:, None] // CHUNK)
    dmat = np.where(visible[None], np.exp(log_g[:, None, None] * dist[None]), 0.0)
    qdec = np.exp(log_g[:, None] * (idx[None, :] + 1.0))
    kdec = np.exp(log_g[:, None] * (tl - 1.0 - idx[None, :]))
    cdec = tuple(float(c) for c in np.exp(log_g * tl).astype(np.float32))
    bcast = lambda a: np.ascontiguousarray(np.broadcast_to(a[:, :, None], (RET_HEADS, tl, RET_DK)))
    return (jnp.asarray(dmat, F32), jnp.asarray(bcast(qdec), F32), jnp.asarray(bcast(kdec), F32), cdec)


def _rotary_tables(pos0, length):
    half = RET_DK // 2
    inv_freq = jnp.power(ROPE_BASE, -jnp.arange(half, dtype=F32) / half)
    ang = (pos0 + jnp.arange(length, dtype=jnp.int32)).astype(F32)[:, None] * inv_freq[None, :]
    cos, sin = jnp.cos(ang), jnp.sin(ang)
    return jnp.concatenate([cos, cos], axis=-1), jnp.concatenate([-sin, sin], axis=-1)


def _layer(x, pos0, tl, state0, hist0, mk, mv, wts):
    b, length, _ = x.shape
    nt = length // tl
    cos, sin = _rotary_tables(pos0, length)
    dmat, qdec, kdec, cdec = _decay_tables(tl)
    kern = functools.partial(_layer_kernel, tl=tl, pos0=pos0, cdec=cdec)
    tok = lambda width: pl.BlockSpec((1, tl, width), lambda i, j: (i, j, 0))
    per_stream = lambda *shape: pl.BlockSpec((1,) + shape, lambda i, j: (i,) + (0,) * len(shape))
    in_specs = [
        tok(D_MODEL),
        pl.BlockSpec((tl, RET_DK), lambda i, j: (j, 0)),
        pl.BlockSpec((tl, RET_DK), lambda i, j: (j, 0)),
        _const_spec((RET_HEADS, tl, tl)),
        _const_spec((RET_HEADS, tl, RET_DK)),
        _const_spec((RET_HEADS, tl, RET_DK)),
        per_stream(RET_HEADS, RET_DK, RET_DV),
        per_stream(HIST_ROWS, POOL_WIDTH),
        per_stream(N_MEM, D_MODEL),
        per_stream(N_MEM, D_MODEL),
    ] + [_const_spec(w.shape) for w in wts]
    out_specs = [
        tok(D_MODEL), tok(D_MODEL), tok(LANES),
        per_stream(RET_HEADS, RET_DK, RET_DV),
        per_stream(POOL_HIST, POOL_WIDTH),
    ]
    out_shape = [
        jax.ShapeDtypeStruct((b, length, D_MODEL), F32),
        jax.ShapeDtypeStruct((b, length, D_MODEL), BF16),
        jax.ShapeDtypeStruct((b, length, LANES), F32),
        jax.ShapeDtypeStruct((b, RET_HEADS, RET_DK, RET_DV), F32),
        jax.ShapeDtypeStruct((b, POOL_HIST, POOL_WIDTH), F32),
    ]
    scratch = [
        pltpu.VMEM((RET_HEADS, RET_DK, RET_DV), F32),
        pltpu.VMEM((HIST_ROWS + tl, POOL_WIDTH), F32),
        pltpu.VMEM((N_MEM, D_MODEL), BF16),
        pltpu.VMEM((N_MEM, D_MODEL), BF16),
    ]
    return pl.pallas_call(
        kern,
        grid=(b, nt),
        in_specs=in_specs,
        out_specs=out_specs,
        out_shape=out_shape,
        scratch_shapes=scratch,
        compiler_params=pltpu.CompilerParams(
            dimension_semantics=("arbitrary", "arbitrary"), vmem_limit_bytes=VMEM_LIMIT),
        name="layer_tl%d" % tl,
    )(x, cos, sin, dmat, qdec, kdec, state0, hist0, mk, mv, *wts)


def _expert_kernel(be_ref, x_ref, wgu_ref, bgu_ref, wd_ref, bd_ref, y_ref):
    del be_ref
    gu = _dot(x_ref[...], wgu_ref[0]) + bgu_ref[0]
    gate = jnp.minimum(gu[:, :D_FF], SWIGLU_LIMIT)
    up = jnp.clip(gu[:, D_FF:], -SWIGLU_LIMIT, SWIGLU_LIMIT)
    act = (up + 1.0) * gate * jax.nn.sigmoid(SWIGLU_ALPHA * gate)
    y_ref[...] = _dot(act.astype(BF16), wd_ref[0]) + bd_ref[0]


def _expert_ffn(block_e, xs, wgu, bgu, wd, bd):
    m_pad = xs.shape[0]
    n_blocks = m_pad // EXPERT_ROWS
    grid_spec = pltpu.PrefetchScalarGridSpec(
        num_scalar_prefetch=1,
        grid=(n_blocks,),
        in_specs=[
            pl.BlockSpec((EXPERT_ROWS, D_MODEL), lambda i, be: (i, 0)),
            pl.BlockSpec((1, D_MODEL, 2 * D_FF), lambda i, be: (be[i], 0, 0)),
            pl.BlockSpec((1, 1, 2 * D_FF), lambda i, be: (be[i], 0, 0)),
            pl.BlockSpec((1, D_FF, D_MODEL), lambda i, be: (be[i], 0, 0)),
            pl.BlockSpec((1, 1, D_MODEL), lambda i, be: (be[i], 0, 0)),
        ],
        out_specs=pl.BlockSpec((EXPERT_ROWS, D_MODEL), lambda i, be: (i, 0)),
    )
    return pl.pallas_call(
        _expert_kernel,
        grid_spec=grid_spec,
        out_shape=jax.ShapeDtypeStruct((m_pad, D_MODEL), F32),
        compiler_params=pltpu.CompilerParams(
            dimension_semantics=("arbitrary",), vmem_limit_bytes=VMEM_LIMIT),
        name="expert_ffn",
    )(block_e, xs, wgu, bgu, wd, bd)


def _routing(top_e, n_tok):
    n_slots = n_tok * TOP_K
    flat_e = top_e.reshape(-1)
    onehot = (flat_e[:, None] == jnp.arange(N_EXPERTS, dtype=jnp.int32)[None, :]).astype(jnp.int32)
    csum = jnp.cumsum(onehot, axis=0)
    rank = jnp.sum(csum * onehot, axis=1) - 1
    counts = csum[-1]
    pcounts = (counts + EXPERT_ROWS - 1) // EXPERT_ROWS * EXPERT_ROWS
    pend = jnp.cumsum(pcounts)
    pstart = pend - pcounts
    pos = pstart[flat_e] + rank
    n_blocks = -(-(n_slots + N_EXPERTS * (EXPERT_ROWS - 1)) // EXPERT_ROWS)
    m_pad = n_blocks * EXPERT_ROWS
    row_tok = jnp.zeros((m_pad,), jnp.int32).at[pos].set(jnp.arange(n_slots, dtype=jnp.int32) // TOP_K)
    block_e = jnp.minimum(
        jnp.searchsorted(pend, jnp.arange(n_blocks, dtype=jnp.int32) * EXPERT_ROWS, side="right"),
        N_EXPERTS - 1).astype(jnp.int32)
    return row_tok, block_e, pos.reshape(n_tok, TOP_K)


def _combine_kernel(x_ref, yg_ref, route_ref, fw_ref, y_ref):
    acc = x_ref[...]
    route = route_ref[...]
    for kk in range(TOP_K):
        acc = acc + yg_ref[:, kk * D_MODEL:(kk + 1) * D_MODEL] * route[:, kk:kk + 1]
    y_ref[...] = _rms(acc, fw_ref[...])


def _combine(x2, yg, route, final_w):
    n_tok = x2.shape[0]
    return pl.pallas_call(
        _combine_kernel,
        grid=(n_tok // COMBINE_ROWS,),
        in_specs=[
            pl.BlockSpec((COMBINE_ROWS, D_MODEL), lambda i: (i, 0)),
            pl.BlockSpec((COMBINE_ROWS, TOP_K * D_MODEL), lambda i: (i, 0)),
            pl.BlockSpec((COMBINE_ROWS, LANES), lambda i: (i, 0)),
            _const_spec((1, D_MODEL)),
        ],
        out_specs=pl.BlockSpec((COMBINE_ROWS, D_MODEL), lambda i: (i, 0)),
        out_shape=jax.ShapeDtypeStruct((n_tok, D_MODEL), F32),
        compiler_params=pltpu.CompilerParams(
            dimension_semantics=("arbitrary",), vmem_limit_bytes=VMEM_LIMIT),
        name="combine",
    )(x2, yg, route, final_w)


def kernel(x_prompt, x_sample, cache_mem_k, cache_mem_v, state_ret, state_pool, mem_prompt,
           norm_mix_w, w_in, ret_gn_w, w_pool, pool_scale, w_out, norm_mem_w, mem_norm_w,
           w_q_mem, w_kv_mem, w_o_mem, norm_ffn_w, router_w, router_b, w_gate_up, b_gate_up,
           w_down, b_down, final_norm_w):
    assert norm_mix_w.shape[0] == 1, "one layer"
    b, seq, _ = x_prompt.shape
    db, dseq, _ = x_sample.shape
    row = lambda a: a.reshape(1, -1)

    mk_p, mv_p = _mem_kv(mem_prompt.reshape(b * N_MEM, D_MODEL), row(mem_norm_w[0]), w_kv_mem[0].astype(BF16))
    mk_p = mk_p.reshape(b, N_MEM, D_MODEL)
    mv_p = mv_p.reshape(b, N_MEM, D_MODEL)

    wts = (row(norm_mix_w[0]), w_in[0].astype(BF16), row(ret_gn_w[0]), w_pool[0].astype(BF16),
           row(pool_scale[0]), w_out[0].astype(BF16), row(norm_mem_w[0]), w_q_mem[0].astype(BF16),
           w_o_mem[0].astype(BF16), row(norm_ffn_w[0]), router_w[0].astype(BF16), row(router_b[0]))

    zero_state = jnp.zeros((b, RET_HEADS, RET_DK, RET_DV), F32)
    zero_hist = jnp.zeros((b, HIST_ROWS, POOL_WIDTH), F32)
    x2_p, h_p, route_p, ret_p, pool_p = _layer(x_prompt, 0, PROMPT_TILE, zero_state, zero_hist, mk_p, mv_p, wts)

    hist_s = jnp.concatenate([jnp.zeros((db, 1, POOL_WIDTH), F32), state_pool[0]], axis=1)
    x2_s, h_s, route_s, ret_s, pool_s = _layer(
        x_sample, PAST_LEN, dseq, state_ret[0], hist_s,
        cache_mem_k[0].reshape(db, N_MEM, D_MODEL), cache_mem_v[0].reshape(db, N_MEM, D_MODEL), wts)

    n_p, n_s = b * seq, db * dseq
    n_tok = n_p + n_s
    h_all = jnp.concatenate([h_p.reshape(n_p, D_MODEL), h_s.reshape(n_s, D_MODEL)], axis=0)
    route = jnp.concatenate([route_p.reshape(n_p, LANES), route_s.reshape(n_s, LANES)], axis=0)
    x2 = jnp.concatenate([x2_p.reshape(n_p, D_MODEL), x2_s.reshape(n_s, D_MODEL)], axis=0)
    top_e = route[:, TOP_K:2 * TOP_K].astype(jnp.int32)

    row_tok, block_e, pos = _routing(top_e, n_tok)
    xs = jnp.take(h_all, row_tok, axis=0)

    wgu = jnp.concatenate([w_gate_up[0][:, :, 0::2], w_gate_up[0][:, :, 1::2]], axis=-1).astype(BF16)
    bgu = jnp.concatenate([b_gate_up[0][:, 0::2], b_gate_up[0][:, 1::2]], axis=-1).reshape(N_EXPERTS, 1, 2 * D_FF)
    yb = _expert_ffn(block_e, xs, wgu, bgu, w_down[0].astype(BF16), b_down[0].reshape(N_EXPERTS, 1, D_MODEL))

    yg = jnp.take(yb, pos.reshape(-1), axis=0).reshape(n_tok, TOP_K * D_MODEL)
    y = _combine(x2, yg, route, row(final_norm_w))

    y_p = y[:n_p].reshape(b, seq, D_MODEL)
    y_s = y[n_p:].reshape(db, dseq, D_MODEL)
    shape_kv = (1, b, N_MEM, MEM_HEADS, MEM_HD)
    return (y_p, y_s, mk_p.reshape(shape_kv), mv_p.reshape(shape_kv), ret_p[None], pool_p[None],
            ret_s[None], pool_s[None])
```

```python
import functools

import numpy as np
import jax
import jax.numpy as jnp
from jax import lax
from jax.experimental import pallas as pl
from jax.experimental.pallas import tpu as pltpu

D_MODEL = 1024
CHUNK = 64
PAST_LEN = 4096
RET_HEADS = 4
RET_DK = 128
RET_DV = 128
RET_QK = RET_HEADS * RET_DK
RET_VW = RET_HEADS * RET_DV
ROPE_BASE = 10000.0
POOL_WINDOWS = (2, 4, 8, 16)
POOL_GROUPS = 4
POOL_WIDTH = D_MODEL // 2
POOL_C = POOL_WIDTH // POOL_GROUPS
POOL_HIST = max(POOL_WINDOWS) - 1
HIST_ROWS = POOL_HIST + 1
IN_WIDTH = 2 * RET_QK + 2 * RET_VW + POOL_WIDTH
N_MEM = 256
MEM_HEADS = 4
MEM_HD = D_MODEL // MEM_HEADS
N_EXPERTS = 32
TOP_K = 4
D_FF = D_MODEL
SWIGLU_LIMIT = 7.0
SWIGLU_ALPHA = 1.702
EPS = 1e-5

LANES = 128
PROMPT_TILE = 256
EXPERT_ROWS = 512
KV_ROWS = 512
COMBINE_ROWS = 512
VMEM_LIMIT = 56 * 1024 * 1024

BF16 = jnp.bfloat16
F32 = jnp.float32


def _rms(x, w):
    return x * lax.rsqrt(jnp.mean(x * x, axis=-1, keepdims=True) + EPS) * w


def _dot(a, b):
    return jnp.dot(a, b, preferred_element_type=F32)


def _dot_nt(a, b):
    return lax.dot_general(a, b, (((1,), (1,)), ((), ())), preferred_element_type=F32)


def _const_spec(shape):
    nd = len(shape)
    return pl.BlockSpec(shape, lambda *_: (0,) * nd, pipeline_mode=pl.Buffered(1))


def _mem_kv_kernel(mem_ref, nw_ref, w_ref, k_ref, v_ref):
    xn = _rms(mem_ref[...], nw_ref[...]).astype(BF16)
    kv = _dot(xn, w_ref[...])
    k_ref[...] = kv[:, :D_MODEL]
    v_ref[...] = kv[:, D_MODEL:]


def _mem_kv(mem2d, mem_norm_w, w_kv_bf):
    rows = mem2d.shape[0]
    return pl.pallas_call(
        _mem_kv_kernel,
        grid=(rows // KV_ROWS,),
        in_specs=[
            pl.BlockSpec((KV_ROWS, D_MODEL), lambda i: (i, 0)),
            _const_spec((1, D_MODEL)),
            _const_spec((D_MODEL, 2 * D_MODEL)),
        ],
        out_specs=[
            pl.BlockSpec((KV_ROWS, D_MODEL), lambda i: (i, 0)),
            pl.BlockSpec((KV_ROWS, D_MODEL), lambda i: (i, 0)),
        ],
        out_shape=[jax.ShapeDtypeStruct((rows, D_MODEL), F32)] * 2,
        compiler_params=pltpu.CompilerParams(
            dimension_semantics=("arbitrary",), vmem_limit_bytes=VMEM_LIMIT),
        name="mem_kv",
    )(mem2d, mem_norm_w, w_kv_bf)


def _layer_kernel(x_ref, cos_ref, sin_ref, dmat_ref, qdec_ref, kdec_ref, state0_ref, hist0_ref,
                  mk_ref, mv_ref, nmix_ref, win_ref, gnw_ref, wpool_ref, pscale_ref, wout_ref,
                  nmem_ref, wq_ref, wo_ref, nffn_ref, rw_ref, rb_ref,
                  x2_ref, h_ref, route_ref, rstate_ref, pstate_ref,
                  s_scr, ext_scr, mk_scr, mv_scr, *, tl, pos0, cdec):
    t = pl.program_id(1)

    @pl.when(t == 0)
    def _():
        s_scr[...] = state0_ref[0]
        ext_scr[0:HIST_ROWS, :] = hist0_ref[0]
        mk_scr[...] = mk_ref[0].astype(BF16)
        mv_scr[...] = mv_ref[0].astype(BF16)

    x = x_ref[0]
    proj = _dot(_rms(x, nmix_ref[...]).astype(BF16), win_ref[...])
    cos = cos_ref[...]
    sin = sin_ref[...]

    outs = []
    for hd in range(RET_HEADS):
        lo = hd * RET_DK
        q = proj[:, lo:lo + RET_DK]
        k = proj[:, RET_QK + lo:RET_QK + lo + RET_DK]
        v = proj[:, 2 * RET_QK + lo:2 * RET_QK + lo + RET_DV]
        g = proj[:, 2 * RET_QK + RET_VW + lo:2 * RET_QK + RET_VW + lo + RET_DV]
        qr = (q * cos + pltpu.roll(q, RET_DK // 2, 1) * sin) * (RET_DK ** -0.5)
        kr = k * cos + pltpu.roll(k, RET_DK // 2, 1) * sin
        vb = v.astype(BF16)
        s = _dot_nt(qr.astype(BF16), kr.astype(BF16)) * dmat_ref[hd]
        o = _dot(s.astype(BF16), vb)
        state = s_scr[hd]
        o = o + _dot((qr * qdec_ref[hd]).astype(BF16), state.astype(BF16))
        kd_t = jnp.transpose(kr * kdec_ref[hd]).astype(BF16)
        s_scr[hd] = cdec[hd] * state + _dot(kd_t, vb)
        mu = jnp.mean(o, axis=-1, keepdims=True)
        oc = o - mu
        var = jnp.mean(oc * oc, axis=-1, keepdims=True)
        on = oc * lax.rsqrt(var + EPS)
        outs.append(on * gnw_ref[:, lo:lo + RET_DV] * (g * jax.nn.sigmoid(g)))
    rstate_ref[0] = s_scr[...]

    pin = proj[:, 2 * RET_QK + 2 * RET_VW:]
    ext_scr[HIST_ROWS:HIST_ROWS + tl, :] = pin
    pstate_ref[0] = pin[tl - POOL_HIST:, :]
    pos = (pos0 + t * tl + lax.broadcasted_iota(jnp.int32, (tl, POOL_C), 0)).astype(F32)
    for gi, w in enumerate(POOL_WINDOWS):
        lo = gi * POOL_C
        wsum = ext_scr[HIST_ROWS:HIST_ROWS + tl, lo:lo + POOL_C]
        for j in range(1, w):
            wsum = wsum + ext_scr[HIST_ROWS - j:HIST_ROWS - j + tl, lo:lo + POOL_C]
        cnt = jnp.minimum(float(w), pos + 1.0)
        d = wsum / cnt - pin[:, lo:lo + POOL_C]
        y = _dot(d.astype(BF16), wpool_ref[gi])
        outs.append(y * pscale_ref[:, lo:lo + POOL_C])
    ext_scr[0:HIST_ROWS, :] = ext_scr[tl:tl + HIST_ROWS, :]

    mix = jnp.concatenate(outs, axis=-1).astype(BF16)
    x1 = x + _dot(mix, wout_ref[...])

    qm = _dot(_rms(x1, nmem_ref[...]).astype(BF16), wq_ref[...])
    aouts = []
    for hd in range(MEM_HEADS):
        lo = hd * MEM_HD
        s = _dot_nt(qm[:, lo:lo + MEM_HD].astype(BF16), mk_scr[:, lo:lo + MEM_HD]) * (MEM_HD ** -0.5)
        e = jnp.exp(s - jnp.max(s, axis=-1, keepdims=True))
        p = e / jnp.sum(e, axis=-1, keepdims=True)
        aouts.append(_dot(p.astype(BF16), mv_scr[:, lo:lo + MEM_HD]))
    att = jnp.concatenate(aouts, axis=-1).astype(BF16)
    x2 = x1 + _dot(att, wo_ref[...])
    x2_ref[0] = x2

    hn = _rms(x2, nffn_ref[...])
    hb = hn.astype(BF16)
    h_ref[0] = hb
    logits = _dot(hb, rw_ref[...]) + rb_ref[...]
    eiota = lax.broadcasted_iota(jnp.int32, (tl, N_EXPERTS), 1)
    neg = jnp.finfo(F32).min
    vals, idxs = [], []
    for _k in range(TOP_K):
        m = jnp.max(logits, axis=-1, keepdims=True)
        idx = jnp.min(jnp.where(logits == m, eiota, N_EXPERTS), axis=-1, keepdims=True)
        vals.append(m)
        idxs.append(idx)
        logits = jnp.where(eiota == idx, neg, logits)
    exps = [jnp.exp(vk - vals[0]) for vk in vals]
    den = exps[0] + exps[1] + exps[2] + exps[3]
    lane = lax.broadcasted_iota(jnp.int32, (tl, LANES), 1)
    route = jnp.zeros((tl, LANES), F32)
    for kk in range(TOP_K):
        route = jnp.where(lane == kk, exps[kk] / den, route)
        route = jnp.where(lane == TOP_K + kk, idxs[kk].astype(F32), route)
    route_ref[0] = route


def _decay_tables(tl):
    hh = np.arange(RET_HEADS, dtype=np.float64)
    log_g = np.log1p(-np.exp2(-5.0 - hh))
    idx = np.arange(tl, dtype=np.float64)
    dist = np.abs(idx[:, None] - idx[None, :])
    visible = (idx[None, :] // CHUNK) <= (idx[:, None] // CHUNK)
    dmat = np.where(visible[None], np.exp(log_g[:, None, None] * dist[None]), 0.0)
    qdec = np.exp(log_g[:, None] * (idx[None, :] + 1.0))
    kdec = np.exp(log_g[:, None] * (tl - 1.0 - idx[None, :]))
    cdec = tuple(float(c) for c in np.exp(log_g * tl).astype(np.float32))
    bcast = lambda a: np.ascontiguousarray(np.broadcast_to(a[:, :, None], (RET_HEADS, tl, RET_DK)))
    return (jnp.asarray(dmat, F32), jnp.asarray(bcast(qdec), F32), jnp.asarray(bcast(kdec), F32), cdec)


def _rotary_tables(pos0, length):
    half = RET_DK // 2
    inv_freq = jnp.power(ROPE_BASE, -jnp.arange(half, dtype=F32) / half)
    ang = (pos0 + jnp.arange(length, dtype=jnp.int32)).astype(F32)[:, None] * inv_freq[None, :]
    cos, sin = jnp.cos(ang), jnp.sin(ang)
    return jnp.concatenate([cos, cos], axis=-1), jnp.concatenate([-sin, sin], axis=-1)


def _layer(x, pos0, tl, state0, hist0, mk, mv, wts):
    b, length, _ = x.shape
    nt = length // tl
    cos, sin = _rotary_tables(pos0, length)
    dmat, qdec, kdec, cdec = _decay_tables(tl)
    kern = functools.partial(_layer_kernel, tl=tl, pos0=pos0, cdec=cdec)
    tok = lambda width: pl.BlockSpec((1, tl, width), lambda i, j: (i, j, 0))
    per_stream = lambda *shape: pl.BlockSpec((1,) + shape, lambda i, j: (i,) + (0,) * len(shape))
    in_specs = [
        tok(D_MODEL),
        pl.BlockSpec((tl, RET_DK), lambda i, j: (j, 0)),
        pl.BlockSpec((tl, RET_DK), lambda i, j: (j, 0)),
        _const_spec((RET_HEADS, tl, tl)),
        _const_spec((RET_HEADS, tl, RET_DK)),
        _const_spec((RET_HEADS, tl, RET_DK)),
        per_stream(RET_HEADS, RET_DK, RET_DV),
        per_stream(HIST_ROWS, POOL_WIDTH),
        per_stream(N_MEM, D_MODEL),
        per_stream(N_MEM, D_MODEL),
    ] + [_const_spec(w.shape) for w in wts]
    out_specs = [
        tok(D_MODEL), tok(D_MODEL), tok(LANES),
        per_stream(RET_HEADS, RET_DK, RET_DV),
        per_stream(POOL_HIST, POOL_WIDTH),
    ]
    out_shape = [
        jax.ShapeDtypeStruct((b, length, D_MODEL), F32),
        jax.ShapeDtypeStruct((b, length, D_MODEL), BF16),
        jax.ShapeDtypeStruct((b, length, LANES), F32),
        jax.ShapeDtypeStruct((b, RET_HEADS, RET_DK, RET_DV), F32),
        jax.ShapeDtypeStruct((b, POOL_HIST, POOL_WIDTH), F32),
    ]
    scratch = [
        pltpu.VMEM((RET_HEADS, RET_DK, RET_DV), F32),
        pltpu.VMEM((HIST_ROWS + tl, POOL_WIDTH), F32),
        pltpu.VMEM((N_MEM, D_MODEL), BF16),
        pltpu.VMEM((N_MEM, D_MODEL), BF16),
    ]
    return pl.pallas_call(
        kern,
        grid=(b, nt),
        in_specs=in_specs,
        out_specs=out_specs,
        out_shape=out_shape,
        scratch_shapes=scratch,
        compiler_params=pltpu.CompilerParams(
            dimension_semantics=("arbitrary", "arbitrary"), vmem_limit_bytes=VMEM_LIMIT),
        name="layer_tl%d" % tl,
    )(x, cos, sin, dmat, qdec, kdec, state0, hist0, mk, mv, *wts)


PREP_ROWS = 512
SPLIT_COLS = 2 * LANES


def _split_kernel(w_ref, perm_ref, o_ref):
    perm = perm_ref[...]
    for c in range(2 * D_FF // SPLIT_COLS):
        wc = w_ref[0, :, c * SPLIT_COLS:(c + 1) * SPLIT_COLS].astype(BF16)
        pc = _dot(wc, perm).astype(BF16)
        o_ref[0, :, c * LANES:(c + 1) * LANES] = pc[:, :LANES]
        o_ref[0, :, D_FF + c * LANES:D_FF + (c + 1) * LANES] = pc[:, LANES:]


def _split_gate_up(w_gate_up):
    perm = np.zeros((SPLIT_COLS, SPLIT_COLS), np.float32)
    j = np.arange(LANES)
    perm[2 * j, j] = 1.0
    perm[2 * j + 1, LANES + j] = 1.0
    blk = pl.BlockSpec((1, PREP_ROWS, 2 * D_FF), lambda e, r: (e, r, 0))
    return pl.pallas_call(
        _split_kernel,
        grid=(N_EXPERTS, D_MODEL // PREP_ROWS),
        in_specs=[blk, _const_spec((SPLIT_COLS, SPLIT_COLS))],
        out_specs=blk,
        out_shape=jax.ShapeDtypeStruct(w_gate_up.shape, BF16),
        compiler_params=pltpu.CompilerParams(
            dimension_semantics=("arbitrary", "arbitrary"), vmem_limit_bytes=VMEM_LIMIT),
        name="split_gate_up",
    )(w_gate_up, jnp.asarray(perm, BF16))


def _expert_kernel(be_ref, x_ref, wgu_ref, bgu_ref, wd_ref, bd_ref, y_ref):
    del be_ref
    gu = _dot(x_ref[...], wgu_ref[0]) + bgu_ref[0]
    gate = jnp.minimum(gu[:, :D_FF], SWIGLU_LIMIT)
    up = jnp.clip(gu[:, D_FF:], -SWIGLU_LIMIT, SWIGLU_LIMIT)
    act = (up + 1.0) * gate * jax.nn.sigmoid(SWIGLU_ALPHA * gate)
    y_ref[...] = _dot(act.astype(BF16), wd_ref[0]) + bd_ref[0]


def _expert_ffn(block_e, xs, wgu, bgu, wd, bd):
    m_pad = xs.shape[0]
    n_blocks = m_pad // EXPERT_ROWS
    grid_spec = pltpu.PrefetchScalarGridSpec(
        num_scalar_prefetch=1,
        grid=(n_blocks,),
        in_specs=[
            pl.BlockSpec((EXPERT_ROWS, D_MODEL), lambda i, be: (i, 0)),
            pl.BlockSpec((1, D_MODEL, 2 * D_FF), lambda i, be: (be[i], 0, 0)),
            pl.BlockSpec((1, 1, 2 * D_FF), lambda i, be: (be[i], 0, 0)),
            pl.BlockSpec((1, D_FF, D_MODEL), lambda i, be: (be[i], 0, 0)),
            pl.BlockSpec((1, 1, D_MODEL), lambda i, be: (be[i], 0, 0)),
        ],
        out_specs=pl.BlockSpec((EXPERT_ROWS, D_MODEL), lambda i, be: (i, 0)),
    )
    return pl.pallas_call(
        _expert_kernel,
        grid_spec=grid_spec,
        out_shape=jax.ShapeDtypeStruct((m_pad, D_MODEL), F32),
        compiler_params=pltpu.CompilerParams(
            dimension_semantics=("arbitrary",), vmem_limit_bytes=VMEM_LIMIT),
        name="expert_ffn",
    )(block_e, xs, wgu, bgu, wd, bd)


def _routing(top_e, n_tok):
    n_slots = n_tok * TOP_K
    flat_e = top_e.reshape(-1)
    onehot = (flat_e[:, None] == jnp.arange(N_EXPERTS, dtype=jnp.int32)[None, :]).astype(jnp.int32)
    csum = jnp.cumsum(onehot, axis=0)
    rank = jnp.sum(csum * onehot, axis=1) - 1
    counts = csum[-1]
    pcounts = (counts + EXPERT_ROWS - 1) // EXPERT_ROWS * EXPERT_ROWS
    pend = jnp.cumsum(pcounts)
    pstart = pend - pcounts
    pos = pstart[flat_e] + rank
    n_blocks = -(-(n_slots + N_EXPERTS * (EXPERT_ROWS - 1)) // EXPERT_ROWS)
    m_pad = n_blocks * EXPERT_ROWS
    row_tok = jnp.zeros((m_pad,), jnp.int32).at[pos].set(jnp.arange(n_slots, dtype=jnp.int32) // TOP_K)
    block_start = jnp.arange(n_blocks, dtype=jnp.int32) * EXPERT_ROWS
    block_e = jnp.minimum(
        jnp.sum((pend[None, :] <= block_start[:, None]).astype(jnp.int32), axis=1), N_EXPERTS - 1)
    return row_tok, block_e, pos.reshape(n_tok, TOP_K).T


def _combine_kernel(x_ref, yg_ref, route_ref, fw_ref, y_ref):
    acc = x_ref[...]
    route = route_ref[...]
    for kk in range(TOP_K):
        acc = acc + yg_ref[kk] * route[:, kk:kk + 1]
    y_ref[...] = _rms(acc, fw_ref[...])


def _combine(x2, yg, route, final_w):
    n_tok = x2.shape[0]
    return pl.pallas_call(
        _combine_kernel,
        grid=(n_tok // COMBINE_ROWS,),
        in_specs=[
            pl.BlockSpec((COMBINE_ROWS, D_MODEL), lambda i: (i, 0)),
            pl.BlockSpec((TOP_K, COMBINE_ROWS, D_MODEL), lambda i: (0, i, 0)),
            pl.BlockSpec((COMBINE_ROWS, LANES), lambda i: (i, 0)),
            _const_spec((1, D_MODEL)),
        ],
        out_specs=pl.BlockSpec((COMBINE_ROWS, D_MODEL), lambda i: (i, 0)),
        out_shape=jax.ShapeDtypeStruct((n_tok, D_MODEL), F32),
        compiler_params=pltpu.CompilerParams(
            dimension_semantics=("arbitrary",), vmem_limit_bytes=VMEM_LIMIT),
        name="combine_%d" % n_tok,
    )(x2, yg, route, final_w)


def kernel(x_prompt, x_sample, cache_mem_k, cache_mem_v, state_ret, state_pool, mem_prompt,
           norm_mix_w, w_in, ret_gn_w, w_pool, pool_scale, w_out, norm_mem_w, mem_norm_w,
           w_q_mem, w_kv_mem, w_o_mem, norm_ffn_w, router_w, router_b, w_gate_up, b_gate_up,
           w_down, b_down, final_norm_w):
    assert norm_mix_w.shape[0] == 1, "one layer"
    b, seq, _ = x_prompt.shape
    db, dseq, _ = x_sample.shape
    row = lambda a: a.reshape(1, -1)

    mk_p, mv_p = _mem_kv(mem_prompt.reshape(b * N_MEM, D_MODEL), row(mem_norm_w[0]), w_kv_mem[0].astype(BF16))
    mk_p = mk_p.reshape(b, N_MEM, D_MODEL)
    mv_p = mv_p.reshape(b, N_MEM, D_MODEL)

    wts = (row(norm_mix_w[0]), w_in[0].astype(BF16), row(ret_gn_w[0]), w_pool[0].astype(BF16),
           row(pool_scale[0]), w_out[0].astype(BF16), row(norm_mem_w[0]), w_q_mem[0].astype(BF16),
           w_o_mem[0].astype(BF16), row(norm_ffn_w[0]), router_w[0].astype(BF16), row(router_b[0]))

    zero_state = jnp.zeros((b, RET_HEADS, RET_DK, RET_DV), F32)
    zero_hist = jnp.zeros((b, HIST_ROWS, POOL_WIDTH), F32)
    x2_p, h_p, route_p, ret_p, pool_p = _layer(x_prompt, 0, PROMPT_TILE, zero_state, zero_hist, mk_p, mv_p, wts)

    hist_s = jnp.concatenate([jnp.zeros((db, 1, POOL_WIDTH), F32), state_pool[0]], axis=1)
    x2_s, h_s, route_s, ret_s, pool_s = _layer(
        x_sample, PAST_LEN, dseq, state_ret[0], hist_s,
        cache_mem_k[0].reshape(db, N_MEM, D_MODEL), cache_mem_v[0].reshape(db, N_MEM, D_MODEL), wts)

    n_p, n_s = b * seq, db * dseq
    n_tok = n_p + n_s
    route_p = route_p.reshape(n_p, LANES)
    route_s = route_s.reshape(n_s, LANES)
    h_all = jnp.concatenate([h_p.reshape(n_p, D_MODEL), h_s.reshape(n_s, D_MODEL)], axis=0)
    top_e = jnp.concatenate([route_p[:, TOP_K:2 * TOP_K], route_s[:, TOP_K:2 * TOP_K]], axis=0).astype(jnp.int32)

    row_tok, block_e, pos = _routing(top_e, n_tok)
    xs = h_all.at[row_tok].get(mode="promise_in_bounds")

    wgu = _split_gate_up(w_gate_up[0])
    bgu = jnp.concatenate([b_gate_up[0][:, 0::2], b_gate_up[0][:, 1::2]], axis=-1).reshape(N_EXPERTS, 1, 2 * D_FF)
    yb = _expert_ffn(block_e, xs, wgu, bgu, w_down[0].astype(BF16), b_down[0].reshape(N_EXPERTS, 1, D_MODEL))

    final_w = row(final_norm_w)
    yg_p = yb.at[pos[:, :n_p]].get(mode="promise_in_bounds")
    yg_s = yb.at[pos[:, n_p:]].get(mode="promise_in_bounds")
    y_p = _combine(x2_p.reshape(n_p, D_MODEL), yg_p, route_p, final_w).reshape(b, seq, D_MODEL)
    y_s = _combine(x2_s.reshape(n_s, D_MODEL), yg_s, route_s, final_w).reshape(db, dseq, D_MODEL)
    shape_kv = (1, b, N_MEM, MEM_HEADS, MEM_HD)
    return (y_p, y_s, mk_p.reshape(shape_kv), mv_p.reshape(shape_kv), ret_p[None], pool_p[None],
            ret_s[None], pool_s[None])
```

```python
import functools

import numpy as np
import jax
import jax.numpy as jnp
from jax import lax
from jax.experimental import pallas as pl
from jax.experimental.pallas import tpu as pltpu
from jax.experimental.pallas import tpu_sc as plsc

D_MODEL = 1024
CHUNK = 64
PAST_LEN = 4096
RET_HEADS = 4
RET_DK = 128
RET_DV = 128
RET_QK = RET_HEADS * RET_DK
RET_VW = RET_HEADS * RET_DV
ROPE_BASE = 10000.0
POOL_WINDOWS = (2, 4, 8, 16)
POOL_GROUPS = 4
POOL_WIDTH = D_MODEL // 2
POOL_C = POOL_WIDTH // POOL_GROUPS
POOL_HIST = max(POOL_WINDOWS) - 1
HIST_ROWS = POOL_HIST + 1
IN_WIDTH = 2 * RET_QK + 2 * RET_VW + POOL_WIDTH
N_MEM = 256
MEM_HEADS = 4
MEM_HD = D_MODEL // MEM_HEADS
N_EXPERTS = 32
TOP_K = 4
D_FF = D_MODEL
SWIGLU_LIMIT = 7.0
SWIGLU_ALPHA = 1.702
EPS = 1e-5

LANES = 128
PROMPT_TILE = 256
EXPERT_ROWS = 512
KV_ROWS = 512
COMBINE_ROWS = 512
VMEM_LIMIT = 56 * 1024 * 1024

BF16 = jnp.bfloat16
F32 = jnp.float32


def _rms(x, w):
    return x * lax.rsqrt(jnp.mean(x * x, axis=-1, keepdims=True) + EPS) * w


def _dot(a, b):
    return jnp.dot(a, b, preferred_element_type=F32)


def _dot_nt(a, b):
    return lax.dot_general(a, b, (((1,), (1,)), ((), ())), preferred_element_type=F32)


def _const_spec(shape):
    nd = len(shape)
    return pl.BlockSpec(shape, lambda *_: (0,) * nd, pipeline_mode=pl.Buffered(1))


def _mem_kv_kernel(mem_ref, nw_ref, w_ref, k_ref, v_ref):
    xn = _rms(mem_ref[...], nw_ref[...]).astype(BF16)
    kv = _dot(xn, w_ref[...])
    k_ref[...] = kv[:, :D_MODEL]
    v_ref[...] = kv[:, D_MODEL:]


def _mem_kv(mem2d, mem_norm_w, w_kv_bf):
    rows = mem2d.shape[0]
    return pl.pallas_call(
        _mem_kv_kernel,
        grid=(rows // KV_ROWS,),
        in_specs=[
            pl.BlockSpec((KV_ROWS, D_MODEL), lambda i: (i, 0)),
            _const_spec((1, D_MODEL)),
            _const_spec((D_MODEL, 2 * D_MODEL)),
        ],
        out_specs=[
            pl.BlockSpec((KV_ROWS, D_MODEL), lambda i: (i, 0)),
            pl.BlockSpec((KV_ROWS, D_MODEL), lambda i: (i, 0)),
        ],
        out_shape=[jax.ShapeDtypeStruct((rows, D_MODEL), F32)] * 2,
        compiler_params=pltpu.CompilerParams(
            dimension_semantics=("arbitrary",), vmem_limit_bytes=VMEM_LIMIT),
        name="mem_kv",
    )(mem2d, mem_norm_w, w_kv_bf)


def _layer_kernel(x_ref, cos_ref, sin_ref, dmat_ref, qdec_ref, kdec_ref, state0_ref, hist0_ref,
                  mk_ref, mv_ref, nmix_ref, win_ref, gnw_ref, wpool_ref, pscale_ref, wout_ref,
                  nmem_ref, wq_ref, wo_ref, nffn_ref, rw_ref, rb_ref, ltri_ref, cnt0_ref,
                  x2_ref, h_ref, route_ref, rstate_ref, pstate_ref, cnt_ref,
                  s_scr, ext_scr, mk_scr, mv_scr, cnt_scr, *, tl, pos0, cdec):
    t = pl.program_id(1)

    @pl.when((pl.program_id(0) == 0) & (t == 0))
    def _():
        cnt_scr[...] = cnt0_ref[...]

    @pl.when(t == 0)
    def _():
        s_scr[...] = state0_ref[0]
        ext_scr[0:HIST_ROWS, :] = hist0_ref[0]
        mk_scr[...] = mk_ref[0].astype(BF16)
        mv_scr[...] = mv_ref[0].astype(BF16)

    x = x_ref[0]
    proj = _dot(_rms(x, nmix_ref[...]).astype(BF16), win_ref[...])
    cos = cos_ref[...]
    sin = sin_ref[...]

    outs = []
    for hd in range(RET_HEADS):
        lo = hd * RET_DK
        q = proj[:, lo:lo + RET_DK]
        k = proj[:, RET_QK + lo:RET_QK + lo + RET_DK]
        v = proj[:, 2 * RET_QK + lo:2 * RET_QK + lo + RET_DV]
        g = proj[:, 2 * RET_QK + RET_VW + lo:2 * RET_QK + RET_VW + lo + RET_DV]
        qr = (q * cos + pltpu.roll(q, RET_DK // 2, 1) * sin) * (RET_DK ** -0.5)
        kr = k * cos + pltpu.roll(k, RET_DK // 2, 1) * sin
        vb = v.astype(BF16)
        s = _dot_nt(qr.astype(BF16), kr.astype(BF16)) * dmat_ref[hd]
        o = _dot(s.astype(BF16), vb)
        state = s_scr[hd]
        o = o + _dot((qr * qdec_ref[hd]).astype(BF16), state.astype(BF16))
        kd_t = jnp.transpose(kr * kdec_ref[hd]).astype(BF16)
        s_scr[hd] = cdec[hd] * state + _dot(kd_t, vb)
        mu = jnp.mean(o, axis=-1, keepdims=True)
        oc = o - mu
        var = jnp.mean(oc * oc, axis=-1, keepdims=True)
        on = oc * lax.rsqrt(var + EPS)
        outs.append(on * gnw_ref[:, lo:lo + RET_DV] * (g * jax.nn.sigmoid(g)))
    rstate_ref[0] = s_scr[...]

    pin = proj[:, 2 * RET_QK + 2 * RET_VW:]
    ext_scr[HIST_ROWS:HIST_ROWS + tl, :] = pin
    pstate_ref[0] = pin[tl - POOL_HIST:, :]
    pos = (pos0 + t * tl + lax.broadcasted_iota(jnp.int32, (tl, POOL_C), 0)).astype(F32)
    for gi, w in enumerate(POOL_WINDOWS):
        lo = gi * POOL_C
        wsum = ext_scr[HIST_ROWS:HIST_ROWS + tl, lo:lo + POOL_C]
        for j in range(1, w):
            wsum = wsum + ext_scr[HIST_ROWS - j:HIST_ROWS - j + tl, lo:lo + POOL_C]
        cnt = jnp.minimum(float(w), pos + 1.0)
        d = wsum / cnt - pin[:, lo:lo + POOL_C]
        y = _dot(d.astype(BF16), wpool_ref[gi])
        outs.append(y * pscale_ref[:, lo:lo + POOL_C])
    ext_scr[0:HIST_ROWS, :] = ext_scr[tl:tl + HIST_ROWS, :]

    mix = jnp.concatenate(outs, axis=-1).astype(BF16)
    x1 = x + _dot(mix, wout_ref[...])

    qm = _dot(_rms(x1, nmem_ref[...]).astype(BF16), wq_ref[...])
    aouts = []
    for hd in range(MEM_HEADS):
        lo = hd * MEM_HD
        s = _dot_nt(qm[:, lo:lo + MEM_HD].astype(BF16), mk_scr[:, lo:lo + MEM_HD]) * (MEM_HD ** -0.5)
        e = jnp.exp(s - jnp.max(s, axis=-1, keepdims=True))
        p = e / jnp.sum(e, axis=-1, keepdims=True)
        aouts.append(_dot(p.astype(BF16), mv_scr[:, lo:lo + MEM_HD]))
    att = jnp.concatenate(aouts, axis=-1).astype(BF16)
    x2 = x1 + _dot(att, wo_ref[...])
    x2_ref[0] = x2

    hn = _rms(x2, nffn_ref[...])
    h_ref[0] = hn
    logits = _dot(hn.astype(BF16), rw_ref[...]) + rb_ref[...]
    eiota = lax.broadcasted_iota(jnp.int32, (tl, N_EXPERTS), 1)
    neg = jnp.finfo(F32).min
    vals, idxs = [], []
    for _k in range(TOP_K):
        m = jnp.max(logits, axis=-1, keepdims=True)
        idx = jnp.min(jnp.where(logits == m, eiota, N_EXPERTS), axis=-1, keepdims=True)
        vals.append(m)
        idxs.append(idx)
        logits = jnp.where(eiota == idx, neg, logits)
    exps = [jnp.exp(vk - vals[0]) for vk in vals]
    den = exps[0] + exps[1] + exps[2] + exps[3]

    onehots = [(eiota == idx).astype(F32) for idx in idxs]
    picked = onehots[0] + onehots[1] + onehots[2] + onehots[3]
    before = cnt_scr[...] + _dot(ltri_ref[...], picked.astype(BF16))
    ranks = [jnp.sum(oh * before, axis=-1, keepdims=True) for oh in onehots]
    cnt_scr[...] = cnt_scr[...] + jnp.sum(picked, axis=0, keepdims=True)
    cnt_ref[...] = cnt_scr[...]

    lane = lax.broadcasted_iota(jnp.int32, (tl, LANES), 1)
    route = jnp.zeros((tl, LANES), F32)
    for kk in range(TOP_K):
        route = jnp.where(lane == kk, exps[kk] / den, route)
        route = jnp.where(lane == TOP_K + kk, idxs[kk].astype(F32), route)
        route = jnp.where(lane == 2 * TOP_K + kk, ranks[kk], route)
    route_ref[0] = route


def _decay_tables(tl):
    hh = np.arange(RET_HEADS, dtype=np.float64)
    log_g = np.log1p(-np.exp2(-5.0 - hh))
    idx = np.arange(tl, dtype=np.float64)
    dist = np.abs(idx[:, None] - idx[None, :])
    visible = (idx[None, :] // CHUNK) <= (idx[:, None] // CHUNK)
    dmat = np.where(visible[None], np.exp(log_g[:, None, None] * dist[None]), 0.0)
    qdec = np.exp(log_g[:, None] * (idx[None, :] + 1.0))
    kdec = np.exp(log_g[:, None] * (tl - 1.0 - idx[None, :]))
    cdec = tuple(float(c) for c in np.exp(log_g * tl).astype(np.float32))
    bcast = lambda a: np.ascontiguousarray(np.broadcast_to(a[:, :, None], (RET_HEADS, tl, RET_DK)))
    return (jnp.asarray(dmat, F32), jnp.asarray(bcast(qdec), F32), jnp.asarray(bcast(kdec), F32), cdec)


def _rotary_tables(pos0, length):
    half = RET_DK // 2
    inv_freq = jnp.power(ROPE_BASE, -jnp.arange(half, dtype=F32) / half)
    ang = (pos0 + jnp.arange(length, dtype=jnp.int32)).astype(F32)[:, None] * inv_freq[None, :]
    cos, sin = jnp.cos(ang), jnp.sin(ang)
    return jnp.concatenate([cos, cos], axis=-1), jnp.concatenate([-sin, sin], axis=-1)


def _layer(x, pos0, tl, state0, hist0, mk, mv, wts, cnt0):
    b, length, _ = x.shape
    nt = length // tl
    cos, sin = _rotary_tables(pos0, length)
    dmat, qdec, kdec, cdec = _decay_tables(tl)
    ltri = jnp.asarray(np.tril(np.ones((tl, tl), np.float32), -1), BF16)
    kern = functools.partial(_layer_kernel, tl=tl, pos0=pos0, cdec=cdec)
    tok = lambda width: pl.BlockSpec((1, tl, width), lambda i, j: (i, j, 0))
    per_stream = lambda *shape: pl.BlockSpec((1,) + shape, lambda i, j: (i,) + (0,) * len(shape))
    in_specs = [
        tok(D_MODEL),
        pl.BlockSpec((tl, RET_DK), lambda i, j: (j, 0)),
        pl.BlockSpec((tl, RET_DK), lambda i, j: (j, 0)),
        _const_spec((RET_HEADS, tl, tl)),
        _const_spec((RET_HEADS, tl, RET_DK)),
        _const_spec((RET_HEADS, tl, RET_DK)),
        per_stream(RET_HEADS, RET_DK, RET_DV),
        per_stream(HIST_ROWS, POOL_WIDTH),
        per_stream(N_MEM, D_MODEL),
        per_stream(N_MEM, D_MODEL),
    ] + [_const_spec(w.shape) for w in wts] + [_const_spec((tl, tl)), _const_spec((1, N_EXPERTS))]
    out_specs = [
        tok(D_MODEL), tok(D_MODEL), tok(LANES),
        per_stream(RET_HEADS, RET_DK, RET_DV),
        per_stream(POOL_HIST, POOL_WIDTH),
        pl.BlockSpec((1, N_EXPERTS), lambda i, j: (0, 0)),
    ]
    out_shape = [
        jax.ShapeDtypeStruct((b, length, D_MODEL), F32),
        jax.ShapeDtypeStruct((b, length, D_MODEL), F32),
        jax.ShapeDtypeStruct((b, length, LANES), F32),
        jax.ShapeDtypeStruct((b, RET_HEADS, RET_DK, RET_DV), F32),
        jax.ShapeDtypeStruct((b, POOL_HIST, POOL_WIDTH), F32),
        jax.ShapeDtypeStruct((1, N_EXPERTS), F32),
    ]
    scratch = [
        pltpu.VMEM((RET_HEADS, RET_DK, RET_DV), F32),
        pltpu.VMEM((HIST_ROWS + tl, POOL_WIDTH), F32),
        pltpu.VMEM((N_MEM, D_MODEL), BF16),
        pltpu.VMEM((N_MEM, D_MODEL), BF16),
        pltpu.VMEM((1, N_EXPERTS), F32),
    ]
    return pl.pallas_call(
        kern,
        grid=(b, nt),
        in_specs=in_specs,
        out_specs=out_specs,
        out_shape=out_shape,
        scratch_shapes=scratch,
        compiler_params=pltpu.CompilerParams(
            dimension_semantics=("arbitrary", "arbitrary"), vmem_limit_bytes=VMEM_LIMIT),
        name="layer_tl%d" % tl,
    )(x, cos, sin, dmat, qdec, kdec, state0, hist0, mk, mv, *wts, ltri, cnt0)


PREP_ROWS = 512
SPLIT_COLS = 2 * LANES


def _split_kernel(w_ref, perm_ref, o_ref):
    perm = perm_ref[...]
    for c in range(2 * D_FF // SPLIT_COLS):
        wc = w_ref[0, :, c * SPLIT_COLS:(c + 1) * SPLIT_COLS].astype(BF16)
        pc = _dot(wc, perm).astype(BF16)
        o_ref[0, :, c * LANES:(c + 1) * LANES] = pc[:, :LANES]
        o_ref[0, :, D_FF + c * LANES:D_FF + (c + 1) * LANES] = pc[:, LANES:]


def _split_gate_up(w_gate_up):
    perm = np.zeros((SPLIT_COLS, SPLIT_COLS), np.float32)
    j = np.arange(LANES)
    perm[2 * j, j] = 1.0
    perm[2 * j + 1, LANES + j] = 1.0
    blk = pl.BlockSpec((1, PREP_ROWS, 2 * D_FF), lambda e, r: (e, r, 0))
    return pl.pallas_call(
        _split_kernel,
        grid=(N_EXPERTS, D_MODEL // PREP_ROWS),
        in_specs=[blk, _const_spec((SPLIT_COLS, SPLIT_COLS))],
        out_specs=blk,
        out_shape=jax.ShapeDtypeStruct(w_gate_up.shape, BF16),
        compiler_params=pltpu.CompilerParams(
            dimension_semantics=("arbitrary", "arbitrary"), vmem_limit_bytes=VMEM_LIMIT),
        name="split_gate_up",
    )(w_gate_up, jnp.asarray(perm, BF16))


def _expert_kernel(be_ref, used_ref, x_ref, wgu_ref, bgu_ref, wd_ref, bd_ref, y_ref):
    del be_ref

    @pl.when(pl.program_id(0) < used_ref[0])
    def _():
        gu = _dot(x_ref[...].astype(BF16), wgu_ref[0]) + bgu_ref[0]
        gate = jnp.minimum(gu[:, :D_FF], SWIGLU_LIMIT)
        up = jnp.clip(gu[:, D_FF:], -SWIGLU_LIMIT, SWIGLU_LIMIT)
        act = (up + 1.0) * gate * jax.nn.sigmoid(SWIGLU_ALPHA * gate)
        y_ref[...] = _dot(act.astype(BF16), wd_ref[0]) + bd_ref[0]


def _expert_ffn(block_e, n_used, xs, wgu, bgu, wd, bd, n_blocks):
    blk = lambda i, be, used: (jnp.minimum(i, used[0] - 1), 0)
    per_expert = lambda i, be, used: (be[i], 0, 0)
    grid_spec = pltpu.PrefetchScalarGridSpec(
        num_scalar_prefetch=2,
        grid=(n_blocks,),
        in_specs=[
            pl.BlockSpec((EXPERT_ROWS, D_MODEL), blk),
            pl.BlockSpec((1, D_MODEL, 2 * D_FF), per_expert),
            pl.BlockSpec((1, 1, 2 * D_FF), per_expert),
            pl.BlockSpec((1, D_FF, D_MODEL), per_expert),
            pl.BlockSpec((1, 1, D_MODEL), per_expert),
        ],
        out_specs=pl.BlockSpec((EXPERT_ROWS, D_MODEL), blk),
    )
    return pl.pallas_call(
        _expert_kernel,
        grid_spec=grid_spec,
        out_shape=jax.ShapeDtypeStruct(xs.shape, F32),
        compiler_params=pltpu.CompilerParams(
            dimension_semantics=("arbitrary",), vmem_limit_bytes=VMEM_LIMIT),
        name="expert_ffn",
    )(block_e, n_used, xs, wgu, bgu, wd, bd)


def _expert_layout(counts, n_blocks):
    pcounts = (counts + EXPERT_ROWS - 1) // EXPERT_ROWS * EXPERT_ROWS
    pend = jnp.cumsum(pcounts)
    n_used = pend[-1:] // EXPERT_ROWS
    block_start = jnp.minimum(jnp.arange(n_blocks, dtype=jnp.int32), n_used[0] - 1) * EXPERT_ROWS
    block_e = jnp.sum((pend[None, :] <= block_start[:, None]).astype(jnp.int32), axis=1)
    return pend - pcounts, jnp.minimum(block_e, N_EXPERTS - 1), n_used


def _slot_rows(route, pstart):
    e = route[:, TOP_K:2 * TOP_K].astype(jnp.int32)
    rank = route[:, 2 * TOP_K:3 * TOP_K].astype(jnp.int32)
    hit = e[:, :, None] == jnp.arange(N_EXPERTS, dtype=jnp.int32)[None, None, :]
    return (rank + jnp.sum(jnp.where(hit, pstart[None, None, :], 0), axis=-1)).T


SC_WINDOW = 128
SC_COLS = 256


def _sc_mesh():
    return plsc.VectorSubcoreMesh(core_axis_name="c", subcore_axis_name="s")


def _dispatch(h_p, h_s, rows_p, rows_s, m_pad):
    @functools.partial(pl.kernel, mesh=_sc_mesh(),
                       out_type=jax.ShapeDtypeStruct((m_pad, D_MODEL), h_p.dtype), scratch_types=[])
    def k(hp_hbm, hs_hbm, rp_hbm, rs_hbm, xs_hbm):
        def body(x_vmem, i_vmem):
            j = pl.program_id(1)
            for kk in range(TOP_K):
                pltpu.sync_copy(x_vmem, xs_hbm.at[i_vmem.at[kk], pl.ds(j * SC_COLS, SC_COLS)])

        for src, rows in ((hp_hbm, rp_hbm), (hs_hbm, rs_hbm)):
            pltpu.emit_pipeline(
                body,
                grid=(src.shape[0] // SC_WINDOW, D_MODEL // SC_COLS),
                in_specs=[pl.BlockSpec((SC_WINDOW, SC_COLS), lambda i, j: (i, j)),
                          pl.BlockSpec((TOP_K, SC_WINDOW), lambda i, j: (0, i))],
                out_specs=[],
                core_axis_name=("c", "s"),
                dimension_semantics=(pltpu.PARALLEL, pltpu.ARBITRARY),
            )(src, rows)

    return k(h_p, h_s, rows_p, rows_s)


def _collect(yb, rows_p, rows_s):
    out_type = [jax.ShapeDtypeStruct((rows_p.shape[1], D_MODEL), yb.dtype),
                jax.ShapeDtypeStruct((rows_s.shape[1], D_MODEL), yb.dtype)]

    @functools.partial(pl.kernel, mesh=_sc_mesh(), out_type=out_type, scratch_types=[])
    def k(yb_hbm, rp_hbm, rs_hbm, op_hbm, os_hbm):
        def body(i_vmem, o_vmem):
            j = pl.program_id(1)
            pltpu.sync_copy(yb_hbm.at[i_vmem.at[0], pl.ds(j * SC_COLS, SC_COLS)], o_vmem)

        for rows, out in ((rp_hbm, op_hbm), (rs_hbm, os_hbm)):
            pltpu.emit_pipeline(
                body,
                grid=(rows.shape[1] // SC_WINDOW, D_MODEL // SC_COLS),
                in_specs=[pl.BlockSpec((1, SC_WINDOW), lambda i, j: (0, i))],
                out_specs=[pl.BlockSpec((SC_WINDOW, SC_COLS), lambda i, j: (i, j))],
                core_axis_name=("c", "s"),
                dimension_semantics=(pltpu.PARALLEL, pltpu.ARBITRARY),
            )(rows, out)

    return k(yb, rows_p, rows_s)


def _combine_kernel(x_ref, yg_ref, route_ref, fw_ref, y_ref):
    acc = x_ref[...]
    route = route_ref[...]
    for kk in range(TOP_K):
        acc = acc + yg_ref[kk] * route[:, kk:kk + 1]
    y_ref[...] = _rms(acc, fw_ref[...])


def _combine(x2, yg, route, final_w):
    n_tok = x2.shape[0]
    return pl.pallas_call(
        _combine_kernel,
        grid=(n_tok // COMBINE_ROWS,),
        in_specs=[
            pl.BlockSpec((COMBINE_ROWS, D_MODEL), lambda i: (i, 0)),
            pl.BlockSpec((TOP_K, COMBINE_ROWS, D_MODEL), lambda i: (0, i, 0)),
            pl.BlockSpec((COMBINE_ROWS, LANES), lambda i: (i, 0)),
            _const_spec((1, D_MODEL)),
        ],
        out_specs=pl.BlockSpec((COMBINE_ROWS, D_MODEL), lambda i: (i, 0)),
        out_shape=jax.ShapeDtypeStruct((n_tok, D_MODEL), F32),
        compiler_params=pltpu.CompilerParams(
            dimension_semantics=("arbitrary",), vmem_limit_bytes=VMEM_LIMIT),
        name="combine_%d" % n_tok,
    )(x2, yg, route, final_w)


def kernel(x_prompt, x_sample, cache_mem_k, cache_mem_v, state_ret, state_pool, mem_prompt,
           norm_mix_w, w_in, ret_gn_w, w_pool, pool_scale, w_out, norm_mem_w, mem_norm_w,
           w_q_mem, w_kv_mem, w_o_mem, norm_ffn_w, router_w, router_b, w_gate_up, b_gate_up,
           w_down, b_down, final_norm_w):
    assert norm_mix_w.shape[0] == 1, "one layer"
    b, seq, _ = x_prompt.shape
    db, dseq, _ = x_sample.shape
    row = lambda a: a.reshape(1, -1)

    mk_p, mv_p = _mem_kv(mem_prompt.reshape(b * N_MEM, D_MODEL), row(mem_norm_w[0]), w_kv_mem[0].astype(BF16))
    mk_p = mk_p.reshape(b, N_MEM, D_MODEL)
    mv_p = mv_p.reshape(b, N_MEM, D_MODEL)

    wts = (row(norm_mix_w[0]), w_in[0].astype(BF16), row(ret_gn_w[0]), w_pool[0].astype(BF16),
           row(pool_scale[0]), w_out[0].astype(BF16), row(norm_mem_w[0]), w_q_mem[0].astype(BF16),
           w_o_mem[0].astype(BF16), row(norm_ffn_w[0]), router_w[0].astype(BF16), row(router_b[0]))

    zero_state = jnp.zeros((b, RET_HEADS, RET_DK, RET_DV), F32)
    zero_hist = jnp.zeros((b, HIST_ROWS, POOL_WIDTH), F32)
    no_counts = jnp.zeros((1, N_EXPERTS), F32)
    x2_p, h_p, route_p, ret_p, pool_p, cnt_p = _layer(
        x_prompt, 0, PROMPT_TILE, zero_state, zero_hist, mk_p, mv_p, wts, no_counts)

    hist_s = jnp.concatenate([jnp.zeros((db, 1, POOL_WIDTH), F32), state_pool[0]], axis=1)
    x2_s, h_s, route_s, ret_s, pool_s, cnt_all = _layer(
        x_sample, PAST_LEN, dseq, state_ret[0], hist_s,
        cache_mem_k[0].reshape(db, N_MEM, D_MODEL), cache_mem_v[0].reshape(db, N_MEM, D_MODEL), wts, cnt_p)

    n_p, n_s = b * seq, db * dseq
    route_p = route_p.reshape(n_p, LANES)
    route_s = route_s.reshape(n_s, LANES)
    n_blocks = -(-((n_p + n_s) * TOP_K + N_EXPERTS * (EXPERT_ROWS - 1)) // EXPERT_ROWS)
    pstart, block_e, n_used = _expert_layout(cnt_all[0].astype(jnp.int32), n_blocks)
    rows_p = _slot_rows(route_p, pstart)
    rows_s = _slot_rows(route_s, pstart)

    xs = _dispatch(h_p.reshape(n_p, D_MODEL), h_s.reshape(n_s, D_MODEL), rows_p, rows_s, n_blocks * EXPERT_ROWS)

    wgu = _split_gate_up(w_gate_up[0])
    bgu = jnp.concatenate([b_gate_up[0][:, 0::2], b_gate_up[0][:, 1::2]], axis=-1).reshape(N_EXPERTS, 1, 2 * D_FF)
    yb = _expert_ffn(block_e, n_used, xs, wgu, bgu, w_down[0].astype(BF16),
                     b_down[0].reshape(N_EXPERTS, 1, D_MODEL), n_blocks)

    final_w = row(final_norm_w)
    yg_p, yg_s = _collect(yb, rows_p.reshape(1, TOP_K * n_p), rows_s.reshape(1, TOP_K * n_s))
    y_p = _combine(x2_p.reshape(n_p, D_MODEL), yg_p.reshape(TOP_K, n_p, D_MODEL), route_p, final_w)
    y_s = _combine(x2_s.reshape(n_s, D_MODEL), yg_s.reshape(TOP_K, n_s, D_MODEL), route_s, final_w)
    y_p = y_p.reshape(b, seq, D_MODEL)
    y_s = y_s.reshape(db, dseq, D_MODEL)
    shape_kv = (1, b, N_MEM, MEM_HEADS, MEM_HD)
    return (y_p, y_s, mk_p.reshape(shape_kv), mv_p.reshape(shape_kv), ret_p[None], pool_p[None],
            ret_s[None], pool_s[None])
```

```python
import functools

import numpy as np
import jax
import jax.numpy as jnp
from jax import lax
from jax.experimental import pallas as pl
from jax.experimental.pallas import tpu as pltpu
from jax.experimental.pallas import tpu_sc as plsc

D_MODEL = 1024
CHUNK = 64
PAST_LEN = 4096
RET_HEADS = 4
RET_DK = 128
RET_DV = 128
RET_QK = RET_HEADS * RET_DK
RET_VW = RET_HEADS * RET_DV
ROPE_BASE = 10000.0
POOL_WINDOWS = (2, 4, 8, 16)
POOL_GROUPS = 4
POOL_WIDTH = D_MODEL // 2
POOL_C = POOL_WIDTH // POOL_GROUPS
POOL_HIST = max(POOL_WINDOWS) - 1
HIST_ROWS = POOL_HIST + 1
IN_WIDTH = 2 * RET_QK + 2 * RET_VW + POOL_WIDTH
N_MEM = 256
MEM_HEADS = 4
MEM_HD = D_MODEL // MEM_HEADS
N_EXPERTS = 32
TOP_K = 4
D_FF = D_MODEL
SWIGLU_LIMIT = 7.0
SWIGLU_ALPHA = 1.702
EPS = 1e-5

LANES = 128
PROMPT_TILE = 256
EXPERT_ROWS = 512
KV_ROWS = 512
COMBINE_ROWS = 512
VMEM_LIMIT = 56 * 1024 * 1024

BF16 = jnp.bfloat16
F32 = jnp.float32


def _rms(x, w):
    return x * lax.rsqrt(jnp.mean(x * x, axis=-1, keepdims=True) + EPS) * w


def _dot(a, b):
    return jnp.dot(a, b, preferred_element_type=F32)


def _dot_nt(a, b):
    return lax.dot_general(a, b, (((1,), (1,)), ((), ())), preferred_element_type=F32)


def _pack_bf16(x):
    bits = lax.bitcast_convert_type(x.astype(BF16).astype(F32), jnp.int32)
    w = x.shape[1] // 2
    return lax.shift_right_logical(bits[:, :w], 16) | (bits[:, w:] & -65536)


def _unpack_bf16(p):
    lo = lax.bitcast_convert_type(lax.shift_left(p, 16), F32)
    hi = lax.bitcast_convert_type(p & -65536, F32)
    return lo, hi


def _const_spec(shape):
    nd = len(shape)
    return pl.BlockSpec(shape, lambda *_: (0,) * nd, pipeline_mode=pl.Buffered(1))


def _mem_kv_kernel(mem_ref, nw_ref, w_ref, k_ref, v_ref):
    xn = _rms(mem_ref[...], nw_ref[...]).astype(BF16)
    kv = _dot(xn, w_ref[...])
    k_ref[...] = kv[:, :D_MODEL]
    v_ref[...] = kv[:, D_MODEL:]


def _mem_kv(mem2d, mem_norm_w, w_kv_bf):
    rows = mem2d.shape[0]
    return pl.pallas_call(
        _mem_kv_kernel,
        grid=(rows // KV_ROWS,),
        in_specs=[
            pl.BlockSpec((KV_ROWS, D_MODEL), lambda i: (i, 0)),
            _const_spec((1, D_MODEL)),
            _const_spec((D_MODEL, 2 * D_MODEL)),
        ],
        out_specs=[
            pl.BlockSpec((KV_ROWS, D_MODEL), lambda i: (i, 0)),
            pl.BlockSpec((KV_ROWS, D_MODEL), lambda i: (i, 0)),
        ],
        out_shape=[jax.ShapeDtypeStruct((rows, D_MODEL), F32)] * 2,
        compiler_params=pltpu.CompilerParams(
            dimension_semantics=("arbitrary",), vmem_limit_bytes=VMEM_LIMIT),
        name="mem_kv",
    )(mem2d, mem_norm_w, w_kv_bf)


def _layer_kernel(x_ref, cos_ref, sin_ref, dmat_ref, qdec_ref, kdec_ref, state0_ref, hist0_ref,
                  mk_ref, mv_ref, nmix_ref, win_ref, gnw_ref, wpool_ref, pscale_ref, wout_ref,
                  nmem_ref, wq_ref, wo_ref, nffn_ref, rw_ref, rb_ref, ltri_ref, cnt0_ref,
                  x2_ref, h_ref, route_ref, rstate_ref, pstate_ref, cnt_ref,
                  s_scr, ext_scr, mk_scr, mv_scr, cnt_scr, *, tl, pos0, cdec):
    t = pl.program_id(1)

    @pl.when((pl.program_id(0) == 0) & (t == 0))
    def _():
        cnt_scr[...] = cnt0_ref[...]

    @pl.when(t == 0)
    def _():
        s_scr[...] = state0_ref[0]
        ext_scr[0:HIST_ROWS, :] = hist0_ref[0]
        mk_scr[...] = mk_ref[0].astype(BF16)
        mv_scr[...] = mv_ref[0].astype(BF16)

    x = x_ref[0]
    proj = _dot(_rms(x, nmix_ref[...]).astype(BF16), win_ref[...])
    cos = cos_ref[...]
    sin = sin_ref[...]

    outs = []
    for hd in range(RET_HEADS):
        lo = hd * RET_DK
        q = proj[:, lo:lo + RET_DK]
        k = proj[:, RET_QK + lo:RET_QK + lo + RET_DK]
        v = proj[:, 2 * RET_QK + lo:2 * RET_QK + lo + RET_DV]
        g = proj[:, 2 * RET_QK + RET_VW + lo:2 * RET_QK + RET_VW + lo + RET_DV]
        qr = (q * cos + pltpu.roll(q, RET_DK // 2, 1) * sin) * (RET_DK ** -0.5)
        kr = k * cos + pltpu.roll(k, RET_DK // 2, 1) * sin
        vb = v.astype(BF16)
        s = _dot_nt(qr.astype(BF16), kr.astype(BF16)) * dmat_ref[hd]
        o = _dot(s.astype(BF16), vb)
        state = s_scr[hd]
        o = o + _dot((qr * qdec_ref[hd]).astype(BF16), state.astype(BF16))
        kd_t = jnp.transpose(kr * kdec_ref[hd]).astype(BF16)
        s_scr[hd] = cdec[hd] * state + _dot(kd_t, vb)
        mu = jnp.mean(o, axis=-1, keepdims=True)
        oc = o - mu
        var = jnp.mean(oc * oc, axis=-1, keepdims=True)
        on = oc * lax.rsqrt(var + EPS)
        outs.append(on * gnw_ref[:, lo:lo + RET_DV] * (g * jax.nn.sigmoid(g)))
    rstate_ref[0] = s_scr[...]

    pin = proj[:, 2 * RET_QK + 2 * RET_VW:]
    ext_scr[HIST_ROWS:HIST_ROWS + tl, :] = pin
    pstate_ref[0] = pin[tl - POOL_HIST:, :]
    pos = (pos0 + t * tl + lax.broadcasted_iota(jnp.int32, (tl, POOL_C), 0)).astype(F32)
    for gi, w in enumerate(POOL_WINDOWS):
        lo = gi * POOL_C
        wsum = ext_scr[HIST_ROWS:HIST_ROWS + tl, lo:lo + POOL_C]
        for j in range(1, w):
            wsum = wsum + ext_scr[HIST_ROWS - j:HIST_ROWS - j + tl, lo:lo + POOL_C]
        cnt = jnp.minimum(float(w), pos + 1.0)
        d = wsum / cnt - pin[:, lo:lo + POOL_C]
        y = _dot(d.astype(BF16), wpool_ref[gi])
        outs.append(y * pscale_ref[:, lo:lo + POOL_C])
    ext_scr[0:HIST_ROWS, :] = ext_scr[tl:tl + HIST_ROWS, :]

    mix = jnp.concatenate(outs, axis=-1).astype(BF16)
    x1 = x + _dot(mix, wout_ref[...])

    qm = _dot(_rms(x1, nmem_ref[...]).astype(BF16), wq_ref[...])
    aouts = []
    for hd in range(MEM_HEADS):
        lo = hd * MEM_HD
        s = _dot_nt(qm[:, lo:lo + MEM_HD].astype(BF16), mk_scr[:, lo:lo + MEM_HD]) * (MEM_HD ** -0.5)
        e = jnp.exp(s - jnp.max(s, axis=-1, keepdims=True))
        p = e / jnp.sum(e, axis=-1, keepdims=True)
        aouts.append(_dot(p.astype(BF16), mv_scr[:, lo:lo + MEM_HD]))
    att = jnp.concatenate(aouts, axis=-1).astype(BF16)
    x2 = x1 + _dot(att, wo_ref[...])
    x2_ref[0] = x2

    hn = _rms(x2, nffn_ref[...])
    h_ref[0] = _pack_bf16(hn)
    logits = _dot(hn.astype(BF16), rw_ref[...]) + rb_ref[...]
    eiota = lax.broadcasted_iota(jnp.int32, (tl, N_EXPERTS), 1)
    neg = jnp.finfo(F32).min
    vals, idxs = [], []
    for _k in range(TOP_K):
        m = jnp.max(logits, axis=-1, keepdims=True)
        idx = jnp.min(jnp.where(logits == m, eiota, N_EXPERTS), axis=-1, keepdims=True)
        vals.append(m)
        idxs.append(idx)
        logits = jnp.where(eiota == idx, neg, logits)
    exps = [jnp.exp(vk - vals[0]) for vk in vals]
    den = exps[0] + exps[1] + exps[2] + exps[3]

    onehots = [(eiota == idx).astype(F32) for idx in idxs]
    picked = onehots[0] + onehots[1] + onehots[2] + onehots[3]
    before = cnt_scr[...] + _dot(ltri_ref[...], picked.astype(BF16))
    ranks = [jnp.sum(oh * before, axis=-1, keepdims=True) for oh in onehots]
    cnt_scr[...] = cnt_scr[...] + jnp.sum(picked, axis=0, keepdims=True)
    cnt_ref[...] = cnt_scr[...]

    lane = lax.broadcasted_iota(jnp.int32, (tl, LANES), 1)
    route = jnp.zeros((tl, LANES), F32)
    for kk in range(TOP_K):
        route = jnp.where(lane == kk, exps[kk] / den, route)
        route = jnp.where(lane == TOP_K + kk, idxs[kk].astype(F32), route)
        route = jnp.where(lane == 2 * TOP_K + kk, ranks[kk], route)
    route_ref[0] = route


def _decay_tables(tl):
    hh = np.arange(RET_HEADS, dtype=np.float64)
    log_g = np.log1p(-np.exp2(-5.0 - hh))
    idx = np.arange(tl, dtype=np.float64)
    dist = np.abs(idx[:, None] - idx[None, :])
    visible = (idx[None, :] // CHUNK) <= (idx[:, None] // CHUNK)
    dmat = np.where(visible[None], np.exp(log_g[:, None, None] * dist[None]), 0.0)
    qdec = np.exp(log_g[:, None] * (idx[None, :] + 1.0))
    kdec = np.exp(log_g[:, None] * (tl - 1.0 - idx[None, :]))
    cdec = tuple(float(c) for c in np.exp(log_g * tl).astype(np.float32))
    bcast = lambda a: np.ascontiguousarray(np.broadcast_to(a[:, :, None], (RET_HEADS, tl, RET_DK)))
    return (jnp.asarray(dmat, F32), jnp.asarray(bcast(qdec), F32), jnp.asarray(bcast(kdec), F32), cdec)


def _rotary_tables(pos0, length):
    half = RET_DK // 2
    inv_freq = jnp.power(ROPE_BASE, -jnp.arange(half, dtype=F32) / half)
    ang = (pos0 + jnp.arange(length, dtype=jnp.int32)).astype(F32)[:, None] * inv_freq[None, :]
    cos, sin = jnp.cos(ang), jnp.sin(ang)
    return jnp.concatenate([cos, cos], axis=-1), jnp.concatenate([-sin, sin], axis=-1)


def _layer(x, pos0, tl, state0, hist0, mk, mv, wts, cnt0):
    b, length, _ = x.shape
    nt = length // tl
    cos, sin = _rotary_tables(pos0, length)
    dmat, qdec, kdec, cdec = _decay_tables(tl)
    ltri = jnp.asarray(np.tril(np.ones((tl, tl), np.float32), -1), BF16)
    kern = functools.partial(_layer_kernel, tl=tl, pos0=pos0, cdec=cdec)
    tok = lambda width: pl.BlockSpec((1, tl, width), lambda i, j: (i, j, 0))
    per_stream = lambda *shape: pl.BlockSpec((1,) + shape, lambda i, j: (i,) + (0,) * len(shape))
    in_specs = [
        tok(D_MODEL),
        pl.BlockSpec((tl, RET_DK), lambda i, j: (j, 0)),
        pl.BlockSpec((tl, RET_DK), lambda i, j: (j, 0)),
        _const_spec((RET_HEADS, tl, tl)),
        _const_spec((RET_HEADS, tl, RET_DK)),
        _const_spec((RET_HEADS, tl, RET_DK)),
        per_stream(RET_HEADS, RET_DK, RET_DV),
        per_stream(HIST_ROWS, POOL_WIDTH),
        per_stream(N_MEM, D_MODEL),
        per_stream(N_MEM, D_MODEL),
    ] + [_const_spec(w.shape) for w in wts] + [_const_spec((tl, tl)), _const_spec((1, N_EXPERTS))]
    out_specs = [
        tok(D_MODEL), tok(D_MODEL // 2), tok(LANES),
        per_stream(RET_HEADS, RET_DK, RET_DV),
        per_stream(POOL_HIST, POOL_WIDTH),
        pl.BlockSpec((1, N_EXPERTS), lambda i, j: (0, 0)),
    ]
    out_shape = [
        jax.ShapeDtypeStruct((b, length, D_MODEL), F32),
        jax.ShapeDtypeStruct((b, length, D_MODEL // 2), jnp.int32),
        jax.ShapeDtypeStruct((b, length, LANES), F32),
        jax.ShapeDtypeStruct((b, RET_HEADS, RET_DK, RET_DV), F32),
        jax.ShapeDtypeStruct((b, POOL_HIST, POOL_WIDTH), F32),
        jax.ShapeDtypeStruct((1, N_EXPERTS), F32),
    ]
    scratch = [
        pltpu.VMEM((RET_HEADS, RET_DK, RET_DV), F32),
        pltpu.VMEM((HIST_ROWS + tl, POOL_WIDTH), F32),
        pltpu.VMEM((N_MEM, D_MODEL), BF16),
        pltpu.VMEM((N_MEM, D_MODEL), BF16),
        pltpu.VMEM((1, N_EXPERTS), F32),
    ]
    return pl.pallas_call(
        kern,
        grid=(b, nt),
        in_specs=in_specs,
        out_specs=out_specs,
        out_shape=out_shape,
        scratch_shapes=scratch,
        compiler_params=pltpu.CompilerParams(
            dimension_semantics=("arbitrary", "arbitrary"), vmem_limit_bytes=VMEM_LIMIT),
        name="layer_tl%d" % tl,
    )(x, cos, sin, dmat, qdec, kdec, state0, hist0, mk, mv, *wts, ltri, cnt0)


PREP_ROWS = 512
SPLIT_COLS = 2 * LANES


def _split_kernel(w_ref, perm_ref, o_ref):
    perm = perm_ref[...]
    for c in range(2 * D_FF // SPLIT_COLS):
        wc = w_ref[0, :, c * SPLIT_COLS:(c + 1) * SPLIT_COLS].astype(BF16)
        pc = _dot(wc, perm).astype(BF16)
        o_ref[0, :, c * LANES:(c + 1) * LANES] = pc[:, :LANES]
        o_ref[0, :, D_FF + c * LANES:D_FF + (c + 1) * LANES] = pc[:, LANES:]


def _split_gate_up(w_gate_up):
    perm = np.zeros((SPLIT_COLS, SPLIT_COLS), np.float32)
    j = np.arange(LANES)
    perm[2 * j, j] = 1.0
    perm[2 * j + 1, LANES + j] = 1.0
    blk = pl.BlockSpec((1, PREP_ROWS, 2 * D_FF), lambda e, r: (e, r, 0))
    return pl.pallas_call(
        _split_kernel,
        grid=(N_EXPERTS, D_MODEL // PREP_ROWS),
        in_specs=[blk, _const_spec((SPLIT_COLS, SPLIT_COLS))],
        out_specs=blk,
        out_shape=jax.ShapeDtypeStruct(w_gate_up.shape, BF16),
        compiler_params=pltpu.CompilerParams(
            dimension_semantics=("arbitrary", "arbitrary"), vmem_limit_bytes=VMEM_LIMIT),
        name="split_gate_up",
    )(w_gate_up, jnp.asarray(perm, BF16))


def _expert_kernel(be_ref, used_ref, x_ref, wgu_ref, bgu_ref, wd_ref, bd_ref, y_ref):
    del be_ref

    @pl.when(pl.program_id(0) < used_ref[0])
    def _():
        half = D_MODEL // 2
        x_lo, x_hi = _unpack_bf16(x_ref[...])
        gu = (_dot(x_lo.astype(BF16), wgu_ref[0, :half, :]) + _dot(x_hi.astype(BF16), wgu_ref[0, half:, :])
              + bgu_ref[0])
        gate = jnp.minimum(gu[:, :D_FF], SWIGLU_LIMIT)
        up = jnp.clip(gu[:, D_FF:], -SWIGLU_LIMIT, SWIGLU_LIMIT)
        act = (up + 1.0) * gate * jax.nn.sigmoid(SWIGLU_ALPHA * gate)
        y_ref[...] = _pack_bf16(_dot(act.astype(BF16), wd_ref[0]) + bd_ref[0])


def _expert_ffn(block_e, n_used, xs, wgu, bgu, wd, bd, n_blocks):
    blk = lambda i, be, used: (jnp.minimum(i, used[0] - 1), 0)
    per_expert = lambda i, be, used: (be[i], 0, 0)
    grid_spec = pltpu.PrefetchScalarGridSpec(
        num_scalar_prefetch=2,
        grid=(n_blocks,),
        in_specs=[
            pl.BlockSpec((EXPERT_ROWS, D_MODEL // 2), blk),
            pl.BlockSpec((1, D_MODEL, 2 * D_FF), per_expert),
            pl.BlockSpec((1, 1, 2 * D_FF), per_expert),
            pl.BlockSpec((1, D_FF, D_MODEL), per_expert),
            pl.BlockSpec((1, 1, D_MODEL), per_expert),
        ],
        out_specs=pl.BlockSpec((EXPERT_ROWS, D_MODEL // 2), blk),
    )
    return pl.pallas_call(
        _expert_kernel,
        grid_spec=grid_spec,
        out_shape=jax.ShapeDtypeStruct(xs.shape, jnp.int32),
        compiler_params=pltpu.CompilerParams(
            dimension_semantics=("arbitrary",), vmem_limit_bytes=VMEM_LIMIT),
        name="expert_ffn",
    )(block_e, n_used, xs, wgu, bgu, wd, bd)


def _expert_layout(counts, n_blocks):
    pcounts = (counts + EXPERT_ROWS - 1) // EXPERT_ROWS * EXPERT_ROWS
    pend = jnp.cumsum(pcounts)
    n_used = pend[-1:] // EXPERT_ROWS
    block_start = jnp.minimum(jnp.arange(n_blocks, dtype=jnp.int32), n_used[0] - 1) * EXPERT_ROWS
    block_e = jnp.sum((pend[None, :] <= block_start[:, None]).astype(jnp.int32), axis=1)
    return pend - pcounts, jnp.minimum(block_e, N_EXPERTS - 1), n_used


def _slot_rows(route, pstart):
    e = route[:, TOP_K:2 * TOP_K].astype(jnp.int32)
    rank = route[:, 2 * TOP_K:3 * TOP_K].astype(jnp.int32)
    hit = e[:, :, None] == jnp.arange(N_EXPERTS, dtype=jnp.int32)[None, None, :]
    return (rank + jnp.sum(jnp.where(hit, pstart[None, None, :], 0), axis=-1)).T


SC_WINDOW = 128
SC_COLS = 256


def _sc_mesh():
    return plsc.VectorSubcoreMesh(core_axis_name="c", subcore_axis_name="s")


def _dispatch(h_p, h_s, rows_p, rows_s, m_pad):
    width = h_p.shape[1]

    @functools.partial(pl.kernel, mesh=_sc_mesh(),
                       out_type=jax.ShapeDtypeStruct((m_pad, width), h_p.dtype), scratch_types=[])
    def k(hp_hbm, hs_hbm, rp_hbm, rs_hbm, xs_hbm):
        def body(x_vmem, i_vmem):
            j = pl.program_id(1)
            for kk in range(TOP_K):
                pltpu.sync_copy(x_vmem, xs_hbm.at[i_vmem.at[kk], pl.ds(j * SC_COLS, SC_COLS)])

        for src, rows in ((hp_hbm, rp_hbm), (hs_hbm, rs_hbm)):
            pltpu.emit_pipeline(
                body,
                grid=(src.shape[0] // SC_WINDOW, width // SC_COLS),
                in_specs=[pl.BlockSpec((SC_WINDOW, SC_COLS), lambda i, j: (i, j)),
                          pl.BlockSpec((TOP_K, SC_WINDOW), lambda i, j: (0, i))],
                out_specs=[],
                core_axis_name=("c", "s"),
                dimension_semantics=(pltpu.PARALLEL, pltpu.ARBITRARY),
            )(src, rows)

    return k(h_p, h_s, rows_p, rows_s)


def _collect(yb, rows_p, rows_s):
    width = yb.shape[1]
    out_type = [jax.ShapeDtypeStruct((rows_p.shape[1], width), yb.dtype),
                jax.ShapeDtypeStruct((rows_s.shape[1], width), yb.dtype)]

    @functools.partial(pl.kernel, mesh=_sc_mesh(), out_type=out_type, scratch_types=[])
    def k(yb_hbm, rp_hbm, rs_hbm, op_hbm, os_hbm):
        def body(i_vmem, o_vmem):
            j = pl.program_id(1)
            pltpu.sync_copy(yb_hbm.at[i_vmem.at[0], pl.ds(j * SC_COLS, SC_COLS)], o_vmem)

        for rows, out in ((rp_hbm, op_hbm), (rs_hbm, os_hbm)):
            pltpu.emit_pipeline(
                body,
                grid=(rows.shape[1] // SC_WINDOW, width // SC_COLS),
                in_specs=[pl.BlockSpec((1, SC_WINDOW), lambda i, j: (0, i))],
                out_specs=[pl.BlockSpec((SC_WINDOW, SC_COLS), lambda i, j: (i, j))],
                core_axis_name=("c", "s"),
                dimension_semantics=(pltpu.PARALLEL, pltpu.ARBITRARY),
            )(rows, out)

    return k(yb, rows_p, rows_s)


def _combine_kernel(x_ref, yg_ref, route_ref, fw_ref, y_ref):
    half = D_MODEL // 2
    acc_lo = x_ref[:, :half]
    acc_hi = x_ref[:, half:]
    route = route_ref[...]
    for kk in range(TOP_K):
        y_lo, y_hi = _unpack_bf16(yg_ref[kk])
        gate = route[:, kk:kk + 1]
        acc_lo = acc_lo + y_lo * gate
        acc_hi = acc_hi + y_hi * gate
    ms = (jnp.sum(acc_lo * acc_lo, axis=-1, keepdims=True)
          + jnp.sum(acc_hi * acc_hi, axis=-1, keepdims=True)) * (1.0 / D_MODEL)
    scale = lax.rsqrt(ms + EPS)
    y_ref[:, :half] = acc_lo * scale * fw_ref[:, :half]
    y_ref[:, half:] = acc_hi * scale * fw_ref[:, half:]


def _combine(x2, yg, route, final_w):
    n_tok = x2.shape[0]
    return pl.pallas_call(
        _combine_kernel,
        grid=(n_tok // COMBINE_ROWS,),
        in_specs=[
            pl.BlockSpec((COMBINE_ROWS, D_MODEL), lambda i: (i, 0)),
            pl.BlockSpec((TOP_K, COMBINE_ROWS, D_MODEL // 2), lambda i: (0, i, 0)),
            pl.BlockSpec((COMBINE_ROWS, LANES), lambda i: (i, 0)),
            _const_spec((1, D_MODEL)),
        ],
        out_specs=pl.BlockSpec((COMBINE_ROWS, D_MODEL), lambda i: (i, 0)),
        out_shape=jax.ShapeDtypeStruct((n_tok, D_MODEL), F32),
        compiler_params=pltpu.CompilerParams(
            dimension_semantics=("arbitrary",), vmem_limit_bytes=VMEM_LIMIT),
        name="combine_%d" % n_tok,
    )(x2, yg, route, final_w)


def kernel(x_prompt, x_sample, cache_mem_k, cache_mem_v, state_ret, state_pool, mem_prompt,
           norm_mix_w, w_in, ret_gn_w, w_pool, pool_scale, w_out, norm_mem_w, mem_norm_w,
           w_q_mem, w_kv_mem, w_o_mem, norm_ffn_w, router_w, router_b, w_gate_up, b_gate_up,
           w_down, b_down, final_norm_w):
    assert norm_mix_w.shape[0] == 1, "one layer"
    b, seq, _ = x_prompt.shape
    db, dseq, _ = x_sample.shape
    row = lambda a: a.reshape(1, -1)

    mk_p, mv_p = _mem_kv(mem_prompt.reshape(b * N_MEM, D_MODEL), row(mem_norm_w[0]), w_kv_mem[0].astype(BF16))
    mk_p = mk_p.reshape(b, N_MEM, D_MODEL)
    mv_p = mv_p.reshape(b, N_MEM, D_MODEL)

    wts = (row(norm_mix_w[0]), w_in[0].astype(BF16), row(ret_gn_w[0]), w_pool[0].astype(BF16),
           row(pool_scale[0]), w_out[0].astype(BF16), row(norm_mem_w[0]), w_q_mem[0].astype(BF16),
           w_o_mem[0].astype(BF16), row(norm_ffn_w[0]), router_w[0].astype(BF16), row(router_b[0]))

    zero_state = jnp.zeros((b, RET_HEADS, RET_DK, RET_DV), F32)
    zero_hist = jnp.zeros((b, HIST_ROWS, POOL_WIDTH), F32)
    no_counts = jnp.zeros((1, N_EXPERTS), F32)
    x2_p, h_p, route_p, ret_p, pool_p, cnt_p = _layer(
        x_prompt, 0, PROMPT_TILE, zero_state, zero_hist, mk_p, mv_p, wts, no_counts)

    hist_s = jnp.concatenate([jnp.zeros((db, 1, POOL_WIDTH), F32), state_pool[0]], axis=1)
    x2_s, h_s, route_s, ret_s, pool_s, cnt_all = _layer(
        x_sample, PAST_LEN, dseq, state_ret[0], hist_s,
        cache_mem_k[0].reshape(db, N_MEM, D_MODEL), cache_mem_v[0].reshape(db, N_MEM, D_MODEL), wts, cnt_p)

    n_p, n_s = b * seq, db * dseq
    route_p = route_p.reshape(n_p, LANES)
    route_s = route_s.reshape(n_s, LANES)
    n_blocks = -(-((n_p + n_s) * TOP_K + N_EXPERTS * (EXPERT_ROWS - 1)) // EXPERT_ROWS)
    pstart, block_e, n_used = _expert_layout(cnt_all[0].astype(jnp.int32), n_blocks)
    rows_p = _slot_rows(route_p, pstart)
    rows_s = _slot_rows(route_s, pstart)

    half = D_MODEL // 2
    xs = _dispatch(h_p.reshape(n_p, half), h_s.reshape(n_s, half), rows_p, rows_s, n_blocks * EXPERT_ROWS)

    wgu = _split_gate_up(w_gate_up[0])
    bgu = jnp.concatenate([b_gate_up[0][:, 0::2], b_gate_up[0][:, 1::2]], axis=-1).reshape(N_EXPERTS, 1, 2 * D_FF)
    yb = _expert_ffn(block_e, n_used, xs, wgu, bgu, w_down[0].astype(BF16),
                     b_down[0].reshape(N_EXPERTS, 1, D_MODEL), n_blocks)

    final_w = row(final_norm_w)
    yg_p, yg_s = _collect(yb, rows_p.reshape(1, TOP_K * n_p), rows_s.reshape(1, TOP_K * n_s))
    y_p = _combine(x2_p.reshape(n_p, D_MODEL), yg_p.reshape(TOP_K, n_p, half), route_p, final_w)
    y_s = _combine(x2_s.reshape(n_s, D_MODEL), yg_s.reshape(TOP_K, n_s, half), route_s, final_w)
    y_p = y_p.reshape(b, seq, D_MODEL)
    y_s = y_s.reshape(db, dseq, D_MODEL)
    shape_kv = (1, b, N_MEM, MEM_HEADS, MEM_HD)
    return (y_p, y_s, mk_p.reshape(shape_kv), mv_p.reshape(shape_kv), ret_p[None], pool_p[None],
            ret_s[None], pool_s[None])
```

```python
import functools

import numpy as np
import jax
import jax.numpy as jnp
from jax import lax
from jax.experimental import pallas as pl
from jax.experimental.pallas import tpu as pltpu
from jax.experimental.pallas import tpu_sc as plsc

D_MODEL = 1024
CHUNK = 64
PAST_LEN = 4096
RET_HEADS = 4
RET_DK = 128
RET_DV = 128
RET_QK = RET_HEADS * RET_DK
RET_VW = RET_HEADS * RET_DV
ROPE_BASE = 10000.0
POOL_WINDOWS = (2, 4, 8, 16)
POOL_GROUPS = 4
POOL_WIDTH = D_MODEL // 2
POOL_C = POOL_WIDTH // POOL_GROUPS
POOL_HIST = max(POOL_WINDOWS) - 1
HIST_ROWS = POOL_HIST + 1
IN_WIDTH = 2 * RET_QK + 2 * RET_VW + POOL_WIDTH
N_MEM = 256
MEM_HEADS = 4
MEM_HD = D_MODEL // MEM_HEADS
N_EXPERTS = 32
TOP_K = 4
D_FF = D_MODEL
SWIGLU_LIMIT = 7.0
SWIGLU_ALPHA = 1.702
EPS = 1e-5

LANES = 128
ROUTE_ROWS = 16
PROMPT_TILE = 256
EXPERT_ROWS = 512
KV_ROWS = 512
COMBINE_ROWS = 512
VMEM_LIMIT = 56 * 1024 * 1024

BF16 = jnp.bfloat16
F32 = jnp.float32


def _rms(x, w):
    return x * lax.rsqrt(jnp.mean(x * x, axis=-1, keepdims=True) + EPS) * w


def _dot(a, b):
    return jnp.dot(a, b, preferred_element_type=F32)


def _dot_nt(a, b):
    return lax.dot_general(a, b, (((1,), (1,)), ((), ())), preferred_element_type=F32)


def _pack_bf16(x):
    bits = lax.bitcast_convert_type(x.astype(BF16).astype(F32), jnp.int32)
    w = x.shape[1] // 2
    return lax.shift_right_logical(bits[:, :w], 16) | (bits[:, w:] & -65536)


def _unpack_bf16(p):
    lo = lax.bitcast_convert_type(lax.shift_left(p, 16), F32)
    hi = lax.bitcast_convert_type(p & -65536, F32)
    return lo, hi


def _const_spec(shape):
    nd = len(shape)
    return pl.BlockSpec(shape, lambda *_: (0,) * nd, pipeline_mode=pl.Buffered(1))


def _mem_kv_kernel(mem_ref, nw_ref, w_ref, k_ref, v_ref):
    xn = _rms(mem_ref[...], nw_ref[...]).astype(BF16)
    kv = _dot(xn, w_ref[...])
    k_ref[...] = kv[:, :D_MODEL]
    v_ref[...] = kv[:, D_MODEL:]


def _mem_kv(mem2d, mem_norm_w, w_kv_bf):
    rows = mem2d.shape[0]
    return pl.pallas_call(
        _mem_kv_kernel,
        grid=(rows // KV_ROWS,),
        in_specs=[
            pl.BlockSpec((KV_ROWS, D_MODEL), lambda i: (i, 0)),
            _const_spec((1, D_MODEL)),
            _const_spec((D_MODEL, 2 * D_MODEL)),
        ],
        out_specs=[
            pl.BlockSpec((KV_ROWS, D_MODEL), lambda i: (i, 0)),
            pl.BlockSpec((KV_ROWS, D_MODEL), lambda i: (i, 0)),
        ],
        out_shape=[jax.ShapeDtypeStruct((rows, D_MODEL), F32)] * 2,
        compiler_params=pltpu.CompilerParams(
            dimension_semantics=("arbitrary",), vmem_limit_bytes=VMEM_LIMIT),
        name="mem_kv",
    )(mem2d, mem_norm_w, w_kv_bf)


def _layer_kernel(x_ref, cos_ref, sin_ref, dmat_ref, qdec_ref, kdec_ref, state0_ref, hist0_ref,
                  mk_ref, mv_ref, nmix_ref, win_ref, gnw_ref, wpool_ref, pscale_ref, wout_ref,
                  nmem_ref, wq_ref, wo_ref, nffn_ref, rw_ref, rb_ref, earlier_ref, cnt0_ref,
                  x2_ref, h_ref, route_ref, rstate_ref, pstate_ref, cnt_ref,
                  s_scr, ext_scr, mk_scr, mv_scr, cnt_scr, *, tl, pos0, cdec):
    t = pl.program_id(1)

    @pl.when((pl.program_id(0) == 0) & (t == 0))
    def _():
        cnt_scr[...] = cnt0_ref[...]

    @pl.when(t == 0)
    def _():
        s_scr[...] = state0_ref[0]
        ext_scr[0:HIST_ROWS, :] = hist0_ref[0]
        mk_scr[...] = mk_ref[0].astype(BF16)
        mv_scr[...] = mv_ref[0].astype(BF16)

    x = x_ref[0]
    proj = _dot(_rms(x, nmix_ref[...]).astype(BF16), win_ref[...])
    cos = cos_ref[...]
    sin = sin_ref[...]

    outs = []
    for hd in range(RET_HEADS):
        lo = hd * RET_DK
        q = proj[:, lo:lo + RET_DK]
        k = proj[:, RET_QK + lo:RET_QK + lo + RET_DK]
        v = proj[:, 2 * RET_QK + lo:2 * RET_QK + lo + RET_DV]
        g = proj[:, 2 * RET_QK + RET_VW + lo:2 * RET_QK + RET_VW + lo + RET_DV]
        qr = (q * cos + pltpu.roll(q, RET_DK // 2, 1) * sin) * (RET_DK ** -0.5)
        kr = k * cos + pltpu.roll(k, RET_DK // 2, 1) * sin
        vb = v.astype(BF16)
        s = _dot_nt(qr.astype(BF16), kr.astype(BF16)) * dmat_ref[hd]
        o = _dot(s.astype(BF16), vb)
        state = s_scr[hd]
        o = o + _dot((qr * qdec_ref[hd]).astype(BF16), state.astype(BF16))
        kd_t = jnp.transpose(kr * kdec_ref[hd]).astype(BF16)
        s_scr[hd] = cdec[hd] * state + _dot(kd_t, vb)
        mu = jnp.mean(o, axis=-1, keepdims=True)
        oc = o - mu
        var = jnp.mean(oc * oc, axis=-1, keepdims=True)
        on = oc * lax.rsqrt(var + EPS)
        outs.append(on * gnw_ref[:, lo:lo + RET_DV] * (g * jax.nn.sigmoid(g)))
    rstate_ref[0] = s_scr[...]

    pin = proj[:, 2 * RET_QK + 2 * RET_VW:]
    ext_scr[HIST_ROWS:HIST_ROWS + tl, :] = pin
    pstate_ref[0] = pin[tl - POOL_HIST:, :]
    pos = (pos0 + t * tl + lax.broadcasted_iota(jnp.int32, (tl, POOL_C), 0)).astype(F32)
    for gi, w in enumerate(POOL_WINDOWS):
        lo = gi * POOL_C
        wsum = ext_scr[:, lo:lo + POOL_C]
        shift = 1
        while shift < w:
            wsum = wsum + pltpu.roll(wsum, shift, 0)
            shift *= 2
        cnt = jnp.minimum(float(w), pos + 1.0)
        d = wsum[HIST_ROWS:, :] / cnt - pin[:, lo:lo + POOL_C]
        y = _dot(d.astype(BF16), wpool_ref[gi])
        outs.append(y * pscale_ref[:, lo:lo + POOL_C])
    ext_scr[0:HIST_ROWS, :] = ext_scr[tl:tl + HIST_ROWS, :]

    mix = jnp.concatenate(outs, axis=-1).astype(BF16)
    x1 = x + _dot(mix, wout_ref[...])

    qm = _dot(_rms(x1, nmem_ref[...]).astype(BF16), wq_ref[...])
    aouts = []
    for hd in range(MEM_HEADS):
        lo = hd * MEM_HD
        s = _dot_nt(qm[:, lo:lo + MEM_HD].astype(BF16), mk_scr[:, lo:lo + MEM_HD]) * (MEM_HD ** -0.5)
        e = jnp.exp(s - jnp.max(s, axis=-1, keepdims=True))
        p = e / jnp.sum(e, axis=-1, keepdims=True)
        aouts.append(_dot(p.astype(BF16), mv_scr[:, lo:lo + MEM_HD]))
    att = jnp.concatenate(aouts, axis=-1).astype(BF16)
    x2 = x1 + _dot(att, wo_ref[...])
    x2_ref[0] = x2

    hn = _rms(x2, nffn_ref[...])
    h_ref[0] = _pack_bf16(hn)
    lane_tile = lambda a: a[:, :tl] if tl <= LANES else jnp.concatenate([a] * (tl // LANES), axis=1)
    logits = _dot_nt(rw_ref[...], hn.astype(BF16)) + lane_tile(rb_ref[...])
    eiota = lax.broadcasted_iota(jnp.int32, (N_EXPERTS, tl), 0)
    neg = jnp.finfo(F32).min
    vals, idxs = [], []
    for _k in range(TOP_K):
        m = jnp.max(logits, axis=0, keepdims=True)
        idx = jnp.min(jnp.where(logits == m, eiota, N_EXPERTS), axis=0, keepdims=True)
        vals.append(m)
        idxs.append(idx)
        logits = jnp.where(eiota == idx, neg, logits)
    exps = [jnp.exp(vk - vals[0]) for vk in vals]
    den = exps[0] + exps[1] + exps[2] + exps[3]

    onehots = [(eiota == idx).astype(F32) for idx in idxs]
    picked = onehots[0] + onehots[1] + onehots[2] + onehots[3]
    before = lane_tile(cnt_scr[...]) + _dot(picked.astype(BF16), earlier_ref[...])
    ranks = [jnp.sum(oh * before, axis=0, keepdims=True) for oh in onehots]
    cnt_scr[...] = cnt_scr[...] + jnp.sum(picked, axis=1, keepdims=True)
    cnt_ref[...] = cnt_scr[...]

    rowi = lax.broadcasted_iota(jnp.int32, (ROUTE_ROWS, tl), 0)
    route = jnp.zeros((ROUTE_ROWS, tl), F32)
    for kk in range(TOP_K):
        route = jnp.where(rowi == kk, exps[kk] / den, route)
        route = jnp.where(rowi == TOP_K + kk, idxs[kk].astype(F32), route)
        route = jnp.where(rowi == 2 * TOP_K + kk, ranks[kk], route)
    route_ref[0] = route


def _decay_tables(tl):
    hh = np.arange(RET_HEADS, dtype=np.float64)
    log_g = np.log1p(-np.exp2(-5.0 - hh))
    idx = np.arange(tl, dtype=np.float64)
    dist = np.abs(idx[:, None] - idx[None, :])
    visible = (idx[None, :] // CHUNK) <= (idx[:, None] // CHUNK)
    dmat = np.where(visible[None], np.exp(log_g[:, None, None] * dist[None]), 0.0)
    qdec = np.exp(log_g[:, None] * (idx[None, :] + 1.0))
    kdec = np.exp(log_g[:, None] * (tl - 1.0 - idx[None, :]))
    cdec = tuple(float(c) for c in np.exp(log_g * tl).astype(np.float32))
    bcast = lambda a: np.ascontiguousarray(np.broadcast_to(a[:, :, None], (RET_HEADS, tl, RET_DK)))
    return (jnp.asarray(dmat, F32), jnp.asarray(bcast(qdec), F32), jnp.asarray(bcast(kdec), F32), cdec)


def _rotary_tables(pos0, length):
    half = RET_DK // 2
    inv_freq = jnp.power(ROPE_BASE, -jnp.arange(half, dtype=F32) / half)
    ang = (pos0 + jnp.arange(length, dtype=jnp.int32)).astype(F32)[:, None] * inv_freq[None, :]
    cos, sin = jnp.cos(ang), jnp.sin(ang)
    return jnp.concatenate([cos, cos], axis=-1), jnp.concatenate([-sin, sin], axis=-1)


def _layer(x, pos0, tl, state0, hist0, mk, mv, wts, cnt0):
    b, length, _ = x.shape
    nt = length // tl
    cos, sin = _rotary_tables(pos0, length)
    dmat, qdec, kdec, cdec = _decay_tables(tl)
    earlier = jnp.asarray(np.triu(np.ones((tl, tl), np.float32), 1), BF16)
    kern = functools.partial(_layer_kernel, tl=tl, pos0=pos0, cdec=cdec)
    tok = lambda width: pl.BlockSpec((1, tl, width), lambda i, j: (i, j, 0))
    per_stream = lambda *shape: pl.BlockSpec((1,) + shape, lambda i, j: (i,) + (0,) * len(shape))
    in_specs = [
        tok(D_MODEL),
        pl.BlockSpec((tl, RET_DK), lambda i, j: (j, 0)),
        pl.BlockSpec((tl, RET_DK), lambda i, j: (j, 0)),
        _const_spec((RET_HEADS, tl, tl)),
        _const_spec((RET_HEADS, tl, RET_DK)),
        _const_spec((RET_HEADS, tl, RET_DK)),
        per_stream(RET_HEADS, RET_DK, RET_DV),
        per_stream(HIST_ROWS, POOL_WIDTH),
        per_stream(N_MEM, D_MODEL),
        per_stream(N_MEM, D_MODEL),
    ] + [_const_spec(w.shape) for w in wts] + [_const_spec((tl, tl)), _const_spec((N_EXPERTS, LANES))]
    out_specs = [
        tok(D_MODEL), tok(D_MODEL // 2),
        pl.BlockSpec((1, ROUTE_ROWS, tl), lambda i, j: (i * nt + j, 0, 0)),
        per_stream(RET_HEADS, RET_DK, RET_DV),
        per_stream(POOL_HIST, POOL_WIDTH),
        pl.BlockSpec((N_EXPERTS, LANES), lambda i, j: (0, 0)),
    ]
    out_shape = [
        jax.ShapeDtypeStruct((b, length, D_MODEL), F32),
        jax.ShapeDtypeStruct((b, length, D_MODEL // 2), jnp.int32),
        jax.ShapeDtypeStruct((b * nt, ROUTE_ROWS, tl), F32),
        jax.ShapeDtypeStruct((b, RET_HEADS, RET_DK, RET_DV), F32),
        jax.ShapeDtypeStruct((b, POOL_HIST, POOL_WIDTH), F32),
        jax.ShapeDtypeStruct((N_EXPERTS, LANES), F32),
    ]
    scratch = [
        pltpu.VMEM((RET_HEADS, RET_DK, RET_DV), F32),
        pltpu.VMEM((HIST_ROWS + tl, POOL_WIDTH), F32),
        pltpu.VMEM((N_MEM, D_MODEL), BF16),
        pltpu.VMEM((N_MEM, D_MODEL), BF16),
        pltpu.VMEM((N_EXPERTS, LANES), F32),
    ]
    return pl.pallas_call(
        kern,
        grid=(b, nt),
        in_specs=in_specs,
        out_specs=out_specs,
        out_shape=out_shape,
        scratch_shapes=scratch,
        compiler_params=pltpu.CompilerParams(
            dimension_semantics=("arbitrary", "arbitrary"), vmem_limit_bytes=VMEM_LIMIT),
        name="layer_tl%d" % tl,
    )(x, cos, sin, dmat, qdec, kdec, state0, hist0, mk, mv, *wts, earlier, cnt0)


PREP_ROWS = 512
SPLIT_COLS = 2 * LANES


def _split_kernel(w_ref, perm_ref, o_ref):
    perm = perm_ref[...]
    for c in range(2 * D_FF // SPLIT_COLS):
        wc = w_ref[0, :, c * SPLIT_COLS:(c + 1) * SPLIT_COLS].astype(BF16)
        pc = _dot(wc, perm).astype(BF16)
        o_ref[0, :, c * LANES:(c + 1) * LANES] = pc[:, :LANES]
        o_ref[0, :, D_FF + c * LANES:D_FF + (c + 1) * LANES] = pc[:, LANES:]


def _split_gate_up(w_gate_up):
    perm = np.zeros((SPLIT_COLS, SPLIT_COLS), np.float32)
    j = np.arange(LANES)
    perm[2 * j, j] = 1.0
    perm[2 * j + 1, LANES + j] = 1.0
    blk = pl.BlockSpec((1, PREP_ROWS, 2 * D_FF), lambda e, r: (e, r, 0))
    return pl.pallas_call(
        _split_kernel,
        grid=(N_EXPERTS, D_MODEL // PREP_ROWS),
        in_specs=[blk, _const_spec((SPLIT_COLS, SPLIT_COLS))],
        out_specs=blk,
        out_shape=jax.ShapeDtypeStruct(w_gate_up.shape, BF16),
        compiler_params=pltpu.CompilerParams(
            dimension_semantics=("arbitrary", "arbitrary"), vmem_limit_bytes=VMEM_LIMIT),
        name="split_gate_up",
    )(w_gate_up, jnp.asarray(perm, BF16))


def _expert_kernel(be_ref, used_ref, x_ref, wgu_ref, bgu_ref, wd_ref, bd_ref, y_ref):
    del be_ref

    @pl.when(pl.program_id(0) < used_ref[0])
    def _():
        half = D_MODEL // 2
        x_lo, x_hi = _unpack_bf16(x_ref[...])
        gu = (_dot(x_lo.astype(BF16), wgu_ref[0, :half, :]) + _dot(x_hi.astype(BF16), wgu_ref[0, half:, :])
              + bgu_ref[0])
        gate = jnp.minimum(gu[:, :D_FF], SWIGLU_LIMIT)
        up = jnp.clip(gu[:, D_FF:], -SWIGLU_LIMIT, SWIGLU_LIMIT)
        act = (up + 1.0) * gate * jax.nn.sigmoid(SWIGLU_ALPHA * gate)
        y_ref[...] = _pack_bf16(_dot(act.astype(BF16), wd_ref[0]) + bd_ref[0])


def _expert_ffn(block_e, n_used, xs, wgu, bgu, wd, bd, n_blocks):
    blk = lambda i, be, used: (jnp.minimum(i, used[0] - 1), 0)
    per_expert = lambda i, be, used: (be[i], 0, 0)
    grid_spec = pltpu.PrefetchScalarGridSpec(
        num_scalar_prefetch=2,
        grid=(n_blocks,),
        in_specs=[
            pl.BlockSpec((EXPERT_ROWS, D_MODEL // 2), blk),
            pl.BlockSpec((1, D_MODEL, 2 * D_FF), per_expert),
            pl.BlockSpec((1, 1, 2 * D_FF), per_expert),
            pl.BlockSpec((1, D_FF, D_MODEL), per_expert),
            pl.BlockSpec((1, 1, D_MODEL), per_expert),
        ],
        out_specs=pl.BlockSpec((EXPERT_ROWS, D_MODEL // 2), blk),
    )
    return pl.pallas_call(
        _expert_kernel,
        grid_spec=grid_spec,
        out_shape=jax.ShapeDtypeStruct(xs.shape, jnp.int32),
        compiler_params=pltpu.CompilerParams(
            dimension_semantics=("arbitrary",), vmem_limit_bytes=VMEM_LIMIT),
        name="expert_ffn",
    )(block_e, n_used, xs, wgu, bgu, wd, bd)


def _expert_layout(counts, n_blocks):
    pcounts = (counts + EXPERT_ROWS - 1) // EXPERT_ROWS * EXPERT_ROWS
    pend = jnp.cumsum(pcounts)
    n_used = pend[-1:] // EXPERT_ROWS
    block_start = jnp.minimum(jnp.arange(n_blocks, dtype=jnp.int32), n_used[0] - 1) * EXPERT_ROWS
    block_e = jnp.sum((pend[None, :] <= block_start[:, None]).astype(jnp.int32), axis=1)
    return pend - pcounts, jnp.minimum(block_e, N_EXPERTS - 1), n_used


def _slot_rows(route, pstart):
    n = route.shape[0] * route.shape[2]
    per_slot = lambda lo: jnp.transpose(route[:, lo:lo + TOP_K, :], (1, 0, 2)).reshape(TOP_K, n)
    e = per_slot(TOP_K).astype(jnp.int32)
    rank = per_slot(2 * TOP_K).astype(jnp.int32)
    hit = e[:, :, None] == jnp.arange(N_EXPERTS, dtype=jnp.int32)[None, None, :]
    rows = rank + jnp.sum(jnp.where(hit, pstart[None, None, :], 0), axis=-1)
    gates = jnp.transpose(route[:, :TOP_K, :], (0, 2, 1)).reshape(n, TOP_K)
    return rows, gates


SC_WINDOW = 128
SC_COLS = 256


def _sc_mesh():
    return plsc.VectorSubcoreMesh(core_axis_name="c", subcore_axis_name="s")


def _dispatch(h_p, h_s, rows_p, rows_s, m_pad):
    width = h_p.shape[1]

    @functools.partial(pl.kernel, mesh=_sc_mesh(),
                       out_type=jax.ShapeDtypeStruct((m_pad, width), h_p.dtype), scratch_types=[])
    def k(hp_hbm, hs_hbm, rp_hbm, rs_hbm, xs_hbm):
        def body(x_vmem, i_vmem):
            j = pl.program_id(1)
            for kk in range(TOP_K):
                pltpu.sync_copy(x_vmem, xs_hbm.at[i_vmem.at[kk], pl.ds(j * SC_COLS, SC_COLS)])

        for src, rows in ((hp_hbm, rp_hbm), (hs_hbm, rs_hbm)):
            pltpu.emit_pipeline(
                body,
                grid=(src.shape[0] // SC_WINDOW, width // SC_COLS),
                in_specs=[pl.BlockSpec((SC_WINDOW, SC_COLS), lambda i, j: (i, j)),
                          pl.BlockSpec((TOP_K, SC_WINDOW), lambda i, j: (0, i))],
                out_specs=[],
                core_axis_name=("c", "s"),
                dimension_semantics=(pltpu.PARALLEL, pltpu.ARBITRARY),
            )(src, rows)

    return k(h_p, h_s, rows_p, rows_s)


def _collect(yb, rows_p, rows_s):
    width = yb.shape[1]
    out_type = [jax.ShapeDtypeStruct((rows_p.shape[1], width), yb.dtype),
                jax.ShapeDtypeStruct((rows_s.shape[1], width), yb.dtype)]

    @functools.partial(pl.kernel, mesh=_sc_mesh(), out_type=out_type, scratch_types=[])
    def k(yb_hbm, rp_hbm, rs_hbm, op_hbm, os_hbm):
        def body(i_vmem, o_vmem):
            j = pl.program_id(1)
            pltpu.sync_copy(yb_hbm.at[i_vmem.at[0], pl.ds(j * SC_COLS, SC_COLS)], o_vmem)

        for rows, out in ((rp_hbm, op_hbm), (rs_hbm, os_hbm)):
            pltpu.emit_pipeline(
                body,
                grid=(rows.shape[1] // SC_WINDOW, width // SC_COLS),
                in_specs=[pl.BlockSpec((1, SC_WINDOW), lambda i, j: (0, i))],
                out_specs=[pl.BlockSpec((SC_WINDOW, SC_COLS), lambda i, j: (i, j))],
                core_axis_name=("c", "s"),
                dimension_semantics=(pltpu.PARALLEL, pltpu.ARBITRARY),
            )(rows, out)

    return k(yb, rows_p, rows_s)


def _combine_kernel(x_ref, yg_ref, route_ref, fw_ref, y_ref):
    half = D_MODEL // 2
    acc_lo = x_ref[:, :half]
    acc_hi = x_ref[:, half:]
    route = route_ref[...]
    for kk in range(TOP_K):
        y_lo, y_hi = _unpack_bf16(yg_ref[kk])
        gate = route[:, kk:kk + 1]
        acc_lo = acc_lo + y_lo * gate
        acc_hi = acc_hi + y_hi * gate
    ms = (jnp.sum(acc_lo * acc_lo, axis=-1, keepdims=True)
          + jnp.sum(acc_hi * acc_hi, axis=-1, keepdims=True)) * (1.0 / D_MODEL)
    scale = lax.rsqrt(ms + EPS)
    y_ref[:, :half] = acc_lo * scale * fw_ref[:, :half]
    y_ref[:, half:] = acc_hi * scale * fw_ref[:, half:]


def _combine(x2, yg, route, final_w):
    n_tok = x2.shape[0]
    return pl.pallas_call(
        _combine_kernel,
        grid=(n_tok // COMBINE_ROWS,),
        in_specs=[
            pl.BlockSpec((COMBINE_ROWS, D_MODEL), lambda i: (i, 0)),
            pl.BlockSpec((TOP_K, COMBINE_ROWS, D_MODEL // 2), lambda i: (0, i, 0)),
            pl.BlockSpec((COMBINE_ROWS, TOP_K), lambda i: (i, 0)),
            _const_spec((1, D_MODEL)),
        ],
        out_specs=pl.BlockSpec((COMBINE_ROWS, D_MODEL), lambda i: (i, 0)),
        out_shape=jax.ShapeDtypeStruct((n_tok, D_MODEL), F32),
        compiler_params=pltpu.CompilerParams(
            dimension_semantics=("arbitrary",), vmem_limit_bytes=VMEM_LIMIT),
        name="combine_%d" % n_tok,
    )(x2, yg, route, final_w)


def kernel(x_prompt, x_sample, cache_mem_k, cache_mem_v, state_ret, state_pool, mem_prompt,
           norm_mix_w, w_in, ret_gn_w, w_pool, pool_scale, w_out, norm_mem_w, mem_norm_w,
           w_q_mem, w_kv_mem, w_o_mem, norm_ffn_w, router_w, router_b, w_gate_up, b_gate_up,
           w_down, b_down, final_norm_w):
    assert norm_mix_w.shape[0] == 1, "one layer"
    b, seq, _ = x_prompt.shape
    db, dseq, _ = x_sample.shape
    row = lambda a: a.reshape(1, -1)

    mk_p, mv_p = _mem_kv(mem_prompt.reshape(b * N_MEM, D_MODEL), row(mem_norm_w[0]), w_kv_mem[0].astype(BF16))
    mk_p = mk_p.reshape(b, N_MEM, D_MODEL)
    mv_p = mv_p.reshape(b, N_MEM, D_MODEL)

    wts = (row(norm_mix_w[0]), w_in[0].astype(BF16), row(ret_gn_w[0]), w_pool[0].astype(BF16),
           row(pool_scale[0]), w_out[0].astype(BF16), row(norm_mem_w[0]), w_q_mem[0].astype(BF16),
           w_o_mem[0].astype(BF16), row(norm_ffn_w[0]), router_w[0].T.astype(BF16),
           jnp.broadcast_to(router_b[0][:, None], (N_EXPERTS, LANES)))

    zero_state = jnp.zeros((b, RET_HEADS, RET_DK, RET_DV), F32)
    zero_hist = jnp.zeros((b, HIST_ROWS, POOL_WIDTH), F32)
    no_counts = jnp.zeros((N_EXPERTS, LANES), F32)
    x2_p, h_p, route_p, ret_p, pool_p, cnt_p = _layer(
        x_prompt, 0, PROMPT_TILE, zero_state, zero_hist, mk_p, mv_p, wts, no_counts)

    hist_s = jnp.concatenate([jnp.zeros((db, 1, POOL_WIDTH), F32), state_pool[0]], axis=1)
    x2_s, h_s, route_s, ret_s, pool_s, cnt_all = _layer(
        x_sample, PAST_LEN, dseq, state_ret[0], hist_s,
        cache_mem_k[0].reshape(db, N_MEM, D_MODEL), cache_mem_v[0].reshape(db, N_MEM, D_MODEL), wts, cnt_p)

    n_p, n_s = b * seq, db * dseq
    n_blocks = -(-((n_p + n_s) * TOP_K + N_EXPERTS * (EXPERT_ROWS - 1)) // EXPERT_ROWS)
    pstart, block_e, n_used = _expert_layout(cnt_all[:, 0].astype(jnp.int32), n_blocks)
    rows_p, gates_p = _slot_rows(route_p, pstart)
    rows_s, gates_s = _slot_rows(route_s, pstart)

    half = D_MODEL // 2
    xs = _dispatch(h_p.reshape(n_p, half), h_s.reshape(n_s, half), rows_p, rows_s, n_blocks * EXPERT_ROWS)

    wgu = _split_gate_up(w_gate_up[0])
    bgu = jnp.concatenate([b_gate_up[0][:, 0::2], b_gate_up[0][:, 1::2]], axis=-1).reshape(N_EXPERTS, 1, 2 * D_FF)
    yb = _expert_ffn(block_e, n_used, xs, wgu, bgu, w_down[0].astype(BF16),
                     b_down[0].reshape(N_EXPERTS, 1, D_MODEL), n_blocks)

    final_w = row(final_norm_w)
    yg_p, yg_s = _collect(yb, rows_p.reshape(1, TOP_K * n_p), rows_s.reshape(1, TOP_K * n_s))
    y_p = _combine(x2_p.reshape(n_p, D_MODEL), yg_p.reshape(TOP_K, n_p, half), gates_p, final_w)
    y_s = _combine(x2_s.reshape(n_s, D_MODEL), yg_s.reshape(TOP_K, n_s, half), gates_s, final_w)
    y_p = y_p.reshape(b, seq, D_MODEL)
    y_s = y_s.reshape(db, dseq, D_MODEL)
    shape_kv = (1, b, N_MEM, MEM_HEADS, MEM_HD)
    return (y_p, y_s, mk_p.reshape(shape_kv), mv_p.reshape(shape_kv), ret_p[None], pool_p[None],
            ret_s[None], pool_s[None])
```

```python
import functools

import numpy as np
import jax
import jax.numpy as jnp
from jax import lax
from jax.experimental import pallas as pl
from jax.experimental.pallas import tpu as pltpu
from jax.experimental.pallas import tpu_sc as plsc

D_MODEL = 1024
CHUNK = 64
PAST_LEN = 4096
RET_HEADS = 4
RET_DK = 128
RET_DV = 128
RET_QK = RET_HEADS * RET_DK
RET_VW = RET_HEADS * RET_DV
ROPE_BASE = 10000.0
POOL_WINDOWS = (2, 4, 8, 16)
POOL_GROUPS = 4
POOL_WIDTH = D_MODEL // 2
POOL_C = POOL_WIDTH // POOL_GROUPS
POOL_HIST = max(POOL_WINDOWS) - 1
HIST_ROWS = POOL_HIST + 1
IN_WIDTH = 2 * RET_QK + 2 * RET_VW + POOL_WIDTH
N_MEM = 256
MEM_HEADS = 4
MEM_HD = D_MODEL // MEM_HEADS
N_EXPERTS = 32
TOP_K = 4
D_FF = D_MODEL
SWIGLU_LIMIT = 7.0
SWIGLU_ALPHA = 1.702
EPS = 1e-5

LANES = 128
ROUTE_ROWS = 16
PROMPT_TILE = 256
PROMPT_STREAMS = 2
SAMPLE_STREAMS = 4
EXPERT_ROWS = 512
KV_ROWS = 512
COMBINE_ROWS = 512
VMEM_LIMIT = 56 * 1024 * 1024

BF16 = jnp.bfloat16
F32 = jnp.float32


def _rms(x, w):
    return x * lax.rsqrt(jnp.mean(x * x, axis=-1, keepdims=True) + EPS) * w


def _dot(a, b):
    return jnp.dot(a, b, preferred_element_type=F32)


def _dot_nt(a, b):
    return lax.dot_general(a, b, (((1,), (1,)), ((), ())), preferred_element_type=F32)


def _pack_bf16(x):
    bits = lax.bitcast_convert_type(x.astype(BF16).astype(F32), jnp.int32)
    w = x.shape[1] // 2
    return lax.shift_right_logical(bits[:, :w], 16) | (bits[:, w:] & -65536)


def _unpack_bf16(p):
    lo = lax.bitcast_convert_type(lax.shift_left(p, 16), F32)
    hi = lax.bitcast_convert_type(p & -65536, F32)
    return lo, hi


def _const_spec(shape):
    nd = len(shape)
    return pl.BlockSpec(shape, lambda *_: (0,) * nd, pipeline_mode=pl.Buffered(1))


def _mem_kv_kernel(mem_ref, nw_ref, w_ref, k_ref, v_ref):
    xn = _rms(mem_ref[...], nw_ref[...]).astype(BF16)
    kv = _dot(xn, w_ref[...])
    k_ref[...] = kv[:, :D_MODEL]
    v_ref[...] = kv[:, D_MODEL:]


def _mem_kv(mem2d, mem_norm_w, w_kv_bf):
    rows = mem2d.shape[0]
    return pl.pallas_call(
        _mem_kv_kernel,
        grid=(rows // KV_ROWS,),
        in_specs=[
            pl.BlockSpec((KV_ROWS, D_MODEL), lambda i: (i, 0)),
            _const_spec((1, D_MODEL)),
            _const_spec((D_MODEL, 2 * D_MODEL)),
        ],
        out_specs=[
            pl.BlockSpec((KV_ROWS, D_MODEL), lambda i: (i, 0)),
            pl.BlockSpec((KV_ROWS, D_MODEL), lambda i: (i, 0)),
        ],
        out_shape=[jax.ShapeDtypeStruct((rows, D_MODEL), F32)] * 2,
        compiler_params=pltpu.CompilerParams(
            dimension_semantics=("arbitrary",), vmem_limit_bytes=VMEM_LIMIT),
        name="mem_kv",
    )(mem2d, mem_norm_w, w_kv_bf)


def _layer_kernel(x_ref, cos_ref, sin_ref, dmat_ref, qdec_ref, kdec_ref, state0_ref, hist0_ref,
                  mk_ref, mv_ref, nmix_ref, win_ref, gnw_ref, wpool_ref, pscale_ref, wout_ref,
                  nmem_ref, wq_ref, wo_ref, nffn_ref, rw_ref, rb_ref, earlier_ref, cnt0_ref,
                  x2_ref, h_ref, route_ref, rstate_ref, pstate_ref, cnt_ref,
                  s_scr, ext_scr, mk_scr, mv_scr, cnt_scr, *, tl, pos0, cdec):
    ns = x_ref.shape[0]
    rows = ns * tl
    t = pl.program_id(1)

    @pl.when((pl.program_id(0) == 0) & (t == 0))
    def _():
        cnt_scr[...] = cnt0_ref[...]

    @pl.when(t == 0)
    def _():
        s_scr[...] = state0_ref[...]
        ext_scr[:, 0:HIST_ROWS, :] = hist0_ref[...]
        mk_scr[...] = mk_ref[...].astype(BF16)
        mv_scr[...] = mv_ref[...].astype(BF16)

    x = x_ref[...].reshape(rows, D_MODEL)
    proj = _dot(_rms(x, nmix_ref[...]).astype(BF16), win_ref[...])
    cos = cos_ref[...]
    sin = sin_ref[...]
    pos = (pos0 + t * tl + lax.broadcasted_iota(jnp.int32, (tl, POOL_C), 0)).astype(F32)

    mixes = []
    for si in range(ns):
        pj = proj[si * tl:(si + 1) * tl]
        outs = []
        for hd in range(RET_HEADS):
            lo = hd * RET_DK
            q = pj[:, lo:lo + RET_DK]
            k = pj[:, RET_QK + lo:RET_QK + lo + RET_DK]
            v = pj[:, 2 * RET_QK + lo:2 * RET_QK + lo + RET_DV]
            g = pj[:, 2 * RET_QK + RET_VW + lo:2 * RET_QK + RET_VW + lo + RET_DV]
            qr = (q * cos + pltpu.roll(q, RET_DK // 2, 1) * sin) * (RET_DK ** -0.5)
            kr = k * cos + pltpu.roll(k, RET_DK // 2, 1) * sin
            vb = v.astype(BF16)
            s = _dot_nt(qr.astype(BF16), kr.astype(BF16)) * dmat_ref[hd]
            o = _dot(s.astype(BF16), vb)
            state = s_scr[si, hd]
            o = o + _dot((qr * qdec_ref[hd]).astype(BF16), state.astype(BF16))
            kd_t = jnp.transpose(kr * kdec_ref[hd]).astype(BF16)
            s_scr[si, hd] = cdec[hd] * state + _dot(kd_t, vb)
            mu = jnp.mean(o, axis=-1, keepdims=True)
            oc = o - mu
            var = jnp.mean(oc * oc, axis=-1, keepdims=True)
            on = oc * lax.rsqrt(var + EPS)
            outs.append(on * gnw_ref[:, lo:lo + RET_DV] * (g * jax.nn.sigmoid(g)))

        pin = pj[:, 2 * RET_QK + 2 * RET_VW:]
        ext_scr[si, HIST_ROWS:HIST_ROWS + tl, :] = pin
        pstate_ref[si] = pin[tl - POOL_HIST:, :]
        for gi, w in enumerate(POOL_WINDOWS):
            lo = gi * POOL_C
            wsum = ext_scr[si, :, lo:lo + POOL_C]
            shift = 1
            while shift < w:
                wsum = wsum + pltpu.roll(wsum, shift, 0)
                shift *= 2
            cnt = jnp.minimum(float(w), pos + 1.0)
            d = wsum[HIST_ROWS:, :] / cnt - pin[:, lo:lo + POOL_C]
            y = _dot(d.astype(BF16), wpool_ref[gi])
            outs.append(y * pscale_ref[:, lo:lo + POOL_C])
        ext_scr[si, 0:HIST_ROWS, :] = ext_scr[si, tl:tl + HIST_ROWS, :]
        mixes.append(jnp.concatenate(outs, axis=-1).astype(BF16))
    rstate_ref[...] = s_scr[...]

    x1 = x + _dot(jnp.concatenate(mixes, axis=0), wout_ref[...])

    qm = _dot(_rms(x1, nmem_ref[...]).astype(BF16), wq_ref[...])
    atts = []
    for si in range(ns):
        aouts = []
        for hd in range(MEM_HEADS):
            lo = hd * MEM_HD
            qh = qm[si * tl:(si + 1) * tl, lo:lo + MEM_HD].astype(BF16)
            s = _dot_nt(qh, mk_scr[si, :, lo:lo + MEM_HD]) * (MEM_HD ** -0.5)
            e = jnp.exp(s - jnp.max(s, axis=-1, keepdims=True))
            p = e / jnp.sum(e, axis=-1, keepdims=True)
            aouts.append(_dot(p.astype(BF16), mv_scr[si, :, lo:lo + MEM_HD]))
        atts.append(jnp.concatenate(aouts, axis=-1).astype(BF16))
    x2 = x1 + _dot(jnp.concatenate(atts, axis=0), wo_ref[...])
    x2_ref[...] = x2.reshape(ns, tl, D_MODEL)

    hn = _rms(x2, nffn_ref[...])
    h_ref[...] = _pack_bf16(hn).reshape(ns, tl, D_MODEL // 2)
    lane_tile = lambda a: a[:, :rows] if rows <= LANES else jnp.concatenate([a] * (rows // LANES), axis=1)
    logits = _dot_nt(rw_ref[...], hn.astype(BF16)) + lane_tile(rb_ref[...])
    eiota = lax.broadcasted_iota(jnp.int32, (N_EXPERTS, rows), 0)
    neg = jnp.finfo(F32).min
    vals, idxs = [], []
    for _k in range(TOP_K):
        m = jnp.max(logits, axis=0, keepdims=True)
        idx = jnp.min(jnp.where(logits == m, eiota, N_EXPERTS), axis=0, keepdims=True)
        vals.append(m)
        idxs.append(idx)
        logits = jnp.where(eiota == idx, neg, logits)
    exps = [jnp.exp(vk - vals[0]) for vk in vals]
    den = exps[0] + exps[1] + exps[2] + exps[3]

    onehots = [(eiota == idx).astype(F32) for idx in idxs]
    picked = onehots[0] + onehots[1] + onehots[2] + onehots[3]
    before = lane_tile(cnt_scr[...]) + _dot(picked.astype(BF16), earlier_ref[...])
    ranks = [jnp.sum(oh * before, axis=0, keepdims=True) for oh in onehots]
    cnt_scr[...] = cnt_scr[...] + jnp.sum(picked, axis=1, keepdims=True)
    cnt_ref[...] = cnt_scr[...]

    rowi = lax.broadcasted_iota(jnp.int32, (ROUTE_ROWS, rows), 0)
    route = jnp.zeros((ROUTE_ROWS, rows), F32)
    for kk in range(TOP_K):
        route = jnp.where(rowi == kk, exps[kk] / den, route)
        route = jnp.where(rowi == TOP_K + kk, idxs[kk].astype(F32), route)
        route = jnp.where(rowi == 2 * TOP_K + kk, ranks[kk], route)
    route_ref[0] = route


def _decay_tables(tl):
    hh = np.arange(RET_HEADS, dtype=np.float64)
    log_g = np.log1p(-np.exp2(-5.0 - hh))
    idx = np.arange(tl, dtype=np.float64)
    dist = np.abs(idx[:, None] - idx[None, :])
    visible = (idx[None, :] // CHUNK) <= (idx[:, None] // CHUNK)
    dmat = np.where(visible[None], np.exp(log_g[:, None, None] * dist[None]), 0.0)
    qdec = np.exp(log_g[:, None] * (idx[None, :] + 1.0))
    kdec = np.exp(log_g[:, None] * (tl - 1.0 - idx[None, :]))
    cdec = tuple(float(c) for c in np.exp(log_g * tl).astype(np.float32))
    bcast = lambda a: np.ascontiguousarray(np.broadcast_to(a[:, :, None], (RET_HEADS, tl, RET_DK)))
    return (jnp.asarray(dmat, F32), jnp.asarray(bcast(qdec), F32), jnp.asarray(bcast(kdec), F32), cdec)


def _rotary_tables(pos0, length):
    half = RET_DK // 2
    inv_freq = jnp.power(ROPE_BASE, -jnp.arange(half, dtype=F32) / half)
    ang = (pos0 + jnp.arange(length, dtype=jnp.int32)).astype(F32)[:, None] * inv_freq[None, :]
    cos, sin = jnp.cos(ang), jnp.sin(ang)
    return jnp.concatenate([cos, cos], axis=-1), jnp.concatenate([-sin, sin], axis=-1)


def _layer(x, pos0, tl, ns, state0, hist0, mk, mv, wts, cnt0):
    b, length, _ = x.shape
    nt = length // tl
    rows = ns * tl
    cos, sin = _rotary_tables(pos0, length)
    dmat, qdec, kdec, cdec = _decay_tables(tl)
    earlier = jnp.asarray(np.triu(np.ones((rows, rows), np.float32), 1), BF16)
    kern = functools.partial(_layer_kernel, tl=tl, pos0=pos0, cdec=cdec)
    tok = lambda width: pl.BlockSpec((ns, tl, width), lambda i, j: (i, j, 0))
    per_stream = lambda *shape: pl.BlockSpec((ns,) + shape, lambda i, j: (i,) + (0,) * len(shape))
    in_specs = [
        tok(D_MODEL),
        pl.BlockSpec((tl, RET_DK), lambda i, j: (j, 0)),
        pl.BlockSpec((tl, RET_DK), lambda i, j: (j, 0)),
        _const_spec((RET_HEADS, tl, tl)),
        _const_spec((RET_HEADS, tl, RET_DK)),
        _const_spec((RET_HEADS, tl, RET_DK)),
        per_stream(RET_HEADS, RET_DK, RET_DV),
        per_stream(HIST_ROWS, POOL_WIDTH),
        per_stream(N_MEM, D_MODEL),
        per_stream(N_MEM, D_MODEL),
    ] + [_const_spec(w.shape) for w in wts] + [_const_spec((rows, rows)), _const_spec((N_EXPERTS, LANES))]
    out_specs = [
        tok(D_MODEL), tok(D_MODEL // 2),
        pl.BlockSpec((1, ROUTE_ROWS, rows), lambda i, j: (i * nt + j, 0, 0)),
        per_stream(RET_HEADS, RET_DK, RET_DV),
        per_stream(POOL_HIST, POOL_WIDTH),
        pl.BlockSpec((N_EXPERTS, LANES), lambda i, j: (0, 0)),
    ]
    out_shape = [
        jax.ShapeDtypeStruct((b, length, D_MODEL), F32),
        jax.ShapeDtypeStruct((b, length, D_MODEL // 2), jnp.int32),
        jax.ShapeDtypeStruct((b // ns * nt, ROUTE_ROWS, rows), F32),
        jax.ShapeDtypeStruct((b, RET_HEADS, RET_DK, RET_DV), F32),
        jax.ShapeDtypeStruct((b, POOL_HIST, POOL_WIDTH), F32),
        jax.ShapeDtypeStruct((N_EXPERTS, LANES), F32),
    ]
    scratch = [
        pltpu.VMEM((ns, RET_HEADS, RET_DK, RET_DV), F32),
        pltpu.VMEM((ns, HIST_ROWS + tl, POOL_WIDTH), F32),
        pltpu.VMEM((ns, N_MEM, D_MODEL), BF16),
        pltpu.VMEM((ns, N_MEM, D_MODEL), BF16),
        pltpu.VMEM((N_EXPERTS, LANES), F32),
    ]
    return pl.pallas_call(
        kern,
        grid=(b // ns, nt),
        in_specs=in_specs,
        out_specs=out_specs,
        out_shape=out_shape,
        scratch_shapes=scratch,
        compiler_params=pltpu.CompilerParams(
            dimension_semantics=("arbitrary", "arbitrary"), vmem_limit_bytes=VMEM_LIMIT),
        name="layer_tl%d" % tl,
    )(x, cos, sin, dmat, qdec, kdec, state0, hist0, mk, mv, *wts, earlier, cnt0)


PREP_ROWS = 512
SPLIT_COLS = 2 * LANES


def _split_kernel(w_ref, perm_ref, o_ref):
    perm = perm_ref[...]
    for c in range(2 * D_FF // SPLIT_COLS):
        wc = w_ref[0, :, c * SPLIT_COLS:(c + 1) * SPLIT_COLS].astype(BF16)
        pc = _dot(wc, perm).astype(BF16)
        o_ref[0, :, c * LANES:(c + 1) * LANES] = pc[:, :LANES]
        o_ref[0, :, D_FF + c * LANES:D_FF + (c + 1) * LANES] = pc[:, LANES:]


def _split_gate_up(w_gate_up):
    perm = np.zeros((SPLIT_COLS, SPLIT_COLS), np.float32)
    j = np.arange(LANES)
    perm[2 * j, j] = 1.0
    perm[2 * j + 1, LANES + j] = 1.0
    blk = pl.BlockSpec((1, PREP_ROWS, 2 * D_FF), lambda e, r: (e, r, 0))
    return pl.pallas_call(
        _split_kernel,
        grid=(N_EXPERTS, D_MODEL // PREP_ROWS),
        in_specs=[blk, _const_spec((SPLIT_COLS, SPLIT_COLS))],
        out_specs=blk,
        out_shape=jax.ShapeDtypeStruct(w_gate_up.shape, BF16),
        compiler_params=pltpu.CompilerParams(
            dimension_semantics=("arbitrary", "arbitrary"), vmem_limit_bytes=VMEM_LIMIT),
        name="split_gate_up",
    )(w_gate_up, jnp.asarray(perm, BF16))


def _expert_kernel(be_ref, used_ref, x_ref, wgu_ref, bgu_ref, wd_ref, bd_ref, y_ref):
    del be_ref

    @pl.when(pl.program_id(0) < used_ref[0])
    def _():
        half = D_MODEL // 2
        x_lo, x_hi = _unpack_bf16(x_ref[...])
        gu = (_dot(x_lo.astype(BF16), wgu_ref[0, :half, :]) + _dot(x_hi.astype(BF16), wgu_ref[0, half:, :])
              + bgu_ref[0])
        gate = jnp.minimum(gu[:, :D_FF], SWIGLU_LIMIT)
        up = jnp.clip(gu[:, D_FF:], -SWIGLU_LIMIT, SWIGLU_LIMIT)
        act = (up + 1.0) * gate * jax.nn.sigmoid(SWIGLU_ALPHA * gate)
        y_ref[...] = _pack_bf16(_dot(act.astype(BF16), wd_ref[0]) + bd_ref[0])


def _expert_ffn(block_e, n_used, xs, wgu, bgu, wd, bd, n_blocks):
    blk = lambda i, be, used: (jnp.minimum(i, used[0] - 1), 0)
    per_expert = lambda i, be, used: (be[i], 0, 0)
    grid_spec = pltpu.PrefetchScalarGridSpec(
        num_scalar_prefetch=2,
        grid=(n_blocks,),
        in_specs=[
            pl.BlockSpec((EXPERT_ROWS, D_MODEL // 2), blk),
            pl.BlockSpec((1, D_MODEL, 2 * D_FF), per_expert),
            pl.BlockSpec((1, 1, 2 * D_FF), per_expert),
            pl.BlockSpec((1, D_FF, D_MODEL), per_expert),
            pl.BlockSpec((1, 1, D_MODEL), per_expert),
        ],
        out_specs=pl.BlockSpec((EXPERT_ROWS, D_MODEL // 2), blk),
    )
    return pl.pallas_call(
        _expert_kernel,
        grid_spec=grid_spec,
        out_shape=jax.ShapeDtypeStruct(xs.shape, jnp.int32),
        compiler_params=pltpu.CompilerParams(
            dimension_semantics=("arbitrary",), vmem_limit_bytes=VMEM_LIMIT),
        name="expert_ffn",
    )(block_e, n_used, xs, wgu, bgu, wd, bd)


def _expert_layout(counts, n_blocks):
    pcounts = (counts + EXPERT_ROWS - 1) // EXPERT_ROWS * EXPERT_ROWS
    pend = jnp.cumsum(pcounts)
    n_used = pend[-1:] // EXPERT_ROWS
    block_start = jnp.minimum(jnp.arange(n_blocks, dtype=jnp.int32), n_used[0] - 1) * EXPERT_ROWS
    block_e = jnp.sum((pend[None, :] <= block_start[:, None]).astype(jnp.int32), axis=1)
    return pend - pcounts, jnp.minimum(block_e, N_EXPERTS - 1), n_used


def _slot_rows(route, pstart, ns, nt, tl):
    n = route.shape[0] * ns * tl
    fields = lambda lo: route[:, lo:lo + TOP_K, :].reshape(-1, nt, TOP_K, ns, tl)
    per_slot = lambda lo: jnp.transpose(fields(lo), (2, 0, 3, 1, 4)).reshape(TOP_K, n)
    e = per_slot(TOP_K).astype(jnp.int32)
    rank = per_slot(2 * TOP_K).astype(jnp.int32)
    hit = e[:, :, None] == jnp.arange(N_EXPERTS, dtype=jnp.int32)[None, None, :]
    rows = rank + jnp.sum(jnp.where(hit, pstart[None, None, :], 0), axis=-1)
    gates = jnp.transpose(fields(0), (0, 3, 1, 4, 2)).reshape(n, TOP_K)
    return rows, gates


SC_WINDOW = 128
SC_COLS = 256


def _sc_mesh():
    return plsc.VectorSubcoreMesh(core_axis_name="c", subcore_axis_name="s")


def _dispatch(h_p, h_s, rows_p, rows_s, m_pad):
    width = h_p.shape[1]

    @functools.partial(pl.kernel, mesh=_sc_mesh(),
                       out_type=jax.ShapeDtypeStruct((m_pad, width), h_p.dtype), scratch_types=[])
    def k(hp_hbm, hs_hbm, rp_hbm, rs_hbm, xs_hbm):
        def body(x_vmem, i_vmem):
            j = pl.program_id(1)
            for kk in range(TOP_K):
                pltpu.sync_copy(x_vmem, xs_hbm.at[i_vmem.at[kk], pl.ds(j * SC_COLS, SC_COLS)])

        for src, rows in ((hp_hbm, rp_hbm), (hs_hbm, rs_hbm)):
            pltpu.emit_pipeline(
                body,
                grid=(src.shape[0] // SC_WINDOW, width // SC_COLS),
                in_specs=[pl.BlockSpec((SC_WINDOW, SC_COLS), lambda i, j: (i, j)),
                          pl.BlockSpec((TOP_K, SC_WINDOW), lambda i, j: (0, i))],
                out_specs=[],
                core_axis_name=("c", "s"),
                dimension_semantics=(pltpu.PARALLEL, pltpu.ARBITRARY),
            )(src, rows)

    return k(h_p, h_s, rows_p, rows_s)


def _collect(yb, rows_p, rows_s):
    width = yb.shape[1]
    out_type = [jax.ShapeDtypeStruct((rows_p.shape[1], width), yb.dtype),
                jax.ShapeDtypeStruct((rows_s.shape[1], width), yb.dtype)]

    @functools.partial(pl.kernel, mesh=_sc_mesh(), out_type=out_type, scratch_types=[])
    def k(yb_hbm, rp_hbm, rs_hbm, op_hbm, os_hbm):
        def body(i_vmem, o_vmem):
            j = pl.program_id(1)
            pltpu.sync_copy(yb_hbm.at[i_vmem.at[0], pl.ds(j * SC_COLS, SC_COLS)], o_vmem)

        for rows, out in ((rp_hbm, op_hbm), (rs_hbm, os_hbm)):
            pltpu.emit_pipeline(
                body,
                grid=(rows.shape[1] // SC_WINDOW, width // SC_COLS),
                in_specs=[pl.BlockSpec((1, SC_WINDOW), lambda i, j: (0, i))],
                out_specs=[pl.BlockSpec((SC_WINDOW, SC_COLS), lambda i, j: (i, j))],
                core_axis_name=("c", "s"),
                dimension_semantics=(pltpu.PARALLEL, pltpu.ARBITRARY),
            )(rows, out)

    return k(yb, rows_p, rows_s)


def _combine_kernel(x_ref, yg_ref, route_ref, fw_ref, y_ref):
    half = D_MODEL // 2
    acc_lo = x_ref[:, :half]
    acc_hi = x_ref[:, half:]
    route = route_ref[...]
    for kk in range(TOP_K):
        y_lo, y_hi = _unpack_bf16(yg_ref[kk])
        gate = route[:, kk:kk + 1]
        acc_lo = acc_lo + y_lo * gate
        acc_hi = acc_hi + y_hi * gate
    ms = (jnp.sum(acc_lo * acc_lo, axis=-1, keepdims=True)
          + jnp.sum(acc_hi * acc_hi, axis=-1, keepdims=True)) * (1.0 / D_MODEL)
    scale = lax.rsqrt(ms + EPS)
    y_ref[:, :half] = acc_lo * scale * fw_ref[:, :half]
    y_ref[:, half:] = acc_hi * scale * fw_ref[:, half:]


def _combine(x2, yg, route, final_w):
    n_tok = x2.shape[0]
    return pl.pallas_call(
        _combine_kernel,
        grid=(n_tok // COMBINE_ROWS,),
        in_specs=[
            pl.BlockSpec((COMBINE_ROWS, D_MODEL), lambda i: (i, 0)),
            pl.BlockSpec((TOP_K, COMBINE_ROWS, D_MODEL // 2), lambda i: (0, i, 0)),
            pl.BlockSpec((COMBINE_ROWS, TOP_K), lambda i: (i, 0)),
            _const_spec((1, D_MODEL)),
        ],
        out_specs=pl.BlockSpec((COMBINE_ROWS, D_MODEL), lambda i: (i, 0)),
        out_shape=jax.ShapeDtypeStruct((n_tok, D_MODEL), F32),
        compiler_params=pltpu.CompilerParams(
            dimension_semantics=("arbitrary",), vmem_limit_bytes=VMEM_LIMIT),
        name="combine_%d" % n_tok,
    )(x2, yg, route, final_w)


def kernel(x_prompt, x_sample, cache_mem_k, cache_mem_v, state_ret, state_pool, mem_prompt,
           norm_mix_w, w_in, ret_gn_w, w_pool, pool_scale, w_out, norm_mem_w, mem_norm_w,
           w_q_mem, w_kv_mem, w_o_mem, norm_ffn_w, router_w, router_b, w_gate_up, b_gate_up,
           w_down, b_down, final_norm_w):
    assert norm_mix_w.shape[0] == 1, "one layer"
    b, seq, _ = x_prompt.shape
    db, dseq, _ = x_sample.shape
    row = lambda a: a.reshape(1, -1)

    mk_p, mv_p = _mem_kv(mem_prompt.reshape(b * N_MEM, D_MODEL), row(mem_norm_w[0]), w_kv_mem[0].astype(BF16))
    mk_p = mk_p.reshape(b, N_MEM, D_MODEL)
    mv_p = mv_p.reshape(b, N_MEM, D_MODEL)

    wts = (row(norm_mix_w[0]), w_in[0].astype(BF16), row(ret_gn_w[0]), w_pool[0].astype(BF16),
           row(pool_scale[0]), w_out[0].astype(BF16), row(norm_mem_w[0]), w_q_mem[0].astype(BF16),
           w_o_mem[0].astype(BF16), row(norm_ffn_w[0]), router_w[0].T.astype(BF16),
           jnp.broadcast_to(router_b[0][:, None], (N_EXPERTS, LANES)))

    zero_state = jnp.zeros((b, RET_HEADS, RET_DK, RET_DV), F32)
    zero_hist = jnp.zeros((b, HIST_ROWS, POOL_WIDTH), F32)
    no_counts = jnp.zeros((N_EXPERTS, LANES), F32)
    x2_p, h_p, route_p, ret_p, pool_p, cnt_p = _layer(
        x_prompt, 0, PROMPT_TILE, PROMPT_STREAMS, zero_state, zero_hist, mk_p, mv_p, wts, no_counts)

    hist_s = jnp.concatenate([jnp.zeros((db, 1, POOL_WIDTH), F32), state_pool[0]], axis=1)
    x2_s, h_s, route_s, ret_s, pool_s, cnt_all = _layer(
        x_sample, PAST_LEN, dseq, SAMPLE_STREAMS, state_ret[0], hist_s,
        cache_mem_k[0].reshape(db, N_MEM, D_MODEL), cache_mem_v[0].reshape(db, N_MEM, D_MODEL), wts, cnt_p)

    n_p, n_s = b * seq, db * dseq
    n_blocks = -(-((n_p + n_s) * TOP_K + N_EXPERTS * (EXPERT_ROWS - 1)) // EXPERT_ROWS)
    pstart, block_e, n_used = _expert_layout(cnt_all[:, 0].astype(jnp.int32), n_blocks)
    rows_p, gates_p = _slot_rows(route_p, pstart, PROMPT_STREAMS, seq // PROMPT_TILE, PROMPT_TILE)
    rows_s, gates_s = _slot_rows(route_s, pstart, SAMPLE_STREAMS, 1, dseq)

    half = D_MODEL // 2
    xs = _dispatch(h_p.reshape(n_p, half), h_s.reshape(n_s, half), rows_p, rows_s, n_blocks * EXPERT_ROWS)

    wgu = _split_gate_up(w_gate_up[0])
    bgu = jnp.concatenate([b_gate_up[0][:, 0::2], b_gate_up[0][:, 1::2]], axis=-1).reshape(N_EXPERTS, 1, 2 * D_FF)
    yb = _expert_ffn(block_e, n_used, xs, wgu, bgu, w_down[0].astype(BF16),
                     b_down[0].reshape(N_EXPERTS, 1, D_MODEL), n_blocks)

    final_w = row(final_norm_w)
    yg_p, yg_s = _collect(yb, rows_p.reshape(1, TOP_K * n_p), rows_s.reshape(1, TOP_K * n_s))
    y_p = _combine(x2_p.reshape(n_p, D_MODEL), yg_p.reshape(TOP_K, n_p, half), gates_p, final_w)
    y_s = _combine(x2_s.reshape(n_s, D_MODEL), yg_s.reshape(TOP_K, n_s, half), gates_s, final_w)
    y_p = y_p.reshape(b, seq, D_MODEL)
    y_s = y_s.reshape(db, dseq, D_MODEL)
    shape_kv = (1, b, N_MEM, MEM_HEADS, MEM_HD)
    return (y_p, y_s, mk_p.reshape(shape_kv), mv_p.reshape(shape_kv), ret_p[None], pool_p[None],
            ret_s[None], pool_s[None])
```

```python
import functools

import numpy as np
import jax
import jax.numpy as jnp
from jax import lax
from jax.experimental import pallas as pl
from jax.experimental.pallas import tpu as pltpu
from jax.experimental.pallas import tpu_sc as plsc

D_MODEL = 1024
CHUNK = 64
PAST_LEN = 4096
RET_HEADS = 4
RET_DK = 128
RET_DV = 128
RET_QK = RET_HEADS * RET_DK
RET_VW = RET_HEADS * RET_DV
ROPE_BASE = 10000.0
POOL_WINDOWS = (2, 4, 8, 16)
POOL_GROUPS = 4
POOL_WIDTH = D_MODEL // 2
POOL_C = POOL_WIDTH // POOL_GROUPS
POOL_HIST = max(POOL_WINDOWS) - 1
HIST_ROWS = POOL_HIST + 1
IN_WIDTH = 2 * RET_QK + 2 * RET_VW + POOL_WIDTH
N_MEM = 256
MEM_HEADS = 4
MEM_HD = D_MODEL // MEM_HEADS
N_EXPERTS = 32
TOP_K = 4
D_FF = D_MODEL
SWIGLU_LIMIT = 7.0
SWIGLU_ALPHA = 1.702
EPS = 1e-5

LANES = 128
ROUTE_ROWS = 16
PROMPT_TILE = 256
PROMPT_STREAMS = 2
SAMPLE_STREAMS = 4
EXPERT_ROWS = 512
KV_STREAMS = 2
COMBINE_ROWS = 512
VMEM_LIMIT = 56 * 1024 * 1024

BF16 = jnp.bfloat16
F32 = jnp.float32


def _rms(x, w):
    return x * lax.rsqrt(jnp.mean(x * x, axis=-1, keepdims=True) + EPS) * w


def _dot(a, b):
    return jnp.dot(a, b, preferred_element_type=F32)


def _dot_nt(a, b):
    return lax.dot_general(a, b, (((1,), (1,)), ((), ())), preferred_element_type=F32)


def _pack_bf16(x):
    bits = lax.bitcast_convert_type(x.astype(BF16).astype(F32), jnp.int32)
    w = x.shape[1] // 2
    return lax.shift_right_logical(bits[:, :w], 16) | (bits[:, w:] & -65536)


def _unpack_bf16(p):
    lo = lax.bitcast_convert_type(lax.shift_left(p, 16), F32)
    hi = lax.bitcast_convert_type(p & -65536, F32)
    return lo, hi


def _const_spec(shape):
    nd = len(shape)
    return pl.BlockSpec(shape, lambda *_: (0,) * nd, pipeline_mode=pl.Buffered(1))


def _mem_kv_kernel(mem_ref, nw_ref, w_ref, k_ref, v_ref):
    ns = mem_ref.shape[0]
    xn = _rms(mem_ref[...].reshape(ns * N_MEM, D_MODEL), nw_ref[...]).astype(BF16)
    kv = _dot(xn, w_ref[...])
    for hd in range(MEM_HEADS):
        lo = hd * MEM_HD
        k_ref[:, :, hd, :] = kv[:, lo:lo + MEM_HD].reshape(ns, N_MEM, MEM_HD)
        v_ref[:, :, hd, :] = kv[:, D_MODEL + lo:D_MODEL + lo + MEM_HD].reshape(ns, N_MEM, MEM_HD)


def _mem_kv(mem, mem_norm_w, w_kv_bf):
    b = mem.shape[0]
    out_spec = pl.BlockSpec((KV_STREAMS, N_MEM, MEM_HEADS, MEM_HD), lambda i: (i, 0, 0, 0))
    return pl.pallas_call(
        _mem_kv_kernel,
        grid=(b // KV_STREAMS,),
        in_specs=[
            pl.BlockSpec((KV_STREAMS, N_MEM, D_MODEL), lambda i: (i, 0, 0)),
            _const_spec((1, D_MODEL)),
            _const_spec((D_MODEL, 2 * D_MODEL)),
        ],
        out_specs=[out_spec, out_spec],
        out_shape=[jax.ShapeDtypeStruct((b, N_MEM, MEM_HEADS, MEM_HD), F32)] * 2,
        compiler_params=pltpu.CompilerParams(
            dimension_semantics=("arbitrary",), vmem_limit_bytes=VMEM_LIMIT),
        name="mem_kv",
    )(mem, mem_norm_w, w_kv_bf)


def _layer_kernel(x_ref, cos_ref, sin_ref, dmat_ref, qdec_ref, kdec_ref, state0_ref, hist0_ref,
                  mk_ref, mv_ref, nmix_ref, win_ref, gnw_ref, wpool_ref, pscale_ref, wout_ref,
                  nmem_ref, wq_ref, wo_ref, nffn_ref, rw_ref, rb_ref, earlier_ref, cnt0_ref,
                  x2_ref, h_ref, route_ref, rstate_ref, pstate_ref, cnt_ref,
                  s_scr, ext_scr, mk_scr, mv_scr, cnt_scr, *, tl, pos0, cdec):
    ns = x_ref.shape[0]
    rows = ns * tl
    t = pl.program_id(1)

    @pl.when((pl.program_id(0) == 0) & (t == 0))
    def _():
        cnt_scr[...] = cnt0_ref[...]

    @pl.when(t == 0)
    def _():
        s_scr[...] = state0_ref[...]
        ext_scr[:, 0:HIST_ROWS, :] = hist0_ref[...]
        for hd in range(MEM_HEADS):
            mk_scr[:, :, hd * MEM_HD:(hd + 1) * MEM_HD] = mk_ref[:, :, hd, :].astype(BF16)
            mv_scr[:, :, hd * MEM_HD:(hd + 1) * MEM_HD] = mv_ref[:, :, hd, :].astype(BF16)

    x = x_ref[...].reshape(rows, D_MODEL)
    proj = _dot(_rms(x, nmix_ref[...]).astype(BF16), win_ref[...])
    cos = cos_ref[...]
    sin = sin_ref[...]
    pos = (pos0 + t * tl + lax.broadcasted_iota(jnp.int32, (tl, POOL_C), 0)).astype(F32)

    mixes = []
    for si in range(ns):
        pj = proj[si * tl:(si + 1) * tl]
        outs = []
        for hd in range(RET_HEADS):
            lo = hd * RET_DK
            q = pj[:, lo:lo + RET_DK]
            k = pj[:, RET_QK + lo:RET_QK + lo + RET_DK]
            v = pj[:, 2 * RET_QK + lo:2 * RET_QK + lo + RET_DV]
            g = pj[:, 2 * RET_QK + RET_VW + lo:2 * RET_QK + RET_VW + lo + RET_DV]
            qr = (q * cos + pltpu.roll(q, RET_DK // 2, 1) * sin) * (RET_DK ** -0.5)
            kr = k * cos + pltpu.roll(k, RET_DK // 2, 1) * sin
            vb = v.astype(BF16)
            s = _dot_nt(qr.astype(BF16), kr.astype(BF16)) * dmat_ref[hd]
            o = _dot(s.astype(BF16), vb)
            state = s_scr[si, hd]
            o = o + _dot((qr * qdec_ref[hd]).astype(BF16), state.astype(BF16))
            kd_t = jnp.transpose(kr * kdec_ref[hd]).astype(BF16)
            s_scr[si, hd] = cdec[hd] * state + _dot(kd_t, vb)
            mu = jnp.mean(o, axis=-1, keepdims=True)
            oc = o - mu
            var = jnp.mean(oc * oc, axis=-1, keepdims=True)
            on = oc * lax.rsqrt(var + EPS)
            outs.append(on * gnw_ref[:, lo:lo + RET_DV] * (g * jax.nn.sigmoid(g)))

        pin = pj[:, 2 * RET_QK + 2 * RET_VW:]
        ext_scr[si, HIST_ROWS:HIST_ROWS + tl, :] = pin
        pstate_ref[si] = pin[tl - POOL_HIST:, :]
        for gi, w in enumerate(POOL_WINDOWS):
            lo = gi * POOL_C
            wsum = ext_scr[si, :, lo:lo + POOL_C]
            shift = 1
            while shift < w:
                wsum = wsum + pltpu.roll(wsum, shift, 0)
                shift *= 2
            cnt = jnp.minimum(float(w), pos + 1.0)
            d = wsum[HIST_ROWS:, :] / cnt - pin[:, lo:lo + POOL_C]
            y = _dot(d.astype(BF16), wpool_ref[gi])
            outs.append(y * pscale_ref[:, lo:lo + POOL_C])
        ext_scr[si, 0:HIST_ROWS, :] = ext_scr[si, tl:tl + HIST_ROWS, :]
        mixes.append(jnp.concatenate(outs, axis=-1).astype(BF16))
    rstate_ref[...] = s_scr[...]

    x1 = x + _dot(jnp.concatenate(mixes, axis=0), wout_ref[...])

    qm = _dot(_rms(x1, nmem_ref[...]).astype(BF16), wq_ref[...])
    atts = []
    for si in range(ns):
        aouts = []
        for hd in range(MEM_HEADS):
            lo = hd * MEM_HD
            qh = qm[si * tl:(si + 1) * tl, lo:lo + MEM_HD].astype(BF16)
            s = _dot_nt(qh, mk_scr[si, :, lo:lo + MEM_HD]) * (MEM_HD ** -0.5)
            e = jnp.exp(s - jnp.max(s, axis=-1, keepdims=True))
            p = e / jnp.sum(e, axis=-1, keepdims=True)
            aouts.append(_dot(p.astype(BF16), mv_scr[si, :, lo:lo + MEM_HD]))
        atts.append(jnp.concatenate(aouts, axis=-1).astype(BF16))
    x2 = x1 + _dot(jnp.concatenate(atts, axis=0), wo_ref[...])
    x2_ref[...] = x2.reshape(ns, tl, D_MODEL)

    hn = _rms(x2, nffn_ref[...])
    h_ref[...] = _pack_bf16(hn).reshape(ns, tl, D_MODEL // 2)
    lane_tile = lambda a: a[:, :rows] if rows <= LANES else jnp.concatenate([a] * (rows // LANES), axis=1)
    logits = _dot_nt(rw_ref[...], hn.astype(BF16)) + lane_tile(rb_ref[...])
    eiota = lax.broadcasted_iota(jnp.int32, (N_EXPERTS, rows), 0)
    neg = jnp.finfo(F32).min
    vals, idxs = [], []
    for _k in range(TOP_K):
        m = jnp.max(logits, axis=0, keepdims=True)
        idx = jnp.min(jnp.where(logits == m, eiota, N_EXPERTS), axis=0, keepdims=True)
        vals.append(m)
        idxs.append(idx)
        logits = jnp.where(eiota == idx, neg, logits)
    exps = [jnp.exp(vk - vals[0]) for vk in vals]
    den = exps[0] + exps[1] + exps[2] + exps[3]

    onehots = [(eiota == idx).astype(F32) for idx in idxs]
    picked = onehots[0] + onehots[1] + onehots[2] + onehots[3]
    before = lane_tile(cnt_scr[...]) + _dot(picked.astype(BF16), earlier_ref[...])
    ranks = [jnp.sum(oh * before, axis=0, keepdims=True) for oh in onehots]
    cnt_scr[...] = cnt_scr[...] + jnp.sum(picked, axis=1, keepdims=True)
    cnt_ref[...] = cnt_scr[...]

    rowi = lax.broadcasted_iota(jnp.int32, (ROUTE_ROWS, rows), 0)
    route = jnp.zeros((ROUTE_ROWS, rows), F32)
    for kk in range(TOP_K):
        route = jnp.where(rowi == kk, exps[kk] / den, route)
        route = jnp.where(rowi == TOP_K + kk, idxs[kk].astype(F32), route)
        route = jnp.where(rowi == 2 * TOP_K + kk, ranks[kk], route)
    route_ref[0] = route


def _decay_tables(tl):
    hh = np.arange(RET_HEADS, dtype=np.float64)
    log_g = np.log1p(-np.exp2(-5.0 - hh))
    idx = np.arange(tl, dtype=np.float64)
    dist = np.abs(idx[:, None] - idx[None, :])
    visible = (idx[None, :] // CHUNK) <= (idx[:, None] // CHUNK)
    dmat = np.where(visible[None], np.exp(log_g[:, None, None] * dist[None]), 0.0)
    qdec = np.exp(log_g[:, None] * (idx[None, :] + 1.0))
    kdec = np.exp(log_g[:, None] * (tl - 1.0 - idx[None, :]))
    cdec = tuple(float(c) for c in np.exp(log_g * tl).astype(np.float32))
    bcast = lambda a: np.ascontiguousarray(np.broadcast_to(a[:, :, None], (RET_HEADS, tl, RET_DK)))
    return (jnp.asarray(dmat, F32), jnp.asarray(bcast(qdec), F32), jnp.asarray(bcast(kdec), F32), cdec)


def _rotary_tables(pos0, length):
    half = RET_DK // 2
    inv_freq = jnp.power(ROPE_BASE, -jnp.arange(half, dtype=F32) / half)
    ang = (pos0 + jnp.arange(length, dtype=jnp.int32)).astype(F32)[:, None] * inv_freq[None, :]
    cos, sin = jnp.cos(ang), jnp.sin(ang)
    return jnp.concatenate([cos, cos], axis=-1), jnp.concatenate([-sin, sin], axis=-1)


def _layer(x, pos0, tl, ns, state0, hist0, mk, mv, wts, cnt0):
    b, length, _ = x.shape
    nt = length // tl
    rows = ns * tl
    cos, sin = _rotary_tables(pos0, length)
    dmat, qdec, kdec, cdec = _decay_tables(tl)
    earlier = jnp.asarray(np.triu(np.ones((rows, rows), np.float32), 1), BF16)
    kern = functools.partial(_layer_kernel, tl=tl, pos0=pos0, cdec=cdec)
    tok = lambda width: pl.BlockSpec((ns, tl, width), lambda i, j: (i, j, 0))
    per_stream = lambda *shape: pl.BlockSpec((ns,) + shape, lambda i, j: (i,) + (0,) * len(shape))
    in_specs = [
        tok(D_MODEL),
        pl.BlockSpec((tl, RET_DK), lambda i, j: (j, 0)),
        pl.BlockSpec((tl, RET_DK), lambda i, j: (j, 0)),
        _const_spec((RET_HEADS, tl, tl)),
        _const_spec((RET_HEADS, tl, RET_DK)),
        _const_spec((RET_HEADS, tl, RET_DK)),
        per_stream(RET_HEADS, RET_DK, RET_DV),
        per_stream(HIST_ROWS, POOL_WIDTH),
        per_stream(N_MEM, MEM_HEADS, MEM_HD),
        per_stream(N_MEM, MEM_HEADS, MEM_HD),
    ] + [_const_spec(w.shape) for w in wts] + [_const_spec((rows, rows)), _const_spec((N_EXPERTS, LANES))]
    out_specs = [
        tok(D_MODEL), tok(D_MODEL // 2),
        pl.BlockSpec((1, ROUTE_ROWS, rows), lambda i, j: (i * nt + j, 0, 0)),
        per_stream(RET_HEADS, RET_DK, RET_DV),
        per_stream(POOL_HIST, POOL_WIDTH),
        pl.BlockSpec((N_EXPERTS, LANES), lambda i, j: (0, 0)),
    ]
    out_shape = [
        jax.ShapeDtypeStruct((b, length, D_MODEL), F32),
        jax.ShapeDtypeStruct((b, length, D_MODEL // 2), jnp.int32),
        jax.ShapeDtypeStruct((b // ns * nt, ROUTE_ROWS, rows), F32),
        jax.ShapeDtypeStruct((b, RET_HEADS, RET_DK, RET_DV), F32),
        jax.ShapeDtypeStruct((b, POOL_HIST, POOL_WIDTH), F32),
        jax.ShapeDtypeStruct((N_EXPERTS, LANES), F32),
    ]
    scratch = [
        pltpu.VMEM((ns, RET_HEADS, RET_DK, RET_DV), F32),
        pltpu.VMEM((ns, HIST_ROWS + tl, POOL_WIDTH), F32),
        pltpu.VMEM((ns, N_MEM, D_MODEL), BF16),
        pltpu.VMEM((ns, N_MEM, D_MODEL), BF16),
        pltpu.VMEM((N_EXPERTS, LANES), F32),
    ]
    return pl.pallas_call(
        kern,
        grid=(b // ns, nt),
        in_specs=in_specs,
        out_specs=out_specs,
        out_shape=out_shape,
        scratch_shapes=scratch,
        compiler_params=pltpu.CompilerParams(
            dimension_semantics=("arbitrary", "arbitrary"), vmem_limit_bytes=VMEM_LIMIT),
        name="layer_tl%d" % tl,
    )(x, cos, sin, dmat, qdec, kdec, state0, hist0, mk, mv, *wts, earlier, cnt0)


SPLIT_COLS = 2 * LANES


def _expert_kernel(be_ref, used_ref, x_ref, wgu_ref, bgu_ref, wd_ref, bd_ref, perm_ref, y_ref,
                   wgu_scr, wd_scr):
    i = pl.program_id(0)
    in_use = i < used_ref[0]

    @pl.when(in_use & ((i == 0) | (be_ref[i] != be_ref[jnp.maximum(i - 1, 0)])))
    def _():
        perm = perm_ref[...]
        for c in range(2 * D_FF // SPLIT_COLS):
            wc = wgu_ref[0, :, c * SPLIT_COLS:(c + 1) * SPLIT_COLS].astype(BF16)
            pc = _dot(wc, perm).astype(BF16)
            wgu_scr[:, c * LANES:(c + 1) * LANES] = pc[:, :LANES]
            wgu_scr[:, D_FF + c * LANES:D_FF + (c + 1) * LANES] = pc[:, LANES:]
        wd_scr[...] = wd_ref[0].astype(BF16)

    @pl.when(in_use)
    def _():
        half = D_MODEL // 2
        x_lo, x_hi = _unpack_bf16(x_ref[...])
        gu = (_dot(x_lo.astype(BF16), wgu_scr[:half, :]) + _dot(x_hi.astype(BF16), wgu_scr[half:, :])
              + bgu_ref[0])
        gate = jnp.minimum(gu[:, :D_FF], SWIGLU_LIMIT)
        up = jnp.clip(gu[:, D_FF:], -SWIGLU_LIMIT, SWIGLU_LIMIT)
        act = (up + 1.0) * gate * jax.nn.sigmoid(SWIGLU_ALPHA * gate)
        y_ref[...] = _pack_bf16(_dot(act.astype(BF16), wd_scr[...]) + bd_ref[0])


def _expert_ffn(block_e, n_used, xs, w_gate_up, bgu, w_down, bd, n_blocks):
    perm = np.zeros((SPLIT_COLS, SPLIT_COLS), np.float32)
    j = np.arange(LANES)
    perm[2 * j, j] = 1.0
    perm[2 * j + 1, LANES + j] = 1.0
    blk = lambda i, be, used: (jnp.minimum(i, used[0] - 1), 0)
    per_expert = lambda i, be, used: (be[i], 0, 0)
    grid_spec = pltpu.PrefetchScalarGridSpec(
        num_scalar_prefetch=2,
        grid=(n_blocks,),
        in_specs=[
            pl.BlockSpec((EXPERT_ROWS, D_MODEL // 2), blk),
            pl.BlockSpec((1, D_MODEL, 2 * D_FF), per_expert),
            pl.BlockSpec((1, 1, 2 * D_FF), per_expert),
            pl.BlockSpec((1, D_FF, D_MODEL), per_expert),
            pl.BlockSpec((1, 1, D_MODEL), per_expert),
            _const_spec((SPLIT_COLS, SPLIT_COLS)),
        ],
        out_specs=pl.BlockSpec((EXPERT_ROWS, D_MODEL // 2), blk),
        scratch_shapes=[pltpu.VMEM((D_MODEL, 2 * D_FF), BF16), pltpu.VMEM((D_FF, D_MODEL), BF16)],
    )
    return pl.pallas_call(
        _expert_kernel,
        grid_spec=grid_spec,
        out_shape=jax.ShapeDtypeStruct(xs.shape, jnp.int32),
        compiler_params=pltpu.CompilerParams(
            dimension_semantics=("arbitrary",), vmem_limit_bytes=VMEM_LIMIT),
        name="expert_ffn",
    )(block_e, n_used, xs, w_gate_up, bgu, w_down, bd, jnp.asarray(perm, BF16))


def _expert_layout(counts, n_blocks):
    pcounts = (counts + EXPERT_ROWS - 1) // EXPERT_ROWS * EXPERT_ROWS
    pend = jnp.cumsum(pcounts)
    n_used = pend[-1:] // EXPERT_ROWS
    block_start = jnp.minimum(jnp.arange(n_blocks, dtype=jnp.int32), n_used[0] - 1) * EXPERT_ROWS
    block_e = jnp.sum((pend[None, :] <= block_start[:, None]).astype(jnp.int32), axis=1)
    return pend - pcounts, jnp.minimum(block_e, N_EXPERTS - 1), n_used


def _slot_rows(route, pstart, ns, nt, tl):
    n = route.shape[0] * ns * tl
    fields = lambda lo: route[:, lo:lo + TOP_K, :].reshape(-1, nt, TOP_K, ns, tl)
    per_slot = lambda lo: jnp.transpose(fields(lo), (2, 0, 3, 1, 4)).reshape(TOP_K, n)
    e = per_slot(TOP_K).astype(jnp.int32)
    rank = per_slot(2 * TOP_K).astype(jnp.int32)
    hit = e[:, :, None] == jnp.arange(N_EXPERTS, dtype=jnp.int32)[None, None, :]
    rows = rank + jnp.sum(jnp.where(hit, pstart[None, None, :], 0), axis=-1)
    gates = jnp.transpose(fields(0), (0, 3, 1, 4, 2)).reshape(n, TOP_K)
    return rows, gates


SC_WINDOW = 128
SC_COLS = 256


def _sc_mesh():
    return plsc.VectorSubcoreMesh(core_axis_name="c", subcore_axis_name="s")


def _dispatch(h_p, h_s, rows_p, rows_s, m_pad):
    width = h_p.shape[1]

    @functools.partial(pl.kernel, mesh=_sc_mesh(),
                       out_type=jax.ShapeDtypeStruct((m_pad, width), h_p.dtype), scratch_types=[])
    def k(hp_hbm, hs_hbm, rp_hbm, rs_hbm, xs_hbm):
        def body(x_vmem, i_vmem):
            j = pl.program_id(1)
            for kk in range(TOP_K):
                pltpu.sync_copy(x_vmem, xs_hbm.at[i_vmem.at[kk], pl.ds(j * SC_COLS, SC_COLS)])

        for src, rows in ((hp_hbm, rp_hbm), (hs_hbm, rs_hbm)):
            pltpu.emit_pipeline(
                body,
                grid=(src.shape[0] // SC_WINDOW, width // SC_COLS),
                in_specs=[pl.BlockSpec((SC_WINDOW, SC_COLS), lambda i, j: (i, j)),
                          pl.BlockSpec((TOP_K, SC_WINDOW), lambda i, j: (0, i))],
                out_specs=[],
                core_axis_name=("c", "s"),
                dimension_semantics=(pltpu.PARALLEL, pltpu.ARBITRARY),
            )(src, rows)

    return k(h_p, h_s, rows_p, rows_s)


def _collect(yb, rows_p, rows_s):
    width = yb.shape[1]
    out_type = [jax.ShapeDtypeStruct((rows_p.shape[1], width), yb.dtype),
                jax.ShapeDtypeStruct((rows_s.shape[1], width), yb.dtype)]

    @functools.partial(pl.kernel, mesh=_sc_mesh(), out_type=out_type, scratch_types=[])
    def k(yb_hbm, rp_hbm, rs_hbm, op_hbm, os_hbm):
        def body(i_vmem, o_vmem):
            j = pl.program_id(1)
            pltpu.sync_copy(yb_hbm.at[i_vmem.at[0], pl.ds(j * SC_COLS, SC_COLS)], o_vmem)

        for rows, out in ((rp_hbm, op_hbm), (rs_hbm, os_hbm)):
            pltpu.emit_pipeline(
                body,
                grid=(rows.shape[1] // SC_WINDOW, width // SC_COLS),
                in_specs=[pl.BlockSpec((1, SC_WINDOW), lambda i, j: (0, i))],
                out_specs=[pl.BlockSpec((SC_WINDOW, SC_COLS), lambda i, j: (i, j))],
                core_axis_name=("c", "s"),
                dimension_semantics=(pltpu.PARALLEL, pltpu.ARBITRARY),
            )(rows, out)

    return k(yb, rows_p, rows_s)


def _combine_kernel(x_ref, yg_ref, route_ref, fw_ref, y_ref):
    half = D_MODEL // 2
    acc_lo = x_ref[:, :half]
    acc_hi = x_ref[:, half:]
    route = route_ref[...]
    for kk in range(TOP_K):
        y_lo, y_hi = _unpack_bf16(yg_ref[kk])
        gate = route[:, kk:kk + 1]
        acc_lo = acc_lo + y_lo * gate
        acc_hi = acc_hi + y_hi * gate
    ms = (jnp.sum(acc_lo * acc_lo, axis=-1, keepdims=True)
          + jnp.sum(acc_hi * acc_hi, axis=-1, keepdims=True)) * (1.0 / D_MODEL)
    scale = lax.rsqrt(ms + EPS)
    y_ref[:, :half] = acc_lo * scale * fw_ref[:, :half]
    y_ref[:, half:] = acc_hi * scale * fw_ref[:, half:]


def _combine(x2, yg, route, final_w):
    n_tok = x2.shape[0]
    return pl.pallas_call(
        _combine_kernel,
        grid=(n_tok // COMBINE_ROWS,),
        in_specs=[
            pl.BlockSpec((COMBINE_ROWS, D_MODEL), lambda i: (i, 0)),
            pl.BlockSpec((TOP_K, COMBINE_ROWS, D_MODEL // 2), lambda i: (0, i, 0)),
            pl.BlockSpec((COMBINE_ROWS, TOP_K), lambda i: (i, 0)),
            _const_spec((1, D_MODEL)),
        ],
        out_specs=pl.BlockSpec((COMBINE_ROWS, D_MODEL), lambda i: (i, 0)),
        out_shape=jax.ShapeDtypeStruct((n_tok, D_MODEL), F32),
        compiler_params=pltpu.CompilerParams(
            dimension_semantics=("arbitrary",), vmem_limit_bytes=VMEM_LIMIT),
        name="combine_%d" % n_tok,
    )(x2, yg, route, final_w)


def kernel(x_prompt, x_sample, cache_mem_k, cache_mem_v, state_ret, state_pool, mem_prompt,
           norm_mix_w, w_in, ret_gn_w, w_pool, pool_scale, w_out, norm_mem_w, mem_norm_w,
           w_q_mem, w_kv_mem, w_o_mem, norm_ffn_w, router_w, router_b, w_gate_up, b_gate_up,
           w_down, b_down, final_norm_w):
    assert norm_mix_w.shape[0] == 1, "one layer"
    b, seq, _ = x_prompt.shape
    db, dseq, _ = x_sample.shape
    row = lambda a: a.reshape(1, -1)

    mk_p, mv_p = _mem_kv(mem_prompt, row(mem_norm_w[0]), w_kv_mem[0].astype(BF16))

    wts = (row(norm_mix_w[0]), w_in[0].astype(BF16), row(ret_gn_w[0]), w_pool[0].astype(BF16),
           row(pool_scale[0]), w_out[0].astype(BF16), row(norm_mem_w[0]), w_q_mem[0].astype(BF16),
           w_o_mem[0].astype(BF16), row(norm_ffn_w[0]), router_w[0].T.astype(BF16),
           jnp.broadcast_to(router_b[0][:, None], (N_EXPERTS, LANES)))

    zero_state = jnp.zeros((b, RET_HEADS, RET_DK, RET_DV), F32)
    zero_hist = jnp.zeros((b, HIST_ROWS, POOL_WIDTH), F32)
    no_counts = jnp.zeros((N_EXPERTS, LANES), F32)
    x2_p, h_p, route_p, ret_p, pool_p, cnt_p = _layer(
        x_prompt, 0, PROMPT_TILE, PROMPT_STREAMS, zero_state, zero_hist, mk_p, mv_p, wts, no_counts)

    hist_s = jnp.concatenate([jnp.zeros((db, 1, POOL_WIDTH), F32), state_pool[0]], axis=1)
    x2_s, h_s, route_s, ret_s, pool_s, cnt_all = _layer(
        x_sample, PAST_LEN, dseq, SAMPLE_STREAMS, state_ret[0], hist_s, cache_mem_k[0], cache_mem_v[0], wts, cnt_p)

    n_p, n_s = b * seq, db * dseq
    n_blocks = -(-((n_p + n_s) * TOP_K + N_EXPERTS * (EXPERT_ROWS - 1)) // EXPERT_ROWS)
    pstart, block_e, n_used = _expert_layout(cnt_all[:, 0].astype(jnp.int32), n_blocks)
    rows_p, gates_p = _slot_rows(route_p, pstart, PROMPT_STREAMS, seq // PROMPT_TILE, PROMPT_TILE)
    rows_s, gates_s = _slot_rows(route_s, pstart, SAMPLE_STREAMS, 1, dseq)

    half = D_MODEL // 2
    xs = _dispatch(h_p.reshape(n_p, half), h_s.reshape(n_s, half), rows_p, rows_s, n_blocks * EXPERT_ROWS)

    bgu = jnp.concatenate([b_gate_up[0][:, 0::2], b_gate_up[0][:, 1::2]], axis=-1).reshape(N_EXPERTS, 1, 2 * D_FF)
    yb = _expert_ffn(block_e, n_used, xs, w_gate_up[0], bgu, w_down[0],
                     b_down[0].reshape(N_EXPERTS, 1, D_MODEL), n_blocks)

    final_w = row(final_norm_w)
    yg_p, yg_s = _collect(yb, rows_p.reshape(1, TOP_K * n_p), rows_s.reshape(1, TOP_K * n_s))
    y_p = _combine(x2_p.reshape(n_p, D_MODEL), yg_p.reshape(TOP_K, n_p, half), gates_p, final_w)
    y_s = _combine(x2_s.reshape(n_s, D_MODEL), yg_s.reshape(TOP_K, n_s, half), gates_s, final_w)
    y_p = y_p.reshape(b, seq, D_MODEL)
    y_s = y_s.reshape(db, dseq, D_MODEL)
    return (y_p, y_s, mk_p[None], mv_p[None], ret_p[None], pool_p[None], ret_s[None], pool_s[None])
```

```python
import functools

import numpy as np
import jax
import jax.numpy as jnp
from jax import lax
from jax.experimental import pallas as pl
from jax.experimental.pallas import tpu as pltpu
from jax.experimental.pallas import tpu_sc as plsc

D_MODEL = 1024
CHUNK = 64
PAST_LEN = 4096
RET_HEADS = 4
RET_DK = 128
RET_DV = 128
RET_QK = RET_HEADS * RET_DK
RET_VW = RET_HEADS * RET_DV
ROPE_BASE = 10000.0
POOL_WINDOWS = (2, 4, 8, 16)
POOL_GROUPS = 4
POOL_WIDTH = D_MODEL // 2
POOL_C = POOL_WIDTH // POOL_GROUPS
POOL_HIST = max(POOL_WINDOWS) - 1
HIST_ROWS = POOL_HIST + 1
IN_WIDTH = 2 * RET_QK + 2 * RET_VW + POOL_WIDTH
N_MEM = 256
MEM_HEADS = 4
MEM_HD = D_MODEL // MEM_HEADS
N_EXPERTS = 32
TOP_K = 4
D_FF = D_MODEL
SWIGLU_LIMIT = 7.0
SWIGLU_ALPHA = 1.702
EPS = 1e-5

LANES = 128
ROUTE_ROWS = 16
PROMPT_TILE = 256
PROMPT_STREAMS = 2
SAMPLE_STREAMS = 4
EXPERT_ROWS = 512
KV_STREAMS = 2
COMBINE_ROWS = 512
COMBINE_CHUNKS = 4
VMEM_LIMIT = 56 * 1024 * 1024

BF16 = jnp.bfloat16
F32 = jnp.float32


def _rms(x, w):
    return x * lax.rsqrt(jnp.mean(x * x, axis=-1, keepdims=True) + EPS) * w


def _dot(a, b):
    return jnp.dot(a, b, preferred_element_type=F32)


def _dot_nt(a, b):
    return lax.dot_general(a, b, (((1,), (1,)), ((), ())), preferred_element_type=F32)


def _pack_bf16(x):
    bits = lax.bitcast_convert_type(x.astype(BF16).astype(F32), jnp.int32)
    w = x.shape[1] // 2
    return lax.shift_right_logical(bits[:, :w], 16) | (bits[:, w:] & -65536)


def _unpack_bf16(p):
    lo = lax.bitcast_convert_type(lax.shift_left(p, 16), F32)
    hi = lax.bitcast_convert_type(p & -65536, F32)
    return lo, hi


def _const_spec(shape):
    nd = len(shape)
    return pl.BlockSpec(shape, lambda *_: (0,) * nd, pipeline_mode=pl.Buffered(1))


def _mem_kv_kernel(mem_ref, nw_ref, w_ref, k_ref, v_ref):
    ns = mem_ref.shape[0]
    xn = _rms(mem_ref[...].reshape(ns * N_MEM, D_MODEL), nw_ref[...]).astype(BF16)
    kv = _dot(xn, w_ref[...])
    for hd in range(MEM_HEADS):
        lo = hd * MEM_HD
        k_ref[:, :, hd, :] = kv[:, lo:lo + MEM_HD].reshape(ns, N_MEM, MEM_HD)
        v_ref[:, :, hd, :] = kv[:, D_MODEL + lo:D_MODEL + lo + MEM_HD].reshape(ns, N_MEM, MEM_HD)


def _mem_kv(mem, mem_norm_w, w_kv_bf):
    b = mem.shape[0]
    out_spec = pl.BlockSpec((KV_STREAMS, N_MEM, MEM_HEADS, MEM_HD), lambda i: (i, 0, 0, 0))
    return pl.pallas_call(
        _mem_kv_kernel,
        grid=(b // KV_STREAMS,),
        in_specs=[
            pl.BlockSpec((KV_STREAMS, N_MEM, D_MODEL), lambda i: (i, 0, 0)),
            _const_spec((1, D_MODEL)),
            _const_spec((D_MODEL, 2 * D_MODEL)),
        ],
        out_specs=[out_spec, out_spec],
        out_shape=[jax.ShapeDtypeStruct((b, N_MEM, MEM_HEADS, MEM_HD), F32)] * 2,
        compiler_params=pltpu.CompilerParams(
            dimension_semantics=("arbitrary",), vmem_limit_bytes=VMEM_LIMIT),
        name="mem_kv",
    )(mem, mem_norm_w, w_kv_bf)


def _layer_kernel(x_ref, cos_ref, sin_ref, dmat_ref, qdec_ref, kdec_ref, state0_ref, hist0_ref,
                  mk_ref, mv_ref, nmix_ref, win_ref, gnw_ref, wpool_ref, pscale_ref, wout_ref,
                  nmem_ref, wq_ref, wo_ref, nffn_ref, rw_ref, rb_ref, earlier_ref, cnt0_ref,
                  x2_ref, h_ref, route_ref, rstate_ref, pstate_ref, cnt_ref,
                  s_scr, ext_scr, mk_scr, mv_scr, cnt_scr, *, tl, pos0, cdec):
    ns = x_ref.shape[0]
    rows = ns * tl
    t = pl.program_id(1)

    @pl.when((pl.program_id(0) == 0) & (t == 0))
    def _():
        cnt_scr[...] = cnt0_ref[...]

    @pl.when(t == 0)
    def _():
        s_scr[...] = state0_ref[...]
        ext_scr[:, 0:HIST_ROWS, :] = hist0_ref[...]
        for hd in range(MEM_HEADS):
            mk_scr[:, :, hd * MEM_HD:(hd + 1) * MEM_HD] = mk_ref[:, :, hd, :].astype(BF16)
            mv_scr[:, :, hd * MEM_HD:(hd + 1) * MEM_HD] = mv_ref[:, :, hd, :].astype(BF16)

    x = x_ref[...].reshape(rows, D_MODEL)
    proj = _dot(_rms(x, nmix_ref[...]).astype(BF16), win_ref[...])
    cos = cos_ref[...]
    sin = sin_ref[...]
    pos = (pos0 + t * tl + lax.broadcasted_iota(jnp.int32, (tl, POOL_C), 0)).astype(F32)

    mixes = []
    for si in range(ns):
        pj = proj[si * tl:(si + 1) * tl]
        outs = []
        for hd in range(RET_HEADS):
            lo = hd * RET_DK
            q = pj[:, lo:lo + RET_DK]
            k = pj[:, RET_QK + lo:RET_QK + lo + RET_DK]
            v = pj[:, 2 * RET_QK + lo:2 * RET_QK + lo + RET_DV]
            g = pj[:, 2 * RET_QK + RET_VW + lo:2 * RET_QK + RET_VW + lo + RET_DV]
            qr = (q * cos + pltpu.roll(q, RET_DK // 2, 1) * sin) * (RET_DK ** -0.5)
            kr = k * cos + pltpu.roll(k, RET_DK // 2, 1) * sin
            vb = v.astype(BF16)
            s = _dot_nt(qr.astype(BF16), kr.astype(BF16)) * dmat_ref[hd]
            o = _dot(s.astype(BF16), vb)
            state = s_scr[si, hd]
            o = o + _dot((qr * qdec_ref[hd]).astype(BF16), state.astype(BF16))
            kd_t = jnp.transpose(kr * kdec_ref[hd]).astype(BF16)
            s_scr[si, hd] = cdec[hd] * state + _dot(kd_t, vb)
            mu = jnp.mean(o, axis=-1, keepdims=True)
            oc = o - mu
            var = jnp.mean(oc * oc, axis=-1, keepdims=True)
            on = oc * lax.rsqrt(var + EPS)
            outs.append(on * gnw_ref[:, lo:lo + RET_DV] * (g * jax.nn.sigmoid(g)))

        pin = pj[:, 2 * RET_QK + 2 * RET_VW:]
        ext_scr[si, HIST_ROWS:HIST_ROWS + tl, :] = pin
        pstate_ref[si] = pin[tl - POOL_HIST:, :]
        for gi, w in enumerate(POOL_WINDOWS):
            lo = gi * POOL_C
            wsum = ext_scr[si, :, lo:lo + POOL_C]
            shift = 1
            while shift < w:
                wsum = wsum + pltpu.roll(wsum, shift, 0)
                shift *= 2
            cnt = jnp.minimum(float(w), pos + 1.0)
            d = wsum[HIST_ROWS:, :] / cnt - pin[:, lo:lo + POOL_C]
            y = _dot(d.astype(BF16), wpool_ref[gi])
            outs.append(y * pscale_ref[:, lo:lo + POOL_C])
        ext_scr[si, 0:HIST_ROWS, :] = ext_scr[si, tl:tl + HIST_ROWS, :]
        mixes.append(jnp.concatenate(outs, axis=-1).astype(BF16))
    rstate_ref[...] = s_scr[...]

    x1 = x + _dot(jnp.concatenate(mixes, axis=0), wout_ref[...])

    qm = _dot(_rms(x1, nmem_ref[...]).astype(BF16), wq_ref[...])
    atts = []
    for si in range(ns):
        aouts = []
        for hd in range(MEM_HEADS):
            lo = hd * MEM_HD
            qh = qm[si * tl:(si + 1) * tl, lo:lo + MEM_HD].astype(BF16)
            s = _dot_nt(qh, mk_scr[si, :, lo:lo + MEM_HD]) * (MEM_HD ** -0.5)
            e = jnp.exp(s - jnp.max(s, axis=-1, keepdims=True))
            p = e / jnp.sum(e, axis=-1, keepdims=True)
            aouts.append(_dot(p.astype(BF16), mv_scr[si, :, lo:lo + MEM_HD]))
        atts.append(jnp.concatenate(aouts, axis=-1).astype(BF16))
    x2 = x1 + _dot(jnp.concatenate(atts, axis=0), wo_ref[...])
    x2_ref[...] = x2.reshape(ns, tl, D_MODEL)

    hn = _rms(x2, nffn_ref[...])
    h_ref[...] = _pack_bf16(hn).reshape(ns, tl, D_MODEL // 2)
    lane_tile = lambda a: a[:, :rows] if rows <= LANES else jnp.concatenate([a] * (rows // LANES), axis=1)
    logits = _dot_nt(rw_ref[...], hn.astype(BF16)) + lane_tile(rb_ref[...])
    eiota = lax.broadcasted_iota(jnp.int32, (N_EXPERTS, rows), 0)
    neg = jnp.finfo(F32).min
    vals, idxs = [], []
    for _k in range(TOP_K):
        m = jnp.max(logits, axis=0, keepdims=True)
        idx = jnp.min(jnp.where(logits == m, eiota, N_EXPERTS), axis=0, keepdims=True)
        vals.append(m)
        idxs.append(idx)
        logits = jnp.where(eiota == idx, neg, logits)
    exps = [jnp.exp(vk - vals[0]) for vk in vals]
    den = exps[0] + exps[1] + exps[2] + exps[3]

    onehots = [(eiota == idx).astype(F32) for idx in idxs]
    picked = onehots[0] + onehots[1] + onehots[2] + onehots[3]
    before = lane_tile(cnt_scr[...]) + _dot(picked.astype(BF16), earlier_ref[...])
    ranks = [jnp.sum(oh * before, axis=0, keepdims=True) for oh in onehots]
    cnt_scr[...] = cnt_scr[...] + jnp.sum(picked, axis=1, keepdims=True)
    cnt_ref[...] = cnt_scr[...]

    rowi = lax.broadcasted_iota(jnp.int32, (ROUTE_ROWS, rows), 0)
    route = jnp.zeros((ROUTE_ROWS, rows), F32)
    for kk in range(TOP_K):
        route = jnp.where(rowi == kk, exps[kk] / den, route)
        route = jnp.where(rowi == TOP_K + kk, idxs[kk].astype(F32), route)
        route = jnp.where(rowi == 2 * TOP_K + kk, ranks[kk], route)
    route_ref[0] = route


def _decay_tables(tl):
    hh = np.arange(RET_HEADS, dtype=np.float64)
    log_g = np.log1p(-np.exp2(-5.0 - hh))
    idx = np.arange(tl, dtype=np.float64)
    dist = np.abs(idx[:, None] - idx[None, :])
    visible = (idx[None, :] // CHUNK) <= (idx[:, None] // CHUNK)
    dmat = np.where(visible[None], np.exp(log_g[:, None, None] * dist[None]), 0.0)
    qdec = np.exp(log_g[:, None] * (idx[None, :] + 1.0))
    kdec = np.exp(log_g[:, None] * (tl - 1.0 - idx[None, :]))
    cdec = tuple(float(c) for c in np.exp(log_g * tl).astype(np.float32))
    bcast = lambda a: np.ascontiguousarray(np.broadcast_to(a[:, :, None], (RET_HEADS, tl, RET_DK)))
    return (jnp.asarray(dmat, F32), jnp.asarray(bcast(qdec), F32), jnp.asarray(bcast(kdec), F32), cdec)


def _rotary_tables(pos0, length):
    half = RET_DK // 2
    inv_freq = jnp.power(ROPE_BASE, -jnp.arange(half, dtype=F32) / half)
    ang = (pos0 + jnp.arange(length, dtype=jnp.int32)).astype(F32)[:, None] * inv_freq[None, :]
    cos, sin = jnp.cos(ang), jnp.sin(ang)
    return jnp.concatenate([cos, cos], axis=-1), jnp.concatenate([-sin, sin], axis=-1)


def _layer(x, pos0, tl, ns, state0, hist0, mk, mv, wts, cnt0):
    b, length, _ = x.shape
    nt = length // tl
    rows = ns * tl
    cos, sin = _rotary_tables(pos0, length)
    dmat, qdec, kdec, cdec = _decay_tables(tl)
    earlier = jnp.asarray(np.triu(np.ones((rows, rows), np.float32), 1), BF16)
    kern = functools.partial(_layer_kernel, tl=tl, pos0=pos0, cdec=cdec)
    tok = lambda width: pl.BlockSpec((ns, tl, width), lambda i, j: (i, j, 0))
    per_stream = lambda *shape: pl.BlockSpec((ns,) + shape, lambda i, j: (i,) + (0,) * len(shape))
    in_specs = [
        tok(D_MODEL),
        pl.BlockSpec((tl, RET_DK), lambda i, j: (j, 0)),
        pl.BlockSpec((tl, RET_DK), lambda i, j: (j, 0)),
        _const_spec((RET_HEADS, tl, tl)),
        _const_spec((RET_HEADS, tl, RET_DK)),
        _const_spec((RET_HEADS, tl, RET_DK)),
        per_stream(RET_HEADS, RET_DK, RET_DV),
        per_stream(HIST_ROWS, POOL_WIDTH),
        per_stream(N_MEM, MEM_HEADS, MEM_HD),
        per_stream(N_MEM, MEM_HEADS, MEM_HD),
    ] + [_const_spec(w.shape) for w in wts] + [_const_spec((rows, rows)), _const_spec((N_EXPERTS, LANES))]
    out_specs = [
        tok(D_MODEL), tok(D_MODEL // 2),
        pl.BlockSpec((1, ROUTE_ROWS, rows), lambda i, j: (i * nt + j, 0, 0)),
        per_stream(RET_HEADS, RET_DK, RET_DV),
        per_stream(POOL_HIST, POOL_WIDTH),
        pl.BlockSpec((N_EXPERTS, LANES), lambda i, j: (0, 0)),
    ]
    out_shape = [
        jax.ShapeDtypeStruct((b, length, D_MODEL), F32),
        jax.ShapeDtypeStruct((b, length, D_MODEL // 2), jnp.int32),
        jax.ShapeDtypeStruct((b // ns * nt, ROUTE_ROWS, rows), F32),
        jax.ShapeDtypeStruct((b, RET_HEADS, RET_DK, RET_DV), F32),
        jax.ShapeDtypeStruct((b, POOL_HIST, POOL_WIDTH), F32),
        jax.ShapeDtypeStruct((N_EXPERTS, LANES), F32),
    ]
    scratch = [
        pltpu.VMEM((ns, RET_HEADS, RET_DK, RET_DV), F32),
        pltpu.VMEM((ns, HIST_ROWS + tl, POOL_WIDTH), F32),
        pltpu.VMEM((ns, N_MEM, D_MODEL), BF16),
        pltpu.VMEM((ns, N_MEM, D_MODEL), BF16),
        pltpu.VMEM((N_EXPERTS, LANES), F32),
    ]
    return pl.pallas_call(
        kern,
        grid=(b // ns, nt),
        in_specs=in_specs,
        out_specs=out_specs,
        out_shape=out_shape,
        scratch_shapes=scratch,
        compiler_params=pltpu.CompilerParams(
            dimension_semantics=("arbitrary", "arbitrary"), vmem_limit_bytes=VMEM_LIMIT),
        name="layer_tl%d" % tl,
    )(x, cos, sin, dmat, qdec, kdec, state0, hist0, mk, mv, *wts, earlier, cnt0)


SPLIT_COLS = 2 * LANES


def _expert_kernel(be_ref, used_ref, x_ref, wgu_ref, bgu_ref, wd_ref, bd_ref, perm_ref, y_ref,
                   wgu_scr, wd_scr):
    i = pl.program_id(0)
    in_use = i < used_ref[0]

    @pl.when(in_use & ((i == 0) | (be_ref[i] != be_ref[jnp.maximum(i - 1, 0)])))
    def _():
        perm = perm_ref[...]
        for c in range(2 * D_FF // SPLIT_COLS):
            wc = wgu_ref[0, :, c * SPLIT_COLS:(c + 1) * SPLIT_COLS].astype(BF16)
            pc = _dot(wc, perm).astype(BF16)
            wgu_scr[:, c * LANES:(c + 1) * LANES] = pc[:, :LANES]
            wgu_scr[:, D_FF + c * LANES:D_FF + (c + 1) * LANES] = pc[:, LANES:]
        wd_scr[...] = wd_ref[0].astype(BF16)

    @pl.when(in_use)
    def _():
        half = D_MODEL // 2
        x_lo, x_hi = _unpack_bf16(x_ref[...])
        gu = (_dot(x_lo.astype(BF16), wgu_scr[:half, :]) + _dot(x_hi.astype(BF16), wgu_scr[half:, :])
              + bgu_ref[0])
        gate = jnp.minimum(gu[:, :D_FF], SWIGLU_LIMIT)
        up = jnp.clip(gu[:, D_FF:], -SWIGLU_LIMIT, SWIGLU_LIMIT)
        act = (up + 1.0) * gate * jax.nn.sigmoid(SWIGLU_ALPHA * gate)
        y_ref[...] = _pack_bf16(_dot(act.astype(BF16), wd_scr[...]) + bd_ref[0])


def _expert_ffn(block_e, n_used, xs, w_gate_up, bgu, w_down, bd, n_blocks):
    perm = np.zeros((SPLIT_COLS, SPLIT_COLS), np.float32)
    j = np.arange(LANES)
    perm[2 * j, j] = 1.0
    perm[2 * j + 1, LANES + j] = 1.0
    blk = lambda i, be, used: (jnp.minimum(i, used[0] - 1), 0)
    per_expert = lambda i, be, used: (be[i], 0, 0)
    grid_spec = pltpu.PrefetchScalarGridSpec(
        num_scalar_prefetch=2,
        grid=(n_blocks,),
        in_specs=[
            pl.BlockSpec((EXPERT_ROWS, D_MODEL // 2), blk),
            pl.BlockSpec((1, D_MODEL, 2 * D_FF), per_expert),
            pl.BlockSpec((1, 1, 2 * D_FF), per_expert),
            pl.BlockSpec((1, D_FF, D_MODEL), per_expert),
            pl.BlockSpec((1, 1, D_MODEL), per_expert),
            _const_spec((SPLIT_COLS, SPLIT_COLS)),
        ],
        out_specs=pl.BlockSpec((EXPERT_ROWS, D_MODEL // 2), blk),
        scratch_shapes=[pltpu.VMEM((D_MODEL, 2 * D_FF), BF16), pltpu.VMEM((D_FF, D_MODEL), BF16)],
    )
    return pl.pallas_call(
        _expert_kernel,
        grid_spec=grid_spec,
        out_shape=jax.ShapeDtypeStruct(xs.shape, jnp.int32),
        compiler_params=pltpu.CompilerParams(
            dimension_semantics=("arbitrary",), vmem_limit_bytes=VMEM_LIMIT),
        name="expert_ffn",
    )(block_e, n_used, xs, w_gate_up, bgu, w_down, bd, jnp.asarray(perm, BF16))


def _expert_layout(counts, n_blocks):
    pcounts = (counts + EXPERT_ROWS - 1) // EXPERT_ROWS * EXPERT_ROWS
    pend = jnp.cumsum(pcounts)
    n_used = pend[-1:] // EXPERT_ROWS
    block_start = jnp.minimum(jnp.arange(n_blocks, dtype=jnp.int32), n_used[0] - 1) * EXPERT_ROWS
    block_e = jnp.sum((pend[None, :] <= block_start[:, None]).astype(jnp.int32), axis=1)
    return pend - pcounts, jnp.minimum(block_e, N_EXPERTS - 1), n_used


def _slot_rows(route, pstart, ns, nt, tl):
    n = route.shape[0] * ns * tl
    fields = lambda lo: route[:, lo:lo + TOP_K, :].reshape(-1, nt, TOP_K, ns, tl)
    per_slot = lambda lo: jnp.transpose(fields(lo), (2, 0, 3, 1, 4)).reshape(TOP_K, n)
    e = per_slot(TOP_K).astype(jnp.int32)
    rank = per_slot(2 * TOP_K).astype(jnp.int32)
    hit = e[:, :, None] == jnp.arange(N_EXPERTS, dtype=jnp.int32)[None, None, :]
    rows = rank + jnp.sum(jnp.where(hit, pstart[None, None, :], 0), axis=-1)
    gates = jnp.transpose(fields(0), (0, 3, 1, 4, 2)).reshape(n, TOP_K)
    return rows, gates


SC_WINDOW = 128
SC_COLS = 256


def _sc_mesh():
    return plsc.VectorSubcoreMesh(core_axis_name="c", subcore_axis_name="s")


def _dispatch(h_p, h_s, rows_p, rows_s, m_pad):
    width = h_p.shape[1]

    @functools.partial(pl.kernel, mesh=_sc_mesh(),
                       out_type=jax.ShapeDtypeStruct((m_pad, width), h_p.dtype), scratch_types=[])
    def k(hp_hbm, hs_hbm, rp_hbm, rs_hbm, xs_hbm):
        def body(x_vmem, i_vmem):
            j = pl.program_id(1)
            for kk in range(TOP_K):
                pltpu.sync_copy(x_vmem, xs_hbm.at[i_vmem.at[kk], pl.ds(j * SC_COLS, SC_COLS)])

        for src, rows in ((hp_hbm, rp_hbm), (hs_hbm, rs_hbm)):
            pltpu.emit_pipeline(
                body,
                grid=(src.shape[0] // SC_WINDOW, width // SC_COLS),
                in_specs=[pl.BlockSpec((SC_WINDOW, SC_COLS), lambda i, j: (i, j)),
                          pl.BlockSpec((TOP_K, SC_WINDOW), lambda i, j: (0, i))],
                out_specs=[],
                core_axis_name=("c", "s"),
                dimension_semantics=(pltpu.PARALLEL, pltpu.ARBITRARY),
            )(src, rows)

    return k(h_p, h_s, rows_p, rows_s)


def _collect(yb, rows):
    width = yb.shape[1]

    @functools.partial(pl.kernel, mesh=_sc_mesh(),
                       out_type=jax.ShapeDtypeStruct((rows.shape[1], width), yb.dtype), scratch_types=[])
    def k(yb_hbm, r_hbm, o_hbm):
        def body(i_vmem, o_vmem):
            j = pl.program_id(1)
            pltpu.sync_copy(yb_hbm.at[i_vmem.at[0], pl.ds(j * SC_COLS, SC_COLS)], o_vmem)

        pltpu.emit_pipeline(
            body,
            grid=(rows.shape[1] // SC_WINDOW, width // SC_COLS),
            in_specs=[pl.BlockSpec((1, SC_WINDOW), lambda i, j: (0, i))],
            out_specs=[pl.BlockSpec((SC_WINDOW, SC_COLS), lambda i, j: (i, j))],
            core_axis_name=("c", "s"),
            dimension_semantics=(pltpu.PARALLEL, pltpu.ARBITRARY),
        )(r_hbm, o_hbm)

    return k(yb, rows)


def _combine_kernel(x_ref, yg_ref, route_ref, fw_ref, y_ref):
    half = D_MODEL // 2
    acc_lo = x_ref[:, :half]
    acc_hi = x_ref[:, half:]
    route = route_ref[...]
    for kk in range(TOP_K):
        y_lo, y_hi = _unpack_bf16(yg_ref[kk])
        gate = route[:, kk:kk + 1]
        acc_lo = acc_lo + y_lo * gate
        acc_hi = acc_hi + y_hi * gate
    ms = (jnp.sum(acc_lo * acc_lo, axis=-1, keepdims=True)
          + jnp.sum(acc_hi * acc_hi, axis=-1, keepdims=True)) * (1.0 / D_MODEL)
    scale = lax.rsqrt(ms + EPS)
    y_ref[:, :half] = acc_lo * scale * fw_ref[:, :half]
    y_ref[:, half:] = acc_hi * scale * fw_ref[:, half:]


def _combine_next_kernel(x_ref, yg_ref, route_ref, fw_ref, prev_ref, y_ref):
    del prev_ref
    _combine_kernel(x_ref, yg_ref, route_ref, fw_ref, y_ref)


def _combine(x2, yg, gates, final_w, y_prev, first_row):
    n_tok = x2.shape[0]
    blk0 = first_row // COMBINE_ROWS
    in_specs = [
        pl.BlockSpec((COMBINE_ROWS, D_MODEL), lambda i: (blk0 + i, 0)),
        pl.BlockSpec((TOP_K, COMBINE_ROWS, D_MODEL // 2), lambda i: (0, i, 0)),
        pl.BlockSpec((COMBINE_ROWS, TOP_K), lambda i: (blk0 + i, 0)),
        _const_spec((1, D_MODEL)),
    ]
    args = [x2, yg, gates, final_w]
    kern, aliases = _combine_kernel, {}
    if y_prev is not None:
        in_specs.append(pl.BlockSpec(memory_space=pl.ANY))
        args.append(y_prev)
        kern, aliases = _combine_next_kernel, {len(args) - 1: 0}
    return pl.pallas_call(
        kern,
        grid=(yg.shape[1] // COMBINE_ROWS,),
        in_specs=in_specs,
        out_specs=pl.BlockSpec((COMBINE_ROWS, D_MODEL), lambda i: (blk0 + i, 0)),
        out_shape=jax.ShapeDtypeStruct((n_tok, D_MODEL), F32),
        input_output_aliases=aliases,
        compiler_params=pltpu.CompilerParams(
            dimension_semantics=("arbitrary",), vmem_limit_bytes=VMEM_LIMIT),
        name="combine_%d_%d" % (n_tok, first_row),
    )(*args)


def _collect_combine(yb, x2, rows, gates, final_w, chunks):
    nc = x2.shape[0] // chunks
    y = None
    for c in range(chunks):
        yg = _collect(yb, rows[:, c * nc:(c + 1) * nc].reshape(1, TOP_K * nc))
        y = _combine(x2, yg.reshape(TOP_K, nc, D_MODEL // 2), gates, final_w, y, c * nc)
    return y


def kernel(x_prompt, x_sample, cache_mem_k, cache_mem_v, state_ret, state_pool, mem_prompt,
           norm_mix_w, w_in, ret_gn_w, w_pool, pool_scale, w_out, norm_mem_w, mem_norm_w,
           w_q_mem, w_kv_mem, w_o_mem, norm_ffn_w, router_w, router_b, w_gate_up, b_gate_up,
           w_down, b_down, final_norm_w):
    assert norm_mix_w.shape[0] == 1, "one layer"
    b, seq, _ = x_prompt.shape
    db, dseq, _ = x_sample.shape
    row = lambda a: a.reshape(1, -1)

    mk_p, mv_p = _mem_kv(mem_prompt, row(mem_norm_w[0]), w_kv_mem[0].astype(BF16))

    wts = (row(norm_mix_w[0]), w_in[0].astype(BF16), row(ret_gn_w[0]), w_pool[0].astype(BF16),
           row(pool_scale[0]), w_out[0].astype(BF16), row(norm_mem_w[0]), w_q_mem[0].astype(BF16),
           w_o_mem[0].astype(BF16), row(norm_ffn_w[0]), router_w[0].T.astype(BF16),
           jnp.broadcast_to(router_b[0][:, None], (N_EXPERTS, LANES)))

    zero_state = jnp.zeros((b, RET_HEADS, RET_DK, RET_DV), F32)
    zero_hist = jnp.zeros((b, HIST_ROWS, POOL_WIDTH), F32)
    no_counts = jnp.zeros((N_EXPERTS, LANES), F32)
    x2_p, h_p, route_p, ret_p, pool_p, cnt_p = _layer(
        x_prompt, 0, PROMPT_TILE, PROMPT_STREAMS, zero_state, zero_hist, mk_p, mv_p, wts, no_counts)

    hist_s = jnp.concatenate([jnp.zeros((db, 1, POOL_WIDTH), F32), state_pool[0]], axis=1)
    x2_s, h_s, route_s, ret_s, pool_s, cnt_all = _layer(
        x_sample, PAST_LEN, dseq, SAMPLE_STREAMS, state_ret[0], hist_s, cache_mem_k[0], cache_mem_v[0], wts, cnt_p)

    n_p, n_s = b * seq, db * dseq
    n_blocks = -(-((n_p + n_s) * TOP_K + N_EXPERTS * (EXPERT_ROWS - 1)) // EXPERT_ROWS)
    pstart, block_e, n_used = _expert_layout(cnt_all[:, 0].astype(jnp.int32), n_blocks)
    rows_p, gates_p = _slot_rows(route_p, pstart, PROMPT_STREAMS, seq // PROMPT_TILE, PROMPT_TILE)
    rows_s, gates_s = _slot_rows(route_s, pstart, SAMPLE_STREAMS, 1, dseq)

    half = D_MODEL // 2
    xs = _dispatch(h_p.reshape(n_p, half), h_s.reshape(n_s, half), rows_p, rows_s, n_blocks * EXPERT_ROWS)

    bgu = jnp.concatenate([b_gate_up[0][:, 0::2], b_gate_up[0][:, 1::2]], axis=-1).reshape(N_EXPERTS, 1, 2 * D_FF)
    yb = _expert_ffn(block_e, n_used, xs, w_gate_up[0], bgu, w_down[0],
                     b_down[0].reshape(N_EXPERTS, 1, D_MODEL), n_blocks)

    final_w = row(final_norm_w)
    y_s = _collect_combine(yb, x2_s.reshape(n_s, D_MODEL), rows_s, gates_s, final_w, 1)
    y_p = _collect_combine(yb, x2_p.reshape(n_p, D_MODEL), rows_p, gates_p, final_w, COMBINE_CHUNKS)
    y_p = y_p.reshape(b, seq, D_MODEL)
    y_s = y_s.reshape(db, dseq, D_MODEL)
    return (y_p, y_s, mk_p[None], mv_p[None], ret_p[None], pool_p[None], ret_s[None], pool_s[None])
```

```python
import functools

import numpy as np
import jax
import jax.numpy as jnp
from jax import lax
from jax.experimental import pallas as pl
from jax.experimental.pallas import tpu as pltpu
from jax.experimental.pallas import tpu_sc as plsc

D_MODEL = 1024
CHUNK = 64
PAST_LEN = 4096
RET_HEADS = 4
RET_DK = 128
RET_DV = 128
RET_QK = RET_HEADS * RET_DK
RET_VW = RET_HEADS * RET_DV
ROPE_BASE = 10000.0
POOL_WINDOWS = (2, 4, 8, 16)
POOL_GROUPS = 4
POOL_WIDTH = D_MODEL // 2
POOL_C = POOL_WIDTH // POOL_GROUPS
POOL_HIST = max(POOL_WINDOWS) - 1
HIST_ROWS = POOL_HIST + 1
IN_WIDTH = 2 * RET_QK + 2 * RET_VW + POOL_WIDTH
N_MEM = 256
MEM_HEADS = 4
MEM_HD = D_MODEL // MEM_HEADS
N_EXPERTS = 32
TOP_K = 4
D_FF = D_MODEL
SWIGLU_LIMIT = 7.0
SWIGLU_ALPHA = 1.702
EPS = 1e-5

LANES = 128
ROUTE_ROWS = 16
PROMPT_TILE = 256
PROMPT_STREAMS = 2
SAMPLE_STREAMS = 4
EXPERT_ROWS = 512
KV_STREAMS = 2
COMBINE_ROWS = 512
COMBINE_CHUNKS = 4
VMEM_LIMIT = 56 * 1024 * 1024

BF16 = jnp.bfloat16
F32 = jnp.float32


def _rms(x, w):
    return x * lax.rsqrt(jnp.mean(x * x, axis=-1, keepdims=True) + EPS) * w


def _dot(a, b):
    return jnp.dot(a, b, preferred_element_type=F32)


def _dot_nt(a, b):
    return lax.dot_general(a, b, (((1,), (1,)), ((), ())), preferred_element_type=F32)


def _pack_bf16(x):
    bits = lax.bitcast_convert_type(x.astype(BF16).astype(F32), jnp.int32)
    w = x.shape[1] // 2
    return lax.shift_right_logical(bits[:, :w], 16) | (bits[:, w:] & -65536)


def _unpack_bf16(p):
    lo = lax.bitcast_convert_type(lax.shift_left(p, 16), F32)
    hi = lax.bitcast_convert_type(p & -65536, F32)
    return lo, hi


def _const_spec(shape):
    nd = len(shape)
    return pl.BlockSpec(shape, lambda *_: (0,) * nd, pipeline_mode=pl.Buffered(1))


def _mem_kv_kernel(mem_ref, nw_ref, w_ref, k_ref, v_ref):
    ns = mem_ref.shape[0]
    xn = _rms(mem_ref[...].reshape(ns * N_MEM, D_MODEL), nw_ref[...]).astype(BF16)
    kv = _dot(xn, w_ref[...])
    for hd in range(MEM_HEADS):
        lo = hd * MEM_HD
        k_ref[:, :, hd, :] = kv[:, lo:lo + MEM_HD].reshape(ns, N_MEM, MEM_HD)
        v_ref[:, :, hd, :] = kv[:, D_MODEL + lo:D_MODEL + lo + MEM_HD].reshape(ns, N_MEM, MEM_HD)


def _mem_kv(mem, mem_norm_w, w_kv_bf):
    b = mem.shape[0]
    out_spec = pl.BlockSpec((KV_STREAMS, N_MEM, MEM_HEADS, MEM_HD), lambda i: (i, 0, 0, 0))
    return pl.pallas_call(
        _mem_kv_kernel,
        grid=(b // KV_STREAMS,),
        in_specs=[
            pl.BlockSpec((KV_STREAMS, N_MEM, D_MODEL), lambda i: (i, 0, 0)),
            _const_spec((1, D_MODEL)),
            _const_spec((D_MODEL, 2 * D_MODEL)),
        ],
        out_specs=[out_spec, out_spec],
        out_shape=[jax.ShapeDtypeStruct((b, N_MEM, MEM_HEADS, MEM_HD), F32)] * 2,
        compiler_params=pltpu.CompilerParams(
            dimension_semantics=("arbitrary",), vmem_limit_bytes=VMEM_LIMIT),
        name="mem_kv",
    )(mem, mem_norm_w, w_kv_bf)


def _layer_kernel(x_ref, cos_ref, sin_ref, dmat_ref, qdec_ref, kdec_ref, state0_ref, hist0_ref,
                  mk_ref, mv_ref, nmix_ref, win_ref, gnw_ref, wpool_ref, pscale_ref, wout_ref,
                  nmem_ref, wq_ref, wo_ref, nffn_ref, rw_ref, rb_ref, earlier_ref, cnt0_ref,
                  x2_ref, h_ref, route_ref, rstate_ref, pstate_ref, cnt_ref,
                  s_scr, ext_scr, mk_scr, mv_scr, cnt_scr, *, tl, chain, pos0, cdec):
    t = pl.program_id(1)

    @pl.when((pl.program_id(0) == 0) & (t == 0))
    def _():
        cnt_scr[...] = cnt0_ref[...]

    @pl.when(t == 0)
    def _():
        s_scr[...] = state0_ref[...]
        ext_scr[:, 0:HIST_ROWS, :] = hist0_ref[...]
        for hd in range(MEM_HEADS):
            mk_scr[:, :, hd * MEM_HD:(hd + 1) * MEM_HD] = mk_ref[:, :, hd, :].astype(BF16)
            mv_scr[:, :, hd * MEM_HD:(hd + 1) * MEM_HD] = mv_ref[:, :, hd, :].astype(BF16)

    for c0 in range(0, x_ref.shape[0], chain):
        _layer_chain(c0, t, x_ref, cos_ref, sin_ref, dmat_ref, qdec_ref, kdec_ref, nmix_ref, win_ref, gnw_ref,
                     wpool_ref, pscale_ref, wout_ref, nmem_ref, wq_ref, wo_ref, nffn_ref, rw_ref, rb_ref,
                     earlier_ref, x2_ref, h_ref, route_ref, pstate_ref, s_scr, ext_scr, mk_scr, mv_scr, cnt_scr,
                     tl=tl, chain=chain, pos0=pos0, cdec=cdec)
    rstate_ref[...] = s_scr[...]
    cnt_ref[...] = cnt_scr[...]


def _layer_chain(c0, t, x_ref, cos_ref, sin_ref, dmat_ref, qdec_ref, kdec_ref, nmix_ref, win_ref, gnw_ref,
                 wpool_ref, pscale_ref, wout_ref, nmem_ref, wq_ref, wo_ref, nffn_ref, rw_ref, rb_ref,
                 earlier_ref, x2_ref, h_ref, route_ref, pstate_ref, s_scr, ext_scr, mk_scr, mv_scr, cnt_scr,
                 *, tl, chain, pos0, cdec):
    rows = chain * tl
    x = x_ref[c0:c0 + chain].reshape(rows, D_MODEL)
    proj = _dot(_rms(x, nmix_ref[...]).astype(BF16), win_ref[...])
    cos = cos_ref[...]
    sin = sin_ref[...]
    pos = (pos0 + t * tl + lax.broadcasted_iota(jnp.int32, (tl, POOL_C), 0)).astype(F32)

    mixes = []
    for sj in range(chain):
        si = c0 + sj
        pj = proj[sj * tl:(sj + 1) * tl]
        outs = []
        for hd in range(RET_HEADS):
            lo = hd * RET_DK
            q = pj[:, lo:lo + RET_DK]
            k = pj[:, RET_QK + lo:RET_QK + lo + RET_DK]
            v = pj[:, 2 * RET_QK + lo:2 * RET_QK + lo + RET_DV]
            g = pj[:, 2 * RET_QK + RET_VW + lo:2 * RET_QK + RET_VW + lo + RET_DV]
            qr = (q * cos + pltpu.roll(q, RET_DK // 2, 1) * sin) * (RET_DK ** -0.5)
            kr = k * cos + pltpu.roll(k, RET_DK // 2, 1) * sin
            vb = v.astype(BF16)
            s = _dot_nt(qr.astype(BF16), kr.astype(BF16)) * dmat_ref[hd]
            o = _dot(s.astype(BF16), vb)
            state = s_scr[si, hd]
            o = o + _dot((qr * qdec_ref[hd]).astype(BF16), state.astype(BF16))
            kd_t = jnp.transpose(kr * kdec_ref[hd]).astype(BF16)
            s_scr[si, hd] = cdec[hd] * state + _dot(kd_t, vb)
            mu = jnp.mean(o, axis=-1, keepdims=True)
            oc = o - mu
            var = jnp.mean(oc * oc, axis=-1, keepdims=True)
            on = oc * lax.rsqrt(var + EPS)
            outs.append(on * gnw_ref[:, lo:lo + RET_DV] * (g * jax.nn.sigmoid(g)))

        pin = pj[:, 2 * RET_QK + 2 * RET_VW:]
        ext_scr[si, HIST_ROWS:HIST_ROWS + tl, :] = pin
        pstate_ref[si] = pin[tl - POOL_HIST:, :]
        for gi, w in enumerate(POOL_WINDOWS):
            lo = gi * POOL_C
            wsum = ext_scr[si, :, lo:lo + POOL_C]
            shift = 1
            while shift < w:
                wsum = wsum + pltpu.roll(wsum, shift, 0)
                shift *= 2
            cnt = jnp.minimum(float(w), pos + 1.0)
            d = wsum[HIST_ROWS:, :] / cnt - pin[:, lo:lo + POOL_C]
            y = _dot(d.astype(BF16), wpool_ref[gi])
            outs.append(y * pscale_ref[:, lo:lo + POOL_C])
        ext_scr[si, 0:HIST_ROWS, :] = ext_scr[si, tl:tl + HIST_ROWS, :]
        mixes.append(jnp.concatenate(outs, axis=-1).astype(BF16))

    x1 = x + _dot(jnp.concatenate(mixes, axis=0), wout_ref[...])

    qm = _dot(_rms(x1, nmem_ref[...]).astype(BF16), wq_ref[...])
    atts = []
    for sj in range(chain):
        si = c0 + sj
        aouts = []
        for hd in range(MEM_HEADS):
            lo = hd * MEM_HD
            qh = qm[sj * tl:(sj + 1) * tl, lo:lo + MEM_HD].astype(BF16)
            s = _dot_nt(qh, mk_scr[si, :, lo:lo + MEM_HD]) * (MEM_HD ** -0.5)
            e = jnp.exp(s - jnp.max(s, axis=-1, keepdims=True))
            p = e / jnp.sum(e, axis=-1, keepdims=True)
            aouts.append(_dot(p.astype(BF16), mv_scr[si, :, lo:lo + MEM_HD]))
        atts.append(jnp.concatenate(aouts, axis=-1).astype(BF16))
    x2 = x1 + _dot(jnp.concatenate(atts, axis=0), wo_ref[...])
    x2_ref[c0:c0 + chain] = x2.reshape(chain, tl, D_MODEL)

    hn = _rms(x2, nffn_ref[...])
    h_ref[c0:c0 + chain] = _pack_bf16(hn).reshape(chain, tl, D_MODEL // 2)
    lane_tile = lambda a: a[:, :rows] if rows <= LANES else jnp.concatenate([a] * (rows // LANES), axis=1)
    logits = _dot_nt(rw_ref[...], hn.astype(BF16)) + lane_tile(rb_ref[...])
    eiota = lax.broadcasted_iota(jnp.int32, (N_EXPERTS, rows), 0)
    neg = jnp.finfo(F32).min
    vals, idxs = [], []
    for _k in range(TOP_K):
        m = jnp.max(logits, axis=0, keepdims=True)
        idx = jnp.min(jnp.where(logits == m, eiota, N_EXPERTS), axis=0, keepdims=True)
        vals.append(m)
        idxs.append(idx)
        logits = jnp.where(eiota == idx, neg, logits)
    exps = [jnp.exp(vk - vals[0]) for vk in vals]
    den = exps[0] + exps[1] + exps[2] + exps[3]

    onehots = [(eiota == idx).astype(F32) for idx in idxs]
    picked = onehots[0] + onehots[1] + onehots[2] + onehots[3]
    before = lane_tile(cnt_scr[...]) + _dot(picked.astype(BF16), earlier_ref[...])
    ranks = [jnp.sum(oh * before, axis=0, keepdims=True) for oh in onehots]
    cnt_scr[...] = cnt_scr[...] + jnp.sum(picked, axis=1, keepdims=True)

    rowi = lax.broadcasted_iota(jnp.int32, (ROUTE_ROWS, rows), 0)
    route = jnp.zeros((ROUTE_ROWS, rows), F32)
    for kk in range(TOP_K):
        route = jnp.where(rowi == kk, exps[kk] / den, route)
        route = jnp.where(rowi == TOP_K + kk, idxs[kk].astype(F32), route)
        route = jnp.where(rowi == 2 * TOP_K + kk, ranks[kk], route)
    route_ref[0, :, c0 * tl:c0 * tl + rows] = route


def _decay_tables(tl):
    hh = np.arange(RET_HEADS, dtype=np.float64)
    log_g = np.log1p(-np.exp2(-5.0 - hh))
    idx = np.arange(tl, dtype=np.float64)
    dist = np.abs(idx[:, None] - idx[None, :])
    visible = (idx[None, :] // CHUNK) <= (idx[:, None] // CHUNK)
    dmat = np.where(visible[None], np.exp(log_g[:, None, None] * dist[None]), 0.0)
    qdec = np.exp(log_g[:, None] * (idx[None, :] + 1.0))
    kdec = np.exp(log_g[:, None] * (tl - 1.0 - idx[None, :]))
    cdec = tuple(float(c) for c in np.exp(log_g * tl).astype(np.float32))
    bcast = lambda a: np.ascontiguousarray(np.broadcast_to(a[:, :, None], (RET_HEADS, tl, RET_DK)))
    return (jnp.asarray(dmat, F32), jnp.asarray(bcast(qdec), F32), jnp.asarray(bcast(kdec), F32), cdec)


def _rotary_tables(pos0, length):
    half = RET_DK // 2
    inv_freq = jnp.power(ROPE_BASE, -jnp.arange(half, dtype=F32) / half)
    ang = (pos0 + jnp.arange(length, dtype=jnp.int32)).astype(F32)[:, None] * inv_freq[None, :]
    cos, sin = jnp.cos(ang), jnp.sin(ang)
    return jnp.concatenate([cos, cos], axis=-1), jnp.concatenate([-sin, sin], axis=-1)


def _layer(x, pos0, tl, ns, chain, state0, hist0, mk, mv, wts, cnt0):
    b, length, _ = x.shape
    nt = length // tl
    rows = ns * tl
    cos, sin = _rotary_tables(pos0, length)
    dmat, qdec, kdec, cdec = _decay_tables(tl)
    crows = chain * tl
    earlier = jnp.asarray(np.triu(np.ones((crows, crows), np.float32), 1), BF16)
    kern = functools.partial(_layer_kernel, tl=tl, chain=chain, pos0=pos0, cdec=cdec)
    tok = lambda width: pl.BlockSpec((ns, tl, width), lambda i, j: (i, j, 0))
    per_stream = lambda *shape: pl.BlockSpec((ns,) + shape, lambda i, j: (i,) + (0,) * len(shape))
    in_specs = [
        tok(D_MODEL),
        pl.BlockSpec((tl, RET_DK), lambda i, j: (j, 0)),
        pl.BlockSpec((tl, RET_DK), lambda i, j: (j, 0)),
        _const_spec((RET_HEADS, tl, tl)),
        _const_spec((RET_HEADS, tl, RET_DK)),
        _const_spec((RET_HEADS, tl, RET_DK)),
        per_stream(RET_HEADS, RET_DK, RET_DV),
        per_stream(HIST_ROWS, POOL_WIDTH),
        per_stream(N_MEM, MEM_HEADS, MEM_HD),
        per_stream(N_MEM, MEM_HEADS, MEM_HD),
    ] + [_const_spec(w.shape) for w in wts] + [_const_spec((crows, crows)), _const_spec((N_EXPERTS, LANES))]
    out_specs = [
        tok(D_MODEL), tok(D_MODEL // 2),
        pl.BlockSpec((1, ROUTE_ROWS, rows), lambda i, j: (i * nt + j, 0, 0)),
        per_stream(RET_HEADS, RET_DK, RET_DV),
        per_stream(POOL_HIST, POOL_WIDTH),
        pl.BlockSpec((N_EXPERTS, LANES), lambda i, j: (0, 0)),
    ]
    out_shape = [
        jax.ShapeDtypeStruct((b, length, D_MODEL), F32),
        jax.ShapeDtypeStruct((b, length, D_MODEL // 2), jnp.int32),
        jax.ShapeDtypeStruct((b // ns * nt, ROUTE_ROWS, rows), F32),
        jax.ShapeDtypeStruct((b, RET_HEADS, RET_DK, RET_DV), F32),
        jax.ShapeDtypeStruct((b, POOL_HIST, POOL_WIDTH), F32),
        jax.ShapeDtypeStruct((N_EXPERTS, LANES), F32),
    ]
    scratch = [
        pltpu.VMEM((ns, RET_HEADS, RET_DK, RET_DV), F32),
        pltpu.VMEM((ns, HIST_ROWS + tl, POOL_WIDTH), F32),
        pltpu.VMEM((ns, N_MEM, D_MODEL), BF16),
        pltpu.VMEM((ns, N_MEM, D_MODEL), BF16),
        pltpu.VMEM((N_EXPERTS, LANES), F32),
    ]
    return pl.pallas_call(
        kern,
        grid=(b // ns, nt),
        in_specs=in_specs,
        out_specs=out_specs,
        out_shape=out_shape,
        scratch_shapes=scratch,
        compiler_params=pltpu.CompilerParams(
            dimension_semantics=("arbitrary", "arbitrary"), vmem_limit_bytes=VMEM_LIMIT),
        name="layer_tl%d" % tl,
    )(x, cos, sin, dmat, qdec, kdec, state0, hist0, mk, mv, *wts, earlier, cnt0)


SPLIT_COLS = 2 * LANES


def _expert_kernel(be_ref, used_ref, x_ref, wgu_ref, bgu_ref, wd_ref, bd_ref, perm_ref, y_ref,
                   wgu_scr, wd_scr):
    i = pl.program_id(0)
    in_use = i < used_ref[0]

    @pl.when(in_use & ((i == 0) | (be_ref[i] != be_ref[jnp.maximum(i - 1, 0)])))
    def _():
        perm = perm_ref[...]
        for c in range(2 * D_FF // SPLIT_COLS):
            wc = wgu_ref[0, :, c * SPLIT_COLS:(c + 1) * SPLIT_COLS].astype(BF16)
            pc = _dot(wc, perm).astype(BF16)
            wgu_scr[:, c * LANES:(c + 1) * LANES] = pc[:, :LANES]
            wgu_scr[:, D_FF + c * LANES:D_FF + (c + 1) * LANES] = pc[:, LANES:]
        wd_scr[...] = wd_ref[0].astype(BF16)

    @pl.when(in_use)
    def _():
        half = D_MODEL // 2
        x_lo, x_hi = _unpack_bf16(x_ref[...])
        gu = (_dot(x_lo.astype(BF16), wgu_scr[:half, :]) + _dot(x_hi.astype(BF16), wgu_scr[half:, :])
              + bgu_ref[0])
        gate = jnp.minimum(gu[:, :D_FF], SWIGLU_LIMIT)
        up = jnp.clip(gu[:, D_FF:], -SWIGLU_LIMIT, SWIGLU_LIMIT)
        act = (up + 1.0) * gate * jax.nn.sigmoid(SWIGLU_ALPHA * gate)
        y_ref[...] = _pack_bf16(_dot(act.astype(BF16), wd_scr[...]) + bd_ref[0])


def _expert_ffn(block_e, n_used, xs, w_gate_up, bgu, w_down, bd, n_blocks):
    perm = np.zeros((SPLIT_COLS, SPLIT_COLS), np.float32)
    j = np.arange(LANES)
    perm[2 * j, j] = 1.0
    perm[2 * j + 1, LANES + j] = 1.0
    blk = lambda i, be, used: (jnp.minimum(i, used[0] - 1), 0)
    per_expert = lambda i, be, used: (be[i], 0, 0)
    grid_spec = pltpu.PrefetchScalarGridSpec(
        num_scalar_prefetch=2,
        grid=(n_blocks,),
        in_specs=[
            pl.BlockSpec((EXPERT_ROWS, D_MODEL // 2), blk),
            pl.BlockSpec((1, D_MODEL, 2 * D_FF), per_expert),
            pl.BlockSpec((1, 1, 2 * D_FF), per_expert),
            pl.BlockSpec((1, D_FF, D_MODEL), per_expert),
            pl.BlockSpec((1, 1, D_MODEL), per_expert),
            _const_spec((SPLIT_COLS, SPLIT_COLS)),
        ],
        out_specs=pl.BlockSpec((EXPERT_ROWS, D_MODEL // 2), blk),
        scratch_shapes=[pltpu.VMEM((D_MODEL, 2 * D_FF), BF16), pltpu.VMEM((D_FF, D_MODEL), BF16)],
    )
    return pl.pallas_call(
        _expert_kernel,
        grid_spec=grid_spec,
        out_shape=jax.ShapeDtypeStruct(xs.shape, jnp.int32),
        compiler_params=pltpu.CompilerParams(
            dimension_semantics=("arbitrary",), vmem_limit_bytes=VMEM_LIMIT),
        name="expert_ffn",
    )(block_e, n_used, xs, w_gate_up, bgu, w_down, bd, jnp.asarray(perm, BF16))


def _expert_layout(counts, n_blocks):
    pcounts = (counts + EXPERT_ROWS - 1) // EXPERT_ROWS * EXPERT_ROWS
    pend = jnp.cumsum(pcounts)
    n_used = pend[-1:] // EXPERT_ROWS
    block_start = jnp.minimum(jnp.arange(n_blocks, dtype=jnp.int32), n_used[0] - 1) * EXPERT_ROWS
    block_e = jnp.sum((pend[None, :] <= block_start[:, None]).astype(jnp.int32), axis=1)
    return pend - pcounts, jnp.minimum(block_e, N_EXPERTS - 1), n_used


def _slot_rows(route, pstart, ns, nt, tl):
    n = route.shape[0] * ns * tl
    fields = lambda lo: route[:, lo:lo + TOP_K, :].reshape(-1, nt, TOP_K, ns, tl)
    per_slot = lambda lo: jnp.transpose(fields(lo), (2, 0, 3, 1, 4)).reshape(TOP_K, n)
    e = per_slot(TOP_K).astype(jnp.int32)
    rank = per_slot(2 * TOP_K).astype(jnp.int32)
    hit = e[:, :, None] == jnp.arange(N_EXPERTS, dtype=jnp.int32)[None, None, :]
    rows = rank + jnp.sum(jnp.where(hit, pstart[None, None, :], 0), axis=-1)
    gates = jnp.transpose(fields(0), (0, 3, 1, 4, 2)).reshape(n, TOP_K)
    return rows, gates


SC_WINDOW = 128
SC_COLS = 256


def _sc_mesh():
    return plsc.VectorSubcoreMesh(core_axis_name="c", subcore_axis_name="s")


def _dispatch(h_p, h_s, rows_p, rows_s, m_pad):
    width = h_p.shape[1]

    @functools.partial(pl.kernel, mesh=_sc_mesh(),
                       out_type=jax.ShapeDtypeStruct((m_pad, width), h_p.dtype), scratch_types=[])
    def k(hp_hbm, hs_hbm, rp_hbm, rs_hbm, xs_hbm):
        def body(x_vmem, i_vmem):
            j = pl.program_id(1)
            for kk in range(TOP_K):
                pltpu.sync_copy(x_vmem, xs_hbm.at[i_vmem.at[kk], pl.ds(j * SC_COLS, SC_COLS)])

        for src, rows in ((hp_hbm, rp_hbm), (hs_hbm, rs_hbm)):
            pltpu.emit_pipeline(
                body,
                grid=(src.shape[0] // SC_WINDOW, width // SC_COLS),
                in_specs=[pl.BlockSpec((SC_WINDOW, SC_COLS), lambda i, j: (i, j)),
                          pl.BlockSpec((TOP_K, SC_WINDOW), lambda i, j: (0, i))],
                out_specs=[],
                core_axis_name=("c", "s"),
                dimension_semantics=(pltpu.PARALLEL, pltpu.ARBITRARY),
            )(src, rows)

    return k(h_p, h_s, rows_p, rows_s)


def _collect(yb, rows):
    width = yb.shape[1]

    @functools.partial(pl.kernel, mesh=_sc_mesh(),
                       out_type=jax.ShapeDtypeStruct((rows.shape[1], width), yb.dtype), scratch_types=[])
    def k(yb_hbm, r_hbm, o_hbm):
        def body(i_vmem, o_vmem):
            j = pl.program_id(1)
            pltpu.sync_copy(yb_hbm.at[i_vmem.at[0], pl.ds(j * SC_COLS, SC_COLS)], o_vmem)

        pltpu.emit_pipeline(
            body,
            grid=(rows.shape[1] // SC_WINDOW, width // SC_COLS),
            in_specs=[pl.BlockSpec((1, SC_WINDOW), lambda i, j: (0, i))],
            out_specs=[pl.BlockSpec((SC_WINDOW, SC_COLS), lambda i, j: (i, j))],
            core_axis_name=("c", "s"),
            dimension_semantics=(pltpu.PARALLEL, pltpu.ARBITRARY),
        )(r_hbm, o_hbm)

    return k(yb, rows)


def _combine_kernel(x_ref, yg_ref, route_ref, fw_ref, y_ref):
    half = D_MODEL // 2
    acc_lo = x_ref[:, :half]
    acc_hi = x_ref[:, half:]
    route = route_ref[...]
    for kk in range(TOP_K):
        y_lo, y_hi = _unpack_bf16(yg_ref[kk])
        gate = route[:, kk:kk + 1]
        acc_lo = acc_lo + y_lo * gate
        acc_hi = acc_hi + y_hi * gate
    ms = (jnp.sum(acc_lo * acc_lo, axis=-1, keepdims=True)
          + jnp.sum(acc_hi * acc_hi, axis=-1, keepdims=True)) * (1.0 / D_MODEL)
    scale = lax.rsqrt(ms + EPS)
    y_ref[:, :half] = acc_lo * scale * fw_ref[:, :half]
    y_ref[:, half:] = acc_hi * scale * fw_ref[:, half:]


def _combine_next_kernel(x_ref, yg_ref, route_ref, fw_ref, prev_ref, y_ref):
    del prev_ref
    _combine_kernel(x_ref, yg_ref, route_ref, fw_ref, y_ref)


def _combine(x2, yg, gates, final_w, y_prev, first_row):
    n_tok = x2.shape[0]
    blk0 = first_row // COMBINE_ROWS
    in_specs = [
        pl.BlockSpec((COMBINE_ROWS, D_MODEL), lambda i: (blk0 + i, 0)),
        pl.BlockSpec((TOP_K, COMBINE_ROWS, D_MODEL // 2), lambda i: (0, i, 0)),
        pl.BlockSpec((COMBINE_ROWS, TOP_K), lambda i: (blk0 + i, 0)),
        _const_spec((1, D_MODEL)),
    ]
    args = [x2, yg, gates, final_w]
    kern, aliases = _combine_kernel, {}
    if y_prev is not None:
        in_specs.append(pl.BlockSpec(memory_space=pl.ANY))
        args.append(y_prev)
        kern, aliases = _combine_next_kernel, {len(args) - 1: 0}
    return pl.pallas_call(
        kern,
        grid=(yg.shape[1] // COMBINE_ROWS,),
        in_specs=in_specs,
        out_specs=pl.BlockSpec((COMBINE_ROWS, D_MODEL), lambda i: (blk0 + i, 0)),
        out_shape=jax.ShapeDtypeStruct((n_tok, D_MODEL), F32),
        input_output_aliases=aliases,
        compiler_params=pltpu.CompilerParams(
            dimension_semantics=("arbitrary",), vmem_limit_bytes=VMEM_LIMIT),
        name="combine_%d_%d" % (n_tok, first_row),
    )(*args)


def _collect_combine(yb, x2, rows, gates, final_w, chunks):
    nc = x2.shape[0] // chunks
    y = None
    for c in range(chunks):
        yg = _collect(yb, rows[:, c * nc:(c + 1) * nc].reshape(1, TOP_K * nc))
        y = _combine(x2, yg.reshape(TOP_K, nc, D_MODEL // 2), gates, final_w, y, c * nc)
    return y


def kernel(x_prompt, x_sample, cache_mem_k, cache_mem_v, state_ret, state_pool, mem_prompt,
           norm_mix_w, w_in, ret_gn_w, w_pool, pool_scale, w_out, norm_mem_w, mem_norm_w,
           w_q_mem, w_kv_mem, w_o_mem, norm_ffn_w, router_w, router_b, w_gate_up, b_gate_up,
           w_down, b_down, final_norm_w):
    assert norm_mix_w.shape[0] == 1, "one layer"
    b, seq, _ = x_prompt.shape
    db, dseq, _ = x_sample.shape
    row = lambda a: a.reshape(1, -1)

    mk_p, mv_p = _mem_kv(mem_prompt, row(mem_norm_w[0]), w_kv_mem[0].astype(BF16))

    wts = (row(norm_mix_w[0]), w_in[0].astype(BF16), row(ret_gn_w[0]), w_pool[0].astype(BF16),
           row(pool_scale[0]), w_out[0].astype(BF16), row(norm_mem_w[0]), w_q_mem[0].astype(BF16),
           w_o_mem[0].astype(BF16), row(norm_ffn_w[0]), router_w[0].T.astype(BF16),
           jnp.broadcast_to(router_b[0][:, None], (N_EXPERTS, LANES)))

    zero_state = jnp.zeros((b, RET_HEADS, RET_DK, RET_DV), F32)
    zero_hist = jnp.zeros((b, HIST_ROWS, POOL_WIDTH), F32)
    no_counts = jnp.zeros((N_EXPERTS, LANES), F32)
    x2_p, h_p, route_p, ret_p, pool_p, cnt_p = _layer(
        x_prompt, 0, PROMPT_TILE, PROMPT_STREAMS, 1, zero_state, zero_hist, mk_p, mv_p, wts, no_counts)

    hist_s = jnp.concatenate([jnp.zeros((db, 1, POOL_WIDTH), F32), state_pool[0]], axis=1)
    x2_s, h_s, route_s, ret_s, pool_s, cnt_all = _layer(
        x_sample, PAST_LEN, dseq, SAMPLE_STREAMS, SAMPLE_STREAMS, state_ret[0], hist_s, cache_mem_k[0], cache_mem_v[0], wts, cnt_p)

    n_p, n_s = b * seq, db * dseq
    n_blocks = -(-((n_p + n_s) * TOP_K + N_EXPERTS * (EXPERT_ROWS - 1)) // EXPERT_ROWS)
    pstart, block_e, n_used = _expert_layout(cnt_all[:, 0].astype(jnp.int32), n_blocks)
    rows_p, gates_p = _slot_rows(route_p, pstart, PROMPT_STREAMS, seq // PROMPT_TILE, PROMPT_TILE)
    rows_s, gates_s = _slot_rows(route_s, pstart, SAMPLE_STREAMS, 1, dseq)

    half = D_MODEL // 2
    xs = _dispatch(h_p.reshape(n_p, half), h_s.reshape(n_s, half), rows_p, rows_s, n_blocks * EXPERT_ROWS)

    bgu = jnp.concatenate([b_gate_up[0][:, 0::2], b_gate_up[0][:, 1::2]], axis=-1).reshape(N_EXPERTS, 1, 2 * D_FF)
    yb = _expert_ffn(block_e, n_used, xs, w_gate_up[0], bgu, w_down[0],
                     b_down[0].reshape(N_EXPERTS, 1, D_MODEL), n_blocks)

    final_w = row(final_norm_w)
    y_s = _collect_combine(yb, x2_s.reshape(n_s, D_MODEL), rows_s, gates_s, final_w, 1)
    y_p = _collect_combine(yb, x2_p.reshape(n_p, D_MODEL), rows_p, gates_p, final_w, COMBINE_CHUNKS)
    y_p = y_p.reshape(b, seq, D_MODEL)
    y_s = y_s.reshape(db, dseq, D_MODEL)
    return (y_p, y_s, mk_p[None], mv_p[None], ret_p[None], pool_p[None], ret_s[None], pool_s[None])
```

```python
import functools

import numpy as np
import jax
import jax.numpy as jnp
from jax import lax
from jax.experimental import pallas as pl
from jax.experimental.pallas import tpu as pltpu
from jax.experimental.pallas import tpu_sc as plsc

D_MODEL = 1024
CHUNK = 64
PAST_LEN = 4096
RET_HEADS = 4
RET_DK = 128
RET_DV = 128
RET_QK = RET_HEADS * RET_DK
RET_VW = RET_HEADS * RET_DV
ROPE_BASE = 10000.0
POOL_WINDOWS = (2, 4, 8, 16)
POOL_GROUPS = 4
POOL_WIDTH = D_MODEL // 2
POOL_C = POOL_WIDTH // POOL_GROUPS
POOL_HIST = max(POOL_WINDOWS) - 1
HIST_ROWS = POOL_HIST + 1
IN_WIDTH = 2 * RET_QK + 2 * RET_VW + POOL_WIDTH
N_MEM = 256
MEM_HEADS = 4
MEM_HD = D_MODEL // MEM_HEADS
N_EXPERTS = 32
TOP_K = 4
D_FF = D_MODEL
SWIGLU_LIMIT = 7.0
SWIGLU_ALPHA = 1.702
EPS = 1e-5

LANES = 128
ROUTE_ROWS = 16
PROMPT_TILE = 256
PROMPT_STREAMS = 4
PROMPT_CHAIN = 2
SAMPLE_STREAMS = 4
EXPERT_ROWS = 512
KV_STREAMS = 2
COMBINE_ROWS = 512
COMBINE_CHUNKS = 4
VMEM_LIMIT = 56 * 1024 * 1024

BF16 = jnp.bfloat16
F32 = jnp.float32


def _rms(x, w):
    return x * lax.rsqrt(jnp.mean(x * x, axis=-1, keepdims=True) + EPS) * w


def _dot(a, b):
    return jnp.dot(a, b, preferred_element_type=F32)


def _dot_nt(a, b):
    return lax.dot_general(a, b, (((1,), (1,)), ((), ())), preferred_element_type=F32)


def _pack_bf16(x):
    bits = lax.bitcast_convert_type(x.astype(BF16).astype(F32), jnp.int32)
    w = x.shape[1] // 2
    return lax.shift_right_logical(bits[:, :w], 16) | (bits[:, w:] & -65536)


def _unpack_bf16(p):
    lo = lax.bitcast_convert_type(lax.shift_left(p, 16), F32)
    hi = lax.bitcast_convert_type(p & -65536, F32)
    return lo, hi


def _const_spec(shape):
    nd = len(shape)
    return pl.BlockSpec(shape, lambda *_: (0,) * nd, pipeline_mode=pl.Buffered(1))


def _mem_kv_kernel(mem_ref, nw_ref, w_ref, k_ref, v_ref, kb_ref, vb_ref):
    ns = mem_ref.shape[0]
    xn = _rms(mem_ref[...].reshape(ns * N_MEM, D_MODEL), nw_ref[...]).astype(BF16)
    kv = _dot(xn, w_ref[...])
    for hd in range(MEM_HEADS):
        lo = hd * MEM_HD
        k_ref[:, :, hd, :] = kv[:, lo:lo + MEM_HD].reshape(ns, N_MEM, MEM_HD)
        v_ref[:, :, hd, :] = kv[:, D_MODEL + lo:D_MODEL + lo + MEM_HD].reshape(ns, N_MEM, MEM_HD)
    kb_ref[...] = kv[:, :D_MODEL].astype(BF16).reshape(ns, N_MEM, D_MODEL)
    vb_ref[...] = kv[:, D_MODEL:].astype(BF16).reshape(ns, N_MEM, D_MODEL)


def _mem_kv(mem, mem_norm_w, w_kv_bf):
    b = mem.shape[0]
    out_spec = pl.BlockSpec((KV_STREAMS, N_MEM, MEM_HEADS, MEM_HD), lambda i: (i, 0, 0, 0))
    flat_spec = pl.BlockSpec((KV_STREAMS, N_MEM, D_MODEL), lambda i: (i, 0, 0))
    return pl.pallas_call(
        _mem_kv_kernel,
        grid=(b // KV_STREAMS,),
        in_specs=[
            pl.BlockSpec((KV_STREAMS, N_MEM, D_MODEL), lambda i: (i, 0, 0)),
            _const_spec((1, D_MODEL)),
            _const_spec((D_MODEL, 2 * D_MODEL)),
        ],
        out_specs=[out_spec, out_spec, flat_spec, flat_spec],
        out_shape=[jax.ShapeDtypeStruct((b, N_MEM, MEM_HEADS, MEM_HD), F32)] * 2
        + [jax.ShapeDtypeStruct((b, N_MEM, D_MODEL), BF16)] * 2,
        compiler_params=pltpu.CompilerParams(
            dimension_semantics=("arbitrary",), vmem_limit_bytes=VMEM_LIMIT),
        name="mem_kv",
    )(mem, mem_norm_w, w_kv_bf)


def _kv_flat_kernel(k_ref, v_ref, kb_ref, vb_ref):
    for hd in range(MEM_HEADS):
        kb_ref[:, :, hd * MEM_HD:(hd + 1) * MEM_HD] = k_ref[:, :, hd, :].astype(BF16)
        vb_ref[:, :, hd * MEM_HD:(hd + 1) * MEM_HD] = v_ref[:, :, hd, :].astype(BF16)


def _kv_flat(mem_k, mem_v):
    b = mem_k.shape[0]
    in_spec = pl.BlockSpec((KV_STREAMS, N_MEM, MEM_HEADS, MEM_HD), lambda i: (i, 0, 0, 0))
    out_spec = pl.BlockSpec((KV_STREAMS, N_MEM, D_MODEL), lambda i: (i, 0, 0))
    return pl.pallas_call(
        _kv_flat_kernel,
        grid=(b // KV_STREAMS,),
        in_specs=[in_spec, in_spec],
        out_specs=[out_spec, out_spec],
        out_shape=[jax.ShapeDtypeStruct((b, N_MEM, D_MODEL), BF16)] * 2,
        compiler_params=pltpu.CompilerParams(
            dimension_semantics=("arbitrary",), vmem_limit_bytes=VMEM_LIMIT),
        name="kv_flat",
    )(mem_k, mem_v)


def _layer_kernel(x_ref, cos_ref, sin_ref, dmat_ref, qdec_ref, kdec_ref, state0_ref, hist0_ref,
                  mk_ref, mv_ref, nmix_ref, win_ref, gnw_ref, wpool_ref, pscale_ref, wout_ref,
                  nmem_ref, wq_ref, wo_ref, nffn_ref, rw_ref, rb_ref, earlier_ref, cnt0_ref,
                  x2_ref, h_ref, route_ref, rstate_ref, pstate_ref, cnt_ref,
                  s_scr, ext_scr, cnt_scr, *, tl, chain, pos0, cdec):
    t = pl.program_id(1)

    @pl.when((pl.program_id(0) == 0) & (t == 0))
    def _():
        cnt_scr[...] = cnt0_ref[...]

    @pl.when(t == 0)
    def _():
        s_scr[...] = state0_ref[...]
        ext_scr[:, 0:HIST_ROWS, :] = hist0_ref[...]

    for c0 in range(0, x_ref.shape[0], chain):
        _layer_chain(c0, t, x_ref, cos_ref, sin_ref, dmat_ref, qdec_ref, kdec_ref, nmix_ref, win_ref, gnw_ref,
                     wpool_ref, pscale_ref, wout_ref, nmem_ref, wq_ref, wo_ref, nffn_ref, rw_ref, rb_ref,
                     earlier_ref, x2_ref, h_ref, route_ref, pstate_ref, s_scr, ext_scr, mk_ref, mv_ref, cnt_scr,
                     tl=tl, chain=chain, pos0=pos0, cdec=cdec)
    rstate_ref[...] = s_scr[...]
    cnt_ref[...] = cnt_scr[...]


def _layer_chain(c0, t, x_ref, cos_ref, sin_ref, dmat_ref, qdec_ref, kdec_ref, nmix_ref, win_ref, gnw_ref,
                 wpool_ref, pscale_ref, wout_ref, nmem_ref, wq_ref, wo_ref, nffn_ref, rw_ref, rb_ref,
                 earlier_ref, x2_ref, h_ref, route_ref, pstate_ref, s_scr, ext_scr, mk_ref, mv_ref, cnt_scr,
                 *, tl, chain, pos0, cdec):
    rows = chain * tl
    x = x_ref[c0:c0 + chain].reshape(rows, D_MODEL)
    proj = _dot(_rms(x, nmix_ref[...]).astype(BF16), win_ref[...])
    cos = cos_ref[...]
    sin = sin_ref[...]
    pos = (pos0 + t * tl + lax.broadcasted_iota(jnp.int32, (tl, POOL_C), 0)).astype(F32)

    mixes = []
    for sj in range(chain):
        si = c0 + sj
        pj = proj[sj * tl:(sj + 1) * tl]
        outs = []
        for hd in range(RET_HEADS):
            lo = hd * RET_DK
            q = pj[:, lo:lo + RET_DK]
            k = pj[:, RET_QK + lo:RET_QK + lo + RET_DK]
            v = pj[:, 2 * RET_QK + lo:2 * RET_QK + lo + RET_DV]
            g = pj[:, 2 * RET_QK + RET_VW + lo:2 * RET_QK + RET_VW + lo + RET_DV]
            qr = (q * cos + pltpu.roll(q, RET_DK // 2, 1) * sin) * (RET_DK ** -0.5)
            kr = k * cos + pltpu.roll(k, RET_DK // 2, 1) * sin
            vb = v.astype(BF16)
            s = _dot_nt(qr.astype(BF16), kr.astype(BF16)) * dmat_ref[hd]
            o = _dot(s.astype(BF16), vb)
            state = s_scr[si, hd]
            o = o + _dot((qr * qdec_ref[hd]).astype(BF16), state.astype(BF16))
            kd_t = jnp.transpose(kr * kdec_ref[hd]).astype(BF16)
            s_scr[si, hd] = cdec[hd] * state + _dot(kd_t, vb)
            mu = jnp.mean(o, axis=-1, keepdims=True)
            oc = o - mu
            var = jnp.mean(oc * oc, axis=-1, keepdims=True)
            on = oc * lax.rsqrt(var + EPS)
            outs.append(on * gnw_ref[:, lo:lo + RET_DV] * (g * jax.nn.sigmoid(g)))

        pin = pj[:, 2 * RET_QK + 2 * RET_VW:]
        ext_scr[si, HIST_ROWS:HIST_ROWS + tl, :] = pin
        pstate_ref[si] = pin[tl - POOL_HIST:, :]
        for gi, w in enumerate(POOL_WINDOWS):
            lo = gi * POOL_C
            wsum = ext_scr[si, :, lo:lo + POOL_C]
            shift = 1
            while shift < w:
                wsum = wsum + pltpu.roll(wsum, shift, 0)
                shift *= 2
            cnt = jnp.minimum(float(w), pos + 1.0)
            d = wsum[HIST_ROWS:, :] / cnt - pin[:, lo:lo + POOL_C]
            y = _dot(d.astype(BF16), wpool_ref[gi])
            outs.append(y * pscale_ref[:, lo:lo + POOL_C])
        ext_scr[si, 0:HIST_ROWS, :] = ext_scr[si, tl:tl + HIST_ROWS, :]
        mixes.append(jnp.concatenate(outs, axis=-1).astype(BF16))

    x1 = x + _dot(jnp.concatenate(mixes, axis=0), wout_ref[...])

    qm = _dot(_rms(x1, nmem_ref[...]).astype(BF16), wq_ref[...])
    atts = []
    for sj in range(chain):
        si = c0 + sj
        aouts = []
        for hd in range(MEM_HEADS):
            lo = hd * MEM_HD
            qh = qm[sj * tl:(sj + 1) * tl, lo:lo + MEM_HD].astype(BF16)
            s = _dot_nt(qh, mk_ref[si, :, lo:lo + MEM_HD]) * (MEM_HD ** -0.5)
            e = jnp.exp(s - jnp.max(s, axis=-1, keepdims=True))
            p = e / jnp.sum(e, axis=-1, keepdims=True)
            aouts.append(_dot(p.astype(BF16), mv_ref[si, :, lo:lo + MEM_HD]))
        atts.append(jnp.concatenate(aouts, axis=-1).astype(BF16))
    x2 = x1 + _dot(jnp.concatenate(atts, axis=0), wo_ref[...])
    x2_ref[c0:c0 + chain] = x2.reshape(chain, tl, D_MODEL)

    hn = _rms(x2, nffn_ref[...])
    h_ref[c0:c0 + chain] = _pack_bf16(hn).reshape(chain, tl, D_MODEL // 2)
    lane_tile = lambda a: a[:, :rows] if rows <= LANES else jnp.concatenate([a] * (rows // LANES), axis=1)
    logits = _dot_nt(rw_ref[...], hn.astype(BF16)) + lane_tile(rb_ref[...])
    eiota = lax.broadcasted_iota(jnp.int32, (N_EXPERTS, rows), 0)
    neg = jnp.finfo(F32).min
    vals, idxs = [], []
    for _k in range(TOP_K):
        m = jnp.max(logits, axis=0, keepdims=True)
        idx = jnp.min(jnp.where(logits == m, eiota, N_EXPERTS), axis=0, keepdims=True)
        vals.append(m)
        idxs.append(idx)
        logits = jnp.where(eiota == idx, neg, logits)
    exps = [jnp.exp(vk - vals[0]) for vk in vals]
    den = exps[0] + exps[1] + exps[2] + exps[3]

    onehots = [(eiota == idx).astype(F32) for idx in idxs]
    picked = onehots[0] + onehots[1] + onehots[2] + onehots[3]
    before = lane_tile(cnt_scr[...]) + _dot(picked.astype(BF16), earlier_ref[...])
    ranks = [jnp.sum(oh * before, axis=0, keepdims=True) for oh in onehots]
    cnt_scr[...] = cnt_scr[...] + jnp.sum(picked, axis=1, keepdims=True)

    rowi = lax.broadcasted_iota(jnp.int32, (ROUTE_ROWS, rows), 0)
    route = jnp.zeros((ROUTE_ROWS, rows), F32)
    for kk in range(TOP_K):
        route = jnp.where(rowi == kk, exps[kk] / den, route)
        route = jnp.where(rowi == TOP_K + kk, idxs[kk].astype(F32), route)
        route = jnp.where(rowi == 2 * TOP_K + kk, ranks[kk], route)
    route_ref[0, :, c0 * tl:c0 * tl + rows] = route


def _decay_tables(tl):
    hh = np.arange(RET_HEADS, dtype=np.float64)
    log_g = np.log1p(-np.exp2(-5.0 - hh))
    idx = np.arange(tl, dtype=np.float64)
    dist = np.abs(idx[:, None] - idx[None, :])
    visible = (idx[None, :] // CHUNK) <= (idx[:, None] // CHUNK)
    dmat = np.where(visible[None], np.exp(log_g[:, None, None] * dist[None]), 0.0)
    qdec = np.exp(log_g[:, None] * (idx[None, :] + 1.0))
    kdec = np.exp(log_g[:, None] * (tl - 1.0 - idx[None, :]))
    cdec = tuple(float(c) for c in np.exp(log_g * tl).astype(np.float32))
    bcast = lambda a: np.ascontiguousarray(np.broadcast_to(a[:, :, None], (RET_HEADS, tl, RET_DK)))
    return (jnp.asarray(dmat, F32), jnp.asarray(bcast(qdec), F32), jnp.asarray(bcast(kdec), F32), cdec)


def _rotary_tables(pos0, length):
    half = RET_DK // 2
    inv_freq = jnp.power(ROPE_BASE, -jnp.arange(half, dtype=F32) / half)
    ang = (pos0 + jnp.arange(length, dtype=jnp.int32)).astype(F32)[:, None] * inv_freq[None, :]
    cos, sin = jnp.cos(ang), jnp.sin(ang)
    return jnp.concatenate([cos, cos], axis=-1), jnp.concatenate([-sin, sin], axis=-1)


def _layer(x, pos0, tl, ns, chain, state0, hist0, mk, mv, wts, cnt0):
    b, length, _ = x.shape
    nt = length // tl
    rows = ns * tl
    cos, sin = _rotary_tables(pos0, length)
    dmat, qdec, kdec, cdec = _decay_tables(tl)
    crows = chain * tl
    earlier = jnp.asarray(np.triu(np.ones((crows, crows), np.float32), 1), BF16)
    kern = functools.partial(_layer_kernel, tl=tl, chain=chain, pos0=pos0, cdec=cdec)
    tok = lambda width: pl.BlockSpec((ns, tl, width), lambda i, j: (i, j, 0))
    per_stream = lambda *shape: pl.BlockSpec((ns,) + shape, lambda i, j: (i,) + (0,) * len(shape))
    per_stream_in = lambda *shape: pl.BlockSpec((ns,) + shape, lambda i, j: (i,) + (0,) * len(shape),
                                                pipeline_mode=pl.Buffered(1))
    in_specs = [
        tok(D_MODEL),
        pl.BlockSpec((tl, RET_DK), lambda i, j: (j, 0)),
        pl.BlockSpec((tl, RET_DK), lambda i, j: (j, 0)),
        _const_spec((RET_HEADS, tl, tl)),
        _const_spec((RET_HEADS, tl, RET_DK)),
        _const_spec((RET_HEADS, tl, RET_DK)),
        per_stream_in(RET_HEADS, RET_DK, RET_DV),
        per_stream_in(HIST_ROWS, POOL_WIDTH),
        per_stream_in(N_MEM, D_MODEL),
        per_stream_in(N_MEM, D_MODEL),
    ] + [_const_spec(w.shape) for w in wts] + [_const_spec((crows, crows)), _const_spec((N_EXPERTS, LANES))]
    out_specs = [
        tok(D_MODEL), tok(D_MODEL // 2),
        pl.BlockSpec((1, ROUTE_ROWS, rows), lambda i, j: (i * nt + j, 0, 0)),
        per_stream(RET_HEADS, RET_DK, RET_DV),
        per_stream(POOL_HIST, POOL_WIDTH),
        pl.BlockSpec((N_EXPERTS, LANES), lambda i, j: (0, 0)),
    ]
    out_shape = [
        jax.ShapeDtypeStruct((b, length, D_MODEL), F32),
        jax.ShapeDtypeStruct((b, length, D_MODEL // 2), jnp.int32),
        jax.ShapeDtypeStruct((b // ns * nt, ROUTE_ROWS, rows), F32),
        jax.ShapeDtypeStruct((b, RET_HEADS, RET_DK, RET_DV), F32),
        jax.ShapeDtypeStruct((b, POOL_HIST, POOL_WIDTH), F32),
        jax.ShapeDtypeStruct((N_EXPERTS, LANES), F32),
    ]
    scratch = [
        pltpu.VMEM((ns, RET_HEADS, RET_DK, RET_DV), F32),
        pltpu.VMEM((ns, HIST_ROWS + tl, POOL_WIDTH), F32),
        pltpu.VMEM((N_EXPERTS, LANES), F32),
    ]
    return pl.pallas_call(
        kern,
        grid=(b // ns, nt),
        in_specs=in_specs,
        out_specs=out_specs,
        out_shape=out_shape,
        scratch_shapes=scratch,
        compiler_params=pltpu.CompilerParams(
            dimension_semantics=("arbitrary", "arbitrary"), vmem_limit_bytes=VMEM_LIMIT),
        name="layer_tl%d" % tl,
    )(x, cos, sin, dmat, qdec, kdec, state0, hist0, mk, mv, *wts, earlier, cnt0)


SPLIT_COLS = 2 * LANES


def _expert_kernel(be_ref, used_ref, x_ref, wgu_ref, bgu_ref, wd_ref, bd_ref, perm_ref, y_ref,
                   wgu_scr, wd_scr):
    i = pl.program_id(0)
    in_use = i < used_ref[0]

    @pl.when(in_use & ((i == 0) | (be_ref[i] != be_ref[jnp.maximum(i - 1, 0)])))
    def _():
        perm = perm_ref[...]
        for c in range(2 * D_FF // SPLIT_COLS):
            wc = wgu_ref[0, :, c * SPLIT_COLS:(c + 1) * SPLIT_COLS].astype(BF16)
            pc = _dot(wc, perm).astype(BF16)
            wgu_scr[:, c * LANES:(c + 1) * LANES] = pc[:, :LANES]
            wgu_scr[:, D_FF + c * LANES:D_FF + (c + 1) * LANES] = pc[:, LANES:]
        wd_scr[...] = wd_ref[0].astype(BF16)

    @pl.when(in_use)
    def _():
        half = D_MODEL // 2
        x_lo, x_hi = _unpack_bf16(x_ref[...])
        gu = (_dot(x_lo.astype(BF16), wgu_scr[:half, :]) + _dot(x_hi.astype(BF16), wgu_scr[half:, :])
              + bgu_ref[0])
        gate = jnp.minimum(gu[:, :D_FF], SWIGLU_LIMIT)
        up = jnp.clip(gu[:, D_FF:], -SWIGLU_LIMIT, SWIGLU_LIMIT)
        act = (up + 1.0) * gate * jax.nn.sigmoid(SWIGLU_ALPHA * gate)
        y_ref[...] = _pack_bf16(_dot(act.astype(BF16), wd_scr[...]) + bd_ref[0])


def _expert_ffn(block_e, n_used, xs, w_gate_up, bgu, w_down, bd, n_blocks):
    perm = np.zeros((SPLIT_COLS, SPLIT_COLS), np.float32)
    j = np.arange(LANES)
    perm[2 * j, j] = 1.0
    perm[2 * j + 1, LANES + j] = 1.0
    blk = lambda i, be, used: (jnp.minimum(i, used[0] - 1), 0)
    per_expert = lambda i, be, used: (be[i], 0, 0)
    grid_spec = pltpu.PrefetchScalarGridSpec(
        num_scalar_prefetch=2,
        grid=(n_blocks,),
        in_specs=[
            pl.BlockSpec((EXPERT_ROWS, D_MODEL // 2), blk),
            pl.BlockSpec((1, D_MODEL, 2 * D_FF), per_expert),
            pl.BlockSpec((1, 1, 2 * D_FF), per_expert),
            pl.BlockSpec((1, D_FF, D_MODEL), per_expert),
            pl.BlockSpec((1, 1, D_MODEL), per_expert),
            _const_spec((SPLIT_COLS, SPLIT_COLS)),
        ],
        out_specs=pl.BlockSpec((EXPERT_ROWS, D_MODEL // 2), blk),
        scratch_shapes=[pltpu.VMEM((D_MODEL, 2 * D_FF), BF16), pltpu.VMEM((D_FF, D_MODEL), BF16)],
    )
    return pl.pallas_call(
        _expert_kernel,
        grid_spec=grid_spec,
        out_shape=jax.ShapeDtypeStruct(xs.shape, jnp.int32),
        compiler_params=pltpu.CompilerParams(
            dimension_semantics=("arbitrary",), vmem_limit_bytes=VMEM_LIMIT),
        name="expert_ffn",
    )(block_e, n_used, xs, w_gate_up, bgu, w_down, bd, jnp.asarray(perm, BF16))


def _expert_layout(counts, n_blocks):
    pcounts = (counts + EXPERT_ROWS - 1) // EXPERT_ROWS * EXPERT_ROWS
    pend = jnp.cumsum(pcounts)
    n_used = pend[-1:] // EXPERT_ROWS
    block_start = jnp.minimum(jnp.arange(n_blocks, dtype=jnp.int32), n_used[0] - 1) * EXPERT_ROWS
    block_e = jnp.sum((pend[None, :] <= block_start[:, None]).astype(jnp.int32), axis=1)
    return pend - pcounts, jnp.minimum(block_e, N_EXPERTS - 1), n_used


def _slot_rows(route, pstart, ns, nt, tl):
    n = route.shape[0] * ns * tl
    fields = lambda lo: route[:, lo:lo + TOP_K, :].reshape(-1, nt, TOP_K, ns, tl)
    per_slot = lambda lo: jnp.transpose(fields(lo), (2, 0, 3, 1, 4)).reshape(TOP_K, n)
    e = per_slot(TOP_K).astype(jnp.int32)
    rank = per_slot(2 * TOP_K).astype(jnp.int32)
    hit = e[:, :, None] == jnp.arange(N_EXPERTS, dtype=jnp.int32)[None, None, :]
    rows = rank + jnp.sum(jnp.where(hit, pstart[None, None, :], 0), axis=-1)
    gates = jnp.transpose(fields(0), (0, 3, 1, 4, 2)).reshape(n, TOP_K)
    return rows, gates


SC_WINDOW = 128
SC_COLS = 256


def _sc_mesh():
    return plsc.VectorSubcoreMesh(core_axis_name="c", subcore_axis_name="s")


def _dispatch(h_p, h_s, rows_p, rows_s, m_pad):
    width = h_p.shape[1]

    @functools.partial(pl.kernel, mesh=_sc_mesh(),
                       out_type=jax.ShapeDtypeStruct((m_pad, width), h_p.dtype), scratch_types=[])
    def k(hp_hbm, hs_hbm, rp_hbm, rs_hbm, xs_hbm):
        def body(x_vmem, i_vmem):
            j = pl.program_id(1)
            for kk in range(TOP_K):
                pltpu.sync_copy(x_vmem, xs_hbm.at[i_vmem.at[kk], pl.ds(j * SC_COLS, SC_COLS)])

        for src, rows in ((hp_hbm, rp_hbm), (hs_hbm, rs_hbm)):
            pltpu.emit_pipeline(
                body,
                grid=(src.shape[0] // SC_WINDOW, width // SC_COLS),
                in_specs=[pl.BlockSpec((SC_WINDOW, SC_COLS), lambda i, j: (i, j)),
                          pl.BlockSpec((TOP_K, SC_WINDOW), lambda i, j: (0, i))],
                out_specs=[],
                core_axis_name=("c", "s"),
                dimension_semantics=(pltpu.PARALLEL, pltpu.ARBITRARY),
            )(src, rows)

    return k(h_p, h_s, rows_p, rows_s)


def _collect(yb, rows):
    width = yb.shape[1]

    @functools.partial(pl.kernel, mesh=_sc_mesh(),
                       out_type=jax.ShapeDtypeStruct((rows.shape[1], width), yb.dtype), scratch_types=[])
    def k(yb_hbm, r_hbm, o_hbm):
        def body(i_vmem, o_vmem):
            j = pl.program_id(1)
            pltpu.sync_copy(yb_hbm.at[i_vmem.at[0], pl.ds(j * SC_COLS, SC_COLS)], o_vmem)

        pltpu.emit_pipeline(
            body,
            grid=(rows.shape[1] // SC_WINDOW, width // SC_COLS),
            in_specs=[pl.BlockSpec((1, SC_WINDOW), lambda i, j: (0, i))],
            out_specs=[pl.BlockSpec((SC_WINDOW, SC_COLS), lambda i, j: (i, j))],
            core_axis_name=("c", "s"),
            dimension_semantics=(pltpu.PARALLEL, pltpu.ARBITRARY),
        )(r_hbm, o_hbm)

    return k(yb, rows)


def _combine_kernel(x_ref, yg_ref, route_ref, fw_ref, y_ref):
    half = D_MODEL // 2
    acc_lo = x_ref[:, :half]
    acc_hi = x_ref[:, half:]
    route = route_ref[...]
    for kk in range(TOP_K):
        y_lo, y_hi = _unpack_bf16(yg_ref[kk])
        gate = route[:, kk:kk + 1]
        acc_lo = acc_lo + y_lo * gate
        acc_hi = acc_hi + y_hi * gate
    ms = (jnp.sum(acc_lo * acc_lo, axis=-1, keepdims=True)
          + jnp.sum(acc_hi * acc_hi, axis=-1, keepdims=True)) * (1.0 / D_MODEL)
    scale = lax.rsqrt(ms + EPS)
    y_ref[:, :half] = acc_lo * scale * fw_ref[:, :half]
    y_ref[:, half:] = acc_hi * scale * fw_ref[:, half:]


def _combine_next_kernel(x_ref, yg_ref, route_ref, fw_ref, prev_ref, y_ref):
    del prev_ref
    _combine_kernel(x_ref, yg_ref, route_ref, fw_ref, y_ref)


def _combine(x2, yg, gates, final_w, y_prev, first_row):
    n_tok = x2.shape[0]
    blk0 = first_row // COMBINE_ROWS
    in_specs = [
        pl.BlockSpec((COMBINE_ROWS, D_MODEL), lambda i: (blk0 + i, 0)),
        pl.BlockSpec((TOP_K, COMBINE_ROWS, D_MODEL // 2), lambda i: (0, i, 0)),
        pl.BlockSpec((COMBINE_ROWS, TOP_K), lambda i: (blk0 + i, 0)),
        _const_spec((1, D_MODEL)),
    ]
    args = [x2, yg, gates, final_w]
    kern, aliases = _combine_kernel, {}
    if y_prev is not None:
        in_specs.append(pl.BlockSpec(memory_space=pl.ANY))
        args.append(y_prev)
        kern, aliases = _combine_next_kernel, {len(args) - 1: 0}
    return pl.pallas_call(
        kern,
        grid=(yg.shape[1] // COMBINE_ROWS,),
        in_specs=in_specs,
        out_specs=pl.BlockSpec((COMBINE_ROWS, D_MODEL), lambda i: (blk0 + i, 0)),
        out_shape=jax.ShapeDtypeStruct((n_tok, D_MODEL), F32),
        input_output_aliases=aliases,
        compiler_params=pltpu.CompilerParams(
            dimension_semantics=("arbitrary",), vmem_limit_bytes=VMEM_LIMIT),
        name="combine_%d_%d" % (n_tok, first_row),
    )(*args)


def _collect_combine(yb, x2, rows, gates, final_w, chunks):
    nc = x2.shape[0] // chunks
    y = None
    for c in range(chunks):
        yg = _collect(yb, rows[:, c * nc:(c + 1) * nc].reshape(1, TOP_K * nc))
        y = _combine(x2, yg.reshape(TOP_K, nc, D_MODEL // 2), gates, final_w, y, c * nc)
    return y


def kernel(x_prompt, x_sample, cache_mem_k, cache_mem_v, state_ret, state_pool, mem_prompt,
           norm_mix_w, w_in, ret_gn_w, w_pool, pool_scale, w_out, norm_mem_w, mem_norm_w,
           w_q_mem, w_kv_mem, w_o_mem, norm_ffn_w, router_w, router_b, w_gate_up, b_gate_up,
           w_down, b_down, final_norm_w):
    assert norm_mix_w.shape[0] == 1, "one layer"
    b, seq, _ = x_prompt.shape
    db, dseq, _ = x_sample.shape
    row = lambda a: a.reshape(1, -1)

    mk_p, mv_p, mkb_p, mvb_p = _mem_kv(mem_prompt, row(mem_norm_w[0]), w_kv_mem[0].astype(BF16))
    mkb_s, mvb_s = _kv_flat(cache_mem_k[0], cache_mem_v[0])

    wts = (row(norm_mix_w[0]), w_in[0].astype(BF16), row(ret_gn_w[0]), w_pool[0].astype(BF16),
           row(pool_scale[0]), w_out[0].astype(BF16), row(norm_mem_w[0]), w_q_mem[0].astype(BF16),
           w_o_mem[0].astype(BF16), row(norm_ffn_w[0]), router_w[0].T.astype(BF16),
           jnp.broadcast_to(router_b[0][:, None], (N_EXPERTS, LANES)))

    zero_state = jnp.zeros((b, RET_HEADS, RET_DK, RET_DV), F32)
    zero_hist = jnp.zeros((b, HIST_ROWS, POOL_WIDTH), F32)
    no_counts = jnp.zeros((N_EXPERTS, LANES), F32)
    x2_p, h_p, route_p, ret_p, pool_p, cnt_p = _layer(
        x_prompt, 0, PROMPT_TILE, PROMPT_STREAMS, PROMPT_CHAIN, zero_state, zero_hist, mkb_p, mvb_p, wts, no_counts)

    hist_s = jnp.concatenate([jnp.zeros((db, 1, POOL_WIDTH), F32), state_pool[0]], axis=1)
    x2_s, h_s, route_s, ret_s, pool_s, cnt_all = _layer(
        x_sample, PAST_LEN, dseq, SAMPLE_STREAMS, SAMPLE_STREAMS, state_ret[0], hist_s, mkb_s, mvb_s, wts, cnt_p)

    n_p, n_s = b * seq, db * dseq
    n_blocks = -(-((n_p + n_s) * TOP_K + N_EXPERTS * (EXPERT_ROWS - 1)) // EXPERT_ROWS)
    pstart, block_e, n_used = _expert_layout(cnt_all[:, 0].astype(jnp.int32), n_blocks)
    rows_p, gates_p = _slot_rows(route_p, pstart, PROMPT_STREAMS, seq // PROMPT_TILE, PROMPT_TILE)
    rows_s, gates_s = _slot_rows(route_s, pstart, SAMPLE_STREAMS, 1, dseq)

    half = D_MODEL // 2
    xs = _dispatch(h_p.reshape(n_p, half), h_s.reshape(n_s, half), rows_p, rows_s, n_blocks * EXPERT_ROWS)

    bgu = jnp.concatenate([b_gate_up[0][:, 0::2], b_gate_up[0][:, 1::2]], axis=-1).reshape(N_EXPERTS, 1, 2 * D_FF)
    yb = _expert_ffn(block_e, n_used, xs, w_gate_up[0], bgu, w_down[0],
                     b_down[0].reshape(N_EXPERTS, 1, D_MODEL), n_blocks)

    final_w = row(final_norm_w)
    y_s = _collect_combine(yb, x2_s.reshape(n_s, D_MODEL), rows_s, gates_s, final_w, 1)
    y_p = _collect_combine(yb, x2_p.reshape(n_p, D_MODEL), rows_p, gates_p, final_w, COMBINE_CHUNKS)
    y_p = y_p.reshape(b, seq, D_MODEL)
    y_s = y_s.reshape(db, dseq, D_MODEL)
    return (y_p, y_s, mk_p[None], mv_p[None], ret_p[None], pool_p[None], ret_s[None], pool_s[None])
```

```python
import functools

import numpy as np
import jax
import jax.numpy as jnp
from jax import lax
from jax.experimental import pallas as pl
from jax.experimental.pallas import tpu as pltpu
from jax.experimental.pallas import tpu_sc as plsc

D_MODEL = 1024
CHUNK = 64
PAST_LEN = 4096
RET_HEADS = 4
RET_DK = 128
RET_DV = 128
RET_QK = RET_HEADS * RET_DK
RET_VW = RET_HEADS * RET_DV
ROPE_BASE = 10000.0
POOL_WINDOWS = (2, 4, 8, 16)
POOL_GROUPS = 4
POOL_WIDTH = D_MODEL // 2
POOL_C = POOL_WIDTH // POOL_GROUPS
POOL_HIST = max(POOL_WINDOWS) - 1
HIST_ROWS = POOL_HIST + 1
IN_WIDTH = 2 * RET_QK + 2 * RET_VW + POOL_WIDTH
N_MEM = 256
MEM_HEADS = 4
MEM_HD = D_MODEL // MEM_HEADS
N_EXPERTS = 32
TOP_K = 4
D_FF = D_MODEL
SWIGLU_LIMIT = 7.0
SWIGLU_ALPHA = 1.702
EPS = 1e-5

LANES = 128
ROUTE_ROWS = 16
PROMPT_TILE = 256
PROMPT_STREAMS = 4
PROMPT_CHAIN = 2
SAMPLE_STREAMS = 8
EXPERT_ROWS = 512
KV_STREAMS = 2
COMBINE_ROWS = 512
COMBINE_CHUNKS = 4
VMEM_LIMIT = 56 * 1024 * 1024

BF16 = jnp.bfloat16
F32 = jnp.float32


def _rms(x, w):
    return x * lax.rsqrt(jnp.mean(x * x, axis=-1, keepdims=True) + EPS) * w


def _dot(a, b):
    return jnp.dot(a, b, preferred_element_type=F32)


def _rms_dot(x, w_norm, w_mat):
    inv = lax.rsqrt(jnp.mean(x * x, axis=-1, keepdims=True) + EPS)
    return _dot((x * w_norm).astype(BF16), w_mat) * inv


def _dot_nt(a, b):
    return lax.dot_general(a, b, (((1,), (1,)), ((), ())), preferred_element_type=F32)


def _pack_bf16(x):
    bits = lax.bitcast_convert_type(x.astype(BF16).astype(F32), jnp.int32)
    w = x.shape[1] // 2
    return lax.shift_right_logical(bits[:, :w], 16) | (bits[:, w:] & -65536)


def _unpack_bf16(p):
    lo = lax.bitcast_convert_type(lax.shift_left(p, 16), F32)
    hi = lax.bitcast_convert_type(p & -65536, F32)
    return lo, hi


def _const_spec(shape):
    nd = len(shape)
    return pl.BlockSpec(shape, lambda *_: (0,) * nd, pipeline_mode=pl.Buffered(1))


def _mem_kv_kernel(mem_ref, nw_ref, w_ref, k_ref, v_ref, kb_ref, vb_ref):
    ns = mem_ref.shape[0]
    xn = _rms(mem_ref[...].reshape(ns * N_MEM, D_MODEL), nw_ref[...]).astype(BF16)
    kv = _dot(xn, w_ref[...])
    for hd in range(MEM_HEADS):
        lo = hd * MEM_HD
        k_ref[:, :, hd, :] = kv[:, lo:lo + MEM_HD].reshape(ns, N_MEM, MEM_HD)
        v_ref[:, :, hd, :] = kv[:, D_MODEL + lo:D_MODEL + lo + MEM_HD].reshape(ns, N_MEM, MEM_HD)
    kb_ref[...] = kv[:, :D_MODEL].astype(BF16).reshape(ns, N_MEM, D_MODEL)
    vb_ref[...] = kv[:, D_MODEL:].astype(BF16).reshape(ns, N_MEM, D_MODEL)


def _mem_kv(mem, mem_norm_w, w_kv_bf):
    b = mem.shape[0]
    out_spec = pl.BlockSpec((KV_STREAMS, N_MEM, MEM_HEADS, MEM_HD), lambda i: (i, 0, 0, 0))
    flat_spec = pl.BlockSpec((KV_STREAMS, N_MEM, D_MODEL), lambda i: (i, 0, 0))
    return pl.pallas_call(
        _mem_kv_kernel,
        grid=(b // KV_STREAMS,),
        in_specs=[
            pl.BlockSpec((KV_STREAMS, N_MEM, D_MODEL), lambda i: (i, 0, 0)),
            _const_spec((1, D_MODEL)),
            _const_spec((D_MODEL, 2 * D_MODEL)),
        ],
        out_specs=[out_spec, out_spec, flat_spec, flat_spec],
        out_shape=[jax.ShapeDtypeStruct((b, N_MEM, MEM_HEADS, MEM_HD), F32)] * 2
        + [jax.ShapeDtypeStruct((b, N_MEM, D_MODEL), BF16)] * 2,
        compiler_params=pltpu.CompilerParams(
            dimension_semantics=("arbitrary",), vmem_limit_bytes=VMEM_LIMIT),
        name="mem_kv",
    )(mem, mem_norm_w, w_kv_bf)


def _kv_flat_kernel(k_ref, v_ref, kb_ref, vb_ref):
    for hd in range(MEM_HEADS):
        kb_ref[:, :, hd * MEM_HD:(hd + 1) * MEM_HD] = k_ref[:, :, hd, :].astype(BF16)
        vb_ref[:, :, hd * MEM_HD:(hd + 1) * MEM_HD] = v_ref[:, :, hd, :].astype(BF16)


def _kv_flat(mem_k, mem_v):
    b = mem_k.shape[0]
    in_spec = pl.BlockSpec((KV_STREAMS, N_MEM, MEM_HEADS, MEM_HD), lambda i: (i, 0, 0, 0))
    out_spec = pl.BlockSpec((KV_STREAMS, N_MEM, D_MODEL), lambda i: (i, 0, 0))
    return pl.pallas_call(
        _kv_flat_kernel,
        grid=(b // KV_STREAMS,),
        in_specs=[in_spec, in_spec],
        out_specs=[out_spec, out_spec],
        out_shape=[jax.ShapeDtypeStruct((b, N_MEM, D_MODEL), BF16)] * 2,
        compiler_params=pltpu.CompilerParams(
            dimension_semantics=("arbitrary",), vmem_limit_bytes=VMEM_LIMIT),
        name="kv_flat",
    )(mem_k, mem_v)


def _layer_kernel(x_ref, cos_ref, sin_ref, dmat_ref, qdec_ref, kdec_ref, state0_ref, hist0_ref,
                  mk_ref, mv_ref, nmix_ref, win_ref, gnw_ref, wpool_ref, pscale_ref, wout_ref,
                  nmem_ref, wq_ref, wo_ref, nffn_ref, rw_ref, rb_ref, earlier_ref, cnt0_ref,
                  x2_ref, h_ref, route_ref, rstate_ref, pstate_ref, cnt_ref,
                  s_scr, ext_scr, cnt_scr, *, tl, chain, pos0, cdec):
    t = pl.program_id(1)

    @pl.when((pl.program_id(0) == 0) & (t == 0))
    def _():
        cnt_scr[...] = cnt0_ref[...]

    @pl.when(t == 0)
    def _():
        s_scr[...] = state0_ref[...]
        ext_scr[:, 0:HIST_ROWS, :] = hist0_ref[...]

    for c0 in range(0, x_ref.shape[0], chain):
        _layer_chain(c0, t, x_ref, cos_ref, sin_ref, dmat_ref, qdec_ref, kdec_ref, nmix_ref, win_ref, gnw_ref,
                     wpool_ref, pscale_ref, wout_ref, nmem_ref, wq_ref, wo_ref, nffn_ref, rw_ref, rb_ref,
                     earlier_ref, x2_ref, h_ref, route_ref, pstate_ref, s_scr, ext_scr, mk_ref, mv_ref, cnt_scr,
                     tl=tl, chain=chain, pos0=pos0, cdec=cdec)
    rstate_ref[...] = s_scr[...]
    cnt_ref[...] = cnt_scr[...]


def _layer_chain(c0, t, x_ref, cos_ref, sin_ref, dmat_ref, qdec_ref, kdec_ref, nmix_ref, win_ref, gnw_ref,
                 wpool_ref, pscale_ref, wout_ref, nmem_ref, wq_ref, wo_ref, nffn_ref, rw_ref, rb_ref,
                 earlier_ref, x2_ref, h_ref, route_ref, pstate_ref, s_scr, ext_scr, mk_ref, mv_ref, cnt_scr,
                 *, tl, chain, pos0, cdec):
    rows = chain * tl
    x = x_ref[c0:c0 + chain].reshape(rows, D_MODEL)
    proj = _rms_dot(x, nmix_ref[...], win_ref[...])
    cos = cos_ref[...]
    sin = sin_ref[...]
    pos = (pos0 + t * tl + lax.broadcasted_iota(jnp.int32, (tl, POOL_C), 0)).astype(F32)

    mixes = []
    for sj in range(chain):
        si = c0 + sj
        pj = proj[sj * tl:(sj + 1) * tl]
        outs = []
        for hd in range(RET_HEADS):
            lo = hd * RET_DK
            q = pj[:, lo:lo + RET_DK]
            k = pj[:, RET_QK + lo:RET_QK + lo + RET_DK]
            v = pj[:, 2 * RET_QK + lo:2 * RET_QK + lo + RET_DV]
            g = pj[:, 2 * RET_QK + RET_VW + lo:2 * RET_QK + RET_VW + lo + RET_DV]
            qr = (q * cos + pltpu.roll(q, RET_DK // 2, 1) * sin) * (RET_DK ** -0.5)
            kr = k * cos + pltpu.roll(k, RET_DK // 2, 1) * sin
            vb = v.astype(BF16)
            s = _dot_nt(qr.astype(BF16), kr.astype(BF16)) * dmat_ref[hd]
            o = _dot(s.astype(BF16), vb)
            state = s_scr[si, hd]
            o = o + _dot((qr * qdec_ref[hd]).astype(BF16), state.astype(BF16))
            kd_t = jnp.transpose(kr * kdec_ref[hd]).astype(BF16)
            s_scr[si, hd] = cdec[hd] * state + _dot(kd_t, vb)
            mu = jnp.mean(o, axis=-1, keepdims=True)
            oc = o - mu
            var = jnp.mean(oc * oc, axis=-1, keepdims=True)
            on = oc * lax.rsqrt(var + EPS)
            outs.append(on * gnw_ref[:, lo:lo + RET_DV] * (g * jax.nn.sigmoid(g)))

        pin = pj[:, 2 * RET_QK + 2 * RET_VW:]
        ext_scr[si, HIST_ROWS:HIST_ROWS + tl, :] = pin
        pstate_ref[si] = pin[tl - POOL_HIST:, :]
        for gi, w in enumerate(POOL_WINDOWS):
            lo = gi * POOL_C
            wsum = ext_scr[si, :, lo:lo + POOL_C]
            shift = 1
            while shift < w:
                wsum = wsum + pltpu.roll(wsum, shift, 0)
                shift *= 2
            cnt = jnp.minimum(float(w), pos + 1.0)
            d = wsum[HIST_ROWS:, :] / cnt - pin[:, lo:lo + POOL_C]
            y = _dot(d.astype(BF16), wpool_ref[gi])
            outs.append(y * pscale_ref[:, lo:lo + POOL_C])
        ext_scr[si, 0:HIST_ROWS, :] = ext_scr[si, tl:tl + HIST_ROWS, :]
        mixes.append(jnp.concatenate(outs, axis=-1).astype(BF16))

    x1 = x + _dot(jnp.concatenate(mixes, axis=0), wout_ref[...])

    qm = _rms_dot(x1, nmem_ref[...], wq_ref[...])
    atts = []
    for sj in range(chain):
        si = c0 + sj
        aouts = []
        for hd in range(MEM_HEADS):
            lo = hd * MEM_HD
            qh = qm[sj * tl:(sj + 1) * tl, lo:lo + MEM_HD].astype(BF16)
            s = _dot_nt(qh, mk_ref[si, :, lo:lo + MEM_HD]) * (MEM_HD ** -0.5)
            e = jnp.exp(s - jnp.max(s, axis=-1, keepdims=True))
            p = e / jnp.sum(e, axis=-1, keepdims=True)
            aouts.append(_dot(p.astype(BF16), mv_ref[si, :, lo:lo + MEM_HD]))
        atts.append(jnp.concatenate(aouts, axis=-1).astype(BF16))
    x2 = x1 + _dot(jnp.concatenate(atts, axis=0), wo_ref[...])
    x2_ref[c0:c0 + chain] = x2.reshape(chain, tl, D_MODEL)

    hn = _rms(x2, nffn_ref[...])
    h_ref[c0:c0 + chain] = _pack_bf16(hn).reshape(chain, tl, D_MODEL // 2)
    lane_tile = lambda a: a[:, :rows] if rows <= LANES else jnp.concatenate([a] * (rows // LANES), axis=1)
    logits = _dot_nt(rw_ref[...], hn.astype(BF16)) + lane_tile(rb_ref[...])
    eiota = lax.broadcasted_iota(jnp.int32, (N_EXPERTS, rows), 0)
    neg = jnp.finfo(F32).min
    vals, idxs = [], []
    for _k in range(TOP_K):
        m = jnp.max(logits, axis=0, keepdims=True)
        idx = jnp.min(jnp.where(logits == m, eiota, N_EXPERTS), axis=0, keepdims=True)
        vals.append(m)
        idxs.append(idx)
        logits = jnp.where(eiota == idx, neg, logits)
    exps = [jnp.exp(vk - vals[0]) for vk in vals]
    den = exps[0] + exps[1] + exps[2] + exps[3]

    onehots = [(eiota == idx).astype(F32) for idx in idxs]
    picked = onehots[0] + onehots[1] + onehots[2] + onehots[3]
    before = lane_tile(cnt_scr[...]) + _dot(picked.astype(BF16), earlier_ref[...])
    ranks = [jnp.sum(oh * before, axis=0, keepdims=True) for oh in onehots]
    cnt_scr[...] = cnt_scr[...] + jnp.sum(picked, axis=1, keepdims=True)

    rowi = lax.broadcasted_iota(jnp.int32, (ROUTE_ROWS, rows), 0)
    route = jnp.zeros((ROUTE_ROWS, rows), F32)
    for kk in range(TOP_K):
        route = jnp.where(rowi == kk, exps[kk] / den, route)
        route = jnp.where(rowi == TOP_K + kk, idxs[kk].astype(F32), route)
        route = jnp.where(rowi == 2 * TOP_K + kk, ranks[kk], route)
    route_ref[0, :, c0 * tl:c0 * tl + rows] = route


def _decay_tables(tl):
    hh = np.arange(RET_HEADS, dtype=np.float64)
    log_g = np.log1p(-np.exp2(-5.0 - hh))
    idx = np.arange(tl, dtype=np.float64)
    dist = np.abs(idx[:, None] - idx[None, :])
    visible = (idx[None, :] // CHUNK) <= (idx[:, None] // CHUNK)
    dmat = np.where(visible[None], np.exp(log_g[:, None, None] * dist[None]), 0.0)
    qdec = np.exp(log_g[:, None] * (idx[None, :] + 1.0))
    kdec = np.exp(log_g[:, None] * (tl - 1.0 - idx[None, :]))
    cdec = tuple(float(c) for c in np.exp(log_g * tl).astype(np.float32))
    bcast = lambda a: np.ascontiguousarray(np.broadcast_to(a[:, :, None], (RET_HEADS, tl, RET_DK)))
    return (jnp.asarray(dmat, F32), jnp.asarray(bcast(qdec), F32), jnp.asarray(bcast(kdec), F32), cdec)


def _rotary_tables(pos0, length):
    half = RET_DK // 2
    inv_freq = jnp.power(ROPE_BASE, -jnp.arange(half, dtype=F32) / half)
    ang = (pos0 + jnp.arange(length, dtype=jnp.int32)).astype(F32)[:, None] * inv_freq[None, :]
    cos, sin = jnp.cos(ang), jnp.sin(ang)
    return jnp.concatenate([cos, cos], axis=-1), jnp.concatenate([-sin, sin], axis=-1)


def _layer(x, pos0, tl, ns, chain, state0, hist0, mk, mv, wts, cnt0):
    b, length, _ = x.shape
    nt = length // tl
    rows = ns * tl
    cos, sin = _rotary_tables(pos0, length)
    dmat, qdec, kdec, cdec = _decay_tables(tl)
    crows = chain * tl
    earlier = jnp.asarray(np.triu(np.ones((crows, crows), np.float32), 1), BF16)
    kern = functools.partial(_layer_kernel, tl=tl, chain=chain, pos0=pos0, cdec=cdec)
    tok = lambda width: pl.BlockSpec((ns, tl, width), lambda i, j: (i, j, 0))
    per_stream = lambda *shape: pl.BlockSpec((ns,) + shape, lambda i, j: (i,) + (0,) * len(shape))
    per_stream_in = lambda *shape: pl.BlockSpec((ns,) + shape, lambda i, j: (i,) + (0,) * len(shape),
                                                pipeline_mode=pl.Buffered(1))
    in_specs = [
        tok(D_MODEL),
        pl.BlockSpec((tl, RET_DK), lambda i, j: (j, 0)),
        pl.BlockSpec((tl, RET_DK), lambda i, j: (j, 0)),
        _const_spec((RET_HEADS, tl, tl)),
        _const_spec((RET_HEADS, tl, RET_DK)),
        _const_spec((RET_HEADS, tl, RET_DK)),
        per_stream_in(RET_HEADS, RET_DK, RET_DV),
        per_stream_in(HIST_ROWS, POOL_WIDTH),
        per_stream_in(N_MEM, D_MODEL),
        per_stream_in(N_MEM, D_MODEL),
    ] + [_const_spec(w.shape) for w in wts] + [_const_spec((crows, crows)), _const_spec((N_EXPERTS, LANES))]
    out_specs = [
        tok(D_MODEL), tok(D_MODEL // 2),
        pl.BlockSpec((1, ROUTE_ROWS, rows), lambda i, j: (i * nt + j, 0, 0)),
        per_stream(RET_HEADS, RET_DK, RET_DV),
        per_stream(POOL_HIST, POOL_WIDTH),
        pl.BlockSpec((N_EXPERTS, LANES), lambda i, j: (0, 0)),
    ]
    out_shape = [
        jax.ShapeDtypeStruct((b, length, D_MODEL), F32),
        jax.ShapeDtypeStruct((b, length, D_MODEL // 2), jnp.int32),
        jax.ShapeDtypeStruct((b // ns * nt, ROUTE_ROWS, rows), F32),
        jax.ShapeDtypeStruct((b, RET_HEADS, RET_DK, RET_DV), F32),
        jax.ShapeDtypeStruct((b, POOL_HIST, POOL_WIDTH), F32),
        jax.ShapeDtypeStruct((N_EXPERTS, LANES), F32),
    ]
    scratch = [
        pltpu.VMEM((ns, RET_HEADS, RET_DK, RET_DV), F32),
        pltpu.VMEM((ns, HIST_ROWS + tl, POOL_WIDTH), F32),
        pltpu.VMEM((N_EXPERTS, LANES), F32),
    ]
    return pl.pallas_call(
        kern,
        grid=(b // ns, nt),
        in_specs=in_specs,
        out_specs=out_specs,
        out_shape=out_shape,
        scratch_shapes=scratch,
        compiler_params=pltpu.CompilerParams(
            dimension_semantics=("arbitrary", "arbitrary"), vmem_limit_bytes=VMEM_LIMIT),
        name="layer_tl%d" % tl,
    )(x, cos, sin, dmat, qdec, kdec, state0, hist0, mk, mv, *wts, earlier, cnt0)


SPLIT_COLS = 2 * LANES


def _expert_kernel(be_ref, used_ref, x_ref, wgu_ref, bgu_ref, wd_ref, bd_ref, perm_ref, y_ref,
                   wgu_scr, wd_scr):
    i = pl.program_id(0)
    in_use = i < used_ref[0]

    @pl.when(in_use & ((i == 0) | (be_ref[i] != be_ref[jnp.maximum(i - 1, 0)])))
    def _():
        perm = perm_ref[...]
        for c in range(2 * D_FF // SPLIT_COLS):
            wc = wgu_ref[0, :, c * SPLIT_COLS:(c + 1) * SPLIT_COLS].astype(BF16)
            pc = _dot(wc, perm).astype(BF16)
            wgu_scr[:, c * LANES:(c + 1) * LANES] = pc[:, :LANES]
            wgu_scr[:, D_FF + c * LANES:D_FF + (c + 1) * LANES] = pc[:, LANES:]
        wd_scr[...] = wd_ref[0].astype(BF16)

    @pl.when(in_use)
    def _():
        half = D_MODEL // 2
        x_lo, x_hi = _unpack_bf16(x_ref[...])
        gu = (_dot(x_lo.astype(BF16), wgu_scr[:half, :]) + _dot(x_hi.astype(BF16), wgu_scr[half:, :])
              + bgu_ref[0])
        gate = jnp.minimum(gu[:, :D_FF], SWIGLU_LIMIT)
        up = jnp.clip(gu[:, D_FF:], -SWIGLU_LIMIT, SWIGLU_LIMIT)
        act = (up + 1.0) * gate * jax.nn.sigmoid(SWIGLU_ALPHA * gate)
        y_ref[...] = _pack_bf16(_dot(act.astype(BF16), wd_scr[...]) + bd_ref[0])


def _expert_ffn(block_e, n_used, xs, w_gate_up, bgu, w_down, bd, n_blocks):
    perm = np.zeros((SPLIT_COLS, SPLIT_COLS), np.float32)
    j = np.arange(LANES)
    perm[2 * j, j] = 1.0
    perm[2 * j + 1, LANES + j] = 1.0
    blk = lambda i, be, used: (jnp.minimum(i, used[0] - 1), 0)
    per_expert = lambda i, be, used: (be[i], 0, 0)
    grid_spec = pltpu.PrefetchScalarGridSpec(
        num_scalar_prefetch=2,
        grid=(n_blocks,),
        in_specs=[
            pl.BlockSpec((EXPERT_ROWS, D_MODEL // 2), blk),
            pl.BlockSpec((1, D_MODEL, 2 * D_FF), per_expert),
            pl.BlockSpec((1, 1, 2 * D_FF), per_expert),
            pl.BlockSpec((1, D_FF, D_MODEL), per_expert),
            pl.BlockSpec((1, 1, D_MODEL), per_expert),
            _const_spec((SPLIT_COLS, SPLIT_COLS)),
        ],
        out_specs=pl.BlockSpec((EXPERT_ROWS, D_MODEL // 2), blk),
        scratch_shapes=[pltpu.VMEM((D_MODEL, 2 * D_FF), BF16), pltpu.VMEM((D_FF, D_MODEL), BF16)],
    )
    return pl.pallas_call(
        _expert_kernel,
        grid_spec=grid_spec,
        out_shape=jax.ShapeDtypeStruct(xs.shape, jnp.int32),
        compiler_params=pltpu.CompilerParams(
            dimension_semantics=("arbitrary",), vmem_limit_bytes=VMEM_LIMIT),
        name="expert_ffn",
    )(block_e, n_used, xs, w_gate_up, bgu, w_down, bd, jnp.asarray(perm, BF16))


def _expert_layout(counts, n_blocks):
    pcounts = (counts + EXPERT_ROWS - 1) // EXPERT_ROWS * EXPERT_ROWS
    pend = jnp.cumsum(pcounts)
    n_used = pend[-1:] // EXPERT_ROWS
    block_start = jnp.minimum(jnp.arange(n_blocks, dtype=jnp.int32), n_used[0] - 1) * EXPERT_ROWS
    block_e = jnp.sum((pend[None, :] <= block_start[:, None]).astype(jnp.int32), axis=1)
    return pend - pcounts, jnp.minimum(block_e, N_EXPERTS - 1), n_used


def _slot_rows(route, pstart, ns, nt, tl):
    n = route.shape[0] * ns * tl
    fields = lambda lo: route[:, lo:lo + TOP_K, :].reshape(-1, nt, TOP_K, ns, tl)
    per_slot = lambda lo: jnp.transpose(fields(lo), (2, 0, 3, 1, 4)).reshape(TOP_K, n)
    e = per_slot(TOP_K).astype(jnp.int32)
    rank = per_slot(2 * TOP_K).astype(jnp.int32)
    hit = e[:, :, None] == jnp.arange(N_EXPERTS, dtype=jnp.int32)[None, None, :]
    rows = rank + jnp.sum(jnp.where(hit, pstart[None, None, :], 0), axis=-1)
    gates = jnp.transpose(fields(0), (0, 3, 1, 4, 2)).reshape(n, TOP_K)
    return rows, gates


SC_WINDOW = 128
SC_COLS = 256


def _sc_mesh():
    return plsc.VectorSubcoreMesh(core_axis_name="c", subcore_axis_name="s")


def _dispatch(h_p, h_s, rows_p, rows_s, m_pad):
    width = h_p.shape[1]

    @functools.partial(pl.kernel, mesh=_sc_mesh(),
                       out_type=jax.ShapeDtypeStruct((m_pad, width), h_p.dtype), scratch_types=[])
    def k(hp_hbm, hs_hbm, rp_hbm, rs_hbm, xs_hbm):
        def body(x_vmem, i_vmem):
            j = pl.program_id(1)
            for kk in range(TOP_K):
                pltpu.sync_copy(x_vmem, xs_hbm.at[i_vmem.at[kk], pl.ds(j * SC_COLS, SC_COLS)])

        for src, rows in ((hp_hbm, rp_hbm), (hs_hbm, rs_hbm)):
            pltpu.emit_pipeline(
                body,
                grid=(src.shape[0] // SC_WINDOW, width // SC_COLS),
                in_specs=[pl.BlockSpec((SC_WINDOW, SC_COLS), lambda i, j: (i, j)),
                          pl.BlockSpec((TOP_K, SC_WINDOW), lambda i, j: (0, i))],
                out_specs=[],
                core_axis_name=("c", "s"),
                dimension_semantics=(pltpu.PARALLEL, pltpu.ARBITRARY),
            )(src, rows)

    return k(h_p, h_s, rows_p, rows_s)


def _collect(yb, rows):
    width = yb.shape[1]

    @functools.partial(pl.kernel, mesh=_sc_mesh(),
                       out_type=jax.ShapeDtypeStruct((rows.shape[1], width), yb.dtype), scratch_types=[])
    def k(yb_hbm, r_hbm, o_hbm):
        def body(i_vmem, o_vmem):
            j = pl.program_id(1)
            pltpu.sync_copy(yb_hbm.at[i_vmem.at[0], pl.ds(j * SC_COLS, SC_COLS)], o_vmem)

        pltpu.emit_pipeline(
            body,
            grid=(rows.shape[1] // SC_WINDOW, width // SC_COLS),
            in_specs=[pl.BlockSpec((1, SC_WINDOW), lambda i, j: (0, i))],
            out_specs=[pl.BlockSpec((SC_WINDOW, SC_COLS), lambda i, j: (i, j))],
            core_axis_name=("c", "s"),
            dimension_semantics=(pltpu.PARALLEL, pltpu.ARBITRARY),
        )(r_hbm, o_hbm)

    return k(yb, rows)


def _combine_kernel(x_ref, yg_ref, route_ref, fw_ref, y_ref):
    half = D_MODEL // 2
    acc_lo = x_ref[:, :half]
    acc_hi = x_ref[:, half:]
    route = route_ref[...]
    for kk in range(TOP_K):
        y_lo, y_hi = _unpack_bf16(yg_ref[kk])
        gate = route[:, kk:kk + 1]
        acc_lo = acc_lo + y_lo * gate
        acc_hi = acc_hi + y_hi * gate
    ms = (jnp.sum(acc_lo * acc_lo, axis=-1, keepdims=True)
          + jnp.sum(acc_hi * acc_hi, axis=-1, keepdims=True)) * (1.0 / D_MODEL)
    scale = lax.rsqrt(ms + EPS)
    y_ref[:, :half] = acc_lo * scale * fw_ref[:, :half]
    y_ref[:, half:] = acc_hi * scale * fw_ref[:, half:]


def _combine_next_kernel(x_ref, yg_ref, route_ref, fw_ref, prev_ref, y_ref):
    del prev_ref
    _combine_kernel(x_ref, yg_ref, route_ref, fw_ref, y_ref)


def _combine(x2, yg, gates, final_w, y_prev, first_row):
    n_tok = x2.shape[0]
    blk0 = first_row // COMBINE_ROWS
    in_specs = [
        pl.BlockSpec((COMBINE_ROWS, D_MODEL), lambda i: (blk0 + i, 0)),
        pl.BlockSpec((TOP_K, COMBINE_ROWS, D_MODEL // 2), lambda i: (0, i, 0)),
        pl.BlockSpec((COMBINE_ROWS, TOP_K), lambda i: (blk0 + i, 0)),
        _const_spec((1, D_MODEL)),
    ]
    args = [x2, yg, gates, final_w]
    kern, aliases = _combine_kernel, {}
    if y_prev is not None:
        in_specs.append(pl.BlockSpec(memory_space=pl.ANY))
        args.append(y_prev)
        kern, aliases = _combine_next_kernel, {len(args) - 1: 0}
    return pl.pallas_call(
        kern,
        grid=(yg.shape[1] // COMBINE_ROWS,),
        in_specs=in_specs,
        out_specs=pl.BlockSpec((COMBINE_ROWS, D_MODEL), lambda i: (blk0 + i, 0)),
        out_shape=jax.ShapeDtypeStruct((n_tok, D_MODEL), F32),
        input_output_aliases=aliases,
        compiler_params=pltpu.CompilerParams(
            dimension_semantics=("arbitrary",), vmem_limit_bytes=VMEM_LIMIT),
        name="combine_%d_%d" % (n_tok, first_row),
    )(*args)


def _collect_combine(yb, x2, rows, gates, final_w, chunks):
    nc = x2.shape[0] // chunks
    y = None
    for c in range(chunks):
        yg = _collect(yb, rows[:, c * nc:(c + 1) * nc].reshape(1, TOP_K * nc))
        y = _combine(x2, yg.reshape(TOP_K, nc, D_MODEL // 2), gates, final_w, y, c * nc)
    return y


def kernel(x_prompt, x_sample, cache_mem_k, cache_mem_v, state_ret, state_pool, mem_prompt,
           norm_mix_w, w_in, ret_gn_w, w_pool, pool_scale, w_out, norm_mem_w, mem_norm_w,
           w_q_mem, w_kv_mem, w_o_mem, norm_ffn_w, router_w, router_b, w_gate_up, b_gate_up,
           w_down, b_down, final_norm_w):
    assert norm_mix_w.shape[0] == 1, "one layer"
    b, seq, _ = x_prompt.shape
    db, dseq, _ = x_sample.shape
    row = lambda a: a.reshape(1, -1)

    mk_p, mv_p, mkb_p, mvb_p = _mem_kv(mem_prompt, row(mem_norm_w[0]), w_kv_mem[0].astype(BF16))
    mkb_s, mvb_s = _kv_flat(cache_mem_k[0], cache_mem_v[0])

    wts = (row(norm_mix_w[0]), w_in[0].astype(BF16), row(ret_gn_w[0]), w_pool[0].astype(BF16),
           row(pool_scale[0]), w_out[0].astype(BF16), row(norm_mem_w[0]), w_q_mem[0].astype(BF16),
           w_o_mem[0].astype(BF16), row(norm_ffn_w[0]), router_w[0].T.astype(BF16),
           jnp.broadcast_to(router_b[0][:, None], (N_EXPERTS, LANES)))

    zero_state = jnp.zeros((b, RET_HEADS, RET_DK, RET_DV), F32)
    zero_hist = jnp.zeros((b, HIST_ROWS, POOL_WIDTH), F32)
    no_counts = jnp.zeros((N_EXPERTS, LANES), F32)
    x2_p, h_p, route_p, ret_p, pool_p, cnt_p = _layer(
        x_prompt, 0, PROMPT_TILE, PROMPT_STREAMS, PROMPT_CHAIN, zero_state, zero_hist, mkb_p, mvb_p, wts, no_counts)

    hist_s = jnp.concatenate([jnp.zeros((db, 1, POOL_WIDTH), F32), state_pool[0]], axis=1)
    x2_s, h_s, route_s, ret_s, pool_s, cnt_all = _layer(
        x_sample, PAST_LEN, dseq, SAMPLE_STREAMS, SAMPLE_STREAMS, state_ret[0], hist_s, mkb_s, mvb_s, wts, cnt_p)

    n_p, n_s = b * seq, db * dseq
    n_blocks = -(-((n_p + n_s) * TOP_K + N_EXPERTS * (EXPERT_ROWS - 1)) // EXPERT_ROWS)
    pstart, block_e, n_used = _expert_layout(cnt_all[:, 0].astype(jnp.int32), n_blocks)
    rows_p, gates_p = _slot_rows(route_p, pstart, PROMPT_STREAMS, seq // PROMPT_TILE, PROMPT_TILE)
    rows_s, gates_s = _slot_rows(route_s, pstart, SAMPLE_STREAMS, 1, dseq)

    half = D_MODEL // 2
    xs = _dispatch(h_p.reshape(n_p, half), h_s.reshape(n_s, half), rows_p, rows_s, n_blocks * EXPERT_ROWS)

    bgu = jnp.concatenate([b_gate_up[0][:, 0::2], b_gate_up[0][:, 1::2]], axis=-1).reshape(N_EXPERTS, 1, 2 * D_FF)
    yb = _expert_ffn(block_e, n_used, xs, w_gate_up[0], bgu, w_down[0],
                     b_down[0].reshape(N_EXPERTS, 1, D_MODEL), n_blocks)

    final_w = row(final_norm_w)
    y_s = _collect_combine(yb, x2_s.reshape(n_s, D_MODEL), rows_s, gates_s, final_w, 1)
    y_p = _collect_combine(yb, x2_p.reshape(n_p, D_MODEL), rows_p, gates_p, final_w, COMBINE_CHUNKS)
    y_p = y_p.reshape(b, seq, D_MODEL)
    y_s = y_s.reshape(db, dseq, D_MODEL)
    return (y_p, y_s, mk_p[None], mv_p[None], ret_p[None], pool_p[None], ret_s[None], pool_s[None])
```

```python
import functools

import numpy as np
import jax
import jax.numpy as jnp
from jax import lax
from jax.experimental import pallas as pl
from jax.experimental.pallas import tpu as pltpu
from jax.experimental.pallas import tpu_sc as plsc

D_MODEL = 1024
CHUNK = 64
PAST_LEN = 4096
RET_HEADS = 4
RET_DK = 128
RET_DV = 128
RET_QK = RET_HEADS * RET_DK
RET_VW = RET_HEADS * RET_DV
ROPE_BASE = 10000.0
POOL_WINDOWS = (2, 4, 8, 16)
POOL_GROUPS = 4
POOL_WIDTH = D_MODEL // 2
POOL_C = POOL_WIDTH // POOL_GROUPS
POOL_HIST = max(POOL_WINDOWS) - 1
HIST_ROWS = POOL_HIST + 1
IN_WIDTH = 2 * RET_QK + 2 * RET_VW + POOL_WIDTH
N_MEM = 256
MEM_HEADS = 4
MEM_HD = D_MODEL // MEM_HEADS
N_EXPERTS = 32
TOP_K = 4
D_FF = D_MODEL
SWIGLU_LIMIT = 7.0
SWIGLU_ALPHA = 1.702
EPS = 1e-5

LANES = 128
ROUTE_ROWS = 16
PROMPT_TILE = 256
PROMPT_STREAMS = 4
PROMPT_CHAIN = 2
SAMPLE_STREAMS = 8
EXPERT_ROWS = 512
KV_STREAMS = 2
COMBINE_ROWS = 512
COMBINE_CHUNKS = 2
MOE_GROUPS = 2
VMEM_LIMIT = 56 * 1024 * 1024

BF16 = jnp.bfloat16
F32 = jnp.float32


def _rms(x, w):
    return x * lax.rsqrt(jnp.mean(x * x, axis=-1, keepdims=True) + EPS) * w


def _dot(a, b):
    return jnp.dot(a, b, preferred_element_type=F32)


def _rms_dot(x, w_norm, w_mat):
    inv = lax.rsqrt(jnp.mean(x * x, axis=-1, keepdims=True) + EPS)
    return _dot((x * w_norm).astype(BF16), w_mat) * inv


def _dot_nt(a, b):
    return lax.dot_general(a, b, (((1,), (1,)), ((), ())), preferred_element_type=F32)


def _pack_bf16(x):
    bits = lax.bitcast_convert_type(x.astype(BF16).astype(F32), jnp.int32)
    w = x.shape[1] // 2
    return lax.shift_right_logical(bits[:, :w], 16) | (bits[:, w:] & -65536)


def _unpack_bf16(p):
    lo = lax.bitcast_convert_type(lax.shift_left(p, 16), F32)
    hi = lax.bitcast_convert_type(p & -65536, F32)
    return lo, hi


def _const_spec(shape):
    nd = len(shape)
    return pl.BlockSpec(shape, lambda *_: (0,) * nd, pipeline_mode=pl.Buffered(1))


def _mem_kv_kernel(mem_ref, nw_ref, w_ref, k_ref, v_ref, kb_ref, vb_ref):
    ns = mem_ref.shape[0]
    xn = _rms(mem_ref[...].reshape(ns * N_MEM, D_MODEL), nw_ref[...]).astype(BF16)
    kv = _dot(xn, w_ref[...])
    for hd in range(MEM_HEADS):
        lo = hd * MEM_HD
        k_ref[:, :, hd, :] = kv[:, lo:lo + MEM_HD].reshape(ns, N_MEM, MEM_HD)
        v_ref[:, :, hd, :] = kv[:, D_MODEL + lo:D_MODEL + lo + MEM_HD].reshape(ns, N_MEM, MEM_HD)
    kb_ref[...] = kv[:, :D_MODEL].astype(BF16).reshape(ns, N_MEM, D_MODEL)
    vb_ref[...] = kv[:, D_MODEL:].astype(BF16).reshape(ns, N_MEM, D_MODEL)


def _mem_kv(mem, mem_norm_w, w_kv_bf):
    b = mem.shape[0]
    out_spec = pl.BlockSpec((KV_STREAMS, N_MEM, MEM_HEADS, MEM_HD), lambda i: (i, 0, 0, 0))
    flat_spec = pl.BlockSpec((KV_STREAMS, N_MEM, D_MODEL), lambda i: (i, 0, 0))
    return pl.pallas_call(
        _mem_kv_kernel,
        grid=(b // KV_STREAMS,),
        in_specs=[
            pl.BlockSpec((KV_STREAMS, N_MEM, D_MODEL), lambda i: (i, 0, 0)),
            _const_spec((1, D_MODEL)),
            _const_spec((D_MODEL, 2 * D_MODEL)),
        ],
        out_specs=[out_spec, out_spec, flat_spec, flat_spec],
        out_shape=[jax.ShapeDtypeStruct((b, N_MEM, MEM_HEADS, MEM_HD), F32)] * 2
        + [jax.ShapeDtypeStruct((b, N_MEM, D_MODEL), BF16)] * 2,
        compiler_params=pltpu.CompilerParams(
            dimension_semantics=("arbitrary",), vmem_limit_bytes=VMEM_LIMIT),
        name="mem_kv",
    )(mem, mem_norm_w, w_kv_bf)


def _kv_flat_kernel(k_ref, v_ref, kb_ref, vb_ref):
    for hd in range(MEM_HEADS):
        kb_ref[:, :, hd * MEM_HD:(hd + 1) * MEM_HD] = k_ref[:, :, hd, :].astype(BF16)
        vb_ref[:, :, hd * MEM_HD:(hd + 1) * MEM_HD] = v_ref[:, :, hd, :].astype(BF16)


def _kv_flat(mem_k, mem_v):
    b = mem_k.shape[0]
    in_spec = pl.BlockSpec((KV_STREAMS, N_MEM, MEM_HEADS, MEM_HD), lambda i: (i, 0, 0, 0))
    out_spec = pl.BlockSpec((KV_STREAMS, N_MEM, D_MODEL), lambda i: (i, 0, 0))
    return pl.pallas_call(
        _kv_flat_kernel,
        grid=(b // KV_STREAMS,),
        in_specs=[in_spec, in_spec],
        out_specs=[out_spec, out_spec],
        out_shape=[jax.ShapeDtypeStruct((b, N_MEM, D_MODEL), BF16)] * 2,
        compiler_params=pltpu.CompilerParams(
            dimension_semantics=("arbitrary",), vmem_limit_bytes=VMEM_LIMIT),
        name="kv_flat",
    )(mem_k, mem_v)


def _layer_kernel(x_ref, cos_ref, sin_ref, dmat_ref, qdec_ref, kdec_ref, state0_ref, hist0_ref,
                  mk_ref, mv_ref, nmix_ref, win_ref, gnw_ref, wpool_ref, pscale_ref, wout_ref,
                  nmem_ref, wq_ref, wo_ref, nffn_ref, rw_ref, rb_ref, earlier_ref, cnt0_ref,
                  x2_ref, h_ref, route_ref, rstate_ref, pstate_ref, cnt_ref,
                  s_scr, ext_scr, cnt_scr, *, tl, chain, pos0, cdec):
    t = pl.program_id(1)

    @pl.when((pl.program_id(0) == 0) & (t == 0))
    def _():
        cnt_scr[...] = cnt0_ref[...]

    @pl.when(t == 0)
    def _():
        s_scr[...] = state0_ref[...]
        ext_scr[:, 0:HIST_ROWS, :] = hist0_ref[...]

    for c0 in range(0, x_ref.shape[0], chain):
        _layer_chain(c0, t, x_ref, cos_ref, sin_ref, dmat_ref, qdec_ref, kdec_ref, nmix_ref, win_ref, gnw_ref,
                     wpool_ref, pscale_ref, wout_ref, nmem_ref, wq_ref, wo_ref, nffn_ref, rw_ref, rb_ref,
                     earlier_ref, x2_ref, h_ref, route_ref, pstate_ref, s_scr, ext_scr, mk_ref, mv_ref, cnt_scr,
                     tl=tl, chain=chain, pos0=pos0, cdec=cdec)
    rstate_ref[...] = s_scr[...]
    cnt_ref[...] = cnt_scr[...]


def _layer_chain(c0, t, x_ref, cos_ref, sin_ref, dmat_ref, qdec_ref, kdec_ref, nmix_ref, win_ref, gnw_ref,
                 wpool_ref, pscale_ref, wout_ref, nmem_ref, wq_ref, wo_ref, nffn_ref, rw_ref, rb_ref,
                 earlier_ref, x2_ref, h_ref, route_ref, pstate_ref, s_scr, ext_scr, mk_ref, mv_ref, cnt_scr,
                 *, tl, chain, pos0, cdec):
    rows = chain * tl
    x = x_ref[c0:c0 + chain].reshape(rows, D_MODEL)
    proj = _rms_dot(x, nmix_ref[...], win_ref[...])
    cos = cos_ref[...]
    sin = sin_ref[...]
    pos = (pos0 + t * tl + lax.broadcasted_iota(jnp.int32, (tl, POOL_C), 0)).astype(F32)

    mixes = []
    for sj in range(chain):
        si = c0 + sj
        pj = proj[sj * tl:(sj + 1) * tl]
        outs = []
        for hd in range(RET_HEADS):
            lo = hd * RET_DK
            q = pj[:, lo:lo + RET_DK]
            k = pj[:, RET_QK + lo:RET_QK + lo + RET_DK]
            v = pj[:, 2 * RET_QK + lo:2 * RET_QK + lo + RET_DV]
            g = pj[:, 2 * RET_QK + RET_VW + lo:2 * RET_QK + RET_VW + lo + RET_DV]
            qr = (q * cos + pltpu.roll(q, RET_DK // 2, 1) * sin) * (RET_DK ** -0.5)
            kr = k * cos + pltpu.roll(k, RET_DK // 2, 1) * sin
            vb = v.astype(BF16)
            s = _dot_nt(qr.astype(BF16), kr.astype(BF16)) * dmat_ref[hd]
            o = _dot(s.astype(BF16), vb)
            state = s_scr[si, hd]
            o = o + _dot((qr * qdec_ref[hd]).astype(BF16), state.astype(BF16))
            kd_t = jnp.transpose(kr * kdec_ref[hd]).astype(BF16)
            s_scr[si, hd] = cdec[hd] * state + _dot(kd_t, vb)
            mu = jnp.mean(o, axis=-1, keepdims=True)
            oc = o - mu
            var = jnp.mean(oc * oc, axis=-1, keepdims=True)
            on = oc * lax.rsqrt(var + EPS)
            outs.append(on * gnw_ref[:, lo:lo + RET_DV] * (g * jax.nn.sigmoid(g)))

        pin = pj[:, 2 * RET_QK + 2 * RET_VW:]
        ext_scr[si, HIST_ROWS:HIST_ROWS + tl, :] = pin
        pstate_ref[si] = pin[tl - POOL_HIST:, :]
        for gi, w in enumerate(POOL_WINDOWS):
            lo = gi * POOL_C
            wsum = ext_scr[si, :, lo:lo + POOL_C]
            shift = 1
            while shift < w:
                wsum = wsum + pltpu.roll(wsum, shift, 0)
                shift *= 2
            cnt = jnp.minimum(float(w), pos + 1.0)
            d = wsum[HIST_ROWS:, :] / cnt - pin[:, lo:lo + POOL_C]
            y = _dot(d.astype(BF16), wpool_ref[gi])
            outs.append(y * pscale_ref[:, lo:lo + POOL_C])
        ext_scr[si, 0:HIST_ROWS, :] = ext_scr[si, tl:tl + HIST_ROWS, :]
        mixes.append(jnp.concatenate(outs, axis=-1).astype(BF16))

    x1 = x + _dot(jnp.concatenate(mixes, axis=0), wout_ref[...])

    qm = _rms_dot(x1, nmem_ref[...], wq_ref[...])
    atts = []
    for sj in range(chain):
        si = c0 + sj
        aouts = []
        for hd in range(MEM_HEADS):
            lo = hd * MEM_HD
            qh = qm[sj * tl:(sj + 1) * tl, lo:lo + MEM_HD].astype(BF16)
            s = _dot_nt(qh, mk_ref[si, :, lo:lo + MEM_HD]) * (MEM_HD ** -0.5)
            e = jnp.exp(s - jnp.max(s, axis=-1, keepdims=True))
            p = e / jnp.sum(e, axis=-1, keepdims=True)
            aouts.append(_dot(p.astype(BF16), mv_ref[si, :, lo:lo + MEM_HD]))
        atts.append(jnp.concatenate(aouts, axis=-1).astype(BF16))
    x2 = x1 + _dot(jnp.concatenate(atts, axis=0), wo_ref[...])
    x2_ref[c0:c0 + chain] = x2.reshape(chain, tl, D_MODEL)

    hn = _rms(x2, nffn_ref[...])
    h_ref[c0:c0 + chain] = _pack_bf16(hn).reshape(chain, tl, D_MODEL // 2)
    lane_tile = lambda a: a[:, :rows] if rows <= LANES else jnp.concatenate([a] * (rows // LANES), axis=1)
    logits = _dot_nt(rw_ref[...], hn.astype(BF16)) + lane_tile(rb_ref[...])
    eiota = lax.broadcasted_iota(jnp.int32, (N_EXPERTS, rows), 0)
    neg = jnp.finfo(F32).min
    vals, idxs = [], []
    for _k in range(TOP_K):
        m = jnp.max(logits, axis=0, keepdims=True)
        idx = jnp.min(jnp.where(logits == m, eiota, N_EXPERTS), axis=0, keepdims=True)
        vals.append(m)
        idxs.append(idx)
        logits = jnp.where(eiota == idx, neg, logits)
    exps = [jnp.exp(vk - vals[0]) for vk in vals]
    den = exps[0] + exps[1] + exps[2] + exps[3]

    onehots = [(eiota == idx).astype(F32) for idx in idxs]
    picked = onehots[0] + onehots[1] + onehots[2] + onehots[3]
    before = lane_tile(cnt_scr[...]) + _dot(picked.astype(BF16), earlier_ref[...])
    ranks = [jnp.sum(oh * before, axis=0, keepdims=True) for oh in onehots]
    cnt_scr[...] = cnt_scr[...] + jnp.sum(picked, axis=1, keepdims=True)

    rowi = lax.broadcasted_iota(jnp.int32, (ROUTE_ROWS, rows), 0)
    route = jnp.zeros((ROUTE_ROWS, rows), F32)
    for kk in range(TOP_K):
        route = jnp.where(rowi == kk, exps[kk] / den, route)
        route = jnp.where(rowi == TOP_K + kk, idxs[kk].astype(F32), route)
        route = jnp.where(rowi == 2 * TOP_K + kk, ranks[kk], route)
    route_ref[0, :, c0 * tl:c0 * tl + rows] = route


def _decay_tables(tl):
    hh = np.arange(RET_HEADS, dtype=np.float64)
    log_g = np.log1p(-np.exp2(-5.0 - hh))
    idx = np.arange(tl, dtype=np.float64)
    dist = np.abs(idx[:, None] - idx[None, :])
    visible = (idx[None, :] // CHUNK) <= (idx[:, None] // CHUNK)
    dmat = np.where(visible[None], np.exp(log_g[:, None, None] * dist[None]), 0.0)
    qdec = np.exp(log_g[:, None] * (idx[None, :] + 1.0))
    kdec = np.exp(log_g[:, None] * (tl - 1.0 - idx[None, :]))
    cdec = tuple(float(c) for c in np.exp(log_g * tl).astype(np.float32))
    bcast = lambda a: np.ascontiguousarray(np.broadcast_to(a[:, :, None], (RET_HEADS, tl, RET_DK)))
    return (jnp.asarray(dmat, F32), jnp.asarray(bcast(qdec), F32), jnp.asarray(bcast(kdec), F32), cdec)


def _rotary_tables(pos0, length):
    half = RET_DK // 2
    inv_freq = jnp.power(ROPE_BASE, -jnp.arange(half, dtype=F32) / half)
    ang = (pos0 + jnp.arange(length, dtype=jnp.int32)).astype(F32)[:, None] * inv_freq[None, :]
    cos, sin = jnp.cos(ang), jnp.sin(ang)
    return jnp.concatenate([cos, cos], axis=-1), jnp.concatenate([-sin, sin], axis=-1)


def _layer(x, stream0, b, pos0, tl, ns, chain, state0, hist0, mk, mv, wts, cnt0):
    length = x.shape[1]
    nt = length // tl
    g0 = stream0 // ns
    rows = ns * tl
    cos, sin = _rotary_tables(pos0, length)
    dmat, qdec, kdec, cdec = _decay_tables(tl)
    crows = chain * tl
    earlier = jnp.asarray(np.triu(np.ones((crows, crows), np.float32), 1), BF16)
    kern = functools.partial(_layer_kernel, tl=tl, chain=chain, pos0=pos0, cdec=cdec)
    tok = lambda width: pl.BlockSpec((ns, tl, width), lambda i, j: (i, j, 0))
    per_stream = lambda *shape: pl.BlockSpec((ns,) + shape, lambda i, j: (i,) + (0,) * len(shape))
    per_stream_in = lambda off, *shape: pl.BlockSpec(
        (ns,) + shape, lambda i, j: (i + off,) + (0,) * len(shape), pipeline_mode=pl.Buffered(1))
    in_specs = [
        pl.BlockSpec((ns, tl, D_MODEL), lambda i, j: (i + g0, j, 0)),
        pl.BlockSpec((tl, RET_DK), lambda i, j: (j, 0)),
        pl.BlockSpec((tl, RET_DK), lambda i, j: (j, 0)),
        _const_spec((RET_HEADS, tl, tl)),
        _const_spec((RET_HEADS, tl, RET_DK)),
        _const_spec((RET_HEADS, tl, RET_DK)),
        per_stream_in(0, RET_HEADS, RET_DK, RET_DV),
        per_stream_in(0, HIST_ROWS, POOL_WIDTH),
        per_stream_in(g0, N_MEM, D_MODEL),
        per_stream_in(g0, N_MEM, D_MODEL),
    ] + [_const_spec(w.shape) for w in wts] + [_const_spec((crows, crows)), _const_spec((N_EXPERTS, LANES))]
    out_specs = [
        tok(D_MODEL), tok(D_MODEL // 2),
        pl.BlockSpec((1, ROUTE_ROWS, rows), lambda i, j: (i * nt + j, 0, 0)),
        per_stream(RET_HEADS, RET_DK, RET_DV),
        per_stream(POOL_HIST, POOL_WIDTH),
        pl.BlockSpec((N_EXPERTS, LANES), lambda i, j: (0, 0)),
    ]
    out_shape = [
        jax.ShapeDtypeStruct((b, length, D_MODEL), F32),
        jax.ShapeDtypeStruct((b, length, D_MODEL // 2), jnp.int32),
        jax.ShapeDtypeStruct((b // ns * nt, ROUTE_ROWS, rows), F32),
        jax.ShapeDtypeStruct((b, RET_HEADS, RET_DK, RET_DV), F32),
        jax.ShapeDtypeStruct((b, POOL_HIST, POOL_WIDTH), F32),
        jax.ShapeDtypeStruct((N_EXPERTS, LANES), F32),
    ]
    scratch = [
        pltpu.VMEM((ns, RET_HEADS, RET_DK, RET_DV), F32),
        pltpu.VMEM((ns, HIST_ROWS + tl, POOL_WIDTH), F32),
        pltpu.VMEM((N_EXPERTS, LANES), F32),
    ]
    return pl.pallas_call(
        kern,
        grid=(b // ns, nt),
        in_specs=in_specs,
        out_specs=out_specs,
        out_shape=out_shape,
        scratch_shapes=scratch,
        compiler_params=pltpu.CompilerParams(
            dimension_semantics=("arbitrary", "arbitrary"), vmem_limit_bytes=VMEM_LIMIT),
        name="layer_tl%d" % tl,
    )(x, cos, sin, dmat, qdec, kdec, state0, hist0, mk, mv, *wts, earlier, cnt0)


SPLIT_COLS = 2 * LANES


def _expert_kernel(be_ref, used_ref, x_ref, wgu_ref, bgu_ref, wd_ref, bd_ref, perm_ref, y_ref,
                   wgu_scr, wd_scr):
    i = pl.program_id(0)
    in_use = i < used_ref[0]

    @pl.when(in_use & ((i == 0) | (be_ref[i] != be_ref[jnp.maximum(i - 1, 0)])))
    def _():
        perm = perm_ref[...]
        for c in range(2 * D_FF // SPLIT_COLS):
            wc = wgu_ref[0, :, c * SPLIT_COLS:(c + 1) * SPLIT_COLS].astype(BF16)
            pc = _dot(wc, perm).astype(BF16)
            wgu_scr[:, c * LANES:(c + 1) * LANES] = pc[:, :LANES]
            wgu_scr[:, D_FF + c * LANES:D_FF + (c + 1) * LANES] = pc[:, LANES:]
        wd_scr[...] = wd_ref[0].astype(BF16)

    @pl.when(in_use)
    def _():
        half = D_MODEL // 2
        x_lo, x_hi = _unpack_bf16(x_ref[...])
        gu = (_dot(x_lo.astype(BF16), wgu_scr[:half, :]) + _dot(x_hi.astype(BF16), wgu_scr[half:, :])
              + bgu_ref[0])
        gate = jnp.minimum(gu[:, :D_FF], SWIGLU_LIMIT)
        up = jnp.clip(gu[:, D_FF:], -SWIGLU_LIMIT, SWIGLU_LIMIT)
        act = (up + 1.0) * gate * jax.nn.sigmoid(SWIGLU_ALPHA * gate)
        y_ref[...] = _pack_bf16(_dot(act.astype(BF16), wd_scr[...]) + bd_ref[0])


def _expert_ffn(block_e, n_used, xs, w_gate_up, bgu, w_down, bd, n_blocks):
    perm = np.zeros((SPLIT_COLS, SPLIT_COLS), np.float32)
    j = np.arange(LANES)
    perm[2 * j, j] = 1.0
    perm[2 * j + 1, LANES + j] = 1.0
    blk = lambda i, be, used: (jnp.minimum(i, used[0] - 1), 0)
    per_expert = lambda i, be, used: (be[i], 0, 0)
    grid_spec = pltpu.PrefetchScalarGridSpec(
        num_scalar_prefetch=2,
        grid=(n_blocks,),
        in_specs=[
            pl.BlockSpec((EXPERT_ROWS, D_MODEL // 2), blk),
            pl.BlockSpec((1, D_MODEL, 2 * D_FF), per_expert),
            pl.BlockSpec((1, 1, 2 * D_FF), per_expert),
            pl.BlockSpec((1, D_FF, D_MODEL), per_expert),
            pl.BlockSpec((1, 1, D_MODEL), per_expert),
            _const_spec((SPLIT_COLS, SPLIT_COLS)),
        ],
        out_specs=pl.BlockSpec((EXPERT_ROWS, D_MODEL // 2), blk),
        scratch_shapes=[pltpu.VMEM((D_MODEL, 2 * D_FF), BF16), pltpu.VMEM((D_FF, D_MODEL), BF16)],
    )
    return pl.pallas_call(
        _expert_kernel,
        grid_spec=grid_spec,
        out_shape=jax.ShapeDtypeStruct(xs.shape, jnp.int32),
        compiler_params=pltpu.CompilerParams(
            dimension_semantics=("arbitrary",), vmem_limit_bytes=VMEM_LIMIT),
        name="expert_ffn",
    )(block_e, n_used, xs, w_gate_up, bgu, w_down, bd, jnp.asarray(perm, BF16))


def _expert_layout(counts, n_blocks):
    pcounts = (counts + EXPERT_ROWS - 1) // EXPERT_ROWS * EXPERT_ROWS
    pend = jnp.cumsum(pcounts)
    n_used = pend[-1:] // EXPERT_ROWS
    block_start = jnp.minimum(jnp.arange(n_blocks, dtype=jnp.int32), n_used[0] - 1) * EXPERT_ROWS
    block_e = jnp.sum((pend[None, :] <= block_start[:, None]).astype(jnp.int32), axis=1)
    return pend - pcounts, jnp.minimum(block_e, N_EXPERTS - 1), n_used


def _slot_rows(route, pstart, ns, nt, tl):
    n = route.shape[0] * ns * tl
    fields = lambda lo: route[:, lo:lo + TOP_K, :].reshape(-1, nt, TOP_K, ns, tl)
    per_slot = lambda lo: jnp.transpose(fields(lo), (2, 0, 3, 1, 4)).reshape(TOP_K, n)
    e = per_slot(TOP_K).astype(jnp.int32)
    rank = per_slot(2 * TOP_K).astype(jnp.int32)
    hit = e[:, :, None] == jnp.arange(N_EXPERTS, dtype=jnp.int32)[None, None, :]
    rows = rank + jnp.sum(jnp.where(hit, pstart[None, None, :], 0), axis=-1)
    gates = jnp.transpose(fields(0), (0, 3, 1, 4, 2)).reshape(n, TOP_K)
    return rows, gates


SC_WINDOW = 128
SC_COLS = 256


def _sc_mesh():
    return plsc.VectorSubcoreMesh(core_axis_name="c", subcore_axis_name="s")


def _dispatch(sources, m_pad):
    width = sources[0][0].shape[1]
    n_src = len(sources)

    @functools.partial(pl.kernel, mesh=_sc_mesh(),
                       out_type=jax.ShapeDtypeStruct((m_pad, width), sources[0][0].dtype), scratch_types=[])
    def k(*refs):
        xs_hbm = refs[2 * n_src]

        def body(x_vmem, i_vmem):
            j = pl.program_id(1)
            for kk in range(TOP_K):
                pltpu.sync_copy(x_vmem, xs_hbm.at[i_vmem.at[kk], pl.ds(j * SC_COLS, SC_COLS)])

        for si in range(n_src):
            src, rows = refs[2 * si], refs[2 * si + 1]
            pltpu.emit_pipeline(
                body,
                grid=(src.shape[0] // SC_WINDOW, width // SC_COLS),
                in_specs=[pl.BlockSpec((SC_WINDOW, SC_COLS), lambda i, j: (i, j)),
                          pl.BlockSpec((TOP_K, SC_WINDOW), lambda i, j: (0, i))],
                out_specs=[],
                core_axis_name=("c", "s"),
                dimension_semantics=(pltpu.PARALLEL, pltpu.ARBITRARY),
            )(src, rows)

    return k(*[a for pair in sources for a in pair])


def _collect(yb, rows):
    width = yb.shape[1]

    @functools.partial(pl.kernel, mesh=_sc_mesh(),
                       out_type=jax.ShapeDtypeStruct((rows.shape[1], width), yb.dtype), scratch_types=[])
    def k(yb_hbm, r_hbm, o_hbm):
        def body(i_vmem, o_vmem):
            j = pl.program_id(1)
            pltpu.sync_copy(yb_hbm.at[i_vmem.at[0], pl.ds(j * SC_COLS, SC_COLS)], o_vmem)

        pltpu.emit_pipeline(
            body,
            grid=(rows.shape[1] // SC_WINDOW, width // SC_COLS),
            in_specs=[pl.BlockSpec((1, SC_WINDOW), lambda i, j: (0, i))],
            out_specs=[pl.BlockSpec((SC_WINDOW, SC_COLS), lambda i, j: (i, j))],
            core_axis_name=("c", "s"),
            dimension_semantics=(pltpu.PARALLEL, pltpu.ARBITRARY),
        )(r_hbm, o_hbm)

    return k(yb, rows)


def _combine_kernel(x_ref, yg_ref, route_ref, fw_ref, y_ref):
    half = D_MODEL // 2
    acc_lo = x_ref[:, :half]
    acc_hi = x_ref[:, half:]
    route = route_ref[...]
    for kk in range(TOP_K):
        y_lo, y_hi = _unpack_bf16(yg_ref[kk])
        gate = route[:, kk:kk + 1]
        acc_lo = acc_lo + y_lo * gate
        acc_hi = acc_hi + y_hi * gate
    ms = (jnp.sum(acc_lo * acc_lo, axis=-1, keepdims=True)
          + jnp.sum(acc_hi * acc_hi, axis=-1, keepdims=True)) * (1.0 / D_MODEL)
    scale = lax.rsqrt(ms + EPS)
    y_ref[:, :half] = acc_lo * scale * fw_ref[:, :half]
    y_ref[:, half:] = acc_hi * scale * fw_ref[:, half:]


def _combine_next_kernel(x_ref, yg_ref, route_ref, fw_ref, prev_ref, y_ref):
    del prev_ref
    _combine_kernel(x_ref, yg_ref, route_ref, fw_ref, y_ref)


def _combine(x2, yg, gates, final_w, y_prev, first_row, out_rows, out_first_row):
    blk_in = first_row // COMBINE_ROWS
    blk_out = out_first_row // COMBINE_ROWS
    in_specs = [
        pl.BlockSpec((COMBINE_ROWS, D_MODEL), lambda i: (blk_in + i, 0)),
        pl.BlockSpec((TOP_K, COMBINE_ROWS, D_MODEL // 2), lambda i: (0, i, 0)),
        pl.BlockSpec((COMBINE_ROWS, TOP_K), lambda i: (blk_in + i, 0)),
        _const_spec((1, D_MODEL)),
    ]
    args = [x2, yg, gates, final_w]
    kern, aliases = _combine_kernel, {}
    if y_prev is not None:
        in_specs.append(pl.BlockSpec(memory_space=pl.ANY))
        args.append(y_prev)
        kern, aliases = _combine_next_kernel, {len(args) - 1: 0}
    return pl.pallas_call(
        kern,
        grid=(yg.shape[1] // COMBINE_ROWS,),
        in_specs=in_specs,
        out_specs=pl.BlockSpec((COMBINE_ROWS, D_MODEL), lambda i: (blk_out + i, 0)),
        out_shape=jax.ShapeDtypeStruct((out_rows, D_MODEL), F32),
        input_output_aliases=aliases,
        compiler_params=pltpu.CompilerParams(
            dimension_semantics=("arbitrary",), vmem_limit_bytes=VMEM_LIMIT),
        name="combine_%d_%d" % (out_rows, out_first_row),
    )(*args)


def _collect_combine(yb, x2, rows, gates, final_w, chunks, y, out_rows, out_first_row):
    nc = x2.shape[0] // chunks
    for c in range(chunks):
        yg = _collect(yb, rows[:, c * nc:(c + 1) * nc].reshape(1, TOP_K * nc))
        y = _combine(x2, yg.reshape(TOP_K, nc, D_MODEL // 2), gates, final_w, y, c * nc,
                     out_rows, out_first_row + c * nc)
    return y


def kernel(x_prompt, x_sample, cache_mem_k, cache_mem_v, state_ret, state_pool, mem_prompt,
           norm_mix_w, w_in, ret_gn_w, w_pool, pool_scale, w_out, norm_mem_w, mem_norm_w,
           w_q_mem, w_kv_mem, w_o_mem, norm_ffn_w, router_w, router_b, w_gate_up, b_gate_up,
           w_down, b_down, final_norm_w):
    assert norm_mix_w.shape[0] == 1, "one layer"
    b, seq, _ = x_prompt.shape
    db, dseq, _ = x_sample.shape
    row = lambda a: a.reshape(1, -1)

    mk_p, mv_p, mkb_p, mvb_p = _mem_kv(mem_prompt, row(mem_norm_w[0]), w_kv_mem[0].astype(BF16))
    mkb_s, mvb_s = _kv_flat(cache_mem_k[0], cache_mem_v[0])

    wts = (row(norm_mix_w[0]), w_in[0].astype(BF16), row(ret_gn_w[0]), w_pool[0].astype(BF16),
           row(pool_scale[0]), w_out[0].astype(BF16), row(norm_mem_w[0]), w_q_mem[0].astype(BF16),
           w_o_mem[0].astype(BF16), row(norm_ffn_w[0]), router_w[0].T.astype(BF16),
           jnp.broadcast_to(router_b[0][:, None], (N_EXPERTS, LANES)))

    per = b // MOE_GROUPS
    n_g, n_s = per * seq, db * dseq
    half = D_MODEL // 2
    final_w = row(final_norm_w)
    bgu = jnp.concatenate([b_gate_up[0][:, 0::2], b_gate_up[0][:, 1::2]], axis=-1).reshape(N_EXPERTS, 1, 2 * D_FF)
    bd = b_down[0].reshape(N_EXPERTS, 1, D_MODEL)
    zero_state = jnp.zeros((per, RET_HEADS, RET_DK, RET_DV), F32)
    zero_hist = jnp.zeros((per, HIST_ROWS, POOL_WIDTH), F32)
    no_counts = jnp.zeros((N_EXPERTS, LANES), F32)
    hist_s = jnp.concatenate([jnp.zeros((db, 1, POOL_WIDTH), F32), state_pool[0]], axis=1)

    y_p, rets, pools = None, [], []
    for g in range(MOE_GROUPS):
        x2_g, h_g, route_g, ret_g, pool_g, cnt = _layer(
            x_prompt, g * per, per, 0, PROMPT_TILE, PROMPT_STREAMS, PROMPT_CHAIN, zero_state, zero_hist,
            mkb_p, mvb_p, wts, no_counts)
        rets.append(ret_g)
        pools.append(pool_g)
        with_sample = g == MOE_GROUPS - 1
        n_slots = n_g * TOP_K
        if with_sample:
            x2_s, h_s, route_s, ret_s, pool_s, cnt = _layer(
                x_sample, 0, db, PAST_LEN, dseq, SAMPLE_STREAMS, SAMPLE_STREAMS, state_ret[0], hist_s,
                mkb_s, mvb_s, wts, cnt)
            n_slots += n_s * TOP_K
        n_blocks = -(-(n_slots + N_EXPERTS * (EXPERT_ROWS - 1)) // EXPERT_ROWS)
        pstart, block_e, n_used = _expert_layout(cnt[:, 0].astype(jnp.int32), n_blocks)
        rows_g, gates_g = _slot_rows(route_g, pstart, PROMPT_STREAMS, seq // PROMPT_TILE, PROMPT_TILE)
        sources = [(h_g.reshape(n_g, half), rows_g)]
        if with_sample:
            rows_s, gates_s = _slot_rows(route_s, pstart, SAMPLE_STREAMS, 1, dseq)
            sources.append((h_s.reshape(n_s, half), rows_s))
        xs = _dispatch(sources, n_blocks * EXPERT_ROWS)
        yb = _expert_ffn(block_e, n_used, xs, w_gate_up[0], bgu, w_down[0], bd, n_blocks)
        if with_sample:
            y_s = _collect_combine(yb, x2_s.reshape(n_s, D_MODEL), rows_s, gates_s, final_w, 1, None, n_s, 0)
        y_p = _collect_combine(yb, x2_g.reshape(n_g, D_MODEL), rows_g, gates_g, final_w, COMBINE_CHUNKS,
                               y_p, b * seq, g * n_g)
    y_p = y_p.reshape(b, seq, D_MODEL)
    y_s = y_s.reshape(db, dseq, D_MODEL)
    ret_p = jnp.concatenate(rets, axis=0)
    pool_p = jnp.concatenate(pools, axis=0)
    return (y_p, y_s, mk_p[None], mv_p[None], ret_p[None], pool_p[None], ret_s[None], pool_s[None])
```

```python
import functools

import numpy as np
import jax
import jax.numpy as jnp
from jax import lax
from jax.experimental import pallas as pl
from jax.experimental.pallas import tpu as pltpu
from jax.experimental.pallas import tpu_sc as plsc

D_MODEL = 1024
CHUNK = 64
PAST_LEN = 4096
RET_HEADS = 4
RET_DK = 128
RET_DV = 128
RET_QK = RET_HEADS * RET_DK
RET_VW = RET_HEADS * RET_DV
ROPE_BASE = 10000.0
POOL_WINDOWS = (2, 4, 8, 16)
POOL_GROUPS = 4
POOL_WIDTH = D_MODEL // 2
POOL_C = POOL_WIDTH // POOL_GROUPS
POOL_HIST = max(POOL_WINDOWS) - 1
HIST_ROWS = POOL_HIST + 1
IN_WIDTH = 2 * RET_QK + 2 * RET_VW + POOL_WIDTH
N_MEM = 256
MEM_HEADS = 4
MEM_HD = D_MODEL // MEM_HEADS
N_EXPERTS = 32
TOP_K = 4
D_FF = D_MODEL
SWIGLU_LIMIT = 7.0
SWIGLU_ALPHA = 1.702
EPS = 1e-5

LANES = 128
ROUTE_ROWS = 16
PROMPT_TILE = 256
PROMPT_STREAMS = 4
PROMPT_CHAIN = 2
SAMPLE_STREAMS = 8
EXPERT_ROWS = 512
KV_STREAMS = 2
COMBINE_ROWS = 512
COMBINE_CHUNKS = 2
MOE_GROUPS = 2
VMEM_LIMIT = 56 * 1024 * 1024

BF16 = jnp.bfloat16
F32 = jnp.float32


def _rms(x, w):
    return x * lax.rsqrt(jnp.mean(x * x, axis=-1, keepdims=True) + EPS) * w


def _dot(a, b):
    return jnp.dot(a, b, preferred_element_type=F32)


def _rms_dot(x, w_norm, w_mat):
    inv = lax.rsqrt(jnp.mean(x * x, axis=-1, keepdims=True) + EPS)
    return _dot((x * w_norm).astype(BF16), w_mat) * inv


def _dot_nt(a, b):
    return lax.dot_general(a, b, (((1,), (1,)), ((), ())), preferred_element_type=F32)


def _pack_bf16(x):
    bits = lax.bitcast_convert_type(x.astype(BF16).astype(F32), jnp.int32)
    w = x.shape[1] // 2
    return lax.shift_right_logical(bits[:, :w], 16) | (bits[:, w:] & -65536)


def _unpack_bf16(p):
    lo = lax.bitcast_convert_type(lax.shift_left(p, 16), F32)
    hi = lax.bitcast_convert_type(p & -65536, F32)
    return lo, hi


def _const_spec(shape):
    nd = len(shape)
    return pl.BlockSpec(shape, lambda *_: (0,) * nd, pipeline_mode=pl.Buffered(1))


def _mem_kv_kernel(mem_ref, nw_ref, w_ref, k_ref, v_ref, kb_ref, vb_ref):
    ns = mem_ref.shape[0]
    xn = _rms(mem_ref[...].reshape(ns * N_MEM, D_MODEL), nw_ref[...]).astype(BF16)
    kv = _dot(xn, w_ref[...])
    for hd in range(MEM_HEADS):
        lo = hd * MEM_HD
        k_ref[:, :, hd, :] = kv[:, lo:lo + MEM_HD].reshape(ns, N_MEM, MEM_HD)
        v_ref[:, :, hd, :] = kv[:, D_MODEL + lo:D_MODEL + lo + MEM_HD].reshape(ns, N_MEM, MEM_HD)
    kb_ref[...] = kv[:, :D_MODEL].astype(BF16).reshape(ns, N_MEM, D_MODEL)
    vb_ref[...] = kv[:, D_MODEL:].astype(BF16).reshape(ns, N_MEM, D_MODEL)


def _mem_kv(mem, mem_norm_w, w_kv_bf):
    b = mem.shape[0]
    out_spec = pl.BlockSpec((KV_STREAMS, N_MEM, MEM_HEADS, MEM_HD), lambda i: (i, 0, 0, 0))
    flat_spec = pl.BlockSpec((KV_STREAMS, N_MEM, D_MODEL), lambda i: (i, 0, 0))
    return pl.pallas_call(
        _mem_kv_kernel,
        grid=(b // KV_STREAMS,),
        in_specs=[
            pl.BlockSpec((KV_STREAMS, N_MEM, D_MODEL), lambda i: (i, 0, 0)),
            _const_spec((1, D_MODEL)),
            _const_spec((D_MODEL, 2 * D_MODEL)),
        ],
        out_specs=[out_spec, out_spec, flat_spec, flat_spec],
        out_shape=[jax.ShapeDtypeStruct((b, N_MEM, MEM_HEADS, MEM_HD), F32)] * 2
        + [jax.ShapeDtypeStruct((b, N_MEM, D_MODEL), BF16)] * 2,
        compiler_params=pltpu.CompilerParams(
            dimension_semantics=("arbitrary",), vmem_limit_bytes=VMEM_LIMIT),
        name="mem_kv",
    )(mem, mem_norm_w, w_kv_bf)


def _kv_flat_kernel(k_ref, v_ref, kb_ref, vb_ref):
    for hd in range(MEM_HEADS):
        kb_ref[:, :, hd * MEM_HD:(hd + 1) * MEM_HD] = k_ref[:, :, hd, :].astype(BF16)
        vb_ref[:, :, hd * MEM_HD:(hd + 1) * MEM_HD] = v_ref[:, :, hd, :].astype(BF16)


def _kv_flat(mem_k, mem_v):
    b = mem_k.shape[0]
    in_spec = pl.BlockSpec((KV_STREAMS, N_MEM, MEM_HEADS, MEM_HD), lambda i: (i, 0, 0, 0))
    out_spec = pl.BlockSpec((KV_STREAMS, N_MEM, D_MODEL), lambda i: (i, 0, 0))
    return pl.pallas_call(
        _kv_flat_kernel,
        grid=(b // KV_STREAMS,),
        in_specs=[in_spec, in_spec],
        out_specs=[out_spec, out_spec],
        out_shape=[jax.ShapeDtypeStruct((b, N_MEM, D_MODEL), BF16)] * 2,
        compiler_params=pltpu.CompilerParams(
            dimension_semantics=("arbitrary",), vmem_limit_bytes=VMEM_LIMIT),
        name="kv_flat",
    )(mem_k, mem_v)


def _layer_kernel(x_ref, cos_ref, sin_ref, dmat_ref, qdec_ref, kdec_ref, state0_ref, hist0_ref,
                  mk_ref, mv_ref, nmix_ref, win_ref, gnw_ref, wpool_ref, pscale_ref, wout_ref,
                  nmem_ref, wq_ref, wo_ref, nffn_ref, rw_ref, rb_ref, earlier_ref, cnt0_ref,
                  x2_ref, h_ref, route_ref, rstate_ref, pstate_ref, cnt_ref,
                  s_scr, ext_scr, cnt_scr, *, tl, chain, pos0, cdec):
    t = pl.program_id(1)

    @pl.when((pl.program_id(0) == 0) & (t == 0))
    def _():
        cnt_scr[...] = cnt0_ref[...]

    @pl.when(t == 0)
    def _():
        s_scr[...] = state0_ref[...]
        ext_scr[:, 0:HIST_ROWS, :] = hist0_ref[...]

    for c0 in range(0, x_ref.shape[0], chain):
        _layer_chain(c0, t, x_ref, cos_ref, sin_ref, dmat_ref, qdec_ref, kdec_ref, nmix_ref, win_ref, gnw_ref,
                     wpool_ref, pscale_ref, wout_ref, nmem_ref, wq_ref, wo_ref, nffn_ref, rw_ref, rb_ref,
                     earlier_ref, x2_ref, h_ref, route_ref, pstate_ref, s_scr, ext_scr, mk_ref, mv_ref, cnt_scr,
                     tl=tl, chain=chain, pos0=pos0, cdec=cdec)
    rstate_ref[...] = s_scr[...]
    cnt_ref[...] = cnt_scr[...]


def _layer_chain(c0, t, x_ref, cos_ref, sin_ref, dmat_ref, qdec_ref, kdec_ref, nmix_ref, win_ref, gnw_ref,
                 wpool_ref, pscale_ref, wout_ref, nmem_ref, wq_ref, wo_ref, nffn_ref, rw_ref, rb_ref,
                 earlier_ref, x2_ref, h_ref, route_ref, pstate_ref, s_scr, ext_scr, mk_ref, mv_ref, cnt_scr,
                 *, tl, chain, pos0, cdec):
    rows = chain * tl
    x = x_ref[c0:c0 + chain].reshape(rows, D_MODEL)
    proj = _rms_dot(x, nmix_ref[...], win_ref[...])
    cos = cos_ref[...]
    sin = sin_ref[...]
    pos = (pos0 + t * tl + lax.broadcasted_iota(jnp.int32, (tl, POOL_C), 0)).astype(F32)

    mixes = []
    for sj in range(chain):
        si = c0 + sj
        pj = proj[sj * tl:(sj + 1) * tl]
        outs = []
        for hd in range(RET_HEADS):
            lo = hd * RET_DK
            q = pj[:, lo:lo + RET_DK]
            k = pj[:, RET_QK + lo:RET_QK + lo + RET_DK]
            v = pj[:, 2 * RET_QK + lo:2 * RET_QK + lo + RET_DV]
            g = pj[:, 2 * RET_QK + RET_VW + lo:2 * RET_QK + RET_VW + lo + RET_DV]
            qr = (q * cos + pltpu.roll(q, RET_DK // 2, 1) * sin) * (RET_DK ** -0.5)
            kr = k * cos + pltpu.roll(k, RET_DK // 2, 1) * sin
            vb = v.astype(BF16)
            s = _dot_nt(qr.astype(BF16), kr.astype(BF16)) * dmat_ref[hd]
            o = _dot(s.astype(BF16), vb)
            state = s_scr[si, hd]
            o = o + _dot((qr * qdec_ref[hd]).astype(BF16), state.astype(BF16))
            kd_t = jnp.transpose(kr * kdec_ref[hd]).astype(BF16)
            s_scr[si, hd] = cdec[hd] * state + _dot(kd_t, vb)
            mu = jnp.mean(o, axis=-1, keepdims=True)
            oc = o - mu
            var = jnp.mean(oc * oc, axis=-1, keepdims=True)
            on = oc * lax.rsqrt(var + EPS)
            outs.append(on * gnw_ref[:, lo:lo + RET_DV] * (g * jax.nn.sigmoid(g)))

        pin = pj[:, 2 * RET_QK + 2 * RET_VW:]
        ext_scr[si, HIST_ROWS:HIST_ROWS + tl, :] = pin
        pstate_ref[si] = pin[tl - POOL_HIST:, :]
        for gi, w in enumerate(POOL_WINDOWS):
            lo = gi * POOL_C
            wsum = ext_scr[si, :, lo:lo + POOL_C]
            shift = 1
            while shift < w:
                wsum = wsum + pltpu.roll(wsum, shift, 0)
                shift *= 2
            cnt = jnp.minimum(float(w), pos + 1.0)
            d = wsum[HIST_ROWS:, :] / cnt - pin[:, lo:lo + POOL_C]
            y = _dot(d.astype(BF16), wpool_ref[gi])
            outs.append(y * pscale_ref[:, lo:lo + POOL_C])
        ext_scr[si, 0:HIST_ROWS, :] = ext_scr[si, tl:tl + HIST_ROWS, :]
        mixes.append(jnp.concatenate(outs, axis=-1).astype(BF16))

    x1 = x + _dot(jnp.concatenate(mixes, axis=0), wout_ref[...])

    qm = _rms_dot(x1, nmem_ref[...], wq_ref[...])
    atts = []
    for sj in range(chain):
        si = c0 + sj
        aouts = []
        for hd in range(MEM_HEADS):
            lo = hd * MEM_HD
            qh = qm[sj * tl:(sj + 1) * tl, lo:lo + MEM_HD].astype(BF16)
            s = _dot_nt(qh, mk_ref[si, :, lo:lo + MEM_HD]) * (MEM_HD ** -0.5)
            e = jnp.exp(s - jnp.max(s, axis=-1, keepdims=True))
            p = e / jnp.sum(e, axis=-1, keepdims=True)
            aouts.append(_dot(p.astype(BF16), mv_ref[si, :, lo:lo + MEM_HD]))
        atts.append(jnp.concatenate(aouts, axis=-1).astype(BF16))
    x2 = x1 + _dot(jnp.concatenate(atts, axis=0), wo_ref[...])
    x2_ref[c0:c0 + chain] = x2.reshape(chain, tl, D_MODEL)

    hn = _rms(x2, nffn_ref[...])
    h_ref[c0:c0 + chain] = _pack_bf16(hn).reshape(chain, tl, D_MODEL // 2)
    lane_tile = lambda a: a[:, :rows] if rows <= LANES else jnp.concatenate([a] * (rows // LANES), axis=1)
    logits = _dot_nt(rw_ref[...], hn.astype(BF16)) + lane_tile(rb_ref[...])
    eiota = lax.broadcasted_iota(jnp.int32, (N_EXPERTS, rows), 0)
    neg = jnp.finfo(F32).min
    vals, idxs = [], []
    for _k in range(TOP_K):
        m = jnp.max(logits, axis=0, keepdims=True)
        idx = jnp.min(jnp.where(logits == m, eiota, N_EXPERTS), axis=0, keepdims=True)
        vals.append(m)
        idxs.append(idx)
        logits = jnp.where(eiota == idx, neg, logits)
    exps = [jnp.exp(vk - vals[0]) for vk in vals]
    den = exps[0] + exps[1] + exps[2] + exps[3]

    onehots = [(eiota == idx).astype(F32) for idx in idxs]
    picked = onehots[0] + onehots[1] + onehots[2] + onehots[3]
    before = lane_tile(cnt_scr[...]) + _dot(picked.astype(BF16), earlier_ref[...])
    ranks = [jnp.sum(oh * before, axis=0, keepdims=True) for oh in onehots]
    cnt_scr[...] = cnt_scr[...] + jnp.sum(picked, axis=1, keepdims=True)

    rowi = lax.broadcasted_iota(jnp.int32, (ROUTE_ROWS, rows), 0)
    route = jnp.zeros((ROUTE_ROWS, rows), F32)
    for kk in range(TOP_K):
        route = jnp.where(rowi == kk, exps[kk] / den, route)
        route = jnp.where(rowi == TOP_K + kk, idxs[kk].astype(F32), route)
        route = jnp.where(rowi == 2 * TOP_K + kk, ranks[kk], route)
    route_ref[0, :, c0 * tl:c0 * tl + rows] = route


def _decay_tables(tl):
    hh = np.arange(RET_HEADS, dtype=np.float64)
    log_g = np.log1p(-np.exp2(-5.0 - hh))
    idx = np.arange(tl, dtype=np.float64)
    dist = np.abs(idx[:, None] - idx[None, :])
    visible = (idx[None, :] // CHUNK) <= (idx[:, None] // CHUNK)
    dmat = np.where(visible[None], np.exp(log_g[:, None, None] * dist[None]), 0.0)
    qdec = np.exp(log_g[:, None] * (idx[None, :] + 1.0))
    kdec = np.exp(log_g[:, None] * (tl - 1.0 - idx[None, :]))
    cdec = tuple(float(c) for c in np.exp(log_g * tl).astype(np.float32))
    bcast = lambda a: np.ascontiguousarray(np.broadcast_to(a[:, :, None], (RET_HEADS, tl, RET_DK)))
    return (jnp.asarray(dmat, F32), jnp.asarray(bcast(qdec), F32), jnp.asarray(bcast(kdec), F32), cdec)


def _rotary_tables(pos0, length):
    half = RET_DK // 2
    inv_freq = jnp.power(ROPE_BASE, -jnp.arange(half, dtype=F32) / half)
    ang = (pos0 + jnp.arange(length, dtype=jnp.int32)).astype(F32)[:, None] * inv_freq[None, :]
    cos, sin = jnp.cos(ang), jnp.sin(ang)
    return jnp.concatenate([cos, cos], axis=-1), jnp.concatenate([-sin, sin], axis=-1)


def _layer(x, stream0, b, pos0, tl, ns, chain, state0, hist0, mk, mv, wts, cnt0):
    length = x.shape[1]
    nt = length // tl
    g0 = stream0 // ns
    rows = ns * tl
    cos, sin = _rotary_tables(pos0, length)
    dmat, qdec, kdec, cdec = _decay_tables(tl)
    crows = chain * tl
    earlier = jnp.asarray(np.triu(np.ones((crows, crows), np.float32), 1), BF16)
    kern = functools.partial(_layer_kernel, tl=tl, chain=chain, pos0=pos0, cdec=cdec)
    tok = lambda width: pl.BlockSpec((ns, tl, width), lambda i, j: (i, j, 0))
    per_stream = lambda *shape: pl.BlockSpec((ns,) + shape, lambda i, j: (i,) + (0,) * len(shape))
    per_stream_in = lambda off, *shape: pl.BlockSpec(
        (ns,) + shape, lambda i, j: (i + off,) + (0,) * len(shape), pipeline_mode=pl.Buffered(1))
    in_specs = [
        pl.BlockSpec((ns, tl, D_MODEL), lambda i, j: (i + g0, j, 0)),
        pl.BlockSpec((tl, RET_DK), lambda i, j: (j, 0)),
        pl.BlockSpec((tl, RET_DK), lambda i, j: (j, 0)),
        _const_spec((RET_HEADS, tl, tl)),
        _const_spec((RET_HEADS, tl, RET_DK)),
        _const_spec((RET_HEADS, tl, RET_DK)),
        per_stream_in(0, RET_HEADS, RET_DK, RET_DV),
        per_stream_in(0, HIST_ROWS, POOL_WIDTH),
        per_stream_in(g0, N_MEM, D_MODEL),
        per_stream_in(g0, N_MEM, D_MODEL),
    ] + [_const_spec(w.shape) for w in wts] + [_const_spec((crows, crows)), _const_spec((N_EXPERTS, LANES))]
    out_specs = [
        tok(D_MODEL), tok(D_MODEL // 2),
        pl.BlockSpec((1, ROUTE_ROWS, rows), lambda i, j: (i * nt + j, 0, 0)),
        per_stream(RET_HEADS, RET_DK, RET_DV),
        per_stream(POOL_HIST, POOL_WIDTH),
        pl.BlockSpec((N_EXPERTS, LANES), lambda i, j: (0, 0)),
    ]
    out_shape = [
        jax.ShapeDtypeStruct((b, length, D_MODEL), F32),
        jax.ShapeDtypeStruct((b, length, D_MODEL // 2), jnp.int32),
        jax.ShapeDtypeStruct((b // ns * nt, ROUTE_ROWS, rows), F32),
        jax.ShapeDtypeStruct((b, RET_HEADS, RET_DK, RET_DV), F32),
        jax.ShapeDtypeStruct((b, POOL_HIST, POOL_WIDTH), F32),
        jax.ShapeDtypeStruct((N_EXPERTS, LANES), F32),
    ]
    scratch = [
        pltpu.VMEM((ns, RET_HEADS, RET_DK, RET_DV), F32),
        pltpu.VMEM((ns, HIST_ROWS + tl, POOL_WIDTH), F32),
        pltpu.VMEM((N_EXPERTS, LANES), F32),
    ]
    return pl.pallas_call(
        kern,
        grid=(b // ns, nt),
        in_specs=in_specs,
        out_specs=out_specs,
        out_shape=out_shape,
        scratch_shapes=scratch,
        compiler_params=pltpu.CompilerParams(
            dimension_semantics=("arbitrary", "arbitrary"), vmem_limit_bytes=VMEM_LIMIT),
        name="layer_tl%d" % tl,
    )(x, cos, sin, dmat, qdec, kdec, state0, hist0, mk, mv, *wts, earlier, cnt0)


SPLIT_COLS = 2 * LANES


def _expert_block(x_ref, wgu, bgu_ref, wd, bd_ref, y_ref):
    half = D_MODEL // 2
    x_lo, x_hi = _unpack_bf16(x_ref[...])
    gu = _dot(x_lo.astype(BF16), wgu[:half, :]) + _dot(x_hi.astype(BF16), wgu[half:, :]) + bgu_ref[0]
    gate = jnp.minimum(gu[:, :D_FF], SWIGLU_LIMIT)
    up = jnp.clip(gu[:, D_FF:], -SWIGLU_LIMIT, SWIGLU_LIMIT)
    act = (up + 1.0) * gate * jax.nn.sigmoid(SWIGLU_ALPHA * gate)
    y_ref[...] = _pack_bf16(_dot(act.astype(BF16), wd[...]) + bd_ref[0])


def _expert_convert_kernel(be_ref, used_ref, x_ref, wgu_ref, bgu_ref, wd_ref, bd_ref, perm_ref,
                           y_ref, wgu_out_ref, wd_out_ref):
    i = pl.program_id(0)
    in_use = i < used_ref[0]

    @pl.when(in_use & ((i == 0) | (be_ref[i] != be_ref[jnp.maximum(i - 1, 0)])))
    def _():
        perm = perm_ref[...]
        for c in range(2 * D_FF // SPLIT_COLS):
            wc = wgu_ref[0, :, c * SPLIT_COLS:(c + 1) * SPLIT_COLS].astype(BF16)
            pc = _dot(wc, perm).astype(BF16)
            wgu_out_ref[0, :, c * LANES:(c + 1) * LANES] = pc[:, :LANES]
            wgu_out_ref[0, :, D_FF + c * LANES:D_FF + (c + 1) * LANES] = pc[:, LANES:]
        wd_out_ref[0] = wd_ref[0].astype(BF16)

    @pl.when(in_use)
    def _():
        _expert_block(x_ref, wgu_out_ref.at[0], bgu_ref, wd_out_ref.at[0], bd_ref, y_ref)


def _expert_ready_kernel(be_ref, used_ref, x_ref, wgu_ref, bgu_ref, wd_ref, bd_ref, y_ref):
    del be_ref

    @pl.when(pl.program_id(0) < used_ref[0])
    def _():
        _expert_block(x_ref, wgu_ref.at[0], bgu_ref, wd_ref.at[0], bd_ref, y_ref)


def _expert_ffn(block_e, n_used, xs, wgu, bgu, wd, bd, n_blocks):
    convert = wgu.dtype != BF16
    blk = lambda i, be, used: (jnp.minimum(i, used[0] - 1), 0)
    per_expert = lambda i, be, used: (be[i], 0, 0)
    in_specs = [
        pl.BlockSpec((EXPERT_ROWS, D_MODEL // 2), blk),
        pl.BlockSpec((1, D_MODEL, 2 * D_FF), per_expert),
        pl.BlockSpec((1, 1, 2 * D_FF), per_expert),
        pl.BlockSpec((1, D_FF, D_MODEL), per_expert),
        pl.BlockSpec((1, 1, D_MODEL), per_expert),
    ]
    out_specs = [pl.BlockSpec((EXPERT_ROWS, D_MODEL // 2), blk)]
    out_shape = [jax.ShapeDtypeStruct(xs.shape, jnp.int32)]
    args = [block_e, n_used, xs, wgu, bgu, wd, bd]
    if convert:
        perm = np.zeros((SPLIT_COLS, SPLIT_COLS), np.float32)
        j = np.arange(LANES)
        perm[2 * j, j] = 1.0
        perm[2 * j + 1, LANES + j] = 1.0
        in_specs.append(_const_spec((SPLIT_COLS, SPLIT_COLS)))
        args.append(jnp.asarray(perm, BF16))
        out_specs += [pl.BlockSpec((1, D_MODEL, 2 * D_FF), per_expert), pl.BlockSpec((1, D_FF, D_MODEL), per_expert)]
        out_shape += [jax.ShapeDtypeStruct(wgu.shape, BF16), jax.ShapeDtypeStruct(wd.shape, BF16)]
    grid_spec = pltpu.PrefetchScalarGridSpec(
        num_scalar_prefetch=2, grid=(n_blocks,), in_specs=in_specs, out_specs=out_specs)
    return pl.pallas_call(
        _expert_convert_kernel if convert else _expert_ready_kernel,
        grid_spec=grid_spec,
        out_shape=out_shape,
        compiler_params=pltpu.CompilerParams(
            dimension_semantics=("arbitrary",), vmem_limit_bytes=VMEM_LIMIT),
        name="expert_ffn_convert" if convert else "expert_ffn",
    )(*args)


def _expert_layout(counts, n_blocks, every_expert):
    pcounts = (counts + EXPERT_ROWS - 1) // EXPERT_ROWS * EXPERT_ROWS
    if every_expert:
        pcounts = jnp.maximum(pcounts, EXPERT_ROWS)
    pend = jnp.cumsum(pcounts)
    n_used = pend[-1:] // EXPERT_ROWS
    block_start = jnp.minimum(jnp.arange(n_blocks, dtype=jnp.int32), n_used[0] - 1) * EXPERT_ROWS
    block_e = jnp.sum((pend[None, :] <= block_start[:, None]).astype(jnp.int32), axis=1)
    return pend - pcounts, jnp.minimum(block_e, N_EXPERTS - 1), n_used


def _slot_rows(route, pstart, ns, nt, tl):
    n = route.shape[0] * ns * tl
    fields = lambda lo: route[:, lo:lo + TOP_K, :].reshape(-1, nt, TOP_K, ns, tl)
    per_slot = lambda lo: jnp.transpose(fields(lo), (2, 0, 3, 1, 4)).reshape(TOP_K, n)
    e = per_slot(TOP_K).astype(jnp.int32)
    rank = per_slot(2 * TOP_K).astype(jnp.int32)
    hit = e[:, :, None] == jnp.arange(N_EXPERTS, dtype=jnp.int32)[None, None, :]
    rows = rank + jnp.sum(jnp.where(hit, pstart[None, None, :], 0), axis=-1)
    gates = jnp.transpose(fields(0), (0, 3, 1, 4, 2)).reshape(n, TOP_K)
    return rows, gates


SC_WINDOW = 128
SC_COLS = 256


def _sc_mesh():
    return plsc.VectorSubcoreMesh(core_axis_name="c", subcore_axis_name="s")


def _dispatch(sources, m_pad):
    width = sources[0][0].shape[1]
    n_src = len(sources)

    @functools.partial(pl.kernel, mesh=_sc_mesh(),
                       out_type=jax.ShapeDtypeStruct((m_pad, width), sources[0][0].dtype), scratch_types=[])
    def k(*refs):
        xs_hbm = refs[2 * n_src]

        def body(x_vmem, i_vmem):
            j = pl.program_id(1)
            for kk in range(TOP_K):
                pltpu.sync_copy(x_vmem, xs_hbm.at[i_vmem.at[kk], pl.ds(j * SC_COLS, SC_COLS)])

        for si in range(n_src):
            src, rows = refs[2 * si], refs[2 * si + 1]
            pltpu.emit_pipeline(
                body,
                grid=(src.shape[0] // SC_WINDOW, width // SC_COLS),
                in_specs=[pl.BlockSpec((SC_WINDOW, SC_COLS), lambda i, j: (i, j)),
                          pl.BlockSpec((TOP_K, SC_WINDOW), lambda i, j: (0, i))],
                out_specs=[],
                core_axis_name=("c", "s"),
                dimension_semantics=(pltpu.PARALLEL, pltpu.ARBITRARY),
            )(src, rows)

    return k(*[a for pair in sources for a in pair])


def _collect(yb, rows):
    width = yb.shape[1]

    @functools.partial(pl.kernel, mesh=_sc_mesh(),
                       out_type=jax.ShapeDtypeStruct((rows.shape[1], width), yb.dtype), scratch_types=[])
    def k(yb_hbm, r_hbm, o_hbm):
        def body(i_vmem, o_vmem):
            j = pl.program_id(1)
            pltpu.sync_copy(yb_hbm.at[i_vmem.at[0], pl.ds(j * SC_COLS, SC_COLS)], o_vmem)

        pltpu.emit_pipeline(
            body,
            grid=(rows.shape[1] // SC_WINDOW, width // SC_COLS),
            in_specs=[pl.BlockSpec((1, SC_WINDOW), lambda i, j: (0, i))],
            out_specs=[pl.BlockSpec((SC_WINDOW, SC_COLS), lambda i, j: (i, j))],
            core_axis_name=("c", "s"),
            dimension_semantics=(pltpu.PARALLEL, pltpu.ARBITRARY),
        )(r_hbm, o_hbm)

    return k(yb, rows)


def _combine_kernel(x_ref, yg_ref, route_ref, fw_ref, y_ref):
    half = D_MODEL // 2
    acc_lo = x_ref[:, :half]
    acc_hi = x_ref[:, half:]
    route = route_ref[...]
    for kk in range(TOP_K):
        y_lo, y_hi = _unpack_bf16(yg_ref[kk])
        gate = route[:, kk:kk + 1]
        acc_lo = acc_lo + y_lo * gate
        acc_hi = acc_hi + y_hi * gate
    ms = (jnp.sum(acc_lo * acc_lo, axis=-1, keepdims=True)
          + jnp.sum(acc_hi * acc_hi, axis=-1, keepdims=True)) * (1.0 / D_MODEL)
    scale = lax.rsqrt(ms + EPS)
    y_ref[:, :half] = acc_lo * scale * fw_ref[:, :half]
    y_ref[:, half:] = acc_hi * scale * fw_ref[:, half:]


def _combine_next_kernel(x_ref, yg_ref, route_ref, fw_ref, prev_ref, y_ref):
    del prev_ref
    _combine_kernel(x_ref, yg_ref, route_ref, fw_ref, y_ref)


def _combine(x2, yg, gates, final_w, y_prev, first_row, out_rows, out_first_row):
    blk_in = first_row // COMBINE_ROWS
    blk_out = out_first_row // COMBINE_ROWS
    in_specs = [
        pl.BlockSpec((COMBINE_ROWS, D_MODEL), lambda i: (blk_in + i, 0)),
        pl.BlockSpec((TOP_K, COMBINE_ROWS, D_MODEL // 2), lambda i: (0, i, 0)),
        pl.BlockSpec((COMBINE_ROWS, TOP_K), lambda i: (blk_in + i, 0)),
        _const_spec((1, D_MODEL)),
    ]
    args = [x2, yg, gates, final_w]
    kern, aliases = _combine_kernel, {}
    if y_prev is not None:
        in_specs.append(pl.BlockSpec(memory_space=pl.ANY))
        args.append(y_prev)
        kern, aliases = _combine_next_kernel, {len(args) - 1: 0}
    return pl.pallas_call(
        kern,
        grid=(yg.shape[1] // COMBINE_ROWS,),
        in_specs=in_specs,
        out_specs=pl.BlockSpec((COMBINE_ROWS, D_MODEL), lambda i: (blk_out + i, 0)),
        out_shape=jax.ShapeDtypeStruct((out_rows, D_MODEL), F32),
        input_output_aliases=aliases,
        compiler_params=pltpu.CompilerParams(
            dimension_semantics=("arbitrary",), vmem_limit_bytes=VMEM_LIMIT),
        name="combine_%d_%d" % (out_rows, out_first_row),
    )(*args)


def _collect_combine(yb, x2, rows, gates, final_w, chunks, y, out_rows, out_first_row):
    nc = x2.shape[0] // chunks
    for c in range(chunks):
        yg = _collect(yb, rows[:, c * nc:(c + 1) * nc].reshape(1, TOP_K * nc))
        y = _combine(x2, yg.reshape(TOP_K, nc, D_MODEL // 2), gates, final_w, y, c * nc,
                     out_rows, out_first_row + c * nc)
    return y


def kernel(x_prompt, x_sample, cache_mem_k, cache_mem_v, state_ret, state_pool, mem_prompt,
           norm_mix_w, w_in, ret_gn_w, w_pool, pool_scale, w_out, norm_mem_w, mem_norm_w,
           w_q_mem, w_kv_mem, w_o_mem, norm_ffn_w, router_w, router_b, w_gate_up, b_gate_up,
           w_down, b_down, final_norm_w):
    assert norm_mix_w.shape[0] == 1, "one layer"
    b, seq, _ = x_prompt.shape
    db, dseq, _ = x_sample.shape
    row = lambda a: a.reshape(1, -1)

    mk_p, mv_p, mkb_p, mvb_p = _mem_kv(mem_prompt, row(mem_norm_w[0]), w_kv_mem[0].astype(BF16))
    mkb_s, mvb_s = _kv_flat(cache_mem_k[0], cache_mem_v[0])

    wts = (row(norm_mix_w[0]), w_in[0].astype(BF16), row(ret_gn_w[0]), w_pool[0].astype(BF16),
           row(pool_scale[0]), w_out[0].astype(BF16), row(norm_mem_w[0]), w_q_mem[0].astype(BF16),
           w_o_mem[0].astype(BF16), row(norm_ffn_w[0]), router_w[0].T.astype(BF16),
           jnp.broadcast_to(router_b[0][:, None], (N_EXPERTS, LANES)))

    per = b // MOE_GROUPS
    n_g, n_s = per * seq, db * dseq
    half = D_MODEL // 2
    final_w = row(final_norm_w)
    bgu = jnp.concatenate([b_gate_up[0][:, 0::2], b_gate_up[0][:, 1::2]], axis=-1).reshape(N_EXPERTS, 1, 2 * D_FF)
    bd = b_down[0].reshape(N_EXPERTS, 1, D_MODEL)
    zero_state = jnp.zeros((per, RET_HEADS, RET_DK, RET_DV), F32)
    zero_hist = jnp.zeros((per, HIST_ROWS, POOL_WIDTH), F32)
    no_counts = jnp.zeros((N_EXPERTS, LANES), F32)
    hist_s = jnp.concatenate([jnp.zeros((db, 1, POOL_WIDTH), F32), state_pool[0]], axis=1)

    y_p, rets, pools = None, [], []
    for g in range(MOE_GROUPS):
        x2_g, h_g, route_g, ret_g, pool_g, cnt = _layer(
            x_prompt, g * per, per, 0, PROMPT_TILE, PROMPT_STREAMS, PROMPT_CHAIN, zero_state, zero_hist,
            mkb_p, mvb_p, wts, no_counts)
        rets.append(ret_g)
        pools.append(pool_g)
        with_sample = g == MOE_GROUPS - 1
        n_slots = n_g * TOP_K
        if with_sample:
            x2_s, h_s, route_s, ret_s, pool_s, cnt = _layer(
                x_sample, 0, db, PAST_LEN, dseq, SAMPLE_STREAMS, SAMPLE_STREAMS, state_ret[0], hist_s,
                mkb_s, mvb_s, wts, cnt)
            n_slots += n_s * TOP_K
        first = g == 0
        n_blocks = -(-(n_slots + N_EXPERTS * (EXPERT_ROWS - 1)) // EXPERT_ROWS) + (N_EXPERTS if first else 0)
        pstart, block_e, n_used = _expert_layout(cnt[:, 0].astype(jnp.int32), n_blocks, first)
        rows_g, gates_g = _slot_rows(route_g, pstart, PROMPT_STREAMS, seq // PROMPT_TILE, PROMPT_TILE)
        sources = [(h_g.reshape(n_g, half), rows_g)]
        if with_sample:
            rows_s, gates_s = _slot_rows(route_s, pstart, SAMPLE_STREAMS, 1, dseq)
            sources.append((h_s.reshape(n_s, half), rows_s))
        xs = _dispatch(sources, n_blocks * EXPERT_ROWS)
        if first:
            yb, wgu_bf, wd_bf = _expert_ffn(block_e, n_used, xs, w_gate_up[0], bgu, w_down[0], bd, n_blocks)
        else:
            (yb,) = _expert_ffn(block_e, n_used, xs, wgu_bf, bgu, wd_bf, bd, n_blocks)
        if with_sample:
            y_s = _collect_combine(yb, x2_s.reshape(n_s, D_MODEL), rows_s, gates_s, final_w, 1, None, n_s, 0)
        y_p = _collect_combine(yb, x2_g.reshape(n_g, D_MODEL), rows_g, gates_g, final_w, COMBINE_CHUNKS,
                               y_p, b * seq, g * n_g)
    y_p = y_p.reshape(b, seq, D_MODEL)
    y_s = y_s.reshape(db, dseq, D_MODEL)
    ret_p = jnp.concatenate(rets, axis=0)
    pool_p = jnp.concatenate(pools, axis=0)
    return (y_p, y_s, mk_p[None], mv_p[None], ret_p[None], pool_p[None], ret_s[None], pool_s[None])
```

```python
import functools

import numpy as np
import jax
import jax.numpy as jnp
from jax import lax
from jax.experimental import pallas as pl
from jax.experimental.pallas import tpu as pltpu
from jax.experimental.pallas import tpu_sc as plsc

D_MODEL = 1024
CHUNK = 64
PAST_LEN = 4096
RET_HEADS = 4
RET_DK = 128
RET_DV = 128
RET_QK = RET_HEADS * RET_DK
RET_VW = RET_HEADS * RET_DV
ROPE_BASE = 10000.0
POOL_WINDOWS = (2, 4, 8, 16)
POOL_GROUPS = 4
POOL_WIDTH = D_MODEL // 2
POOL_C = POOL_WIDTH // POOL_GROUPS
POOL_HIST = max(POOL_WINDOWS) - 1
HIST_ROWS = POOL_HIST + 1
IN_WIDTH = 2 * RET_QK + 2 * RET_VW + POOL_WIDTH
N_MEM = 256
MEM_HEADS = 4
MEM_HD = D_MODEL // MEM_HEADS
N_EXPERTS = 32
TOP_K = 4
D_FF = D_MODEL
SWIGLU_LIMIT = 7.0
SWIGLU_ALPHA = 1.702
EPS = 1e-5

LANES = 128
ROUTE_ROWS = 16
PROMPT_TILE = 256
PROMPT_STREAMS = 4
PROMPT_CHAIN = 2
SAMPLE_STREAMS = 8
EXPERT_ROWS = 512
KV_STREAMS = 2
COMBINE_ROWS = 512
COMBINE_CHUNKS = 2
MOE_GROUPS = 2
VMEM_LIMIT = 56 * 1024 * 1024

BF16 = jnp.bfloat16
F32 = jnp.float32


def _rms(x, w):
    return x * lax.rsqrt(jnp.mean(x * x, axis=-1, keepdims=True) + EPS) * w


def _dot(a, b):
    return jnp.dot(a, b, preferred_element_type=F32)


def _rms_dot(x, w_norm, w_mat):
    inv = lax.rsqrt(jnp.mean(x * x, axis=-1, keepdims=True) + EPS)
    return _dot((x * w_norm).astype(BF16), w_mat) * inv


def _dot_nt(a, b):
    return lax.dot_general(a, b, (((1,), (1,)), ((), ())), preferred_element_type=F32)


def _pack_bf16(x):
    bits = lax.bitcast_convert_type(x.astype(BF16).astype(F32), jnp.int32)
    w = x.shape[1] // 2
    return lax.shift_right_logical(bits[:, :w], 16) | (bits[:, w:] & -65536)


def _unpack_bf16(p):
    lo = lax.bitcast_convert_type(lax.shift_left(p, 16), F32)
    hi = lax.bitcast_convert_type(p & -65536, F32)
    return lo, hi


def _const_spec(shape):
    nd = len(shape)
    return pl.BlockSpec(shape, lambda *_: (0,) * nd, pipeline_mode=pl.Buffered(1))


def _mem_kv_kernel(mem_ref, nw_ref, w_ref, k_ref, v_ref, kb_ref, vb_ref):
    ns = mem_ref.shape[0]
    xn = _rms(mem_ref[...].reshape(ns * N_MEM, D_MODEL), nw_ref[...]).astype(BF16)
    kv = _dot(xn, w_ref[...])
    for hd in range(MEM_HEADS):
        lo = hd * MEM_HD
        k_ref[:, :, hd, :] = kv[:, lo:lo + MEM_HD].reshape(ns, N_MEM, MEM_HD)
        v_ref[:, :, hd, :] = kv[:, D_MODEL + lo:D_MODEL + lo + MEM_HD].reshape(ns, N_MEM, MEM_HD)
    kb_ref[...] = kv[:, :D_MODEL].astype(BF16).reshape(ns, N_MEM, D_MODEL)
    vb_ref[...] = kv[:, D_MODEL:].astype(BF16).reshape(ns, N_MEM, D_MODEL)


def _mem_kv(mem, mem_norm_w, w_kv_bf):
    b = mem.shape[0]
    out_spec = pl.BlockSpec((KV_STREAMS, N_MEM, MEM_HEADS, MEM_HD), lambda i: (i, 0, 0, 0))
    flat_spec = pl.BlockSpec((KV_STREAMS, N_MEM, D_MODEL), lambda i: (i, 0, 0))
    return pl.pallas_call(
        _mem_kv_kernel,
        grid=(b // KV_STREAMS,),
        in_specs=[
            pl.BlockSpec((KV_STREAMS, N_MEM, D_MODEL), lambda i: (i, 0, 0)),
            _const_spec((1, D_MODEL)),
            _const_spec((D_MODEL, 2 * D_MODEL)),
        ],
        out_specs=[out_spec, out_spec, flat_spec, flat_spec],
        out_shape=[jax.ShapeDtypeStruct((b, N_MEM, MEM_HEADS, MEM_HD), F32)] * 2
        + [jax.ShapeDtypeStruct((b, N_MEM, D_MODEL), BF16)] * 2,
        compiler_params=pltpu.CompilerParams(
            dimension_semantics=("arbitrary",), vmem_limit_bytes=VMEM_LIMIT),
        name="mem_kv",
    )(mem, mem_norm_w, w_kv_bf)


def _kv_flat_kernel(k_ref, v_ref, kb_ref, vb_ref):
    for hd in range(MEM_HEADS):
        kb_ref[:, :, hd * MEM_HD:(hd + 1) * MEM_HD] = k_ref[:, :, hd, :].astype(BF16)
        vb_ref[:, :, hd * MEM_HD:(hd + 1) * MEM_HD] = v_ref[:, :, hd, :].astype(BF16)


def _kv_flat(mem_k, mem_v):
    b = mem_k.shape[0]
    in_spec = pl.BlockSpec((KV_STREAMS, N_MEM, MEM_HEADS, MEM_HD), lambda i: (i, 0, 0, 0))
    out_spec = pl.BlockSpec((KV_STREAMS, N_MEM, D_MODEL), lambda i: (i, 0, 0))
    return pl.pallas_call(
        _kv_flat_kernel,
        grid=(b // KV_STREAMS,),
        in_specs=[in_spec, in_spec],
        out_specs=[out_spec, out_spec],
        out_shape=[jax.ShapeDtypeStruct((b, N_MEM, D_MODEL), BF16)] * 2,
        compiler_params=pltpu.CompilerParams(
            dimension_semantics=("arbitrary",), vmem_limit_bytes=VMEM_LIMIT),
        name="kv_flat",
    )(mem_k, mem_v)


def _layer_kernel(x_ref, cos_ref, sin_ref, dmat_ref, qdec_ref, kdec_ref, state0_ref, hist0_ref,
                  mk_ref, mv_ref, nmix_ref, win_ref, gnw_ref, wpool_ref, pscale_ref, wout_ref,
                  nmem_ref, wq_ref, wo_ref, nffn_ref, rw_ref, rb_ref, earlier_ref, cnt0_ref,
                  x2_ref, h_ref, route_ref, rstate_ref, pstate_ref, cnt_ref,
                  s_scr, ext_scr, cnt_scr, *, tl, chain, pos0, cdec):
    t = pl.program_id(1)

    @pl.when((pl.program_id(0) == 0) & (t == 0))
    def _():
        cnt_scr[...] = cnt0_ref[...]

    @pl.when(t == 0)
    def _():
        s_scr[...] = state0_ref[...]
        ext_scr[:, 0:HIST_ROWS, :] = hist0_ref[...]

    waiting = [
        _layer_chain(c0, t, x_ref, cos_ref, sin_ref, dmat_ref, qdec_ref, kdec_ref, nmix_ref, win_ref, gnw_ref,
                     wpool_ref, pscale_ref, wout_ref, nmem_ref, wq_ref, wo_ref, nffn_ref, rw_ref, rb_ref,
                     earlier_ref, x2_ref, h_ref, route_ref, pstate_ref, s_scr, ext_scr, mk_ref, mv_ref, cnt_scr,
                     tl=tl, chain=chain, pos0=pos0, cdec=cdec)
        for c0 in range(0, x_ref.shape[0], chain)]
    running = []
    while waiting or running:
        if waiting:
            running.append(waiting.pop(0))
        for phases in list(running):
            if next(phases, "done") == "done":
                running.remove(phases)
    rstate_ref[...] = s_scr[...]
    cnt_ref[...] = cnt_scr[...]


def _layer_chain(c0, t, x_ref, cos_ref, sin_ref, dmat_ref, qdec_ref, kdec_ref, nmix_ref, win_ref, gnw_ref,
                 wpool_ref, pscale_ref, wout_ref, nmem_ref, wq_ref, wo_ref, nffn_ref, rw_ref, rb_ref,
                 earlier_ref, x2_ref, h_ref, route_ref, pstate_ref, s_scr, ext_scr, mk_ref, mv_ref, cnt_scr,
                 *, tl, chain, pos0, cdec):
    rows = chain * tl
    x = x_ref[c0:c0 + chain].reshape(rows, D_MODEL)
    proj = _rms_dot(x, nmix_ref[...], win_ref[...])
    cos = cos_ref[...]
    sin = sin_ref[...]
    pos = (pos0 + t * tl + lax.broadcasted_iota(jnp.int32, (tl, POOL_C), 0)).astype(F32)
    yield

    mixes = []
    for sj in range(chain):
        si = c0 + sj
        pj = proj[sj * tl:(sj + 1) * tl]
        outs = []
        for hd in range(RET_HEADS):
            lo = hd * RET_DK
            q = pj[:, lo:lo + RET_DK]
            k = pj[:, RET_QK + lo:RET_QK + lo + RET_DK]
            v = pj[:, 2 * RET_QK + lo:2 * RET_QK + lo + RET_DV]
            g = pj[:, 2 * RET_QK + RET_VW + lo:2 * RET_QK + RET_VW + lo + RET_DV]
            qr = (q * cos + pltpu.roll(q, RET_DK // 2, 1) * sin) * (RET_DK ** -0.5)
            kr = k * cos + pltpu.roll(k, RET_DK // 2, 1) * sin
            vb = v.astype(BF16)
            s = _dot_nt(qr.astype(BF16), kr.astype(BF16)) * dmat_ref[hd]
            o = _dot(s.astype(BF16), vb)
            state = s_scr[si, hd]
            o = o + _dot((qr * qdec_ref[hd]).astype(BF16), state.astype(BF16))
            kd_t = jnp.transpose(kr * kdec_ref[hd]).astype(BF16)
            s_scr[si, hd] = cdec[hd] * state + _dot(kd_t, vb)
            mu = jnp.mean(o, axis=-1, keepdims=True)
            oc = o - mu
            var = jnp.mean(oc * oc, axis=-1, keepdims=True)
            on = oc * lax.rsqrt(var + EPS)
            outs.append(on * gnw_ref[:, lo:lo + RET_DV] * (g * jax.nn.sigmoid(g)))

        pin = pj[:, 2 * RET_QK + 2 * RET_VW:]
        ext_scr[si, HIST_ROWS:HIST_ROWS + tl, :] = pin
        pstate_ref[si] = pin[tl - POOL_HIST:, :]
        for gi, w in enumerate(POOL_WINDOWS):
            lo = gi * POOL_C
            wsum = ext_scr[si, :, lo:lo + POOL_C]
            shift = 1
            while shift < w:
                wsum = wsum + pltpu.roll(wsum, shift, 0)
                shift *= 2
            cnt = jnp.minimum(float(w), pos + 1.0)
            d = wsum[HIST_ROWS:, :] / cnt - pin[:, lo:lo + POOL_C]
            y = _dot(d.astype(BF16), wpool_ref[gi])
            outs.append(y * pscale_ref[:, lo:lo + POOL_C])
        ext_scr[si, 0:HIST_ROWS, :] = ext_scr[si, tl:tl + HIST_ROWS, :]
        mixes.append(jnp.concatenate(outs, axis=-1).astype(BF16))
        yield

    x1 = x + _dot(jnp.concatenate(mixes, axis=0), wout_ref[...])

    qm = _rms_dot(x1, nmem_ref[...], wq_ref[...])
    yield
    atts = []
    for sj in range(chain):
        si = c0 + sj
        aouts = []
        for hd in range(MEM_HEADS):
            lo = hd * MEM_HD
            qh = qm[sj * tl:(sj + 1) * tl, lo:lo + MEM_HD].astype(BF16)
            s = _dot_nt(qh, mk_ref[si, :, lo:lo + MEM_HD]) * (MEM_HD ** -0.5)
            e = jnp.exp(s - jnp.max(s, axis=-1, keepdims=True))
            p = e / jnp.sum(e, axis=-1, keepdims=True)
            aouts.append(_dot(p.astype(BF16), mv_ref[si, :, lo:lo + MEM_HD]))
        atts.append(jnp.concatenate(aouts, axis=-1).astype(BF16))
        yield
    x2 = x1 + _dot(jnp.concatenate(atts, axis=0), wo_ref[...])
    x2_ref[c0:c0 + chain] = x2.reshape(chain, tl, D_MODEL)

    hn = _rms(x2, nffn_ref[...])
    h_ref[c0:c0 + chain] = _pack_bf16(hn).reshape(chain, tl, D_MODEL // 2)
    lane_tile = lambda a: a[:, :rows] if rows <= LANES else jnp.concatenate([a] * (rows // LANES), axis=1)
    logits = _dot_nt(rw_ref[...], hn.astype(BF16)) + lane_tile(rb_ref[...])
    eiota = lax.broadcasted_iota(jnp.int32, (N_EXPERTS, rows), 0)
    neg = jnp.finfo(F32).min
    vals, idxs = [], []
    for _k in range(TOP_K):
        m = jnp.max(logits, axis=0, keepdims=True)
        idx = jnp.min(jnp.where(logits == m, eiota, N_EXPERTS), axis=0, keepdims=True)
        vals.append(m)
        idxs.append(idx)
        logits = jnp.where(eiota == idx, neg, logits)
    exps = [jnp.exp(vk - vals[0]) for vk in vals]
    den = exps[0] + exps[1] + exps[2] + exps[3]

    onehots = [(eiota == idx).astype(F32) for idx in idxs]
    picked = onehots[0] + onehots[1] + onehots[2] + onehots[3]
    before = lane_tile(cnt_scr[...]) + _dot(picked.astype(BF16), earlier_ref[...])
    ranks = [jnp.sum(oh * before, axis=0, keepdims=True) for oh in onehots]
    cnt_scr[...] = cnt_scr[...] + jnp.sum(picked, axis=1, keepdims=True)

    rowi = lax.broadcasted_iota(jnp.int32, (ROUTE_ROWS, rows), 0)
    route = jnp.zeros((ROUTE_ROWS, rows), F32)
    for kk in range(TOP_K):
        route = jnp.where(rowi == kk, exps[kk] / den, route)
        route = jnp.where(rowi == TOP_K + kk, idxs[kk].astype(F32), route)
        route = jnp.where(rowi == 2 * TOP_K + kk, ranks[kk], route)
    route_ref[0, :, c0 * tl:c0 * tl + rows] = route


def _decay_tables(tl):
    hh = np.arange(RET_HEADS, dtype=np.float64)
    log_g = np.log1p(-np.exp2(-5.0 - hh))
    idx = np.arange(tl, dtype=np.float64)
    dist = np.abs(idx[:, None] - idx[None, :])
    visible = (idx[None, :] // CHUNK) <= (idx[:, None] // CHUNK)
    dmat = np.where(visible[None], np.exp(log_g[:, None, None] * dist[None]), 0.0)
    qdec = np.exp(log_g[:, None] * (idx[None, :] + 1.0))
    kdec = np.exp(log_g[:, None] * (tl - 1.0 - idx[None, :]))
    cdec = tuple(float(c) for c in np.exp(log_g * tl).astype(np.float32))
    bcast = lambda a: np.ascontiguousarray(np.broadcast_to(a[:, :, None], (RET_HEADS, tl, RET_DK)))
    return (jnp.asarray(dmat, F32), jnp.asarray(bcast(qdec), F32), jnp.asarray(bcast(kdec), F32), cdec)


def _rotary_tables(pos0, length):
    half = RET_DK // 2
    inv_freq = jnp.power(ROPE_BASE, -jnp.arange(half, dtype=F32) / half)
    ang = (pos0 + jnp.arange(length, dtype=jnp.int32)).astype(F32)[:, None] * inv_freq[None, :]
    cos, sin = jnp.cos(ang), jnp.sin(ang)
    return jnp.concatenate([cos, cos], axis=-1), jnp.concatenate([-sin, sin], axis=-1)


def _layer(x, stream0, b, pos0, tl, ns, chain, state0, hist0, mk, mv, wts, cnt0):
    length = x.shape[1]
    nt = length // tl
    g0 = stream0 // ns
    rows = ns * tl
    cos, sin = _rotary_tables(pos0, length)
    dmat, qdec, kdec, cdec = _decay_tables(tl)
    crows = chain * tl
    earlier = jnp.asarray(np.triu(np.ones((crows, crows), np.float32), 1), BF16)
    kern = functools.partial(_layer_kernel, tl=tl, chain=chain, pos0=pos0, cdec=cdec)
    tok = lambda width: pl.BlockSpec((ns, tl, width), lambda i, j: (i, j, 0))
    per_stream = lambda *shape: pl.BlockSpec((ns,) + shape, lambda i, j: (i,) + (0,) * len(shape))
    per_stream_in = lambda off, *shape: pl.BlockSpec(
        (ns,) + shape, lambda i, j: (i + off,) + (0,) * len(shape), pipeline_mode=pl.Buffered(1))
    in_specs = [
        pl.BlockSpec((ns, tl, D_MODEL), lambda i, j: (i + g0, j, 0)),
        pl.BlockSpec((tl, RET_DK), lambda i, j: (j, 0)),
        pl.BlockSpec((tl, RET_DK), lambda i, j: (j, 0)),
        _const_spec((RET_HEADS, tl, tl)),
        _const_spec((RET_HEADS, tl, RET_DK)),
        _const_spec((RET_HEADS, tl, RET_DK)),
        per_stream_in(0, RET_HEADS, RET_DK, RET_DV),
        per_stream_in(0, HIST_ROWS, POOL_WIDTH),
        per_stream_in(g0, N_MEM, D_MODEL),
        per_stream_in(g0, N_MEM, D_MODEL),
    ] + [_const_spec(w.shape) for w in wts] + [_const_spec((crows, crows)), _const_spec((N_EXPERTS, LANES))]
    out_specs = [
        tok(D_MODEL), tok(D_MODEL // 2),
        pl.BlockSpec((1, ROUTE_ROWS, rows), lambda i, j: (i * nt + j, 0, 0)),
        per_stream(RET_HEADS, RET_DK, RET_DV),
        per_stream(POOL_HIST, POOL_WIDTH),
        pl.BlockSpec((N_EXPERTS, LANES), lambda i, j: (0, 0)),
    ]
    out_shape = [
        jax.ShapeDtypeStruct((b, length, D_MODEL), F32),
        jax.ShapeDtypeStruct((b, length, D_MODEL // 2), jnp.int32),
        jax.ShapeDtypeStruct((b // ns * nt, ROUTE_ROWS, rows), F32),
        jax.ShapeDtypeStruct((b, RET_HEADS, RET_DK, RET_DV), F32),
        jax.ShapeDtypeStruct((b, POOL_HIST, POOL_WIDTH), F32),
        jax.ShapeDtypeStruct((N_EXPERTS, LANES), F32),
    ]
    scratch = [
        pltpu.VMEM((ns, RET_HEADS, RET_DK, RET_DV), F32),
        pltpu.VMEM((ns, HIST_ROWS + tl, POOL_WIDTH), F32),
        pltpu.VMEM((N_EXPERTS, LANES), F32),
    ]
    return pl.pallas_call(
        kern,
        grid=(b // ns, nt),
        in_specs=in_specs,
        out_specs=out_specs,
        out_shape=out_shape,
        scratch_shapes=scratch,
        compiler_params=pltpu.CompilerParams(
            dimension_semantics=("arbitrary", "arbitrary"), vmem_limit_bytes=VMEM_LIMIT),
        name="layer_tl%d" % tl,
    )(x, cos, sin, dmat, qdec, kdec, state0, hist0, mk, mv, *wts, earlier, cnt0)


SPLIT_COLS = 2 * LANES


def _expert_block(x_ref, wgu, bgu_ref, wd, bd_ref, y_ref):
    x_lo, x_hi = _unpack_bf16(x_ref[...])
    xb = jnp.concatenate([x_lo.astype(BF16), x_hi.astype(BF16)], axis=1)
    gu = _dot(xb, wgu[...]) + bgu_ref[0]
    gate = jnp.minimum(gu[:, :D_FF], SWIGLU_LIMIT)
    up = jnp.clip(gu[:, D_FF:], -SWIGLU_LIMIT, SWIGLU_LIMIT)
    act = (up + 1.0) * gate * jax.nn.sigmoid(SWIGLU_ALPHA * gate)
    y_ref[...] = _pack_bf16(_dot(act.astype(BF16), wd[...]) + bd_ref[0])


def _expert_convert_kernel(be_ref, used_ref, x_ref, wgu_ref, bgu_ref, wd_ref, bd_ref, perm_ref,
                           y_ref, wgu_out_ref, wd_out_ref):
    i = pl.program_id(0)
    in_use = i < used_ref[0]

    @pl.when(in_use & ((i == 0) | (be_ref[i] != be_ref[jnp.maximum(i - 1, 0)])))
    def _():
        perm = perm_ref[...]
        for c in range(2 * D_FF // SPLIT_COLS):
            wc = wgu_ref[0, :, c * SPLIT_COLS:(c + 1) * SPLIT_COLS].astype(BF16)
            pc = _dot(wc, perm).astype(BF16)
            wgu_out_ref[0, :, c * LANES:(c + 1) * LANES] = pc[:, :LANES]
            wgu_out_ref[0, :, D_FF + c * LANES:D_FF + (c + 1) * LANES] = pc[:, LANES:]
        wd_out_ref[0] = wd_ref[0].astype(BF16)

    @pl.when(in_use)
    def _():
        _expert_block(x_ref, wgu_out_ref.at[0], bgu_ref, wd_out_ref.at[0], bd_ref, y_ref)


def _expert_ready_kernel(be_ref, used_ref, x_ref, wgu_ref, bgu_ref, wd_ref, bd_ref, y_ref):
    del be_ref

    @pl.when(pl.program_id(0) < used_ref[0])
    def _():
        _expert_block(x_ref, wgu_ref.at[0], bgu_ref, wd_ref.at[0], bd_ref, y_ref)


def _expert_ffn(block_e, n_used, xs, wgu, bgu, wd, bd, n_blocks):
    convert = wgu.dtype != BF16
    blk = lambda i, be, used: (jnp.minimum(i, used[0] - 1), 0)
    per_expert = lambda i, be, used: (be[i], 0, 0)
    in_specs = [
        pl.BlockSpec((EXPERT_ROWS, D_MODEL // 2), blk),
        pl.BlockSpec((1, D_MODEL, 2 * D_FF), per_expert),
        pl.BlockSpec((1, 1, 2 * D_FF), per_expert),
        pl.BlockSpec((1, D_FF, D_MODEL), per_expert),
        pl.BlockSpec((1, 1, D_MODEL), per_expert),
    ]
    out_specs = [pl.BlockSpec((EXPERT_ROWS, D_MODEL // 2), blk)]
    out_shape = [jax.ShapeDtypeStruct(xs.shape, jnp.int32)]
    args = [block_e, n_used, xs, wgu, bgu, wd, bd]
    if convert:
        perm = np.zeros((SPLIT_COLS, SPLIT_COLS), np.float32)
        j = np.arange(LANES)
        perm[2 * j, j] = 1.0
        perm[2 * j + 1, LANES + j] = 1.0
        in_specs.append(_const_spec((SPLIT_COLS, SPLIT_COLS)))
        args.append(jnp.asarray(perm, BF16))
        out_specs += [pl.BlockSpec((1, D_MODEL, 2 * D_FF), per_expert), pl.BlockSpec((1, D_FF, D_MODEL), per_expert)]
        out_shape += [jax.ShapeDtypeStruct(wgu.shape, BF16), jax.ShapeDtypeStruct(wd.shape, BF16)]
    grid_spec = pltpu.PrefetchScalarGridSpec(
        num_scalar_prefetch=2, grid=(n_blocks,), in_specs=in_specs, out_specs=out_specs)
    return pl.pallas_call(
        _expert_convert_kernel if convert else _expert_ready_kernel,
        grid_spec=grid_spec,
        out_shape=out_shape,
        compiler_params=pltpu.CompilerParams(
            dimension_semantics=("arbitrary",), vmem_limit_bytes=VMEM_LIMIT),
        name="expert_ffn_convert" if convert else "expert_ffn",
    )(*args)


def _expert_layout(counts, n_blocks, every_expert):
    pcounts = (counts + EXPERT_ROWS - 1) // EXPERT_ROWS * EXPERT_ROWS
    if every_expert:
        pcounts = jnp.maximum(pcounts, EXPERT_ROWS)
    pend = jnp.cumsum(pcounts)
    n_used = pend[-1:] // EXPERT_ROWS
    block_start = jnp.minimum(jnp.arange(n_blocks, dtype=jnp.int32), n_used[0] - 1) * EXPERT_ROWS
    block_e = jnp.sum((pend[None, :] <= block_start[:, None]).astype(jnp.int32), axis=1)
    return pend - pcounts, jnp.minimum(block_e, N_EXPERTS - 1), n_used


def _slot_rows(route, pstart, ns, nt, tl):
    n = route.shape[0] * ns * tl
    fields = lambda lo: route[:, lo:lo + TOP_K, :].reshape(-1, nt, TOP_K, ns, tl)
    per_slot = lambda lo: jnp.transpose(fields(lo), (2, 0, 3, 1, 4)).reshape(TOP_K, n)
    e = per_slot(TOP_K).astype(jnp.int32)
    rank = per_slot(2 * TOP_K).astype(jnp.int32)
    hit = e[:, :, None] == jnp.arange(N_EXPERTS, dtype=jnp.int32)[None, None, :]
    rows = rank + jnp.sum(jnp.where(hit, pstart[None, None, :], 0), axis=-1)
    gates = jnp.transpose(fields(0), (0, 3, 1, 4, 2)).reshape(n, TOP_K)
    return rows, gates


SC_WINDOW = 128
SC_COLS = 256


def _sc_mesh():
    return plsc.VectorSubcoreMesh(core_axis_name="c", subcore_axis_name="s")


def _dispatch(sources, m_pad):
    width = sources[0][0].shape[1]
    n_src = len(sources)

    @functools.partial(pl.kernel, mesh=_sc_mesh(),
                       out_type=jax.ShapeDtypeStruct((m_pad, width), sources[0][0].dtype), scratch_types=[])
    def k(*refs):
        xs_hbm = refs[2 * n_src]

        def body(x_vmem, i_vmem):
            j = pl.program_id(1)
            for kk in range(TOP_K):
                pltpu.sync_copy(x_vmem, xs_hbm.at[i_vmem.at[kk], pl.ds(j * SC_COLS, SC_COLS)])

        for si in range(n_src):
            src, rows = refs[2 * si], refs[2 * si + 1]
            pltpu.emit_pipeline(
                body,
                grid=(src.shape[0] // SC_WINDOW, width // SC_COLS),
                in_specs=[pl.BlockSpec((SC_WINDOW, SC_COLS), lambda i, j: (i, j)),
                          pl.BlockSpec((TOP_K, SC_WINDOW), lambda i, j: (0, i))],
                out_specs=[],
                core_axis_name=("c", "s"),
                dimension_semantics=(pltpu.PARALLEL, pltpu.ARBITRARY),
            )(src, rows)

    return k(*[a for pair in sources for a in pair])


def _collect(yb, rows):
    width = yb.shape[1]

    @functools.partial(pl.kernel, mesh=_sc_mesh(),
                       out_type=jax.ShapeDtypeStruct((rows.shape[1], width), yb.dtype), scratch_types=[])
    def k(yb_hbm, r_hbm, o_hbm):
        def body(i_vmem, o_vmem):
            j = pl.program_id(1)
            pltpu.sync_copy(yb_hbm.at[i_vmem.at[0], pl.ds(j * SC_COLS, SC_COLS)], o_vmem)

        pltpu.emit_pipeline(
            body,
            grid=(rows.shape[1] // SC_WINDOW, width // SC_COLS),
            in_specs=[pl.BlockSpec((1, SC_WINDOW), lambda i, j: (0, i))],
            out_specs=[pl.BlockSpec((SC_WINDOW, SC_COLS), lambda i, j: (i, j))],
            core_axis_name=("c", "s"),
            dimension_semantics=(pltpu.PARALLEL, pltpu.ARBITRARY),
        )(r_hbm, o_hbm)

    return k(yb, rows)


def _combine_kernel(x_ref, yg_ref, route_ref, fw_ref, y_ref):
    half = D_MODEL // 2
    acc_lo = x_ref[:, :half]
    acc_hi = x_ref[:, half:]
    route = route_ref[...]
    for kk in range(TOP_K):
        y_lo, y_hi = _unpack_bf16(yg_ref[kk])
        gate = route[:, kk:kk + 1]
        acc_lo = acc_lo + y_lo * gate
        acc_hi = acc_hi + y_hi * gate
    ms = (jnp.sum(acc_lo * acc_lo, axis=-1, keepdims=True)
          + jnp.sum(acc_hi * acc_hi, axis=-1, keepdims=True)) * (1.0 / D_MODEL)
    scale = lax.rsqrt(ms + EPS)
    y_ref[:, :half] = acc_lo * scale * fw_ref[:, :half]
    y_ref[:, half:] = acc_hi * scale * fw_ref[:, half:]


def _combine_next_kernel(x_ref, yg_ref, route_ref, fw_ref, prev_ref, y_ref):
    del prev_ref
    _combine_kernel(x_ref, yg_ref, route_ref, fw_ref, y_ref)


def _combine(x2, yg, gates, final_w, y_prev, first_row, out_rows, out_first_row):
    blk_in = first_row // COMBINE_ROWS
    blk_out = out_first_row // COMBINE_ROWS
    in_specs = [
        pl.BlockSpec((COMBINE_ROWS, D_MODEL), lambda i: (blk_in + i, 0)),
        pl.BlockSpec((TOP_K, COMBINE_ROWS, D_MODEL // 2), lambda i: (0, i, 0)),
        pl.BlockSpec((COMBINE_ROWS, TOP_K), lambda i: (blk_in + i, 0)),
        _const_spec((1, D_MODEL)),
    ]
    args = [x2, yg, gates, final_w]
    kern, aliases = _combine_kernel, {}
    if y_prev is not None:
        in_specs.append(pl.BlockSpec(memory_space=pl.ANY))
        args.append(y_prev)
        kern, aliases = _combine_next_kernel, {len(args) - 1: 0}
    return pl.pallas_call(
        kern,
        grid=(yg.shape[1] // COMBINE_ROWS,),
        in_specs=in_specs,
        out_specs=pl.BlockSpec((COMBINE_ROWS, D_MODEL), lambda i: (blk_out + i, 0)),
        out_shape=jax.ShapeDtypeStruct((out_rows, D_MODEL), F32),
        input_output_aliases=aliases,
        compiler_params=pltpu.CompilerParams(
            dimension_semantics=("arbitrary",), vmem_limit_bytes=VMEM_LIMIT),
        name="combine_%d_%d" % (out_rows, out_first_row),
    )(*args)


def _collect_combine(yb, x2, rows, gates, final_w, chunks, y, out_rows, out_first_row):
    nc = x2.shape[0] // chunks
    for c in range(chunks):
        yg = _collect(yb, rows[:, c * nc:(c + 1) * nc].reshape(1, TOP_K * nc))
        y = _combine(x2, yg.reshape(TOP_K, nc, D_MODEL // 2), gates, final_w, y, c * nc,
                     out_rows, out_first_row + c * nc)
    return y


def kernel(x_prompt, x_sample, cache_mem_k, cache_mem_v, state_ret, state_pool, mem_prompt,
           norm_mix_w, w_in, ret_gn_w, w_pool, pool_scale, w_out, norm_mem_w, mem_norm_w,
           w_q_mem, w_kv_mem, w_o_mem, norm_ffn_w, router_w, router_b, w_gate_up, b_gate_up,
           w_down, b_down, final_norm_w):
    assert norm_mix_w.shape[0] == 1, "one layer"
    b, seq, _ = x_prompt.shape
    db, dseq, _ = x_sample.shape
    row = lambda a: a.reshape(1, -1)

    mk_p, mv_p, mkb_p, mvb_p = _mem_kv(mem_prompt, row(mem_norm_w[0]), w_kv_mem[0].astype(BF16))
    mkb_s, mvb_s = _kv_flat(cache_mem_k[0], cache_mem_v[0])

    wts = (row(norm_mix_w[0]), w_in[0].astype(BF16), row(ret_gn_w[0]), w_pool[0].astype(BF16),
           row(pool_scale[0]), w_out[0].astype(BF16), row(norm_mem_w[0]), w_q_mem[0].astype(BF16),
           w_o_mem[0].astype(BF16), row(norm_ffn_w[0]), router_w[0].T.astype(BF16),
           jnp.broadcast_to(router_b[0][:, None], (N_EXPERTS, LANES)))

    per = b // MOE_GROUPS
    n_g, n_s = per * seq, db * dseq
    half = D_MODEL // 2
    final_w = row(final_norm_w)
    bgu = jnp.concatenate([b_gate_up[0][:, 0::2], b_gate_up[0][:, 1::2]], axis=-1).reshape(N_EXPERTS, 1, 2 * D_FF)
    bd = b_down[0].reshape(N_EXPERTS, 1, D_MODEL)
    zero_state = jnp.zeros((per, RET_HEADS, RET_DK, RET_DV), F32)
    zero_hist = jnp.zeros((per, HIST_ROWS, POOL_WIDTH), F32)
    no_counts = jnp.zeros((N_EXPERTS, LANES), F32)
    hist_s = jnp.concatenate([jnp.zeros((db, 1, POOL_WIDTH), F32), state_pool[0]], axis=1)

    y_p, rets, pools = None, [], []
    for g in range(MOE_GROUPS):
        x2_g, h_g, route_g, ret_g, pool_g, cnt = _layer(
            x_prompt, g * per, per, 0, PROMPT_TILE, PROMPT_STREAMS, PROMPT_CHAIN, zero_state, zero_hist,
            mkb_p, mvb_p, wts, no_counts)
        rets.append(ret_g)
        pools.append(pool_g)
        with_sample = g == MOE_GROUPS - 1
        n_slots = n_g * TOP_K
        if with_sample:
            x2_s, h_s, route_s, ret_s, pool_s, cnt = _layer(
                x_sample, 0, db, PAST_LEN, dseq, SAMPLE_STREAMS, SAMPLE_STREAMS, state_ret[0], hist_s,
                mkb_s, mvb_s, wts, cnt)
            n_slots += n_s * TOP_K
        first = g == 0
        n_blocks = -(-(n_slots + N_EXPERTS * (EXPERT_ROWS - 1)) // EXPERT_ROWS) + (N_EXPERTS if first else 0)
        pstart, block_e, n_used = _expert_layout(cnt[:, 0].astype(jnp.int32), n_blocks, first)
        rows_g, gates_g = _slot_rows(route_g, pstart, PROMPT_STREAMS, seq // PROMPT_TILE, PROMPT_TILE)
        sources = [(h_g.reshape(n_g, half), rows_g)]
        if with_sample:
            rows_s, gates_s = _slot_rows(route_s, pstart, SAMPLE_STREAMS, 1, dseq)
            sources.append((h_s.reshape(n_s, half), rows_s))
        xs = _dispatch(sources, n_blocks * EXPERT_ROWS)
        if first:
            yb, wgu_bf, wd_bf = _expert_ffn(block_e, n_used, xs, w_gate_up[0], bgu, w_down[0], bd, n_blocks)
        else:
            (yb,) = _expert_ffn(block_e, n_used, xs, wgu_bf, bgu, wd_bf, bd, n_blocks)
        if with_sample:
            y_s = _collect_combine(yb, x2_s.reshape(n_s, D_MODEL), rows_s, gates_s, final_w, 1, None, n_s, 0)
        y_p = _collect_combine(yb, x2_g.reshape(n_g, D_MODEL), rows_g, gates_g, final_w, COMBINE_CHUNKS,
                               y_p, b * seq, g * n_g)
    y_p = y_p.reshape(b, seq, D_MODEL)
    y_s = y_s.reshape(db, dseq, D_MODEL)
    ret_p = jnp.concatenate(rets, axis=0)
    pool_p = jnp.concatenate(pools, axis=0)
    return (y_p, y_s, mk_p[None], mv_p[None], ret_p[None], pool_p[None], ret_s[None], pool_s[None])
```

```python
import functools

import numpy as np
import jax
import jax.numpy as jnp
from jax import lax
from jax.experimental import pallas as pl
from jax.experimental.pallas import tpu as pltpu
from jax.experimental.pallas import tpu_sc as plsc

D_MODEL = 1024
CHUNK = 64
PAST_LEN = 4096
RET_HEADS = 4
RET_DK = 128
RET_DV = 128
RET_QK = RET_HEADS * RET_DK
RET_VW = RET_HEADS * RET_DV
ROPE_BASE = 10000.0
POOL_WINDOWS = (2, 4, 8, 16)
POOL_GROUPS = 4
POOL_WIDTH = D_MODEL // 2
POOL_C = POOL_WIDTH // POOL_GROUPS
POOL_HIST = max(POOL_WINDOWS) - 1
HIST_ROWS = POOL_HIST + 1
IN_WIDTH = 2 * RET_QK + 2 * RET_VW + POOL_WIDTH
N_MEM = 256
MEM_HEADS = 4
MEM_HD = D_MODEL // MEM_HEADS
N_EXPERTS = 32
TOP_K = 4
D_FF = D_MODEL
SWIGLU_LIMIT = 7.0
SWIGLU_ALPHA = 1.702
EPS = 1e-5

LANES = 128
ROUTE_ROWS = 16
PROMPT_TILE = 256
PROMPT_STREAMS = 4
PROMPT_CHAIN = 2
SAMPLE_STREAMS = 8
CONVERT_ROWS = 512
EXPERT_ROWS = 1024
GROUP_STREAMS = (8, 24)
KV_STREAMS = 2
COMBINE_ROWS = 1024
COMBINE_TOKENS = 16384
VMEM_LIMIT = 56 * 1024 * 1024

BF16 = jnp.bfloat16
F32 = jnp.float32


def _rms(x, w):
    return x * lax.rsqrt(jnp.mean(x * x, axis=-1, keepdims=True) + EPS) * w


def _dot(a, b):
    return jnp.dot(a, b, preferred_element_type=F32)


def _rms_dot(x, w_norm, w_mat):
    inv = lax.rsqrt(jnp.mean(x * x, axis=-1, keepdims=True) + EPS)
    return _dot((x * w_norm).astype(BF16), w_mat) * inv


def _dot_nt(a, b):
    return lax.dot_general(a, b, (((1,), (1,)), ((), ())), preferred_element_type=F32)


BF16_BITS = 16
HIGH_HALF = -(1 << BF16_BITS)


def _pack_bf16(x):
    bits = lax.bitcast_convert_type(x.astype(BF16).astype(F32), jnp.int32)
    w = x.shape[1] // 2
    return lax.shift_right_logical(bits[:, :w], BF16_BITS) | (bits[:, w:] & HIGH_HALF)


def _unpack_bf16(p):
    lo = lax.bitcast_convert_type(lax.shift_left(p, BF16_BITS), F32)
    hi = lax.bitcast_convert_type(p & HIGH_HALF, F32)
    return lo, hi


def _const_spec(shape):
    nd = len(shape)
    return pl.BlockSpec(shape, lambda *_: (0,) * nd, pipeline_mode=pl.Buffered(1))


def _mem_kv_kernel(mem_ref, nw_ref, w_ref, k_ref, v_ref, kb_ref, vb_ref):
    ns = mem_ref.shape[0]
    xn = _rms(mem_ref[...].reshape(ns * N_MEM, D_MODEL), nw_ref[...]).astype(BF16)
    kv = _dot(xn, w_ref[...])
    for hd in range(MEM_HEADS):
        lo = hd * MEM_HD
        k_ref[:, :, hd, :] = kv[:, lo:lo + MEM_HD].reshape(ns, N_MEM, MEM_HD)
        v_ref[:, :, hd, :] = kv[:, D_MODEL + lo:D_MODEL + lo + MEM_HD].reshape(ns, N_MEM, MEM_HD)
    kb_ref[...] = kv[:, :D_MODEL].astype(BF16).reshape(ns, N_MEM, D_MODEL)
    vb_ref[...] = kv[:, D_MODEL:].astype(BF16).reshape(ns, N_MEM, D_MODEL)


def _mem_kv(mem, mem_norm_w, w_kv_bf):
    b = mem.shape[0]
    out_spec = pl.BlockSpec((KV_STREAMS, N_MEM, MEM_HEADS, MEM_HD), lambda i: (i, 0, 0, 0))
    flat_spec = pl.BlockSpec((KV_STREAMS, N_MEM, D_MODEL), lambda i: (i, 0, 0))
    return pl.pallas_call(
        _mem_kv_kernel,
        grid=(b // KV_STREAMS,),
        in_specs=[
            pl.BlockSpec((KV_STREAMS, N_MEM, D_MODEL), lambda i: (i, 0, 0)),
            _const_spec((1, D_MODEL)),
            _const_spec((D_MODEL, 2 * D_MODEL)),
        ],
        out_specs=[out_spec, out_spec, flat_spec, flat_spec],
        out_shape=[jax.ShapeDtypeStruct((b, N_MEM, MEM_HEADS, MEM_HD), F32)] * 2
        + [jax.ShapeDtypeStruct((b, N_MEM, D_MODEL), BF16)] * 2,
        compiler_params=pltpu.CompilerParams(
            dimension_semantics=("arbitrary",), vmem_limit_bytes=VMEM_LIMIT),
        name="mem_kv",
    )(mem, mem_norm_w, w_kv_bf)


def _kv_flat_kernel(k_ref, v_ref, kb_ref, vb_ref):
    for hd in range(MEM_HEADS):
        kb_ref[:, :, hd * MEM_HD:(hd + 1) * MEM_HD] = k_ref[:, :, hd, :].astype(BF16)
        vb_ref[:, :, hd * MEM_HD:(hd + 1) * MEM_HD] = v_ref[:, :, hd, :].astype(BF16)


def _kv_flat(mem_k, mem_v):
    b = mem_k.shape[0]
    in_spec = pl.BlockSpec((KV_STREAMS, N_MEM, MEM_HEADS, MEM_HD), lambda i: (i, 0, 0, 0))
    out_spec = pl.BlockSpec((KV_STREAMS, N_MEM, D_MODEL), lambda i: (i, 0, 0))
    return pl.pallas_call(
        _kv_flat_kernel,
        grid=(b // KV_STREAMS,),
        in_specs=[in_spec, in_spec],
        out_specs=[out_spec, out_spec],
        out_shape=[jax.ShapeDtypeStruct((b, N_MEM, D_MODEL), BF16)] * 2,
        compiler_params=pltpu.CompilerParams(
            dimension_semantics=("arbitrary",), vmem_limit_bytes=VMEM_LIMIT),
        name="kv_flat",
    )(mem_k, mem_v)


def _layer_kernel(x_ref, cos_ref, sin_ref, dmat_ref, qdec_ref, kdec_ref, state0_ref, hist0_ref,
                  mk_ref, mv_ref, nmix_ref, win_ref, gnw_ref, wpool_ref, pscale_ref, wout_ref,
                  nmem_ref, wq_ref, wo_ref, nffn_ref, rw_ref, rb_ref, earlier_ref, cnt0_ref,
                  x2_ref, h_ref, route_ref, rstate_ref, pstate_ref, cnt_ref,
                  s_scr, ext_scr, cnt_scr, *, tl, chain, pos0, cdec):
    t = pl.program_id(1)

    @pl.when((pl.program_id(0) == 0) & (t == 0))
    def _():
        cnt_scr[...] = cnt0_ref[...]

    @pl.when(t == 0)
    def _():
        s_scr[...] = state0_ref[...]
        ext_scr[:, 0:HIST_ROWS, :] = hist0_ref[...]

    for c0 in range(0, x_ref.shape[0], chain):
        _layer_chain(c0, t, x_ref, cos_ref, sin_ref, dmat_ref, qdec_ref, kdec_ref, nmix_ref, win_ref, gnw_ref,
                     wpool_ref, pscale_ref, wout_ref, nmem_ref, wq_ref, wo_ref, nffn_ref, rw_ref, rb_ref,
                     earlier_ref, x2_ref, h_ref, route_ref, pstate_ref, s_scr, ext_scr, mk_ref, mv_ref, cnt_scr,
                     tl=tl, chain=chain, pos0=pos0, cdec=cdec)
    rstate_ref[...] = s_scr[...]
    cnt_ref[...] = cnt_scr[...]


def _layer_chain(c0, t, x_ref, cos_ref, sin_ref, dmat_ref, qdec_ref, kdec_ref, nmix_ref, win_ref, gnw_ref,
                 wpool_ref, pscale_ref, wout_ref, nmem_ref, wq_ref, wo_ref, nffn_ref, rw_ref, rb_ref,
                 earlier_ref, x2_ref, h_ref, route_ref, pstate_ref, s_scr, ext_scr, mk_ref, mv_ref, cnt_scr,
                 *, tl, chain, pos0, cdec):
    rows = chain * tl
    x = x_ref[c0:c0 + chain].reshape(rows, D_MODEL)
    proj = _rms_dot(x, nmix_ref[...], win_ref[...])
    cos = cos_ref[...]
    sin = sin_ref[...]
    pos = (pos0 + t * tl + lax.broadcasted_iota(jnp.int32, (tl, POOL_C), 0)).astype(F32)

    mixes = []
    for sj in range(chain):
        si = c0 + sj
        pj = proj[sj * tl:(sj + 1) * tl]
        outs = []
        for hd in range(RET_HEADS):
            lo = hd * RET_DK
            q = pj[:, lo:lo + RET_DK]
            k = pj[:, RET_QK + lo:RET_QK + lo + RET_DK]
            v = pj[:, 2 * RET_QK + lo:2 * RET_QK + lo + RET_DV]
            g = pj[:, 2 * RET_QK + RET_VW + lo:2 * RET_QK + RET_VW + lo + RET_DV]
            qr = (q * cos + pltpu.roll(q, RET_DK // 2, 1) * sin) * (RET_DK ** -0.5)
            kr = k * cos + pltpu.roll(k, RET_DK // 2, 1) * sin
            vb = v.astype(BF16)
            s = _dot_nt(qr.astype(BF16), kr.astype(BF16)) * dmat_ref[hd]
            o = _dot(s.astype(BF16), vb)
            state = s_scr[si, hd]
            o = o + _dot((qr * qdec_ref[hd]).astype(BF16), state.astype(BF16))
            kd_t = jnp.transpose(kr * kdec_ref[hd]).astype(BF16)
            s_scr[si, hd] = cdec[hd] * state + _dot(kd_t, vb)
            mu = jnp.mean(o, axis=-1, keepdims=True)
            oc = o - mu
            var = jnp.mean(oc * oc, axis=-1, keepdims=True)
            on = oc * lax.rsqrt(var + EPS)
            outs.append(on * gnw_ref[:, lo:lo + RET_DV] * (g * jax.nn.sigmoid(g)))

        pin = pj[:, 2 * RET_QK + 2 * RET_VW:]
        ext_scr[si, HIST_ROWS:HIST_ROWS + tl, :] = pin
        pstate_ref[si] = pin[tl - POOL_HIST:, :]
        for gi, w in enumerate(POOL_WINDOWS):
            lo = gi * POOL_C
            wsum = ext_scr[si, :, lo:lo + POOL_C]
            shift = 1
            while shift < w:
                wsum = wsum + pltpu.roll(wsum, shift, 0)
                shift *= 2
            cnt = jnp.minimum(float(w), pos + 1.0)
            d = wsum[HIST_ROWS:, :] / cnt - pin[:, lo:lo + POOL_C]
            y = _dot(d.astype(BF16), wpool_ref[gi])
            outs.append(y * pscale_ref[:, lo:lo + POOL_C])
        ext_scr[si, 0:HIST_ROWS, :] = ext_scr[si, tl:tl + HIST_ROWS, :]
        mixes.append(jnp.concatenate(outs, axis=-1).astype(BF16))

    x1 = x + _dot(jnp.concatenate(mixes, axis=0), wout_ref[...])

    qm = _rms_dot(x1, nmem_ref[...], wq_ref[...])
    atts = []
    for sj in range(chain):
        si = c0 + sj
        aouts = []
        for hd in range(MEM_HEADS):
            lo = hd * MEM_HD
            qh = qm[sj * tl:(sj + 1) * tl, lo:lo + MEM_HD].astype(BF16)
            s = _dot_nt(qh, mk_ref[si, :, lo:lo + MEM_HD]) * (MEM_HD ** -0.5)
            e = jnp.exp(s - jnp.max(s, axis=-1, keepdims=True))
            p = e / jnp.sum(e, axis=-1, keepdims=True)
            aouts.append(_dot(p.astype(BF16), mv_ref[si, :, lo:lo + MEM_HD]))
        atts.append(jnp.concatenate(aouts, axis=-1).astype(BF16))
    x2 = x1 + _dot(jnp.concatenate(atts, axis=0), wo_ref[...])
    x2_ref[c0:c0 + chain] = x2.reshape(chain, tl, D_MODEL)

    hn = _rms(x2, nffn_ref[...])
    h_ref[c0:c0 + chain] = _pack_bf16(hn).reshape(chain, tl, D_MODEL // 2)
    lane_tile = lambda a: a[:, :rows] if rows <= LANES else jnp.concatenate([a] * (rows // LANES), axis=1)
    logits = _dot_nt(rw_ref[...], hn.astype(BF16)) + lane_tile(rb_ref[...])
    eiota = lax.broadcasted_iota(jnp.int32, (N_EXPERTS, rows), 0)
    neg = jnp.finfo(F32).min
    vals, idxs = [], []
    for _k in range(TOP_K):
        m = jnp.max(logits, axis=0, keepdims=True)
        idx = jnp.min(jnp.where(logits == m, eiota, N_EXPERTS), axis=0, keepdims=True)
        vals.append(m)
        idxs.append(idx)
        logits = jnp.where(eiota == idx, neg, logits)
    exps = [jnp.exp(vk - vals[0]) for vk in vals]
    den = exps[0] + exps[1] + exps[2] + exps[3]

    onehots = [(eiota == idx).astype(F32) for idx in idxs]
    picked = onehots[0] + onehots[1] + onehots[2] + onehots[3]
    before = lane_tile(cnt_scr[...]) + _dot(picked.astype(BF16), earlier_ref[...])
    ranks = [jnp.sum(oh * before, axis=0, keepdims=True) for oh in onehots]
    cnt_scr[...] = cnt_scr[...] + jnp.sum(picked, axis=1, keepdims=True)

    rowi = lax.broadcasted_iota(jnp.int32, (ROUTE_ROWS, rows), 0)
    route = jnp.zeros((ROUTE_ROWS, rows), F32)
    for kk in range(TOP_K):
        route = jnp.where(rowi == kk, exps[kk] / den, route)
        route = jnp.where(rowi == TOP_K + kk, idxs[kk].astype(F32), route)
        route = jnp.where(rowi == 2 * TOP_K + kk, ranks[kk], route)
    route_ref[0, :, c0 * tl:c0 * tl + rows] = route


def _decay_tables(tl):
    hh = np.arange(RET_HEADS, dtype=np.float64)
    log_g = np.log1p(-np.exp2(-5.0 - hh))
    idx = np.arange(tl, dtype=np.float64)
    dist = np.abs(idx[:, None] - idx[None, :])
    visible = (idx[None, :] // CHUNK) <= (idx[:, None] // CHUNK)
    dmat = np.where(visible[None], np.exp(log_g[:, None, None] * dist[None]), 0.0)
    qdec = np.exp(log_g[:, None] * (idx[None, :] + 1.0))
    kdec = np.exp(log_g[:, None] * (tl - 1.0 - idx[None, :]))
    cdec = tuple(float(c) for c in np.exp(log_g * tl).astype(np.float32))
    bcast = lambda a: np.ascontiguousarray(np.broadcast_to(a[:, :, None], (RET_HEADS, tl, RET_DK)))
    return (jnp.asarray(dmat, F32), jnp.asarray(bcast(qdec), F32), jnp.asarray(bcast(kdec), F32), cdec)


def _rotary_tables(pos0, length):
    half = RET_DK // 2
    inv_freq = jnp.power(ROPE_BASE, -jnp.arange(half, dtype=F32) / half)
    ang = (pos0 + jnp.arange(length, dtype=jnp.int32)).astype(F32)[:, None] * inv_freq[None, :]
    cos, sin = jnp.cos(ang), jnp.sin(ang)
    return jnp.concatenate([cos, cos], axis=-1), jnp.concatenate([-sin, sin], axis=-1)


def _layer(x, stream0, b, pos0, tl, ns, chain, state0, hist0, mk, mv, wts, cnt0):
    length = x.shape[1]
    assert b % ns == 0 and stream0 % ns == 0 and ns % chain == 0, (b, stream0, ns, chain)
    assert length % tl == 0 and tl % CHUNK == 0 and tl > POOL_HIST, (length, tl)
    nt = length // tl
    g0 = stream0 // ns
    rows = ns * tl
    cos, sin = _rotary_tables(pos0, length)
    dmat, qdec, kdec, cdec = _decay_tables(tl)
    crows = chain * tl
    earlier = jnp.asarray(np.triu(np.ones((crows, crows), np.float32), 1), BF16)
    kern = functools.partial(_layer_kernel, tl=tl, chain=chain, pos0=pos0, cdec=cdec)
    tok = lambda width: pl.BlockSpec((ns, tl, width), lambda i, j: (i, j, 0))
    per_stream = lambda *shape: pl.BlockSpec((ns,) + shape, lambda i, j: (i,) + (0,) * len(shape))
    per_stream_in = lambda off, *shape: pl.BlockSpec(
        (ns,) + shape, lambda i, j: (i + off,) + (0,) * len(shape), pipeline_mode=pl.Buffered(1))
    in_specs = [
        pl.BlockSpec((ns, tl, D_MODEL), lambda i, j: (i + g0, j, 0)),
        pl.BlockSpec((tl, RET_DK), lambda i, j: (j, 0)),
        pl.BlockSpec((tl, RET_DK), lambda i, j: (j, 0)),
        _const_spec((RET_HEADS, tl, tl)),
        _const_spec((RET_HEADS, tl, RET_DK)),
        _const_spec((RET_HEADS, tl, RET_DK)),
        per_stream_in(0, RET_HEADS, RET_DK, RET_DV),
        per_stream_in(0, HIST_ROWS, POOL_WIDTH),
        per_stream_in(g0, N_MEM, D_MODEL),
        per_stream_in(g0, N_MEM, D_MODEL),
    ] + [_const_spec(w.shape) for w in wts] + [_const_spec((crows, crows)), _const_spec((N_EXPERTS, LANES))]
    out_specs = [
        tok(D_MODEL), tok(D_MODEL // 2),
        pl.BlockSpec((1, ROUTE_ROWS, rows), lambda i, j: (i * nt + j, 0, 0)),
        per_stream(RET_HEADS, RET_DK, RET_DV),
        per_stream(POOL_HIST, POOL_WIDTH),
        pl.BlockSpec((N_EXPERTS, LANES), lambda i, j: (0, 0)),
    ]
    out_shape = [
        jax.ShapeDtypeStruct((b, length, D_MODEL), F32),
        jax.ShapeDtypeStruct((b, length, D_MODEL // 2), jnp.int32),
        jax.ShapeDtypeStruct((b // ns * nt, ROUTE_ROWS, rows), F32),
        jax.ShapeDtypeStruct((b, RET_HEADS, RET_DK, RET_DV), F32),
        jax.ShapeDtypeStruct((b, POOL_HIST, POOL_WIDTH), F32),
        jax.ShapeDtypeStruct((N_EXPERTS, LANES), F32),
    ]
    scratch = [
        pltpu.VMEM((ns, RET_HEADS, RET_DK, RET_DV), F32),
        pltpu.VMEM((ns, HIST_ROWS + tl, POOL_WIDTH), F32),
        pltpu.VMEM((N_EXPERTS, LANES), F32),
    ]
    return pl.pallas_call(
        kern,
        grid=(b // ns, nt),
        in_specs=in_specs,
        out_specs=out_specs,
        out_shape=out_shape,
        scratch_shapes=scratch,
        compiler_params=pltpu.CompilerParams(
            dimension_semantics=("arbitrary", "arbitrary"), vmem_limit_bytes=VMEM_LIMIT),
        name="layer_tl%d" % tl,
    )(x, cos, sin, dmat, qdec, kdec, state0, hist0, mk, mv, *wts, earlier, cnt0)


SPLIT_COLS = 2 * LANES


def _expert_block(x_ref, wgu, bgu_ref, wd, bd_ref, y_ref):
    x_lo, x_hi = _unpack_bf16(x_ref[...])
    xb = jnp.concatenate([x_lo.astype(BF16), x_hi.astype(BF16)], axis=1)
    gu = _dot(xb, wgu[...]) + bgu_ref[0]
    gate = jnp.minimum(gu[:, :D_FF], SWIGLU_LIMIT)
    up = jnp.clip(gu[:, D_FF:], -SWIGLU_LIMIT, SWIGLU_LIMIT)
    act = (up + 1.0) * gate * jax.nn.sigmoid(SWIGLU_ALPHA * gate)
    y_ref[...] = _pack_bf16(_dot(act.astype(BF16), wd[...]) + bd_ref[0])


def _expert_convert_kernel(be_ref, used_ref, x_ref, wgu_ref, bgu_ref, wd_ref, bd_ref, perm_ref,
                           y_ref, wgu_out_ref, wd_out_ref):
    i = pl.program_id(0)
    in_use = i < used_ref[0]

    @pl.when(in_use & ((i == 0) | (be_ref[i] != be_ref[jnp.maximum(i - 1, 0)])))
    def _():
        perm = perm_ref[...]
        for c in range(2 * D_FF // SPLIT_COLS):
            wc = wgu_ref[0, :, c * SPLIT_COLS:(c + 1) * SPLIT_COLS].astype(BF16)
            pc = _dot(wc, perm).astype(BF16)
            wgu_out_ref[0, :, c * LANES:(c + 1) * LANES] = pc[:, :LANES]
            wgu_out_ref[0, :, D_FF + c * LANES:D_FF + (c + 1) * LANES] = pc[:, LANES:]
        wd_out_ref[0] = wd_ref[0].astype(BF16)

    @pl.when(in_use)
    def _():
        _expert_block(x_ref, wgu_out_ref.at[0], bgu_ref, wd_out_ref.at[0], bd_ref, y_ref)


def _expert_ready_kernel(be_ref, used_ref, x_ref, wgu_ref, bgu_ref, wd_ref, bd_ref, y_ref):
    del be_ref

    @pl.when(pl.program_id(0) < used_ref[0])
    def _():
        _expert_block(x_ref, wgu_ref.at[0], bgu_ref, wd_ref.at[0], bd_ref, y_ref)


def _expert_ffn(block_e, n_used, xs, wgu, bgu, wd, bd, n_blocks, block_rows):
    convert = wgu.dtype != BF16
    blk = lambda i, be, used: (jnp.minimum(i, used[0] - 1), 0)
    per_expert = lambda i, be, used: (be[i], 0, 0)
    in_specs = [
        pl.BlockSpec((block_rows, D_MODEL // 2), blk),
        pl.BlockSpec((1, D_MODEL, 2 * D_FF), per_expert),
        pl.BlockSpec((1, 1, 2 * D_FF), per_expert),
        pl.BlockSpec((1, D_FF, D_MODEL), per_expert),
        pl.BlockSpec((1, 1, D_MODEL), per_expert),
    ]
    out_specs = [pl.BlockSpec((block_rows, D_MODEL // 2), blk)]
    out_shape = [jax.ShapeDtypeStruct(xs.shape, jnp.int32)]
    args = [block_e, n_used, xs, wgu, bgu, wd, bd]
    if convert:
        perm = np.zeros((SPLIT_COLS, SPLIT_COLS), np.float32)
        j = np.arange(LANES)
        perm[2 * j, j] = 1.0
        perm[2 * j + 1, LANES + j] = 1.0
        in_specs.append(_const_spec((SPLIT_COLS, SPLIT_COLS)))
        args.append(jnp.asarray(perm, BF16))
        out_specs += [pl.BlockSpec((1, D_MODEL, 2 * D_FF), per_expert), pl.BlockSpec((1, D_FF, D_MODEL), per_expert)]
        out_shape += [jax.ShapeDtypeStruct(wgu.shape, BF16), jax.ShapeDtypeStruct(wd.shape, BF16)]
    grid_spec = pltpu.PrefetchScalarGridSpec(
        num_scalar_prefetch=2, grid=(n_blocks,), in_specs=in_specs, out_specs=out_specs)
    return pl.pallas_call(
        _expert_convert_kernel if convert else _expert_ready_kernel,
        grid_spec=grid_spec,
        out_shape=out_shape,
        compiler_params=pltpu.CompilerParams(
            dimension_semantics=("arbitrary",), vmem_limit_bytes=VMEM_LIMIT),
        name="expert_ffn_convert" if convert else "expert_ffn",
    )(*args)


def _expert_layout(counts, n_blocks, block_rows, every_expert):
    pcounts = (counts + block_rows - 1) // block_rows * block_rows
    if every_expert:
        pcounts = jnp.maximum(pcounts, block_rows)
    pend = jnp.cumsum(pcounts)
    n_used = pend[-1:] // block_rows
    block_start = jnp.minimum(jnp.arange(n_blocks, dtype=jnp.int32), n_used[0] - 1) * block_rows
    block_e = jnp.sum((pend[None, :] <= block_start[:, None]).astype(jnp.int32), axis=1)
    return pend - pcounts, jnp.minimum(block_e, N_EXPERTS - 1), n_used


def _slot_rows(route, pstart, ns, nt, tl):
    n = route.shape[0] * ns * tl
    fields = lambda lo: route[:, lo:lo + TOP_K, :].reshape(-1, nt, TOP_K, ns, tl)
    per_slot = lambda lo: jnp.transpose(fields(lo), (2, 0, 3, 1, 4)).reshape(TOP_K, n)
    e = per_slot(TOP_K).astype(jnp.int32)
    rank = per_slot(2 * TOP_K).astype(jnp.int32)
    hit = e[:, :, None] == jnp.arange(N_EXPERTS, dtype=jnp.int32)[None, None, :]
    rows = rank + jnp.sum(jnp.where(hit, pstart[None, None, :], 0), axis=-1)
    gates = jnp.transpose(fields(0), (0, 3, 1, 4, 2)).reshape(n, TOP_K)
    return rows, gates


SC_WINDOW = 128
SC_COLS = 256


def _sc_mesh():
    return plsc.VectorSubcoreMesh(core_axis_name="c", subcore_axis_name="s")


def _dispatch(sources, m_pad):
    width = sources[0][0].shape[1]
    n_src = len(sources)
    assert width % SC_COLS == 0 and all(h.shape[0] % SC_WINDOW == 0 for h, _ in sources)

    @functools.partial(pl.kernel, mesh=_sc_mesh(),
                       out_type=jax.ShapeDtypeStruct((m_pad, width), sources[0][0].dtype), scratch_types=[])
    def k(*refs):
        xs_hbm = refs[2 * n_src]

        def body(x_vmem, i_vmem):
            j = pl.program_id(1)
            for kk in range(TOP_K):
                pltpu.sync_copy(x_vmem, xs_hbm.at[i_vmem.at[kk], pl.ds(j * SC_COLS, SC_COLS)])

        for si in range(n_src):
            src, rows = refs[2 * si], refs[2 * si + 1]
            pltpu.emit_pipeline(
                body,
                grid=(src.shape[0] // SC_WINDOW, width // SC_COLS),
                in_specs=[pl.BlockSpec((SC_WINDOW, SC_COLS), lambda i, j: (i, j)),
                          pl.BlockSpec((TOP_K, SC_WINDOW), lambda i, j: (0, i))],
                out_specs=[],
                core_axis_name=("c", "s"),
                dimension_semantics=(pltpu.PARALLEL, pltpu.ARBITRARY),
            )(src, rows)

    return k(*[a for pair in sources for a in pair])


def _collect(yb, rows):
    width = yb.shape[1]
    assert width % SC_COLS == 0 and rows.shape[1] % SC_WINDOW == 0

    @functools.partial(pl.kernel, mesh=_sc_mesh(),
                       out_type=jax.ShapeDtypeStruct((rows.shape[1], width), yb.dtype), scratch_types=[])
    def k(yb_hbm, r_hbm, o_hbm):
        def body(i_vmem, o_vmem):
            j = pl.program_id(1)
            pltpu.sync_copy(yb_hbm.at[i_vmem.at[0], pl.ds(j * SC_COLS, SC_COLS)], o_vmem)

        pltpu.emit_pipeline(
            body,
            grid=(rows.shape[1] // SC_WINDOW, width // SC_COLS),
            in_specs=[pl.BlockSpec((1, SC_WINDOW), lambda i, j: (0, i))],
            out_specs=[pl.BlockSpec((SC_WINDOW, SC_COLS), lambda i, j: (i, j))],
            core_axis_name=("c", "s"),
            dimension_semantics=(pltpu.PARALLEL, pltpu.ARBITRARY),
        )(r_hbm, o_hbm)

    return k(yb, rows)


def _combine_kernel(x_ref, yg_ref, route_ref, fw_ref, y_ref):
    half = D_MODEL // 2
    acc_lo = x_ref[:, :half]
    acc_hi = x_ref[:, half:]
    route = route_ref[...]
    for kk in range(TOP_K):
        y_lo, y_hi = _unpack_bf16(yg_ref[kk])
        gate = route[:, kk:kk + 1]
        acc_lo = acc_lo + y_lo * gate
        acc_hi = acc_hi + y_hi * gate
    ms = (jnp.sum(acc_lo * acc_lo, axis=-1, keepdims=True)
          + jnp.sum(acc_hi * acc_hi, axis=-1, keepdims=True)) * (1.0 / D_MODEL)
    scale = lax.rsqrt(ms + EPS)
    y_ref[:, :half] = acc_lo * scale * fw_ref[:, :half]
    y_ref[:, half:] = acc_hi * scale * fw_ref[:, half:]


def _combine_next_kernel(x_ref, yg_ref, route_ref, fw_ref, prev_ref, y_ref):
    del prev_ref
    _combine_kernel(x_ref, yg_ref, route_ref, fw_ref, y_ref)


def _combine(x2, yg, gates, final_w, y_prev, first_row, out_rows, out_first_row):
    assert yg.shape[1] % COMBINE_ROWS == 0 and first_row % COMBINE_ROWS == 0 and out_first_row % COMBINE_ROWS == 0
    blk_in = first_row // COMBINE_ROWS
    blk_out = out_first_row // COMBINE_ROWS
    in_specs = [
        pl.BlockSpec((COMBINE_ROWS, D_MODEL), lambda i: (blk_in + i, 0)),
        pl.BlockSpec((TOP_K, COMBINE_ROWS, D_MODEL // 2), lambda i: (0, i, 0)),
        pl.BlockSpec((COMBINE_ROWS, TOP_K), lambda i: (blk_in + i, 0)),
        _const_spec((1, D_MODEL)),
    ]
    args = [x2, yg, gates, final_w]
    kern, aliases = _combine_kernel, {}
    if y_prev is not None:
        in_specs.append(pl.BlockSpec(memory_space=pl.ANY))
        args.append(y_prev)
        kern, aliases = _combine_next_kernel, {len(args) - 1: 0}
    return pl.pallas_call(
        kern,
        grid=(yg.shape[1] // COMBINE_ROWS,),
        in_specs=in_specs,
        out_specs=pl.BlockSpec((COMBINE_ROWS, D_MODEL), lambda i: (blk_out + i, 0)),
        out_shape=jax.ShapeDtypeStruct((out_rows, D_MODEL), F32),
        input_output_aliases=aliases,
        compiler_params=pltpu.CompilerParams(
            dimension_semantics=("arbitrary",), vmem_limit_bytes=VMEM_LIMIT),
        name="combine_%d_%d" % (out_rows, out_first_row),
    )(*args)


def _collect_combine(yb, x2, rows, gates, final_w, chunks, y, out_rows, out_first_row):
    nc = x2.shape[0] // chunks
    for c in range(chunks):
        yg = _collect(yb, rows[:, c * nc:(c + 1) * nc].reshape(1, TOP_K * nc))
        y = _combine(x2, yg.reshape(TOP_K, nc, D_MODEL // 2), gates, final_w, y, c * nc,
                     out_rows, out_first_row + c * nc)
    return y


def kernel(x_prompt, x_sample, cache_mem_k, cache_mem_v, state_ret, state_pool, mem_prompt,
           norm_mix_w, w_in, ret_gn_w, w_pool, pool_scale, w_out, norm_mem_w, mem_norm_w,
           w_q_mem, w_kv_mem, w_o_mem, norm_ffn_w, router_w, router_b, w_gate_up, b_gate_up,
           w_down, b_down, final_norm_w):
    assert norm_mix_w.shape[0] == 1, "one layer"
    b, seq, _ = x_prompt.shape
    db, dseq, _ = x_sample.shape
    row = lambda a: a.reshape(1, -1)

    mk_p, mv_p, mkb_p, mvb_p = _mem_kv(mem_prompt, row(mem_norm_w[0]), w_kv_mem[0].astype(BF16))
    mkb_s, mvb_s = _kv_flat(cache_mem_k[0], cache_mem_v[0])

    wts = (row(norm_mix_w[0]), w_in[0].astype(BF16), row(ret_gn_w[0]), w_pool[0].astype(BF16),
           row(pool_scale[0]), w_out[0].astype(BF16), row(norm_mem_w[0]), w_q_mem[0].astype(BF16),
           w_o_mem[0].astype(BF16), row(norm_ffn_w[0]), router_w[0].T.astype(BF16),
           jnp.broadcast_to(router_b[0][:, None], (N_EXPERTS, LANES)))

    assert sum(GROUP_STREAMS) == b
    n_s = db * dseq
    half = D_MODEL // 2
    final_w = row(final_norm_w)
    bgu = jnp.concatenate([b_gate_up[0][:, 0::2], b_gate_up[0][:, 1::2]], axis=-1).reshape(N_EXPERTS, 1, 2 * D_FF)
    bd = b_down[0].reshape(N_EXPERTS, 1, D_MODEL)
    no_counts = jnp.zeros((N_EXPERTS, LANES), F32)
    hist_s = jnp.concatenate([jnp.zeros((db, 1, POOL_WIDTH), F32), state_pool[0]], axis=1)

    y_p, rets, pools, stream0 = None, [], [], 0
    for g, per in enumerate(GROUP_STREAMS):
        n_g = per * seq
        x2_g, h_g, route_g, ret_g, pool_g, cnt = _layer(
            x_prompt, stream0, per, 0, PROMPT_TILE, PROMPT_STREAMS, PROMPT_CHAIN,
            jnp.zeros((per, RET_HEADS, RET_DK, RET_DV), F32), jnp.zeros((per, HIST_ROWS, POOL_WIDTH), F32),
            mkb_p, mvb_p, wts, no_counts)
        rets.append(ret_g)
        pools.append(pool_g)
        with_sample = g == len(GROUP_STREAMS) - 1
        n_slots = n_g * TOP_K
        if with_sample:
            x2_s, h_s, route_s, ret_s, pool_s, cnt = _layer(
                x_sample, 0, db, PAST_LEN, dseq, SAMPLE_STREAMS, SAMPLE_STREAMS, state_ret[0], hist_s,
                mkb_s, mvb_s, wts, cnt)
            n_slots += n_s * TOP_K
        first = g == 0
        block_rows = CONVERT_ROWS if first else EXPERT_ROWS
        n_blocks = -(-(n_slots + N_EXPERTS * (block_rows - 1)) // block_rows) + (N_EXPERTS if first else 0)
        pstart, block_e, n_used = _expert_layout(cnt[:, 0].astype(jnp.int32), n_blocks, block_rows, first)
        rows_g, gates_g = _slot_rows(route_g, pstart, PROMPT_STREAMS, seq // PROMPT_TILE, PROMPT_TILE)
        sources = [(h_g.reshape(n_g, half), rows_g)]
        if with_sample:
            rows_s, gates_s = _slot_rows(route_s, pstart, SAMPLE_STREAMS, 1, dseq)
            sources.append((h_s.reshape(n_s, half), rows_s))
        xs = _dispatch(sources, n_blocks * block_rows)
        if first:
            yb, wgu_bf, wd_bf = _expert_ffn(block_e, n_used, xs, w_gate_up[0], bgu, w_down[0], bd, n_blocks,
                                            block_rows)
        else:
            (yb,) = _expert_ffn(block_e, n_used, xs, wgu_bf, bgu, wd_bf, bd, n_blocks, block_rows)
        if with_sample:
            y_s = _collect_combine(yb, x2_s.reshape(n_s, D_MODEL), rows_s, gates_s, final_w, 1, None, n_s, 0)
        y_p = _collect_combine(yb, x2_g.reshape(n_g, D_MODEL), rows_g, gates_g, final_w, n_g // COMBINE_TOKENS,
                               y_p, b * seq, stream0 * seq)
        stream0 += per
    y_p = y_p.reshape(b, seq, D_MODEL)
    y_s = y_s.reshape(db, dseq, D_MODEL)
    ret_p = jnp.concatenate(rets, axis=0)
    pool_p = jnp.concatenate(pools, axis=0)
    return (y_p, y_s, mk_p[None], mv_p[None], ret_p[None], pool_p[None], ret_s[None], pool_s[None])
```

```python
import functools

import numpy as np
import jax
import jax.numpy as jnp
from jax import lax
from jax.experimental import pallas as pl
from jax.experimental.pallas import tpu as pltpu
from jax.experimental.pallas import tpu_sc as plsc

D_MODEL = 1024
CHUNK = 64
PAST_LEN = 4096
RET_HEADS = 4
RET_DK = 128
RET_DV = 128
RET_QK = RET_HEADS * RET_DK
RET_VW = RET_HEADS * RET_DV
ROPE_BASE = 10000.0
POOL_WINDOWS = (2, 4, 8, 16)
POOL_GROUPS = 4
POOL_WIDTH = D_MODEL // 2
POOL_C = POOL_WIDTH // POOL_GROUPS
POOL_HIST = max(POOL_WINDOWS) - 1
HIST_ROWS = POOL_HIST + 1
IN_WIDTH = 2 * RET_QK + 2 * RET_VW + POOL_WIDTH
N_MEM = 256
MEM_HEADS = 4
MEM_HD = D_MODEL // MEM_HEADS
N_EXPERTS = 32
TOP_K = 4
D_FF = D_MODEL
SWIGLU_LIMIT = 7.0
SWIGLU_ALPHA = 1.702
EPS = 1e-5

LANES = 128
ROUTE_ROWS = 16
PROMPT_TILE = 256
PROMPT_STREAMS = 4
PROMPT_CHAIN = 2
SAMPLE_STREAMS = 8
CONVERT_ROWS = 512
EXPERT_ROWS = 1024
GROUP_STREAMS = (16, 16)
KV_STREAMS = 2
COMBINE_ROWS = 1024
COMBINE_TOKENS = 16384
VMEM_LIMIT = 56 * 1024 * 1024

BF16 = jnp.bfloat16
F32 = jnp.float32


def _rms(x, w):
    return x * lax.rsqrt(jnp.mean(x * x, axis=-1, keepdims=True) + EPS) * w


def _dot(a, b):
    return jnp.dot(a, b, preferred_element_type=F32)


def _rms_dot(x, w_norm, w_mat):
    inv = lax.rsqrt(jnp.mean(x * x, axis=-1, keepdims=True) + EPS)
    return _dot((x * w_norm).astype(BF16), w_mat) * inv


def _dot_nt(a, b):
    return lax.dot_general(a, b, (((1,), (1,)), ((), ())), preferred_element_type=F32)


BF16_BITS = 16
HIGH_HALF = -(1 << BF16_BITS)


def _pack_bf16(x):
    bits = lax.bitcast_convert_type(x.astype(BF16).astype(F32), jnp.int32)
    w = x.shape[1] // 2
    return lax.shift_right_logical(bits[:, :w], BF16_BITS) | (bits[:, w:] & HIGH_HALF)


def _unpack_bf16(p):
    lo = lax.bitcast_convert_type(lax.shift_left(p, BF16_BITS), F32)
    hi = lax.bitcast_convert_type(p & HIGH_HALF, F32)
    return lo, hi


def _const_spec(shape):
    nd = len(shape)
    return pl.BlockSpec(shape, lambda *_: (0,) * nd, pipeline_mode=pl.Buffered(1))


def _mem_kv_kernel(mem_ref, nw_ref, w_ref, k_ref, v_ref, kb_ref, vb_ref):
    ns = mem_ref.shape[0]
    xn = _rms(mem_ref[...].reshape(ns * N_MEM, D_MODEL), nw_ref[...]).astype(BF16)
    kv = _dot(xn, w_ref[...])
    for hd in range(MEM_HEADS):
        lo = hd * MEM_HD
        k_ref[:, :, hd, :] = kv[:, lo:lo + MEM_HD].reshape(ns, N_MEM, MEM_HD)
        v_ref[:, :, hd, :] = kv[:, D_MODEL + lo:D_MODEL + lo + MEM_HD].reshape(ns, N_MEM, MEM_HD)
    kb_ref[...] = kv[:, :D_MODEL].astype(BF16).reshape(ns, N_MEM, D_MODEL)
    vb_ref[...] = kv[:, D_MODEL:].astype(BF16).reshape(ns, N_MEM, D_MODEL)


def _mem_kv(mem, mem_norm_w, w_kv_bf):
    b = mem.shape[0]
    out_spec = pl.BlockSpec((KV_STREAMS, N_MEM, MEM_HEADS, MEM_HD), lambda i: (i, 0, 0, 0))
    flat_spec = pl.BlockSpec((KV_STREAMS, N_MEM, D_MODEL), lambda i: (i, 0, 0))
    return pl.pallas_call(
        _mem_kv_kernel,
        grid=(b // KV_STREAMS,),
        in_specs=[
            pl.BlockSpec((KV_STREAMS, N_MEM, D_MODEL), lambda i: (i, 0, 0)),
            _const_spec((1, D_MODEL)),
            _const_spec((D_MODEL, 2 * D_MODEL)),
        ],
        out_specs=[out_spec, out_spec, flat_spec, flat_spec],
        out_shape=[jax.ShapeDtypeStruct((b, N_MEM, MEM_HEADS, MEM_HD), F32)] * 2
        + [jax.ShapeDtypeStruct((b, N_MEM, D_MODEL), BF16)] * 2,
        compiler_params=pltpu.CompilerParams(
            dimension_semantics=("arbitrary",), vmem_limit_bytes=VMEM_LIMIT),
        name="mem_kv",
    )(mem, mem_norm_w, w_kv_bf)


def _kv_flat_kernel(k_ref, v_ref, kb_ref, vb_ref):
    for hd in range(MEM_HEADS):
        kb_ref[:, :, hd * MEM_HD:(hd + 1) * MEM_HD] = k_ref[:, :, hd, :].astype(BF16)
        vb_ref[:, :, hd * MEM_HD:(hd + 1) * MEM_HD] = v_ref[:, :, hd, :].astype(BF16)


def _kv_flat(mem_k, mem_v):
    b = mem_k.shape[0]
    in_spec = pl.BlockSpec((KV_STREAMS, N_MEM, MEM_HEADS, MEM_HD), lambda i: (i, 0, 0, 0))
    out_spec = pl.BlockSpec((KV_STREAMS, N_MEM, D_MODEL), lambda i: (i, 0, 0))
    return pl.pallas_call(
        _kv_flat_kernel,
        grid=(b // KV_STREAMS,),
        in_specs=[in_spec, in_spec],
        out_specs=[out_spec, out_spec],
        out_shape=[jax.ShapeDtypeStruct((b, N_MEM, D_MODEL), BF16)] * 2,
        compiler_params=pltpu.CompilerParams(
            dimension_semantics=("arbitrary",), vmem_limit_bytes=VMEM_LIMIT),
        name="kv_flat",
    )(mem_k, mem_v)


def _layer_kernel(x_ref, cos_ref, sin_ref, dmat_ref, qdec_ref, kdec_ref, state0_ref, hist0_ref,
                  mk_ref, mv_ref, nmix_ref, win_ref, gnw_ref, wpool_ref, pscale_ref, wout_ref,
                  nmem_ref, wq_ref, wo_ref, nffn_ref, rw_ref, rb_ref, earlier_ref, cnt0_ref,
                  x2_ref, h_ref, route_ref, rstate_ref, pstate_ref, cnt_ref,
                  s_scr, ext_scr, cnt_scr, *, tl, chain, pos0, cdec):
    t = pl.program_id(1)

    @pl.when((pl.program_id(0) == 0) & (t == 0))
    def _():
        cnt_scr[...] = cnt0_ref[...]

    @pl.when(t == 0)
    def _():
        s_scr[...] = state0_ref[...]
        ext_scr[:, 0:HIST_ROWS, :] = hist0_ref[...]

    for c0 in range(0, x_ref.shape[0], chain):
        _layer_chain(c0, t, x_ref, cos_ref, sin_ref, dmat_ref, qdec_ref, kdec_ref, nmix_ref, win_ref, gnw_ref,
                     wpool_ref, pscale_ref, wout_ref, nmem_ref, wq_ref, wo_ref, nffn_ref, rw_ref, rb_ref,
                     earlier_ref, x2_ref, h_ref, route_ref, pstate_ref, s_scr, ext_scr, mk_ref, mv_ref, cnt_scr,
                     tl=tl, chain=chain, pos0=pos0, cdec=cdec)
    rstate_ref[...] = s_scr[...]
    cnt_ref[...] = cnt_scr[...]


def _layer_chain(c0, t, x_ref, cos_ref, sin_ref, dmat_ref, qdec_ref, kdec_ref, nmix_ref, win_ref, gnw_ref,
                 wpool_ref, pscale_ref, wout_ref, nmem_ref, wq_ref, wo_ref, nffn_ref, rw_ref, rb_ref,
                 earlier_ref, x2_ref, h_ref, route_ref, pstate_ref, s_scr, ext_scr, mk_ref, mv_ref, cnt_scr,
                 *, tl, chain, pos0, cdec):
    rows = chain * tl
    x = x_ref[c0:c0 + chain].reshape(rows, D_MODEL)
    proj = _rms_dot(x, nmix_ref[...], win_ref[...])
    cos = cos_ref[...]
    sin = sin_ref[...]
    pos = (pos0 + t * tl + lax.broadcasted_iota(jnp.int32, (tl, POOL_C), 0)).astype(F32)

    mixes = []
    for sj in range(chain):
        si = c0 + sj
        pj = proj[sj * tl:(sj + 1) * tl]
        outs = []
        for hd in range(RET_HEADS):
            lo = hd * RET_DK
            q = pj[:, lo:lo + RET_DK]
            k = pj[:, RET_QK + lo:RET_QK + lo + RET_DK]
            v = pj[:, 2 * RET_QK + lo:2 * RET_QK + lo + RET_DV]
            g = pj[:, 2 * RET_QK + RET_VW + lo:2 * RET_QK + RET_VW + lo + RET_DV]
            qr = (q * cos + pltpu.roll(q, RET_DK // 2, 1) * sin) * (RET_DK ** -0.5)
            kr = k * cos + pltpu.roll(k, RET_DK // 2, 1) * sin
            vb = v.astype(BF16)
            s = _dot_nt(qr.astype(BF16), kr.astype(BF16)) * dmat_ref[hd]
            o = _dot(s.astype(BF16), vb)
            state = s_scr[si, hd]
            o = o + _dot((qr * qdec_ref[hd]).astype(BF16), state.astype(BF16))
            kd_t = jnp.transpose(kr * kdec_ref[hd]).astype(BF16)
            s_scr[si, hd] = cdec[hd] * state + _dot(kd_t, vb)
            mu = jnp.mean(o, axis=-1, keepdims=True)
            oc = o - mu
            var = jnp.mean(oc * oc, axis=-1, keepdims=True)
            on = oc * lax.rsqrt(var + EPS)
            outs.append(on * gnw_ref[:, lo:lo + RET_DV] * (g * jax.nn.sigmoid(g)))

        pin = pj[:, 2 * RET_QK + 2 * RET_VW:]
        ext_scr[si, HIST_ROWS:HIST_ROWS + tl, :] = pin
        pstate_ref[si] = pin[tl - POOL_HIST:, :]
        for gi, w in enumerate(POOL_WINDOWS):
            lo = gi * POOL_C
            wsum = ext_scr[si, :, lo:lo + POOL_C]
            shift = 1
            while shift < w:
                wsum = wsum + pltpu.roll(wsum, shift, 0)
                shift *= 2
            cnt = jnp.minimum(float(w), pos + 1.0)
            d = wsum[HIST_ROWS:, :] / cnt - pin[:, lo:lo + POOL_C]
            y = _dot(d.astype(BF16), wpool_ref[gi])
            outs.append(y * pscale_ref[:, lo:lo + POOL_C])
        ext_scr[si, 0:HIST_ROWS, :] = ext_scr[si, tl:tl + HIST_ROWS, :]
        mixes.append(jnp.concatenate(outs, axis=-1).astype(BF16))

    x1 = x + _dot(jnp.concatenate(mixes, axis=0), wout_ref[...])

    qm = _rms_dot(x1, nmem_ref[...], wq_ref[...])
    atts = []
    for sj in range(chain):
        si = c0 + sj
        aouts = []
        for hd in range(MEM_HEADS):
            lo = hd * MEM_HD
            qh = qm[sj * tl:(sj + 1) * tl, lo:lo + MEM_HD].astype(BF16)
            s = _dot_nt(qh, mk_ref[si, :, lo:lo + MEM_HD]) * (MEM_HD ** -0.5)
            e = jnp.exp(s - jnp.max(s, axis=-1, keepdims=True))
            p = e / jnp.sum(e, axis=-1, keepdims=True)
            aouts.append(_dot(p.astype(BF16), mv_ref[si, :, lo:lo + MEM_HD]))
        atts.append(jnp.concatenate(aouts, axis=-1).astype(BF16))
    x2 = x1 + _dot(jnp.concatenate(atts, axis=0), wo_ref[...])
    x2_ref[c0:c0 + chain] = x2.reshape(chain, tl, D_MODEL)

    hn = _rms(x2, nffn_ref[...])
    h_ref[c0:c0 + chain] = _pack_bf16(hn).reshape(chain, tl, D_MODEL // 2)
    lane_tile = lambda a: a[:, :rows] if rows <= LANES else jnp.concatenate([a] * (rows // LANES), axis=1)
    logits = _dot_nt(rw_ref[...], hn.astype(BF16)) + lane_tile(rb_ref[...])
    eiota = lax.broadcasted_iota(jnp.int32, (N_EXPERTS, rows), 0)
    neg = jnp.finfo(F32).min
    vals, idxs = [], []
    for _k in range(TOP_K):
        m = jnp.max(logits, axis=0, keepdims=True)
        idx = jnp.min(jnp.where(logits == m, eiota, N_EXPERTS), axis=0, keepdims=True)
        vals.append(m)
        idxs.append(idx)
        logits = jnp.where(eiota == idx, neg, logits)
    exps = [jnp.exp(vk - vals[0]) for vk in vals]
    den = exps[0] + exps[1] + exps[2] + exps[3]

    onehots = [(eiota == idx).astype(F32) for idx in idxs]
    picked = onehots[0] + onehots[1] + onehots[2] + onehots[3]
    before = lane_tile(cnt_scr[...]) + _dot(picked.astype(BF16), earlier_ref[...])
    ranks = [jnp.sum(oh * before, axis=0, keepdims=True) for oh in onehots]
    cnt_scr[...] = cnt_scr[...] + jnp.sum(picked, axis=1, keepdims=True)

    rowi = lax.broadcasted_iota(jnp.int32, (ROUTE_ROWS, rows), 0)
    route = jnp.zeros((ROUTE_ROWS, rows), F32)
    for kk in range(TOP_K):
        route = jnp.where(rowi == kk, exps[kk] / den, route)
        route = jnp.where(rowi == TOP_K + kk, idxs[kk].astype(F32), route)
        route = jnp.where(rowi == 2 * TOP_K + kk, ranks[kk], route)
    route_ref[0, :, c0 * tl:c0 * tl + rows] = route


def _decay_tables(tl):
    hh = np.arange(RET_HEADS, dtype=np.float64)
    log_g = np.log1p(-np.exp2(-5.0 - hh))
    idx = np.arange(tl, dtype=np.float64)
    dist = np.abs(idx[:, None] - idx[None, :])
    visible = (idx[None, :] // CHUNK) <= (idx[:, None] // CHUNK)
    dmat = np.where(visible[None], np.exp(log_g[:, None, None] * dist[None]), 0.0)
    qdec = np.exp(log_g[:, None] * (idx[None, :] + 1.0))
    kdec = np.exp(log_g[:, None] * (tl - 1.0 - idx[None, :]))
    cdec = tuple(float(c) for c in np.exp(log_g * tl).astype(np.float32))
    bcast = lambda a: np.ascontiguousarray(np.broadcast_to(a[:, :, None], (RET_HEADS, tl, RET_DK)))
    return (jnp.asarray(dmat, F32), jnp.asarray(bcast(qdec), F32), jnp.asarray(bcast(kdec), F32), cdec)


def _rotary_tables(pos0, length):
    half = RET_DK // 2
    inv_freq = jnp.power(ROPE_BASE, -jnp.arange(half, dtype=F32) / half)
    ang = (pos0 + jnp.arange(length, dtype=jnp.int32)).astype(F32)[:, None] * inv_freq[None, :]
    cos, sin = jnp.cos(ang), jnp.sin(ang)
    return jnp.concatenate([cos, cos], axis=-1), jnp.concatenate([-sin, sin], axis=-1)


def _layer(x, stream0, b, pos0, tl, ns, chain, state0, hist0, mk, mv, wts, cnt0):
    length = x.shape[1]
    assert b % ns == 0 and stream0 % ns == 0 and ns % chain == 0, (b, stream0, ns, chain)
    assert length % tl == 0 and tl % CHUNK == 0 and tl > POOL_HIST, (length, tl)
    nt = length // tl
    g0 = stream0 // ns
    rows = ns * tl
    cos, sin = _rotary_tables(pos0, length)
    dmat, qdec, kdec, cdec = _decay_tables(tl)
    crows = chain * tl
    earlier = jnp.asarray(np.triu(np.ones((crows, crows), np.float32), 1), BF16)
    kern = functools.partial(_layer_kernel, tl=tl, chain=chain, pos0=pos0, cdec=cdec)
    tok = lambda width: pl.BlockSpec((ns, tl, width), lambda i, j: (i, j, 0))
    per_stream = lambda *shape: pl.BlockSpec((ns,) + shape, lambda i, j: (i,) + (0,) * len(shape))
    per_stream_in = lambda off, *shape: pl.BlockSpec(
        (ns,) + shape, lambda i, j: (i + off,) + (0,) * len(shape), pipeline_mode=pl.Buffered(1))
    in_specs = [
        pl.BlockSpec((ns, tl, D_MODEL), lambda i, j: (i + g0, j, 0)),
        pl.BlockSpec((tl, RET_DK), lambda i, j: (j, 0)),
        pl.BlockSpec((tl, RET_DK), lambda i, j: (j, 0)),
        _const_spec((RET_HEADS, tl, tl)),
        _const_spec((RET_HEADS, tl, RET_DK)),
        _const_spec((RET_HEADS, tl, RET_DK)),
        per_stream_in(0, RET_HEADS, RET_DK, RET_DV),
        per_stream_in(0, HIST_ROWS, POOL_WIDTH),
        per_stream_in(g0, N_MEM, D_MODEL),
        per_stream_in(g0, N_MEM, D_MODEL),
    ] + [_const_spec(w.shape) for w in wts] + [_const_spec((crows, crows)), _const_spec((N_EXPERTS, LANES))]
    out_specs = [
        tok(D_MODEL), tok(D_MODEL // 2),
        pl.BlockSpec((1, ROUTE_ROWS, rows), lambda i, j: (i * nt + j, 0, 0)),
        per_stream(RET_HEADS, RET_DK, RET_DV),
        per_stream(POOL_HIST, POOL_WIDTH),
        pl.BlockSpec((N_EXPERTS, LANES), lambda i, j: (0, 0)),
    ]
    out_shape = [
        jax.ShapeDtypeStruct((b, length, D_MODEL), F32),
        jax.ShapeDtypeStruct((b, length, D_MODEL // 2), jnp.int32),
        jax.ShapeDtypeStruct((b // ns * nt, ROUTE_ROWS, rows), F32),
        jax.ShapeDtypeStruct((b, RET_HEADS, RET_DK, RET_DV), F32),
        jax.ShapeDtypeStruct((b, POOL_HIST, POOL_WIDTH), F32),
        jax.ShapeDtypeStruct((N_EXPERTS, LANES), F32),
    ]
    scratch = [
        pltpu.VMEM((ns, RET_HEADS, RET_DK, RET_DV), F32),
        pltpu.VMEM((ns, HIST_ROWS + tl, POOL_WIDTH), F32),
        pltpu.VMEM((N_EXPERTS, LANES), F32),
    ]
    return pl.pallas_call(
        kern,
        grid=(b // ns, nt),
        in_specs=in_specs,
        out_specs=out_specs,
        out_shape=out_shape,
        scratch_shapes=scratch,
        compiler_params=pltpu.CompilerParams(
            dimension_semantics=("arbitrary", "arbitrary"), vmem_limit_bytes=VMEM_LIMIT),
        name="layer_tl%d" % tl,
    )(x, cos, sin, dmat, qdec, kdec, state0, hist0, mk, mv, *wts, earlier, cnt0)


SPLIT_COLS = 2 * LANES


def _expert_block(x_ref, wgu, bgu_ref, wd, bd_ref, y_ref):
    x_lo, x_hi = _unpack_bf16(x_ref[...])
    xb = jnp.concatenate([x_lo.astype(BF16), x_hi.astype(BF16)], axis=1)
    gu = _dot(xb, wgu[...]) + bgu_ref[0]
    gate = jnp.minimum(gu[:, :D_FF], SWIGLU_LIMIT)
    up = jnp.clip(gu[:, D_FF:], -SWIGLU_LIMIT, SWIGLU_LIMIT)
    act = (up + 1.0) * gate * jax.nn.sigmoid(SWIGLU_ALPHA * gate)
    y_ref[...] = _pack_bf16(_dot(act.astype(BF16), wd[...]) + bd_ref[0])


def _expert_convert_kernel(be_ref, used_ref, x_ref, wgu_ref, bgu_ref, wd_ref, bd_ref, perm_ref,
                           y_ref, wgu_out_ref, wd_out_ref):
    i = pl.program_id(0)
    in_use = i < used_ref[0]

    @pl.when(in_use & ((i == 0) | (be_ref[i] != be_ref[jnp.maximum(i - 1, 0)])))
    def _():
        perm = perm_ref[...]
        for c in range(2 * D_FF // SPLIT_COLS):
            wc = wgu_ref[0, :, c * SPLIT_COLS:(c + 1) * SPLIT_COLS].astype(BF16)
            pc = _dot(wc, perm).astype(BF16)
            wgu_out_ref[0, :, c * LANES:(c + 1) * LANES] = pc[:, :LANES]
            wgu_out_ref[0, :, D_FF + c * LANES:D_FF + (c + 1) * LANES] = pc[:, LANES:]
        wd_out_ref[0] = wd_ref[0].astype(BF16)

    @pl.when(in_use)
    def _():
        _expert_block(x_ref, wgu_out_ref.at[0], bgu_ref, wd_out_ref.at[0], bd_ref, y_ref)


def _expert_ready_kernel(be_ref, used_ref, x_ref, wgu_ref, bgu_ref, wd_ref, bd_ref, y_ref):
    del be_ref

    @pl.when(pl.program_id(0) < used_ref[0])
    def _():
        _expert_block(x_ref, wgu_ref.at[0], bgu_ref, wd_ref.at[0], bd_ref, y_ref)


def _expert_ffn(block_e, n_used, xs, wgu, bgu, wd, bd, n_blocks, block_rows):
    convert = wgu.dtype != BF16
    blk = lambda i, be, used: (jnp.minimum(i, used[0] - 1), 0)
    per_expert = lambda i, be, used: (be[i], 0, 0)
    in_specs = [
        pl.BlockSpec((block_rows, D_MODEL // 2), blk),
        pl.BlockSpec((1, D_MODEL, 2 * D_FF), per_expert),
        pl.BlockSpec((1, 1, 2 * D_FF), per_expert),
        pl.BlockSpec((1, D_FF, D_MODEL), per_expert),
        pl.BlockSpec((1, 1, D_MODEL), per_expert),
    ]
    out_specs = [pl.BlockSpec((block_rows, D_MODEL // 2), blk)]
    out_shape = [jax.ShapeDtypeStruct(xs.shape, jnp.int32)]
    args = [block_e, n_used, xs, wgu, bgu, wd, bd]
    if convert:
        perm = np.zeros((SPLIT_COLS, SPLIT_COLS), np.float32)
        j = np.arange(LANES)
        perm[2 * j, j] = 1.0
        perm[2 * j + 1, LANES + j] = 1.0
        in_specs.append(_const_spec((SPLIT_COLS, SPLIT_COLS)))
        args.append(jnp.asarray(perm, BF16))
        out_specs += [pl.BlockSpec((1, D_MODEL, 2 * D_FF), per_expert), pl.BlockSpec((1, D_FF, D_MODEL), per_expert)]
        out_shape += [jax.ShapeDtypeStruct(wgu.shape, BF16), jax.ShapeDtypeStruct(wd.shape, BF16)]
    grid_spec = pltpu.PrefetchScalarGridSpec(
        num_scalar_prefetch=2, grid=(n_blocks,), in_specs=in_specs, out_specs=out_specs)
    return pl.pallas_call(
        _expert_convert_kernel if convert else _expert_ready_kernel,
        grid_spec=grid_spec,
        out_shape=out_shape,
        compiler_params=pltpu.CompilerParams(
            dimension_semantics=("arbitrary",), vmem_limit_bytes=VMEM_LIMIT),
        name="expert_ffn_convert" if convert else "expert_ffn",
    )(*args)


def _expert_layout(counts, n_blocks, block_rows, every_expert):
    pcounts = (counts + block_rows - 1) // block_rows * block_rows
    if every_expert:
        pcounts = jnp.maximum(pcounts, block_rows)
    pend = jnp.cumsum(pcounts)
    n_used = pend[-1:] // block_rows
    block_start = jnp.minimum(jnp.arange(n_blocks, dtype=jnp.int32), n_used[0] - 1) * block_rows
    block_e = jnp.sum((pend[None, :] <= block_start[:, None]).astype(jnp.int32), axis=1)
    return pend - pcounts, jnp.minimum(block_e, N_EXPERTS - 1), n_used


def _slot_rows(route, pstart, ns, nt, tl):
    n = route.shape[0] * ns * tl
    fields = lambda lo: route[:, lo:lo + TOP_K, :].reshape(-1, nt, TOP_K, ns, tl)
    per_slot = lambda lo: jnp.transpose(fields(lo), (2, 0, 3, 1, 4)).reshape(TOP_K, n)
    e = per_slot(TOP_K).astype(jnp.int32)
    rank = per_slot(2 * TOP_K).astype(jnp.int32)
    hit = e[:, :, None] == jnp.arange(N_EXPERTS, dtype=jnp.int32)[None, None, :]
    rows = rank + jnp.sum(jnp.where(hit, pstart[None, None, :], 0), axis=-1)
    gates = jnp.transpose(fields(0), (0, 3, 1, 4, 2)).reshape(n, TOP_K)
    return rows, gates


SC_WINDOW = 128
SC_COLS = 256


def _sc_mesh():
    return plsc.VectorSubcoreMesh(core_axis_name="c", subcore_axis_name="s")


def _dispatch(sources, m_pad):
    width = sources[0][0].shape[1]
    n_src = len(sources)
    assert width % SC_COLS == 0 and all(h.shape[0] % SC_WINDOW == 0 for h, _ in sources)

    @functools.partial(pl.kernel, mesh=_sc_mesh(),
                       out_type=jax.ShapeDtypeStruct((m_pad, width), sources[0][0].dtype), scratch_types=[])
    def k(*refs):
        xs_hbm = refs[2 * n_src]

        def body(x_vmem, i_vmem):
            j = pl.program_id(1)
            for kk in range(TOP_K):
                pltpu.sync_copy(x_vmem, xs_hbm.at[i_vmem.at[kk], pl.ds(j * SC_COLS, SC_COLS)])

        for si in range(n_src):
            src, rows = refs[2 * si], refs[2 * si + 1]
            pltpu.emit_pipeline(
                body,
                grid=(src.shape[0] // SC_WINDOW, width // SC_COLS),
                in_specs=[pl.BlockSpec((SC_WINDOW, SC_COLS), lambda i, j: (i, j)),
                          pl.BlockSpec((TOP_K, SC_WINDOW), lambda i, j: (0, i))],
                out_specs=[],
                core_axis_name=("c", "s"),
                dimension_semantics=(pltpu.PARALLEL, pltpu.ARBITRARY),
            )(src, rows)

    return k(*[a for pair in sources for a in pair])


def _collect(yb, rows):
    width = yb.shape[1]
    assert width % SC_COLS == 0 and rows.shape[1] % SC_WINDOW == 0

    @functools.partial(pl.kernel, mesh=_sc_mesh(),
                       out_type=jax.ShapeDtypeStruct((rows.shape[1], width), yb.dtype), scratch_types=[])
    def k(yb_hbm, r_hbm, o_hbm):
        def body(i_vmem, o_vmem):
            j = pl.program_id(1)
            pltpu.sync_copy(yb_hbm.at[i_vmem.at[0], pl.ds(j * SC_COLS, SC_COLS)], o_vmem)

        pltpu.emit_pipeline(
            body,
            grid=(rows.shape[1] // SC_WINDOW, width // SC_COLS),
            in_specs=[pl.BlockSpec((1, SC_WINDOW), lambda i, j: (0, i))],
            out_specs=[pl.BlockSpec((SC_WINDOW, SC_COLS), lambda i, j: (i, j))],
            core_axis_name=("c", "s"),
            dimension_semantics=(pltpu.PARALLEL, pltpu.ARBITRARY),
        )(r_hbm, o_hbm)

    return k(yb, rows)


def _combine_kernel(x_ref, yg_ref, route_ref, fw_ref, y_ref):
    half = D_MODEL // 2
    acc_lo = x_ref[:, :half]
    acc_hi = x_ref[:, half:]
    route = route_ref[...]
    for kk in range(TOP_K):
        y_lo, y_hi = _unpack_bf16(yg_ref[kk])
        gate = route[:, kk:kk + 1]
        acc_lo = acc_lo + y_lo * gate
        acc_hi = acc_hi + y_hi * gate
    ms = (jnp.sum(acc_lo * acc_lo, axis=-1, keepdims=True)
          + jnp.sum(acc_hi * acc_hi, axis=-1, keepdims=True)) * (1.0 / D_MODEL)
    scale = lax.rsqrt(ms + EPS)
    y_ref[:, :half] = acc_lo * scale * fw_ref[:, :half]
    y_ref[:, half:] = acc_hi * scale * fw_ref[:, half:]


def _combine_next_kernel(x_ref, yg_ref, route_ref, fw_ref, prev_ref, y_ref):
    del prev_ref
    _combine_kernel(x_ref, yg_ref, route_ref, fw_ref, y_ref)


def _combine(x2, yg, gates, final_w, y_prev, first_row, out_rows, out_first_row):
    assert yg.shape[1] % COMBINE_ROWS == 0 and first_row % COMBINE_ROWS == 0 and out_first_row % COMBINE_ROWS == 0
    blk_in = first_row // COMBINE_ROWS
    blk_out = out_first_row // COMBINE_ROWS
    in_specs = [
        pl.BlockSpec((COMBINE_ROWS, D_MODEL), lambda i: (blk_in + i, 0)),
        pl.BlockSpec((TOP_K, COMBINE_ROWS, D_MODEL // 2), lambda i: (0, i, 0)),
        pl.BlockSpec((COMBINE_ROWS, TOP_K), lambda i: (blk_in + i, 0)),
        _const_spec((1, D_MODEL)),
    ]
    args = [x2, yg, gates, final_w]
    kern, aliases = _combine_kernel, {}
    if y_prev is not None:
        in_specs.append(pl.BlockSpec(memory_space=pl.ANY))
        args.append(y_prev)
        kern, aliases = _combine_next_kernel, {len(args) - 1: 0}
    return pl.pallas_call(
        kern,
        grid=(yg.shape[1] // COMBINE_ROWS,),
        in_specs=in_specs,
        out_specs=pl.BlockSpec((COMBINE_ROWS, D_MODEL), lambda i: (blk_out + i, 0)),
        out_shape=jax.ShapeDtypeStruct((out_rows, D_MODEL), F32),
        input_output_aliases=aliases,
        compiler_params=pltpu.CompilerParams(
            dimension_semantics=("arbitrary",), vmem_limit_bytes=VMEM_LIMIT),
        name="combine_%d_%d" % (out_rows, out_first_row),
    )(*args)


def _collect_combine(yb, x2, rows, gates, final_w, chunks, y, out_rows, out_first_row):
    nc = x2.shape[0] // chunks
    for c in range(chunks):
        yg = _collect(yb, rows[:, c * nc:(c + 1) * nc].reshape(1, TOP_K * nc))
        y = _combine(x2, yg.reshape(TOP_K, nc, D_MODEL // 2), gates, final_w, y, c * nc,
                     out_rows, out_first_row + c * nc)
    return y


def kernel(x_prompt, x_sample, cache_mem_k, cache_mem_v, state_ret, state_pool, mem_prompt,
           norm_mix_w, w_in, ret_gn_w, w_pool, pool_scale, w_out, norm_mem_w, mem_norm_w,
           w_q_mem, w_kv_mem, w_o_mem, norm_ffn_w, router_w, router_b, w_gate_up, b_gate_up,
           w_down, b_down, final_norm_w):
    assert norm_mix_w.shape[0] == 1, "one layer"
    b, seq, _ = x_prompt.shape
    db, dseq, _ = x_sample.shape
    row = lambda a: a.reshape(1, -1)

    mk_p, mv_p, mkb_p, mvb_p = _mem_kv(mem_prompt, row(mem_norm_w[0]), w_kv_mem[0].astype(BF16))
    mkb_s, mvb_s = _kv_flat(cache_mem_k[0], cache_mem_v[0])

    wts = (row(norm_mix_w[0]), w_in[0].astype(BF16), row(ret_gn_w[0]), w_pool[0].astype(BF16),
           row(pool_scale[0]), w_out[0].astype(BF16), row(norm_mem_w[0]), w_q_mem[0].astype(BF16),
           w_o_mem[0].astype(BF16), row(norm_ffn_w[0]), router_w[0].T.astype(BF16),
           jnp.broadcast_to(router_b[0][:, None], (N_EXPERTS, LANES)))

    assert sum(GROUP_STREAMS) == b
    n_s = db * dseq
    half = D_MODEL // 2
    final_w = row(final_norm_w)
    bgu = jnp.concatenate([b_gate_up[0][:, 0::2], b_gate_up[0][:, 1::2]], axis=-1).reshape(N_EXPERTS, 1, 2 * D_FF)
    bd = b_down[0].reshape(N_EXPERTS, 1, D_MODEL)
    no_counts = jnp.zeros((N_EXPERTS, LANES), F32)
    hist_s = jnp.concatenate([jnp.zeros((db, 1, POOL_WIDTH), F32), state_pool[0]], axis=1)

    y_p, rets, pools, stream0 = None, [], [], 0
    for g, per in enumerate(GROUP_STREAMS):
        n_g = per * seq
        x2_g, h_g, route_g, ret_g, pool_g, cnt = _layer(
            x_prompt, stream0, per, 0, PROMPT_TILE, PROMPT_STREAMS, PROMPT_CHAIN,
            jnp.zeros((per, RET_HEADS, RET_DK, RET_DV), F32), jnp.zeros((per, HIST_ROWS, POOL_WIDTH), F32),
            mkb_p, mvb_p, wts, no_counts)
        rets.append(ret_g)
        pools.append(pool_g)
        with_sample = g == len(GROUP_STREAMS) - 1
        n_slots = n_g * TOP_K
        if with_sample:
            x2_s, h_s, route_s, ret_s, pool_s, cnt = _layer(
                x_sample, 0, db, PAST_LEN, dseq, SAMPLE_STREAMS, SAMPLE_STREAMS, state_ret[0], hist_s,
                mkb_s, mvb_s, wts, cnt)
            n_slots += n_s * TOP_K
        first = g == 0
        block_rows = CONVERT_ROWS if first else EXPERT_ROWS
        n_blocks = -(-(n_slots + N_EXPERTS * (block_rows - 1)) // block_rows) + (N_EXPERTS if first else 0)
        pstart, block_e, n_used = _expert_layout(cnt[:, 0].astype(jnp.int32), n_blocks, block_rows, first)
        rows_g, gates_g = _slot_rows(route_g, pstart, PROMPT_STREAMS, seq // PROMPT_TILE, PROMPT_TILE)
        sources = [(h_g.reshape(n_g, half), rows_g)]
        if with_sample:
            rows_s, gates_s = _slot_rows(route_s, pstart, SAMPLE_STREAMS, 1, dseq)
            sources.append((h_s.reshape(n_s, half), rows_s))
        xs = _dispatch(sources, n_blocks * block_rows)
        if first:
            yb, wgu_bf, wd_bf = _expert_ffn(block_e, n_used, xs, w_gate_up[0], bgu, w_down[0], bd, n_blocks,
                                            block_rows)
        else:
            (yb,) = _expert_ffn(block_e, n_used, xs, wgu_bf, bgu, wd_bf, bd, n_blocks, block_rows)
        if with_sample:
            y_s = _collect_combine(yb, x2_s.reshape(n_s, D_MODEL), rows_s, gates_s, final_w, 1, None, n_s, 0)
        y_p = _collect_combine(yb, x2_g.reshape(n_g, D_MODEL), rows_g, gates_g, final_w, n_g // COMBINE_TOKENS,
                               y_p, b * seq, stream0 * seq)
        stream0 += per
    y_p = y_p.reshape(b, seq, D_MODEL)
    y_s = y_s.reshape(db, dseq, D_MODEL)
    ret_p = jnp.concatenate(rets, axis=0)
    pool_p = jnp.concatenate(pools, axis=0)
    return (y_p, y_s, mk_p[None], mv_p[None], ret_p[None], pool_p[None], ret_s[None], pool_s[None])
```

```python
import functools

import numpy as np
import jax
import jax.numpy as jnp
from jax import lax
from jax.experimental import pallas as pl
from jax.experimental.pallas import tpu as pltpu
from jax.experimental.pallas import tpu_sc as plsc

D_MODEL = 1024
CHUNK = 64
PAST_LEN = 4096
RET_HEADS = 4
RET_DK = 128
RET_DV = 128
RET_QK = RET_HEADS * RET_DK
RET_VW = RET_HEADS * RET_DV
ROPE_BASE = 10000.0
POOL_WINDOWS = (2, 4, 8, 16)
POOL_GROUPS = 4
POOL_WIDTH = D_MODEL // 2
POOL_C = POOL_WIDTH // POOL_GROUPS
POOL_HIST = max(POOL_WINDOWS) - 1
HIST_ROWS = POOL_HIST + 1
IN_WIDTH = 2 * RET_QK + 2 * RET_VW + POOL_WIDTH
N_MEM = 256
MEM_HEADS = 4
MEM_HD = D_MODEL // MEM_HEADS
N_EXPERTS = 32
TOP_K = 4
D_FF = D_MODEL
SWIGLU_LIMIT = 7.0
SWIGLU_ALPHA = 1.702
EPS = 1e-5

LANES = 128
ROUTE_ROWS = 16
PROMPT_TILE = 256
PROMPT_STREAMS = 4
PROMPT_CHAIN = 2
SAMPLE_STREAMS = 8
CONVERT_ROWS = 512
EXPERT_ROWS = 1024
GROUP_STREAMS = (16, 16)
KV_STREAMS = 2
COMBINE_ROWS = 1024
COMBINE_TOKENS = 16384
VMEM_LIMIT = 56 * 1024 * 1024

BF16 = jnp.bfloat16
F32 = jnp.float32


def _rms(x, w):
    return x * lax.rsqrt(jnp.mean(x * x, axis=-1, keepdims=True) + EPS) * w


def _dot(a, b):
    return jnp.dot(a, b, preferred_element_type=F32)


def _rms_dot(x, w_norm, w_mat):
    inv = lax.rsqrt(jnp.mean(x * x, axis=-1, keepdims=True) + EPS)
    return _dot((x * w_norm).astype(BF16), w_mat) * inv


def _dot_nt(a, b):
    return lax.dot_general(a, b, (((1,), (1,)), ((), ())), preferred_element_type=F32)


BF16_BITS = 16
HIGH_HALF = -(1 << BF16_BITS)


def _pack_bf16(x):
    bits = lax.bitcast_convert_type(x.astype(BF16).astype(F32), jnp.int32)
    w = x.shape[1] // 2
    return lax.shift_right_logical(bits[:, :w], BF16_BITS) | (bits[:, w:] & HIGH_HALF)


def _unpack_bf16(p):
    lo = lax.bitcast_convert_type(lax.shift_left(p, BF16_BITS), F32)
    hi = lax.bitcast_convert_type(p & HIGH_HALF, F32)
    return lo, hi


def _const_spec(shape):
    nd = len(shape)
    return pl.BlockSpec(shape, lambda *_: (0,) * nd, pipeline_mode=pl.Buffered(1))


def _mem_kv_kernel(mem_ref, nw_ref, w_ref, k_ref, v_ref, kb_ref, vb_ref):
    ns = mem_ref.shape[0]
    xn = _rms(mem_ref[...].reshape(ns * N_MEM, D_MODEL), nw_ref[...]).astype(BF16)
    kv = _dot(xn, w_ref[...])
    for hd in range(MEM_HEADS):
        lo = hd * MEM_HD
        k_ref[:, :, hd, :] = kv[:, lo:lo + MEM_HD].reshape(ns, N_MEM, MEM_HD)
        v_ref[:, :, hd, :] = kv[:, D_MODEL + lo:D_MODEL + lo + MEM_HD].reshape(ns, N_MEM, MEM_HD)
    kb_ref[...] = kv[:, :D_MODEL].astype(BF16).reshape(ns, N_MEM, D_MODEL)
    vb_ref[...] = kv[:, D_MODEL:].astype(BF16).reshape(ns, N_MEM, D_MODEL)


def _mem_kv(mem, mem_norm_w, w_kv_bf):
    b = mem.shape[0]
    out_spec = pl.BlockSpec((KV_STREAMS, N_MEM, MEM_HEADS, MEM_HD), lambda i: (i, 0, 0, 0))
    flat_spec = pl.BlockSpec((KV_STREAMS, N_MEM, D_MODEL), lambda i: (i, 0, 0))
    return pl.pallas_call(
        _mem_kv_kernel,
        grid=(b // KV_STREAMS,),
        in_specs=[
            pl.BlockSpec((KV_STREAMS, N_MEM, D_MODEL), lambda i: (i, 0, 0)),
            _const_spec((1, D_MODEL)),
            _const_spec((D_MODEL, 2 * D_MODEL)),
        ],
        out_specs=[out_spec, out_spec, flat_spec, flat_spec],
        out_shape=[jax.ShapeDtypeStruct((b, N_MEM, MEM_HEADS, MEM_HD), F32)] * 2
        + [jax.ShapeDtypeStruct((b, N_MEM, D_MODEL), BF16)] * 2,
        compiler_params=pltpu.CompilerParams(
            dimension_semantics=("arbitrary",), vmem_limit_bytes=VMEM_LIMIT),
        name="mem_kv",
    )(mem, mem_norm_w, w_kv_bf)


def _kv_flat_kernel(k_ref, v_ref, kb_ref, vb_ref):
    for hd in range(MEM_HEADS):
        kb_ref[:, :, hd * MEM_HD:(hd + 1) * MEM_HD] = k_ref[:, :, hd, :].astype(BF16)
        vb_ref[:, :, hd * MEM_HD:(hd + 1) * MEM_HD] = v_ref[:, :, hd, :].astype(BF16)


def _kv_flat(mem_k, mem_v):
    b = mem_k.shape[0]
    in_spec = pl.BlockSpec((KV_STREAMS, N_MEM, MEM_HEADS, MEM_HD), lambda i: (i, 0, 0, 0))
    out_spec = pl.BlockSpec((KV_STREAMS, N_MEM, D_MODEL), lambda i: (i, 0, 0))
    return pl.pallas_call(
        _kv_flat_kernel,
        grid=(b // KV_STREAMS,),
        in_specs=[in_spec, in_spec],
        out_specs=[out_spec, out_spec],
        out_shape=[jax.ShapeDtypeStruct((b, N_MEM, D_MODEL), BF16)] * 2,
        compiler_params=pltpu.CompilerParams(
            dimension_semantics=("arbitrary",), vmem_limit_bytes=VMEM_LIMIT),
        name="kv_flat",
    )(mem_k, mem_v)


def _layer_kernel(x_ref, cos_ref, sin_ref, dmat_ref, qdec_ref, kdec_ref, state0_ref, hist0_ref,
                  mk_ref, mv_ref, nmix_ref, win_ref, gnw_ref, wpool_ref, pscale_ref, wout_ref,
                  nmem_ref, wq_ref, wo_ref, nffn_ref, rw_ref, rb_ref, earlier_ref, cnt0_ref,
                  x2_ref, h_ref, route_ref, rstate_ref, pstate_ref, cnt_ref,
                  s_scr, ext_scr, cnt_scr, *, tl, chain, pos0, cdec):
    t = pl.program_id(1)

    @pl.when((pl.program_id(0) == 0) & (t == 0))
    def _():
        cnt_scr[...] = cnt0_ref[...]

    @pl.when(t == 0)
    def _():
        s_scr[...] = state0_ref[...]
        ext_scr[:, 0:HIST_ROWS, :] = hist0_ref[...]

    for c0 in range(0, x_ref.shape[0], chain):
        _layer_chain(c0, t, x_ref, cos_ref, sin_ref, dmat_ref, qdec_ref, kdec_ref, nmix_ref, win_ref, gnw_ref,
                     wpool_ref, pscale_ref, wout_ref, nmem_ref, wq_ref, wo_ref, nffn_ref, rw_ref, rb_ref,
                     earlier_ref, x2_ref, h_ref, route_ref, pstate_ref, s_scr, ext_scr, mk_ref, mv_ref, cnt_scr,
                     tl=tl, chain=chain, pos0=pos0, cdec=cdec)
    rstate_ref[...] = s_scr[...]
    cnt_ref[...] = cnt_scr[...]


def _layer_chain(c0, t, x_ref, cos_ref, sin_ref, dmat_ref, qdec_ref, kdec_ref, nmix_ref, win_ref, gnw_ref,
                 wpool_ref, pscale_ref, wout_ref, nmem_ref, wq_ref, wo_ref, nffn_ref, rw_ref, rb_ref,
                 earlier_ref, x2_ref, h_ref, route_ref, pstate_ref, s_scr, ext_scr, mk_ref, mv_ref, cnt_scr,
                 *, tl, chain, pos0, cdec):
    rows = chain * tl
    x = x_ref[c0:c0 + chain].reshape(rows, D_MODEL)
    proj = _rms_dot(x, nmix_ref[...], win_ref[...])
    cos = cos_ref[...]
    sin = sin_ref[...]
    pos = (pos0 + t * tl + lax.broadcasted_iota(jnp.int32, (tl, POOL_C), 0)).astype(F32)

    mixes = []
    for sj in range(chain):
        si = c0 + sj
        pj = proj[sj * tl:(sj + 1) * tl]
        outs = []
        for hd in range(RET_HEADS):
            lo = hd * RET_DK
            q = pj[:, lo:lo + RET_DK]
            k = pj[:, RET_QK + lo:RET_QK + lo + RET_DK]
            v = pj[:, 2 * RET_QK + lo:2 * RET_QK + lo + RET_DV]
            g = pj[:, 2 * RET_QK + RET_VW + lo:2 * RET_QK + RET_VW + lo + RET_DV]
            qr = (q * cos + pltpu.roll(q, RET_DK // 2, 1) * sin) * (RET_DK ** -0.5)
            kr = k * cos + pltpu.roll(k, RET_DK // 2, 1) * sin
            vb = v.astype(BF16)
            s = _dot_nt(qr.astype(BF16), kr.astype(BF16)) * dmat_ref[hd]
            o = _dot(s.astype(BF16), vb)
            state = s_scr[si, hd]
            o = o + _dot((qr * qdec_ref[hd]).astype(BF16), state.astype(BF16))
            kd_t = jnp.transpose(kr * kdec_ref[hd]).astype(BF16)
            s_scr[si, hd] = cdec[hd] * state + _dot(kd_t, vb)
            mu = jnp.mean(o, axis=-1, keepdims=True)
            oc = o - mu
            var = jnp.mean(oc * oc, axis=-1, keepdims=True)
            on = oc * lax.rsqrt(var + EPS)
            outs.append(on * gnw_ref[:, lo:lo + RET_DV] * (g * jax.nn.sigmoid(g)))

        pin = pj[:, 2 * RET_QK + 2 * RET_VW:]
        ext_scr[si, HIST_ROWS:HIST_ROWS + tl, :] = pin
        pstate_ref[si] = pin[tl - POOL_HIST:, :]
        for gi, w in enumerate(POOL_WINDOWS):
            lo = gi * POOL_C
            wsum = ext_scr[si, :, lo:lo + POOL_C]
            shift = 1
            while shift < w:
                wsum = wsum + pltpu.roll(wsum, shift, 0)
                shift *= 2
            cnt = jnp.minimum(float(w), pos + 1.0)
            d = wsum[HIST_ROWS:, :] / cnt - pin[:, lo:lo + POOL_C]
            y = _dot(d.astype(BF16), wpool_ref[gi])
            outs.append(y * pscale_ref[:, lo:lo + POOL_C])
        ext_scr[si, 0:HIST_ROWS, :] = ext_scr[si, tl:tl + HIST_ROWS, :]
        mixes.append(jnp.concatenate(outs, axis=-1).astype(BF16))

    x1 = x + _dot(jnp.concatenate(mixes, axis=0), wout_ref[...])

    qm = _rms_dot(x1, nmem_ref[...], wq_ref[...])
    atts = []
    for sj in range(chain):
        si = c0 + sj
        aouts = []
        for hd in range(MEM_HEADS):
            lo = hd * MEM_HD
            qh = qm[sj * tl:(sj + 1) * tl, lo:lo + MEM_HD].astype(BF16)
            s = _dot_nt(qh, mk_ref[si, :, lo:lo + MEM_HD]) * (MEM_HD ** -0.5)
            e = jnp.exp(s - jnp.max(s, axis=-1, keepdims=True))
            p = e / jnp.sum(e, axis=-1, keepdims=True)
            aouts.append(_dot(p.astype(BF16), mv_ref[si, :, lo:lo + MEM_HD]))
        atts.append(jnp.concatenate(aouts, axis=-1).astype(BF16))
    x2 = x1 + _dot(jnp.concatenate(atts, axis=0), wo_ref[...])
    x2_ref[c0:c0 + chain] = x2.reshape(chain, tl, D_MODEL)

    hn = _rms(x2, nffn_ref[...])
    h_ref[c0:c0 + chain] = _pack_bf16(hn).reshape(chain, tl, D_MODEL // 2)
    lane_tile = lambda a: a[:, :rows] if rows <= LANES else jnp.concatenate([a] * (rows // LANES), axis=1)
    logits = _dot_nt(rw_ref[...], hn.astype(BF16)) + lane_tile(rb_ref[...])
    eiota = lax.broadcasted_iota(jnp.int32, (N_EXPERTS, rows), 0)
    neg = jnp.finfo(F32).min
    vals, idxs = [], []
    for _k in range(TOP_K):
        m = jnp.max(logits, axis=0, keepdims=True)
        idx = jnp.min(jnp.where(logits == m, eiota, N_EXPERTS), axis=0, keepdims=True)
        vals.append(m)
        idxs.append(idx)
        logits = jnp.where(eiota == idx, neg, logits)
    exps = [jnp.exp(vk - vals[0]) for vk in vals]
    den = exps[0] + exps[1] + exps[2] + exps[3]

    onehots = [(eiota == idx).astype(F32) for idx in idxs]
    picked = onehots[0] + onehots[1] + onehots[2] + onehots[3]
    before = lane_tile(cnt_scr[...]) + _dot(picked.astype(BF16), earlier_ref[...])
    ranks = [jnp.sum(oh * before, axis=0, keepdims=True) for oh in onehots]
    cnt_scr[...] = cnt_scr[...] + jnp.sum(picked, axis=1, keepdims=True)

    rowi = lax.broadcasted_iota(jnp.int32, (ROUTE_ROWS, rows), 0)
    route = jnp.zeros((ROUTE_ROWS, rows), F32)
    for kk in range(TOP_K):
        route = jnp.where(rowi == kk, exps[kk] / den, route)
        route = jnp.where(rowi == TOP_K + kk, idxs[kk].astype(F32), route)
        route = jnp.where(rowi == 2 * TOP_K + kk, ranks[kk], route)
    route_ref[0, :, c0 * tl:c0 * tl + rows] = route


def _decay_tables(tl):
    hh = np.arange(RET_HEADS, dtype=np.float64)
    log_g = np.log1p(-np.exp2(-5.0 - hh))
    idx = np.arange(tl, dtype=np.float64)
    dist = np.abs(idx[:, None] - idx[None, :])
    visible = (idx[None, :] // CHUNK) <= (idx[:, None] // CHUNK)
    dmat = np.where(visible[None], np.exp(log_g[:, None, None] * dist[None]), 0.0)
    qdec = np.exp(log_g[:, None] * (idx[None, :] + 1.0))
    kdec = np.exp(log_g[:, None] * (tl - 1.0 - idx[None, :]))
    cdec = tuple(float(c) for c in np.exp(log_g * tl).astype(np.float32))
    bcast = lambda a: np.ascontiguousarray(np.broadcast_to(a[:, :, None], (RET_HEADS, tl, RET_DK)))
    return (jnp.asarray(dmat, F32), jnp.asarray(bcast(qdec), F32), jnp.asarray(bcast(kdec), F32), cdec)


def _rotary_tables(pos0, length):
    half = RET_DK // 2
    inv_freq = jnp.power(ROPE_BASE, -jnp.arange(half, dtype=F32) / half)
    ang = (pos0 + jnp.arange(length, dtype=jnp.int32)).astype(F32)[:, None] * inv_freq[None, :]
    cos, sin = jnp.cos(ang), jnp.sin(ang)
    return jnp.concatenate([cos, cos], axis=-1), jnp.concatenate([-sin, sin], axis=-1)


def _layer(x, stream0, b, pos0, tl, ns, chain, state0, hist0, mk, mv, wts, cnt0):
    length = x.shape[1]
    assert b % ns == 0 and stream0 % ns == 0 and ns % chain == 0, (b, stream0, ns, chain)
    assert length % tl == 0 and tl % CHUNK == 0 and tl > POOL_HIST, (length, tl)
    nt = length // tl
    g0 = stream0 // ns
    rows = ns * tl
    cos, sin = _rotary_tables(pos0, length)
    dmat, qdec, kdec, cdec = _decay_tables(tl)
    crows = chain * tl
    earlier = jnp.asarray(np.triu(np.ones((crows, crows), np.float32), 1), BF16)
    kern = functools.partial(_layer_kernel, tl=tl, chain=chain, pos0=pos0, cdec=cdec)
    tok = lambda width: pl.BlockSpec((ns, tl, width), lambda i, j: (i, j, 0))
    per_stream = lambda *shape: pl.BlockSpec((ns,) + shape, lambda i, j: (i,) + (0,) * len(shape))
    per_stream_in = lambda off, *shape: pl.BlockSpec(
        (ns,) + shape, lambda i, j: (i + off,) + (0,) * len(shape), pipeline_mode=pl.Buffered(1))
    in_specs = [
        pl.BlockSpec((ns, tl, D_MODEL), lambda i, j: (i + g0, j, 0)),
        pl.BlockSpec((tl, RET_DK), lambda i, j: (j, 0)),
        pl.BlockSpec((tl, RET_DK), lambda i, j: (j, 0)),
        _const_spec((RET_HEADS, tl, tl)),
        _const_spec((RET_HEADS, tl, RET_DK)),
        _const_spec((RET_HEADS, tl, RET_DK)),
        per_stream_in(0, RET_HEADS, RET_DK, RET_DV),
        per_stream_in(0, HIST_ROWS, POOL_WIDTH),
        per_stream_in(g0, N_MEM, D_MODEL),
        per_stream_in(g0, N_MEM, D_MODEL),
    ] + [_const_spec(w.shape) for w in wts] + [_const_spec((crows, crows)), _const_spec((N_EXPERTS, LANES))]
    out_specs = [
        tok(D_MODEL), tok(D_MODEL // 2),
        pl.BlockSpec((1, ROUTE_ROWS, rows), lambda i, j: (i * nt + j, 0, 0)),
        per_stream(RET_HEADS, RET_DK, RET_DV),
        per_stream(POOL_HIST, POOL_WIDTH),
        pl.BlockSpec((N_EXPERTS, LANES), lambda i, j: (0, 0)),
    ]
    out_shape = [
        jax.ShapeDtypeStruct((b, length, D_MODEL), F32),
        jax.ShapeDtypeStruct((b, length, D_MODEL // 2), jnp.int32),
        jax.ShapeDtypeStruct((b // ns * nt, ROUTE_ROWS, rows), F32),
        jax.ShapeDtypeStruct((b, RET_HEADS, RET_DK, RET_DV), F32),
        jax.ShapeDtypeStruct((b, POOL_HIST, POOL_WIDTH), F32),
        jax.ShapeDtypeStruct((N_EXPERTS, LANES), F32),
    ]
    scratch = [
        pltpu.VMEM((ns, RET_HEADS, RET_DK, RET_DV), F32),
        pltpu.VMEM((ns, HIST_ROWS + tl, POOL_WIDTH), F32),
        pltpu.VMEM((N_EXPERTS, LANES), F32),
    ]
    return pl.pallas_call(
        kern,
        grid=(b // ns, nt),
        in_specs=in_specs,
        out_specs=out_specs,
        out_shape=out_shape,
        scratch_shapes=scratch,
        compiler_params=pltpu.CompilerParams(
            dimension_semantics=("arbitrary", "arbitrary"), vmem_limit_bytes=VMEM_LIMIT),
        name="layer_tl%d" % tl,
    )(x, cos, sin, dmat, qdec, kdec, state0, hist0, mk, mv, *wts, earlier, cnt0)


SPLIT_COLS = 2 * LANES


def _expert_block(x_ref, wgu, bgu_ref, wd, bd_ref, y_ref):
    x_lo, x_hi = _unpack_bf16(x_ref[...])
    xb = jnp.concatenate([x_lo.astype(BF16), x_hi.astype(BF16)], axis=1)
    gu = _dot(xb, wgu[...]) + bgu_ref[0]
    gate = jnp.minimum(gu[:, :D_FF], SWIGLU_LIMIT)
    up = jnp.clip(gu[:, D_FF:], -SWIGLU_LIMIT, SWIGLU_LIMIT)
    act = (up + 1.0) * gate * jax.nn.sigmoid(SWIGLU_ALPHA * gate)
    y_ref[...] = _pack_bf16(_dot(act.astype(BF16), wd[...]) + bd_ref[0])


def _expert_convert_kernel(be_ref, used_ref, x_ref, wgu_ref, bgu_ref, wd_ref, bd_ref, perm_ref,
                           y_ref, wgu_out_ref, wd_out_ref):
    i = pl.program_id(0)
    in_use = i < used_ref[0]

    @pl.when(in_use & ((i == 0) | (be_ref[i] != be_ref[jnp.maximum(i - 1, 0)])))
    def _():
        perm = perm_ref[...]
        for c in range(2 * D_FF // SPLIT_COLS):
            wc = wgu_ref[0, :, c * SPLIT_COLS:(c + 1) * SPLIT_COLS].astype(BF16)
            pc = _dot(wc, perm).astype(BF16)
            wgu_out_ref[0, :, c * LANES:(c + 1) * LANES] = pc[:, :LANES]
            wgu_out_ref[0, :, D_FF + c * LANES:D_FF + (c + 1) * LANES] = pc[:, LANES:]
        wd_out_ref[0] = wd_ref[0].astype(BF16)

    @pl.when(in_use)
    def _():
        _expert_block(x_ref, wgu_out_ref.at[0], bgu_ref, wd_out_ref.at[0], bd_ref, y_ref)


def _expert_ready_kernel(be_ref, used_ref, x_ref, wgu_ref, bgu_ref, wd_ref, bd_ref, y_ref):
    del be_ref

    @pl.when(pl.program_id(0) < used_ref[0])
    def _():
        _expert_block(x_ref, wgu_ref.at[0], bgu_ref, wd_ref.at[0], bd_ref, y_ref)


def _expert_ffn(block_e, n_used, xs, wgu, bgu, wd, bd, n_blocks, block_rows):
    convert = wgu.dtype != BF16
    blk = lambda i, be, used: (jnp.minimum(i, used[0] - 1), 0)
    per_expert = lambda i, be, used: (be[i], 0, 0)
    in_specs = [
        pl.BlockSpec((block_rows, D_MODEL // 2), blk),
        pl.BlockSpec((1, D_MODEL, 2 * D_FF), per_expert),
        pl.BlockSpec((1, 1, 2 * D_FF), per_expert),
        pl.BlockSpec((1, D_FF, D_MODEL), per_expert),
        pl.BlockSpec((1, 1, D_MODEL), per_expert),
    ]
    out_specs = [pl.BlockSpec((block_rows, D_MODEL // 2), blk)]
    out_shape = [jax.ShapeDtypeStruct(xs.shape, jnp.int32)]
    args = [block_e, n_used, xs, wgu, bgu, wd, bd]
    if convert:
        perm = np.zeros((SPLIT_COLS, SPLIT_COLS), np.float32)
        j = np.arange(LANES)
        perm[2 * j, j] = 1.0
        perm[2 * j + 1, LANES + j] = 1.0
        in_specs.append(_const_spec((SPLIT_COLS, SPLIT_COLS)))
        args.append(jnp.asarray(perm, BF16))
        out_specs += [pl.BlockSpec((1, D_MODEL, 2 * D_FF), per_expert), pl.BlockSpec((1, D_FF, D_MODEL), per_expert)]
        out_shape += [jax.ShapeDtypeStruct(wgu.shape, BF16), jax.ShapeDtypeStruct(wd.shape, BF16)]
    grid_spec = pltpu.PrefetchScalarGridSpec(
        num_scalar_prefetch=2, grid=(n_blocks,), in_specs=in_specs, out_specs=out_specs)
    return pl.pallas_call(
        _expert_convert_kernel if convert else _expert_ready_kernel,
        grid_spec=grid_spec,
        out_shape=out_shape,
        compiler_params=pltpu.CompilerParams(
            dimension_semantics=("arbitrary",), vmem_limit_bytes=VMEM_LIMIT),
        name="expert_ffn_convert" if convert else "expert_ffn",
    )(*args)


def _expert_layout(counts, n_blocks, block_rows, every_expert):
    pcounts = (counts + block_rows - 1) // block_rows * block_rows
    if every_expert:
        pcounts = jnp.maximum(pcounts, block_rows)
    pend = jnp.cumsum(pcounts)
    n_used = pend[-1:] // block_rows
    block_start = jnp.minimum(jnp.arange(n_blocks, dtype=jnp.int32), n_used[0] - 1) * block_rows
    block_e = jnp.sum((pend[None, :] <= block_start[:, None]).astype(jnp.int32), axis=1)
    return pend - pcounts, jnp.minimum(block_e, N_EXPERTS - 1), n_used


def _slot_rows(route, pstart, ns, nt, tl):
    n = route.shape[0] * ns * tl
    fields = lambda lo: route[:, lo:lo + TOP_K, :].reshape(-1, nt, TOP_K, ns, tl)
    per_slot = lambda lo: jnp.transpose(fields(lo), (2, 0, 3, 1, 4)).reshape(TOP_K, n)
    e = per_slot(TOP_K).astype(jnp.int32)
    rank = per_slot(2 * TOP_K).astype(jnp.int32)
    hit = e[:, :, None] == jnp.arange(N_EXPERTS, dtype=jnp.int32)[None, None, :]
    rows = rank + jnp.sum(jnp.where(hit, pstart[None, None, :], 0), axis=-1)
    gates = jnp.transpose(fields(0), (0, 3, 1, 4, 2)).reshape(n, TOP_K)
    return rows, gates


SC_WINDOW = 128
SC_COLS = 256


def _sc_mesh():
    return plsc.VectorSubcoreMesh(core_axis_name="c", subcore_axis_name="s")


def _dispatch(sources, m_pad):
    width = sources[0][0].shape[1]
    n_src = len(sources)
    assert width % SC_COLS == 0 and all(h.shape[0] % SC_WINDOW == 0 for h, _ in sources)

    @functools.partial(pl.kernel, mesh=_sc_mesh(),
                       out_type=jax.ShapeDtypeStruct((m_pad, width), sources[0][0].dtype), scratch_types=[])
    def k(*refs):
        xs_hbm = refs[2 * n_src]

        def body(x_vmem, i_vmem):
            j = pl.program_id(1)
            for kk in range(TOP_K):
                pltpu.sync_copy(x_vmem, xs_hbm.at[i_vmem.at[kk], pl.ds(j * SC_COLS, SC_COLS)])

        for si in range(n_src):
            src, rows = refs[2 * si], refs[2 * si + 1]
            pltpu.emit_pipeline(
                body,
                grid=(src.shape[0] // SC_WINDOW, width // SC_COLS),
                in_specs=[pl.BlockSpec((SC_WINDOW, SC_COLS), lambda i, j: (i, j)),
                          pl.BlockSpec((TOP_K, SC_WINDOW), lambda i, j: (0, i))],
                out_specs=[],
                core_axis_name=("c", "s"),
                dimension_semantics=(pltpu.PARALLEL, pltpu.ARBITRARY),
            )(src, rows)

    return k(*[a for pair in sources for a in pair])


def _collect(yb, rows):
    width = yb.shape[1]
    assert width % SC_COLS == 0 and rows.shape[1] % SC_WINDOW == 0

    @functools.partial(pl.kernel, mesh=_sc_mesh(),
                       out_type=jax.ShapeDtypeStruct((rows.shape[1], width), yb.dtype), scratch_types=[])
    def k(yb_hbm, r_hbm, o_hbm):
        def body(i_vmem, o_vmem):
            j = pl.program_id(1)
            pltpu.sync_copy(yb_hbm.at[i_vmem.at[0], pl.ds(j * SC_COLS, SC_COLS)], o_vmem)

        pltpu.emit_pipeline(
            body,
            grid=(rows.shape[1] // SC_WINDOW, width // SC_COLS),
            in_specs=[pl.BlockSpec((1, SC_WINDOW), lambda i, j: (0, i))],
            out_specs=[pl.BlockSpec((SC_WINDOW, SC_COLS), lambda i, j: (i, j))],
            core_axis_name=("c", "s"),
            dimension_semantics=(pltpu.PARALLEL, pltpu.ARBITRARY),
        )(r_hbm, o_hbm)

    return k(yb, rows)


def _combine_kernel(x_ref, yg_ref, route_ref, fw_ref, *rest):
    y_ref = rest[-1]
    half = D_MODEL // 2
    acc_lo = x_ref[:, :half]
    acc_hi = x_ref[:, half:]
    route = route_ref[...]
    for kk in range(TOP_K):
        y_lo, y_hi = _unpack_bf16(yg_ref[kk])
        gate = route[:, kk:kk + 1]
        acc_lo = acc_lo + y_lo * gate
        acc_hi = acc_hi + y_hi * gate
    ms = (jnp.sum(acc_lo * acc_lo, axis=-1, keepdims=True)
          + jnp.sum(acc_hi * acc_hi, axis=-1, keepdims=True)) * (1.0 / D_MODEL)
    scale = lax.rsqrt(ms + EPS)
    y_ref[:, :half] = acc_lo * scale * fw_ref[:, :half]
    y_ref[:, half:] = acc_hi * scale * fw_ref[:, half:]


def _combine(x2, yg, gates, final_w, y_prev, first_row, out_rows, out_first_row, after):
    assert yg.shape[1] % COMBINE_ROWS == 0 and first_row % COMBINE_ROWS == 0 and out_first_row % COMBINE_ROWS == 0
    blk_in = first_row // COMBINE_ROWS
    blk_out = out_first_row // COMBINE_ROWS
    in_specs = [
        pl.BlockSpec((COMBINE_ROWS, D_MODEL), lambda i: (blk_in + i, 0)),
        pl.BlockSpec((TOP_K, COMBINE_ROWS, D_MODEL // 2), lambda i: (0, i, 0)),
        pl.BlockSpec((COMBINE_ROWS, TOP_K), lambda i: (blk_in + i, 0)),
        _const_spec((1, D_MODEL)),
    ]
    args = [x2, yg, gates, final_w]
    aliases = {}
    if y_prev is not None:
        in_specs.append(pl.BlockSpec(memory_space=pl.ANY))
        args.append(y_prev)
        aliases = {len(args) - 1: 0}
    if after is not None:
        in_specs.append(pl.BlockSpec(memory_space=pl.ANY))
        args.append(after)
    return pl.pallas_call(
        _combine_kernel,
        grid=(yg.shape[1] // COMBINE_ROWS,),
        in_specs=in_specs,
        out_specs=pl.BlockSpec((COMBINE_ROWS, D_MODEL), lambda i: (blk_out + i, 0)),
        out_shape=jax.ShapeDtypeStruct((out_rows, D_MODEL), F32),
        input_output_aliases=aliases,
        compiler_params=pltpu.CompilerParams(
            dimension_semantics=("arbitrary",), vmem_limit_bytes=VMEM_LIMIT),
        name="combine_%d_%d" % (out_rows, out_first_row),
    )(*args)


def _collect_combine(yb, x2, rows, gates, final_w, chunks, y, out_rows, out_first_row, after):
    nc = x2.shape[0] // chunks
    for c in range(chunks):
        yg = _collect(yb, rows[:, c * nc:(c + 1) * nc].reshape(1, TOP_K * nc))
        y = _combine(x2, yg.reshape(TOP_K, nc, D_MODEL // 2), gates, final_w, y, c * nc,
                     out_rows, out_first_row + c * nc, after)
    return y


def kernel(x_prompt, x_sample, cache_mem_k, cache_mem_v, state_ret, state_pool, mem_prompt,
           norm_mix_w, w_in, ret_gn_w, w_pool, pool_scale, w_out, norm_mem_w, mem_norm_w,
           w_q_mem, w_kv_mem, w_o_mem, norm_ffn_w, router_w, router_b, w_gate_up, b_gate_up,
           w_down, b_down, final_norm_w):
    assert norm_mix_w.shape[0] == 1, "one layer"
    b, seq, _ = x_prompt.shape
    db, dseq, _ = x_sample.shape
    row = lambda a: a.reshape(1, -1)

    mk_p, mv_p, mkb_p, mvb_p = _mem_kv(mem_prompt, row(mem_norm_w[0]), w_kv_mem[0].astype(BF16))
    mkb_s, mvb_s = _kv_flat(cache_mem_k[0], cache_mem_v[0])

    wts = (row(norm_mix_w[0]), w_in[0].astype(BF16), row(ret_gn_w[0]), w_pool[0].astype(BF16),
           row(pool_scale[0]), w_out[0].astype(BF16), row(norm_mem_w[0]), w_q_mem[0].astype(BF16),
           w_o_mem[0].astype(BF16), row(norm_ffn_w[0]), router_w[0].T.astype(BF16),
           jnp.broadcast_to(router_b[0][:, None], (N_EXPERTS, LANES)))

    assert sum(GROUP_STREAMS) == b
    n_s = db * dseq
    half = D_MODEL // 2
    final_w = row(final_norm_w)
    bgu = jnp.concatenate([b_gate_up[0][:, 0::2], b_gate_up[0][:, 1::2]], axis=-1).reshape(N_EXPERTS, 1, 2 * D_FF)
    bd = b_down[0].reshape(N_EXPERTS, 1, D_MODEL)
    no_counts = jnp.zeros((N_EXPERTS, LANES), F32)
    hist_s = jnp.concatenate([jnp.zeros((db, 1, POOL_WIDTH), F32), state_pool[0]], axis=1)

    done, rets, pools, stream0 = [], [], [], 0
    for g, per in enumerate(GROUP_STREAMS):
        n_g = per * seq
        x2_g, h_g, route_g, ret_g, pool_g, cnt = _layer(
            x_prompt, stream0, per, 0, PROMPT_TILE, PROMPT_STREAMS, PROMPT_CHAIN,
            jnp.zeros((per, RET_HEADS, RET_DK, RET_DV), F32), jnp.zeros((per, HIST_ROWS, POOL_WIDTH), F32),
            mkb_p, mvb_p, wts, no_counts)
        rets.append(ret_g)
        pools.append(pool_g)
        with_sample = g == len(GROUP_STREAMS) - 1
        n_slots = n_g * TOP_K
        if with_sample:
            x2_s, h_s, route_s, ret_s, pool_s, cnt = _layer(
                x_sample, 0, db, PAST_LEN, dseq, SAMPLE_STREAMS, SAMPLE_STREAMS, state_ret[0], hist_s,
                mkb_s, mvb_s, wts, cnt)
            n_slots += n_s * TOP_K
        first = g == 0
        block_rows = CONVERT_ROWS if first else EXPERT_ROWS
        n_blocks = -(-(n_slots + N_EXPERTS * (block_rows - 1)) // block_rows) + (N_EXPERTS if first else 0)
        pstart, block_e, n_used = _expert_layout(cnt[:, 0].astype(jnp.int32), n_blocks, block_rows, first)
        rows_g, gates_g = _slot_rows(route_g, pstart, PROMPT_STREAMS, seq // PROMPT_TILE, PROMPT_TILE)
        sources = [(h_g.reshape(n_g, half), rows_g)]
        if with_sample:
            rows_s, gates_s = _slot_rows(route_s, pstart, SAMPLE_STREAMS, 1, dseq)
            sources.append((h_s.reshape(n_s, half), rows_s))
        xs = _dispatch(sources, n_blocks * block_rows)
        if first:
            yb, wgu_bf, wd_bf = _expert_ffn(block_e, n_used, xs, w_gate_up[0], bgu, w_down[0], bd, n_blocks,
                                            block_rows)
        else:
            (yb,) = _expert_ffn(block_e, n_used, xs, wgu_bf, bgu, wd_bf, bd, n_blocks, block_rows)
        done.append((yb, x2_g.reshape(n_g, D_MODEL), rows_g, gates_g, n_g // COMBINE_TOKENS, stream0 * seq))
        stream0 += per
    y_s = _collect_combine(yb, x2_s.reshape(n_s, D_MODEL), rows_s, gates_s, final_w, 1, None, n_s, 0, None)
    y_p = None
    for yb_g, x2_g, rows_g, gates_g, chunks, row0 in done:
        y_p = _collect_combine(yb_g, x2_g, rows_g, gates_g, final_w, chunks, y_p, b * seq, row0,
                               None if yb_g is yb else yb)
    y_p = y_p.reshape(b, seq, D_MODEL)
    y_s = y_s.reshape(db, dseq, D_MODEL)
    ret_p = jnp.concatenate(rets, axis=0)
    pool_p = jnp.concatenate(pools, axis=0)
    return (y_p, y_s, mk_p[None], mv_p[None], ret_p[None], pool_p[None], ret_s[None], pool_s[None])
```

```python
import functools

import numpy as np
import jax
import jax.numpy as jnp
from jax import lax
from jax.experimental import pallas as pl
from jax.experimental.pallas import tpu as pltpu
from jax.experimental.pallas import tpu_sc as plsc

D_MODEL = 1024
CHUNK = 64
PAST_LEN = 4096
RET_HEADS = 4
RET_DK = 128
RET_DV = 128
RET_QK = RET_HEADS * RET_DK
RET_VW = RET_HEADS * RET_DV
ROPE_BASE = 10000.0
POOL_WINDOWS = (2, 4, 8, 16)
POOL_GROUPS = 4
POOL_WIDTH = D_MODEL // 2
POOL_C = POOL_WIDTH // POOL_GROUPS
POOL_HIST = max(POOL_WINDOWS) - 1
HIST_ROWS = POOL_HIST + 1
IN_WIDTH = 2 * RET_QK + 2 * RET_VW + POOL_WIDTH
N_MEM = 256
MEM_HEADS = 4
MEM_HD = D_MODEL // MEM_HEADS
N_EXPERTS = 32
TOP_K = 4
D_FF = D_MODEL
SWIGLU_LIMIT = 7.0
SWIGLU_ALPHA = 1.702
EPS = 1e-5

LANES = 128
ROUTE_ROWS = 16
PROMPT_TILE = 256
PROMPT_STREAMS = 4
PROMPT_CHAIN = 2
SAMPLE_STREAMS = 8
EXPERT_ROWS = 512
GROUP_STREAMS = (16, 16)
KV_STREAMS = 2
COMBINE_ROWS = 1024
COMBINE_TOKENS = 16384
VMEM_LIMIT = 56 * 1024 * 1024

BF16 = jnp.bfloat16
F32 = jnp.float32


def _rms(x, w):
    return x * lax.rsqrt(jnp.mean(x * x, axis=-1, keepdims=True) + EPS) * w


def _dot(a, b):
    return jnp.dot(a, b, preferred_element_type=F32)


def _rms_dot(x, w_norm, w_mat):
    inv = lax.rsqrt(jnp.mean(x * x, axis=-1, keepdims=True) + EPS)
    return _dot((x * w_norm).astype(BF16), w_mat) * inv


def _dot_nt(a, b):
    return lax.dot_general(a, b, (((1,), (1,)), ((), ())), preferred_element_type=F32)


BF16_BITS = 16
HIGH_HALF = -(1 << BF16_BITS)


def _pack_bf16(x):
    bits = lax.bitcast_convert_type(x.astype(BF16).astype(F32), jnp.int32)
    w = x.shape[1] // 2
    return lax.shift_right_logical(bits[:, :w], BF16_BITS) | (bits[:, w:] & HIGH_HALF)


def _unpack_bf16(p):
    lo = lax.bitcast_convert_type(lax.shift_left(p, BF16_BITS), F32)
    hi = lax.bitcast_convert_type(p & HIGH_HALF, F32)
    return lo, hi


def _const_spec(shape):
    nd = len(shape)
    return pl.BlockSpec(shape, lambda *_: (0,) * nd, pipeline_mode=pl.Buffered(1))


def _mem_kv_kernel(mem_ref, nw_ref, w_ref, k_ref, v_ref, kb_ref, vb_ref):
    ns = mem_ref.shape[0]
    xn = _rms(mem_ref[...].reshape(ns * N_MEM, D_MODEL), nw_ref[...]).astype(BF16)
    kv = _dot(xn, w_ref[...])
    for hd in range(MEM_HEADS):
        lo = hd * MEM_HD
        k_ref[:, :, hd, :] = kv[:, lo:lo + MEM_HD].reshape(ns, N_MEM, MEM_HD)
        v_ref[:, :, hd, :] = kv[:, D_MODEL + lo:D_MODEL + lo + MEM_HD].reshape(ns, N_MEM, MEM_HD)
    kb_ref[...] = kv[:, :D_MODEL].astype(BF16).reshape(ns, N_MEM, D_MODEL)
    vb_ref[...] = kv[:, D_MODEL:].astype(BF16).reshape(ns, N_MEM, D_MODEL)


def _mem_kv(mem, mem_norm_w, w_kv_bf):
    b = mem.shape[0]
    out_spec = pl.BlockSpec((KV_STREAMS, N_MEM, MEM_HEADS, MEM_HD), lambda i: (i, 0, 0, 0))
    flat_spec = pl.BlockSpec((KV_STREAMS, N_MEM, D_MODEL), lambda i: (i, 0, 0))
    return pl.pallas_call(
        _mem_kv_kernel,
        grid=(b // KV_STREAMS,),
        in_specs=[
            pl.BlockSpec((KV_STREAMS, N_MEM, D_MODEL), lambda i: (i, 0, 0)),
            _const_spec((1, D_MODEL)),
            _const_spec((D_MODEL, 2 * D_MODEL)),
        ],
        out_specs=[out_spec, out_spec, flat_spec, flat_spec],
        out_shape=[jax.ShapeDtypeStruct((b, N_MEM, MEM_HEADS, MEM_HD), F32)] * 2
        + [jax.ShapeDtypeStruct((b, N_MEM, D_MODEL), BF16)] * 2,
        compiler_params=pltpu.CompilerParams(
            dimension_semantics=("arbitrary",), vmem_limit_bytes=VMEM_LIMIT),
        name="mem_kv",
    )(mem, mem_norm_w, w_kv_bf)


def _kv_flat_kernel(k_ref, v_ref, kb_ref, vb_ref):
    for hd in range(MEM_HEADS):
        kb_ref[:, :, hd * MEM_HD:(hd + 1) * MEM_HD] = k_ref[:, :, hd, :].astype(BF16)
        vb_ref[:, :, hd * MEM_HD:(hd + 1) * MEM_HD] = v_ref[:, :, hd, :].astype(BF16)


def _kv_flat(mem_k, mem_v):
    b = mem_k.shape[0]
    in_spec = pl.BlockSpec((KV_STREAMS, N_MEM, MEM_HEADS, MEM_HD), lambda i: (i, 0, 0, 0))
    out_spec = pl.BlockSpec((KV_STREAMS, N_MEM, D_MODEL), lambda i: (i, 0, 0))
    return pl.pallas_call(
        _kv_flat_kernel,
        grid=(b // KV_STREAMS,),
        in_specs=[in_spec, in_spec],
        out_specs=[out_spec, out_spec],
        out_shape=[jax.ShapeDtypeStruct((b, N_MEM, D_MODEL), BF16)] * 2,
        compiler_params=pltpu.CompilerParams(
            dimension_semantics=("arbitrary",), vmem_limit_bytes=VMEM_LIMIT),
        name="kv_flat",
    )(mem_k, mem_v)


def _layer_kernel(x_ref, cos_ref, sin_ref, dmat_ref, qdec_ref, kdec_ref, state0_ref, hist0_ref,
                  mk_ref, mv_ref, nmix_ref, win_ref, gnw_ref, wpool_ref, pscale_ref, wout_ref,
                  nmem_ref, wq_ref, wo_ref, nffn_ref, rw_ref, rb_ref, earlier_ref, cnt0_ref,
                  x2_ref, h_ref, route_ref, rstate_ref, pstate_ref, cnt_ref,
                  s_scr, ext_scr, cnt_scr, *, tl, chain, pos0, cdec):
    t = pl.program_id(1)

    @pl.when((pl.program_id(0) == 0) & (t == 0))
    def _():
        cnt_scr[...] = cnt0_ref[...]

    @pl.when(t == 0)
    def _():
        s_scr[...] = state0_ref[...]
        ext_scr[:, 0:HIST_ROWS, :] = hist0_ref[...]

    for c0 in range(0, x_ref.shape[0], chain):
        _layer_chain(c0, t, x_ref, cos_ref, sin_ref, dmat_ref, qdec_ref, kdec_ref, nmix_ref, win_ref, gnw_ref,
                     wpool_ref, pscale_ref, wout_ref, nmem_ref, wq_ref, wo_ref, nffn_ref, rw_ref, rb_ref,
                     earlier_ref, x2_ref, h_ref, route_ref, pstate_ref, s_scr, ext_scr, mk_ref, mv_ref, cnt_scr,
                     tl=tl, chain=chain, pos0=pos0, cdec=cdec)
    rstate_ref[...] = s_scr[...]
    cnt_ref[...] = cnt_scr[...]


def _layer_chain(c0, t, x_ref, cos_ref, sin_ref, dmat_ref, qdec_ref, kdec_ref, nmix_ref, win_ref, gnw_ref,
                 wpool_ref, pscale_ref, wout_ref, nmem_ref, wq_ref, wo_ref, nffn_ref, rw_ref, rb_ref,
                 earlier_ref, x2_ref, h_ref, route_ref, pstate_ref, s_scr, ext_scr, mk_ref, mv_ref, cnt_scr,
                 *, tl, chain, pos0, cdec):
    rows = chain * tl
    x = x_ref[c0:c0 + chain].reshape(rows, D_MODEL)
    proj = _rms_dot(x, nmix_ref[...], win_ref[...])
    cos = cos_ref[...]
    sin = sin_ref[...]
    pos = (pos0 + t * tl + lax.broadcasted_iota(jnp.int32, (tl, POOL_C), 0)).astype(F32)

    mixes = []
    for sj in range(chain):
        si = c0 + sj
        pj = proj[sj * tl:(sj + 1) * tl]
        outs = []
        for hd in range(RET_HEADS):
            lo = hd * RET_DK
            q = pj[:, lo:lo + RET_DK]
            k = pj[:, RET_QK + lo:RET_QK + lo + RET_DK]
            v = pj[:, 2 * RET_QK + lo:2 * RET_QK + lo + RET_DV]
            g = pj[:, 2 * RET_QK + RET_VW + lo:2 * RET_QK + RET_VW + lo + RET_DV]
            qr = (q * cos + pltpu.roll(q, RET_DK // 2, 1) * sin) * (RET_DK ** -0.5)
            kr = k * cos + pltpu.roll(k, RET_DK // 2, 1) * sin
            vb = v.astype(BF16)
            s = _dot_nt(qr.astype(BF16), kr.astype(BF16)) * dmat_ref[hd]
            o = _dot(s.astype(BF16), vb)
            state = s_scr[si, hd]
            o = o + _dot((qr * qdec_ref[hd]).astype(BF16), state.astype(BF16))
            kd_t = jnp.transpose(kr * kdec_ref[hd]).astype(BF16)
            s_scr[si, hd] = cdec[hd] * state + _dot(kd_t, vb)
            mu = jnp.mean(o, axis=-1, keepdims=True)
            oc = o - mu
            var = jnp.mean(oc * oc, axis=-1, keepdims=True)
            on = oc * lax.rsqrt(var + EPS)
            outs.append(on * gnw_ref[:, lo:lo + RET_DV] * (g * jax.nn.sigmoid(g)))

        pin = pj[:, 2 * RET_QK + 2 * RET_VW:]
        ext_scr[si, HIST_ROWS:HIST_ROWS + tl, :] = pin
        pstate_ref[si] = pin[tl - POOL_HIST:, :]
        for gi, w in enumerate(POOL_WINDOWS):
            lo = gi * POOL_C
            wsum = ext_scr[si, :, lo:lo + POOL_C]
            shift = 1
            while shift < w:
                wsum = wsum + pltpu.roll(wsum, shift, 0)
                shift *= 2
            cnt = jnp.minimum(float(w), pos + 1.0)
            d = wsum[HIST_ROWS:, :] / cnt - pin[:, lo:lo + POOL_C]
            y = _dot(d.astype(BF16), wpool_ref[gi])
            outs.append(y * pscale_ref[:, lo:lo + POOL_C])
        ext_scr[si, 0:HIST_ROWS, :] = ext_scr[si, tl:tl + HIST_ROWS, :]
        mixes.append(jnp.concatenate(outs, axis=-1).astype(BF16))

    x1 = x + _dot(jnp.concatenate(mixes, axis=0), wout_ref[...])

    qm = _rms_dot(x1, nmem_ref[...], wq_ref[...])
    atts = []
    for sj in range(chain):
        si = c0 + sj
        aouts = []
        for hd in range(MEM_HEADS):
            lo = hd * MEM_HD
            qh = qm[sj * tl:(sj + 1) * tl, lo:lo + MEM_HD].astype(BF16)
            s = _dot_nt(qh, mk_ref[si, :, lo:lo + MEM_HD]) * (MEM_HD ** -0.5)
            e = jnp.exp(s - jnp.max(s, axis=-1, keepdims=True))
            p = e / jnp.sum(e, axis=-1, keepdims=True)
            aouts.append(_dot(p.astype(BF16), mv_ref[si, :, lo:lo + MEM_HD]))
        atts.append(jnp.concatenate(aouts, axis=-1).astype(BF16))
    x2 = x1 + _dot(jnp.concatenate(atts, axis=0), wo_ref[...])
    x2_ref[c0:c0 + chain] = x2.reshape(chain, tl, D_MODEL)

    hn = _rms(x2, nffn_ref[...])
    h_ref[c0:c0 + chain] = _pack_bf16(hn).reshape(chain, tl, D_MODEL // 2)
    lane_tile = lambda a: a[:, :rows] if rows <= LANES else jnp.concatenate([a] * (rows // LANES), axis=1)
    logits = _dot_nt(rw_ref[...], hn.astype(BF16)) + lane_tile(rb_ref[...])
    eiota = lax.broadcasted_iota(jnp.int32, (N_EXPERTS, rows), 0)
    neg = jnp.finfo(F32).min
    vals, idxs = [], []
    for _k in range(TOP_K):
        m = jnp.max(logits, axis=0, keepdims=True)
        idx = jnp.min(jnp.where(logits == m, eiota, N_EXPERTS), axis=0, keepdims=True)
        vals.append(m)
        idxs.append(idx)
        logits = jnp.where(eiota == idx, neg, logits)
    exps = [jnp.exp(vk - vals[0]) for vk in vals]
    den = exps[0] + exps[1] + exps[2] + exps[3]

    onehots = [(eiota == idx).astype(F32) for idx in idxs]
    picked = onehots[0] + onehots[1] + onehots[2] + onehots[3]
    before = lane_tile(cnt_scr[...]) + _dot(picked.astype(BF16), earlier_ref[...])
    ranks = [jnp.sum(oh * before, axis=0, keepdims=True) for oh in onehots]
    cnt_scr[...] = cnt_scr[...] + jnp.sum(picked, axis=1, keepdims=True)

    rowi = lax.broadcasted_iota(jnp.int32, (ROUTE_ROWS, rows), 0)
    route = jnp.zeros((ROUTE_ROWS, rows), F32)
    for kk in range(TOP_K):
        route = jnp.where(rowi == kk, exps[kk] / den, route)
        route = jnp.where(rowi == TOP_K + kk, idxs[kk].astype(F32), route)
        route = jnp.where(rowi == 2 * TOP_K + kk, ranks[kk], route)
    route_ref[0, :, c0 * tl:c0 * tl + rows] = route


def _decay_tables(tl):
    hh = np.arange(RET_HEADS, dtype=np.float64)
    log_g = np.log1p(-np.exp2(-5.0 - hh))
    idx = np.arange(tl, dtype=np.float64)
    dist = np.abs(idx[:, None] - idx[None, :])
    visible = (idx[None, :] // CHUNK) <= (idx[:, None] // CHUNK)
    dmat = np.where(visible[None], np.exp(log_g[:, None, None] * dist[None]), 0.0)
    qdec = np.exp(log_g[:, None] * (idx[None, :] + 1.0))
    kdec = np.exp(log_g[:, None] * (tl - 1.0 - idx[None, :]))
    cdec = tuple(float(c) for c in np.exp(log_g * tl).astype(np.float32))
    bcast = lambda a: np.ascontiguousarray(np.broadcast_to(a[:, :, None], (RET_HEADS, tl, RET_DK)))
    return (jnp.asarray(dmat, F32), jnp.asarray(bcast(qdec), F32), jnp.asarray(bcast(kdec), F32), cdec)


def _rotary_tables(pos0, length):
    half = RET_DK // 2
    inv_freq = jnp.power(ROPE_BASE, -jnp.arange(half, dtype=F32) / half)
    ang = (pos0 + jnp.arange(length, dtype=jnp.int32)).astype(F32)[:, None] * inv_freq[None, :]
    cos, sin = jnp.cos(ang), jnp.sin(ang)
    return jnp.concatenate([cos, cos], axis=-1), jnp.concatenate([-sin, sin], axis=-1)


def _layer(x, stream0, b, pos0, tl, ns, chain, state0, hist0, mk, mv, wts, cnt0):
    length = x.shape[1]
    assert b % ns == 0 and stream0 % ns == 0 and ns % chain == 0, (b, stream0, ns, chain)
    assert length % tl == 0 and tl % CHUNK == 0 and tl > POOL_HIST, (length, tl)
    nt = length // tl
    g0 = stream0 // ns
    rows = ns * tl
    cos, sin = _rotary_tables(pos0, length)
    dmat, qdec, kdec, cdec = _decay_tables(tl)
    crows = chain * tl
    earlier = jnp.asarray(np.triu(np.ones((crows, crows), np.float32), 1), BF16)
    kern = functools.partial(_layer_kernel, tl=tl, chain=chain, pos0=pos0, cdec=cdec)
    tok = lambda width: pl.BlockSpec((ns, tl, width), lambda i, j: (i, j, 0))
    per_stream = lambda *shape: pl.BlockSpec((ns,) + shape, lambda i, j: (i,) + (0,) * len(shape))
    per_stream_in = lambda off, *shape: pl.BlockSpec(
        (ns,) + shape, lambda i, j: (i + off,) + (0,) * len(shape), pipeline_mode=pl.Buffered(1))
    in_specs = [
        pl.BlockSpec((ns, tl, D_MODEL), lambda i, j: (i + g0, j, 0)),
        pl.BlockSpec((tl, RET_DK), lambda i, j: (j, 0)),
        pl.BlockSpec((tl, RET_DK), lambda i, j: (j, 0)),
        _const_spec((RET_HEADS, tl, tl)),
        _const_spec((RET_HEADS, tl, RET_DK)),
        _const_spec((RET_HEADS, tl, RET_DK)),
        per_stream_in(0, RET_HEADS, RET_DK, RET_DV),
        per_stream_in(0, HIST_ROWS, POOL_WIDTH),
        per_stream_in(g0, N_MEM, D_MODEL),
        per_stream_in(g0, N_MEM, D_MODEL),
    ] + [_const_spec(w.shape) for w in wts] + [_const_spec((crows, crows)), _const_spec((N_EXPERTS, LANES))]
    out_specs = [
        tok(D_MODEL), tok(D_MODEL // 2),
        pl.BlockSpec((1, ROUTE_ROWS, rows), lambda i, j: (i * nt + j, 0, 0)),
        per_stream(RET_HEADS, RET_DK, RET_DV),
        per_stream(POOL_HIST, POOL_WIDTH),
        pl.BlockSpec((N_EXPERTS, LANES), lambda i, j: (0, 0)),
    ]
    out_shape = [
        jax.ShapeDtypeStruct((b, length, D_MODEL), F32),
        jax.ShapeDtypeStruct((b, length, D_MODEL // 2), jnp.int32),
        jax.ShapeDtypeStruct((b // ns * nt, ROUTE_ROWS, rows), F32),
        jax.ShapeDtypeStruct((b, RET_HEADS, RET_DK, RET_DV), F32),
        jax.ShapeDtypeStruct((b, POOL_HIST, POOL_WIDTH), F32),
        jax.ShapeDtypeStruct((N_EXPERTS, LANES), F32),
    ]
    scratch = [
        pltpu.VMEM((ns, RET_HEADS, RET_DK, RET_DV), F32),
        pltpu.VMEM((ns, HIST_ROWS + tl, POOL_WIDTH), F32),
        pltpu.VMEM((N_EXPERTS, LANES), F32),
    ]
    return pl.pallas_call(
        kern,
        grid=(b // ns, nt),
        in_specs=in_specs,
        out_specs=out_specs,
        out_shape=out_shape,
        scratch_shapes=scratch,
        compiler_params=pltpu.CompilerParams(
            dimension_semantics=("arbitrary", "arbitrary"), vmem_limit_bytes=VMEM_LIMIT),
        name="layer_tl%d" % tl,
    )(x, cos, sin, dmat, qdec, kdec, state0, hist0, mk, mv, *wts, earlier, cnt0)


SPLIT_COLS = 2 * LANES


def _expert_block(x_ref, wgu, bgu_ref, wd, bd_ref, y_ref):
    x_lo, x_hi = _unpack_bf16(x_ref[...])
    xb = jnp.concatenate([x_lo.astype(BF16), x_hi.astype(BF16)], axis=1)
    gu = _dot(xb, wgu[...]) + bgu_ref[0]
    gate = jnp.minimum(gu[:, :D_FF], SWIGLU_LIMIT)
    up = jnp.clip(gu[:, D_FF:], -SWIGLU_LIMIT, SWIGLU_LIMIT)
    act = (up + 1.0) * gate * jax.nn.sigmoid(SWIGLU_ALPHA * gate)
    y_ref[...] = _pack_bf16(_dot(act.astype(BF16), wd[...]) + bd_ref[0])


def _expert_convert_kernel(be_ref, used_ref, x_ref, wgu_ref, bgu_ref, wd_ref, bd_ref, perm_ref,
                           y_ref, wgu_out_ref, wd_out_ref):
    i = pl.program_id(0)
    in_use = i < used_ref[0]

    @pl.when(in_use & ((i == 0) | (be_ref[i] != be_ref[jnp.maximum(i - 1, 0)])))
    def _():
        perm = perm_ref[...]
        for c in range(2 * D_FF // SPLIT_COLS):
            wc = wgu_ref[0, :, c * SPLIT_COLS:(c + 1) * SPLIT_COLS].astype(BF16)
            pc = _dot(wc, perm).astype(BF16)
            wgu_out_ref[0, :, c * LANES:(c + 1) * LANES] = pc[:, :LANES]
            wgu_out_ref[0, :, D_FF + c * LANES:D_FF + (c + 1) * LANES] = pc[:, LANES:]
        wd_out_ref[0] = wd_ref[0].astype(BF16)

    @pl.when(in_use)
    def _():
        _expert_block(x_ref, wgu_out_ref.at[0], bgu_ref, wd_out_ref.at[0], bd_ref, y_ref)


def _expert_ready_kernel(be_ref, used_ref, x_ref, wgu_ref, bgu_ref, wd_ref, bd_ref, y_ref):
    del be_ref

    @pl.when(pl.program_id(0) < used_ref[0])
    def _():
        _expert_block(x_ref, wgu_ref.at[0], bgu_ref, wd_ref.at[0], bd_ref, y_ref)


def _expert_ffn(block_e, n_used, xs, wgu, bgu, wd, bd, n_blocks):
    convert = wgu.dtype != BF16
    blk = lambda i, be, used: (jnp.minimum(i, used[0] - 1), 0)
    per_expert = lambda i, be, used: (be[i], 0, 0)
    in_specs = [
        pl.BlockSpec((EXPERT_ROWS, D_MODEL // 2), blk),
        pl.BlockSpec((1, D_MODEL, 2 * D_FF), per_expert),
        pl.BlockSpec((1, 1, 2 * D_FF), per_expert),
        pl.BlockSpec((1, D_FF, D_MODEL), per_expert),
        pl.BlockSpec((1, 1, D_MODEL), per_expert),
    ]
    out_specs = [pl.BlockSpec((EXPERT_ROWS, D_MODEL // 2), blk)]
    out_shape = [jax.ShapeDtypeStruct(xs.shape, jnp.int32)]
    args = [block_e, n_used, xs, wgu, bgu, wd, bd]
    if convert:
        perm = np.zeros((SPLIT_COLS, SPLIT_COLS), np.float32)
        j = np.arange(LANES)
        perm[2 * j, j] = 1.0
        perm[2 * j + 1, LANES + j] = 1.0
        in_specs.append(_const_spec((SPLIT_COLS, SPLIT_COLS)))
        args.append(jnp.asarray(perm, BF16))
        out_specs += [pl.BlockSpec((1, D_MODEL, 2 * D_FF), per_expert), pl.BlockSpec((1, D_FF, D_MODEL), per_expert)]
        out_shape += [jax.ShapeDtypeStruct(wgu.shape, BF16), jax.ShapeDtypeStruct(wd.shape, BF16)]
    grid_spec = pltpu.PrefetchScalarGridSpec(
        num_scalar_prefetch=2, grid=(n_blocks,), in_specs=in_specs, out_specs=out_specs)
    return pl.pallas_call(
        _expert_convert_kernel if convert else _expert_ready_kernel,
        grid_spec=grid_spec,
        out_shape=out_shape,
        compiler_params=pltpu.CompilerParams(
            dimension_semantics=("arbitrary",), vmem_limit_bytes=VMEM_LIMIT),
        name="expert_ffn_convert" if convert else "expert_ffn",
    )(*args)


def _expert_layout(counts, n_blocks, every_expert):
    pcounts = (counts + EXPERT_ROWS - 1) // EXPERT_ROWS * EXPERT_ROWS
    if every_expert:
        pcounts = jnp.maximum(pcounts, EXPERT_ROWS)
    pend = jnp.cumsum(pcounts)
    n_used = pend[-1:] // EXPERT_ROWS
    block_start = jnp.minimum(jnp.arange(n_blocks, dtype=jnp.int32), n_used[0] - 1) * EXPERT_ROWS
    block_e = jnp.sum((pend[None, :] <= block_start[:, None]).astype(jnp.int32), axis=1)
    return pend - pcounts, jnp.minimum(block_e, N_EXPERTS - 1), n_used


def _slot_rows(route, pstart, ns, nt, tl):
    n = route.shape[0] * ns * tl
    fields = lambda lo: route[:, lo:lo + TOP_K, :].reshape(-1, nt, TOP_K, ns, tl)
    per_slot = lambda lo: jnp.transpose(fields(lo), (2, 0, 3, 1, 4)).reshape(TOP_K, n)
    e = per_slot(TOP_K).astype(jnp.int32)
    rank = per_slot(2 * TOP_K).astype(jnp.int32)
    hit = e[:, :, None] == jnp.arange(N_EXPERTS, dtype=jnp.int32)[None, None, :]
    rows = rank + jnp.sum(jnp.where(hit, pstart[None, None, :], 0), axis=-1)
    gates = jnp.transpose(fields(0), (0, 3, 1, 4, 2)).reshape(n, TOP_K)
    return rows, gates


SC_WINDOW = 128
SC_COLS = 256


def _sc_mesh():
    return plsc.VectorSubcoreMesh(core_axis_name="c", subcore_axis_name="s")


def _dispatch(sources, m_pad):
    width = sources[0][0].shape[1]
    n_src = len(sources)
    assert width % SC_COLS == 0 and all(h.shape[0] % SC_WINDOW == 0 for h, _ in sources)

    @functools.partial(pl.kernel, mesh=_sc_mesh(),
                       out_type=jax.ShapeDtypeStruct((m_pad, width), sources[0][0].dtype), scratch_types=[])
    def k(*refs):
        xs_hbm = refs[2 * n_src]

        def body(x_vmem, i_vmem):
            j = pl.program_id(1)
            for kk in range(TOP_K):
                pltpu.sync_copy(x_vmem, xs_hbm.at[i_vmem.at[kk], pl.ds(j * SC_COLS, SC_COLS)])

        for si in range(n_src):
            src, rows = refs[2 * si], refs[2 * si + 1]
            pltpu.emit_pipeline(
                body,
                grid=(src.shape[0] // SC_WINDOW, width // SC_COLS),
                in_specs=[pl.BlockSpec((SC_WINDOW, SC_COLS), lambda i, j: (i, j)),
                          pl.BlockSpec((TOP_K, SC_WINDOW), lambda i, j: (0, i))],
                out_specs=[],
                core_axis_name=("c", "s"),
                dimension_semantics=(pltpu.PARALLEL, pltpu.ARBITRARY),
            )(src, rows)

    return k(*[a for pair in sources for a in pair])


def _collect(yb, rows):
    width = yb.shape[1]
    assert width % SC_COLS == 0 and rows.shape[1] % SC_WINDOW == 0

    @functools.partial(pl.kernel, mesh=_sc_mesh(),
                       out_type=jax.ShapeDtypeStruct((rows.shape[1], width), yb.dtype), scratch_types=[])
    def k(yb_hbm, r_hbm, o_hbm):
        def body(i_vmem, o_vmem):
            j = pl.program_id(1)
            pltpu.sync_copy(yb_hbm.at[i_vmem.at[0], pl.ds(j * SC_COLS, SC_COLS)], o_vmem)

        pltpu.emit_pipeline(
            body,
            grid=(rows.shape[1] // SC_WINDOW, width // SC_COLS),
            in_specs=[pl.BlockSpec((1, SC_WINDOW), lambda i, j: (0, i))],
            out_specs=[pl.BlockSpec((SC_WINDOW, SC_COLS), lambda i, j: (i, j))],
            core_axis_name=("c", "s"),
            dimension_semantics=(pltpu.PARALLEL, pltpu.ARBITRARY),
        )(r_hbm, o_hbm)

    return k(yb, rows)


def _combine_kernel(x_ref, yg_ref, route_ref, fw_ref, y_ref):
    half = D_MODEL // 2
    acc_lo = x_ref[:, :half]
    acc_hi = x_ref[:, half:]
    route = route_ref[...]
    for kk in range(TOP_K):
        y_lo, y_hi = _unpack_bf16(yg_ref[kk])
        gate = route[:, kk:kk + 1]
        acc_lo = acc_lo + y_lo * gate
        acc_hi = acc_hi + y_hi * gate
    ms = (jnp.sum(acc_lo * acc_lo, axis=-1, keepdims=True)
          + jnp.sum(acc_hi * acc_hi, axis=-1, keepdims=True)) * (1.0 / D_MODEL)
    scale = lax.rsqrt(ms + EPS)
    y_ref[:, :half] = acc_lo * scale * fw_ref[:, :half]
    y_ref[:, half:] = acc_hi * scale * fw_ref[:, half:]


def _combine_next_kernel(x_ref, yg_ref, route_ref, fw_ref, prev_ref, y_ref):
    del prev_ref
    _combine_kernel(x_ref, yg_ref, route_ref, fw_ref, y_ref)


def _combine(x2, yg, gates, final_w, y_prev, first_row, out_rows, out_first_row):
    assert yg.shape[1] % COMBINE_ROWS == 0 and first_row % COMBINE_ROWS == 0 and out_first_row % COMBINE_ROWS == 0
    blk_in = first_row // COMBINE_ROWS
    blk_out = out_first_row // COMBINE_ROWS
    in_specs = [
        pl.BlockSpec((COMBINE_ROWS, D_MODEL), lambda i: (blk_in + i, 0)),
        pl.BlockSpec((TOP_K, COMBINE_ROWS, D_MODEL // 2), lambda i: (0, i, 0)),
        pl.BlockSpec((COMBINE_ROWS, TOP_K), lambda i: (blk_in + i, 0)),
        _const_spec((1, D_MODEL)),
    ]
    args = [x2, yg, gates, final_w]
    kern, aliases = _combine_kernel, {}
    if y_prev is not None:
        in_specs.append(pl.BlockSpec(memory_space=pl.ANY))
        args.append(y_prev)
        kern, aliases = _combine_next_kernel, {len(args) - 1: 0}
    return pl.pallas_call(
        kern,
        grid=(yg.shape[1] // COMBINE_ROWS,),
        in_specs=in_specs,
        out_specs=pl.BlockSpec((COMBINE_ROWS, D_MODEL), lambda i: (blk_out + i, 0)),
        out_shape=jax.ShapeDtypeStruct((out_rows, D_MODEL), F32),
        input_output_aliases=aliases,
        compiler_params=pltpu.CompilerParams(
            dimension_semantics=("arbitrary",), vmem_limit_bytes=VMEM_LIMIT),
        name="combine_%d_%d" % (out_rows, out_first_row),
    )(*args)


def _collect_combine(yb, x2, rows, gates, final_w, chunks, y, out_rows, out_first_row):
    nc = x2.shape[0] // chunks
    for c in range(chunks):
        yg = _collect(yb, rows[:, c * nc:(c + 1) * nc].reshape(1, TOP_K * nc))
        y = _combine(x2, yg.reshape(TOP_K, nc, D_MODEL // 2), gates, final_w, y, c * nc,
                     out_rows, out_first_row + c * nc)
    return y


def kernel(x_prompt, x_sample, cache_mem_k, cache_mem_v, state_ret, state_pool, mem_prompt,
           norm_mix_w, w_in, ret_gn_w, w_pool, pool_scale, w_out, norm_mem_w, mem_norm_w,
           w_q_mem, w_kv_mem, w_o_mem, norm_ffn_w, router_w, router_b, w_gate_up, b_gate_up,
           w_down, b_down, final_norm_w):
    assert norm_mix_w.shape[0] == 1, "one layer"
    b, seq, _ = x_prompt.shape
    db, dseq, _ = x_sample.shape
    row = lambda a: a.reshape(1, -1)

    mk_p, mv_p, mkb_p, mvb_p = _mem_kv(mem_prompt, row(mem_norm_w[0]), w_kv_mem[0].astype(BF16))
    mkb_s, mvb_s = _kv_flat(cache_mem_k[0], cache_mem_v[0])

    wts = (row(norm_mix_w[0]), w_in[0].astype(BF16), row(ret_gn_w[0]), w_pool[0].astype(BF16),
           row(pool_scale[0]), w_out[0].astype(BF16), row(norm_mem_w[0]), w_q_mem[0].astype(BF16),
           w_o_mem[0].astype(BF16), row(norm_ffn_w[0]), router_w[0].T.astype(BF16),
           jnp.broadcast_to(router_b[0][:, None], (N_EXPERTS, LANES)))

    assert sum(GROUP_STREAMS) == b
    n_s = db * dseq
    half = D_MODEL // 2
    final_w = row(final_norm_w)
    bgu = jnp.concatenate([b_gate_up[0][:, 0::2], b_gate_up[0][:, 1::2]], axis=-1).reshape(N_EXPERTS, 1, 2 * D_FF)
    bd = b_down[0].reshape(N_EXPERTS, 1, D_MODEL)
    no_counts = jnp.zeros((N_EXPERTS, LANES), F32)
    hist_s = jnp.concatenate([jnp.zeros((db, 1, POOL_WIDTH), F32), state_pool[0]], axis=1)

    y_p, rets, pools, stream0 = None, [], [], 0
    for g, per in enumerate(GROUP_STREAMS):
        n_g = per * seq
        x2_g, h_g, route_g, ret_g, pool_g, cnt = _layer(
            x_prompt, stream0, per, 0, PROMPT_TILE, PROMPT_STREAMS, PROMPT_CHAIN,
            jnp.zeros((per, RET_HEADS, RET_DK, RET_DV), F32), jnp.zeros((per, HIST_ROWS, POOL_WIDTH), F32),
            mkb_p, mvb_p, wts, no_counts)
        rets.append(ret_g)
        pools.append(pool_g)
        with_sample = g == len(GROUP_STREAMS) - 1
        n_slots = n_g * TOP_K
        if with_sample:
            x2_s, h_s, route_s, ret_s, pool_s, cnt = _layer(
                x_sample, 0, db, PAST_LEN, dseq, SAMPLE_STREAMS, SAMPLE_STREAMS, state_ret[0], hist_s,
                mkb_s, mvb_s, wts, cnt)
            n_slots += n_s * TOP_K
        first = g == 0
        n_blocks = -(-(n_slots + N_EXPERTS * (EXPERT_ROWS - 1)) // EXPERT_ROWS) + (N_EXPERTS if first else 0)
        pstart, block_e, n_used = _expert_layout(cnt[:, 0].astype(jnp.int32), n_blocks, first)
        rows_g, gates_g = _slot_rows(route_g, pstart, PROMPT_STREAMS, seq // PROMPT_TILE, PROMPT_TILE)
        sources = [(h_g.reshape(n_g, half), rows_g)]
        if with_sample:
            rows_s, gates_s = _slot_rows(route_s, pstart, SAMPLE_STREAMS, 1, dseq)
            sources.append((h_s.reshape(n_s, half), rows_s))
        xs = _dispatch(sources, n_blocks * EXPERT_ROWS)
        if first:
            yb, wgu_bf, wd_bf = _expert_ffn(block_e, n_used, xs, w_gate_up[0], bgu, w_down[0], bd, n_blocks)
        else:
            (yb,) = _expert_ffn(block_e, n_used, xs, wgu_bf, bgu, wd_bf, bd, n_blocks)
        if with_sample:
            y_s = _collect_combine(yb, x2_s.reshape(n_s, D_MODEL), rows_s, gates_s, final_w, 1, None, n_s, 0)
        y_p = _collect_combine(yb, x2_g.reshape(n_g, D_MODEL), rows_g, gates_g, final_w, n_g // COMBINE_TOKENS,
                               y_p, b * seq, stream0 * seq)
        stream0 += per
    y_p = y_p.reshape(b, seq, D_MODEL)
    y_s = y_s.reshape(db, dseq, D_MODEL)
    ret_p = jnp.concatenate(rets, axis=0)
    pool_p = jnp.concatenate(pools, axis=0)
    return (y_p, y_s, mk_p[None], mv_p[None], ret_p[None], pool_p[None], ret_s[None], pool_s[None])
```

```python
import functools

import numpy as np
import jax
import jax.numpy as jnp
from jax import lax
from jax.experimental import pallas as pl
from jax.experimental.pallas import tpu as pltpu
from jax.experimental.pallas import tpu_sc as plsc

D_MODEL = 1024
CHUNK = 64
PAST_LEN = 4096
RET_HEADS = 4
RET_DK = 128
RET_DV = 128
RET_QK = RET_HEADS * RET_DK
RET_VW = RET_HEADS * RET_DV
ROPE_BASE = 10000.0
POOL_WINDOWS = (2, 4, 8, 16)
POOL_GROUPS = 4
POOL_WIDTH = D_MODEL // 2
POOL_C = POOL_WIDTH // POOL_GROUPS
POOL_HIST = max(POOL_WINDOWS) - 1
HIST_ROWS = POOL_HIST + 1
IN_WIDTH = 2 * RET_QK + 2 * RET_VW + POOL_WIDTH
N_MEM = 256
MEM_HEADS = 4
MEM_HD = D_MODEL // MEM_HEADS
N_EXPERTS = 32
TOP_K = 4
D_FF = D_MODEL
SWIGLU_LIMIT = 7.0
SWIGLU_ALPHA = 1.702
EPS = 1e-5

LANES = 128
ROUTE_ROWS = 16
PROMPT_TILE = 256
PROMPT_STREAMS = 4
PROMPT_CHAIN = 2
SAMPLE_STREAMS = 8
EXPERT_ROWS = 512
GROUP_STREAMS = (16, 16)
KV_STREAMS = 2
COMBINE_ROWS = 1024
COMBINE_TOKENS = 16384
VMEM_LIMIT = 56 * 1024 * 1024

BF16 = jnp.bfloat16
F32 = jnp.float32


def _rms(x, w):
    return x * lax.rsqrt(jnp.mean(x * x, axis=-1, keepdims=True) + EPS) * w


def _dot(a, b):
    return jnp.dot(a, b, preferred_element_type=F32)


def _rms_dot(x, w_norm, w_mat):
    inv = lax.rsqrt(jnp.mean(x * x, axis=-1, keepdims=True) + EPS)
    return _dot((x * w_norm).astype(BF16), w_mat) * inv


def _dot_nt(a, b):
    return lax.dot_general(a, b, (((1,), (1,)), ((), ())), preferred_element_type=F32)


BF16_BITS = 16
HIGH_HALF = -(1 << BF16_BITS)


def _pack_bf16(x):
    bits = lax.bitcast_convert_type(x.astype(BF16).astype(F32), jnp.int32)
    w = x.shape[1] // 2
    return lax.shift_right_logical(bits[:, :w], BF16_BITS) | (bits[:, w:] & HIGH_HALF)


def _unpack_bf16(p):
    lo = lax.bitcast_convert_type(lax.shift_left(p, BF16_BITS), F32)
    hi = lax.bitcast_convert_type(p & HIGH_HALF, F32)
    return lo, hi


def _const_spec(shape):
    nd = len(shape)
    return pl.BlockSpec(shape, lambda *_: (0,) * nd, pipeline_mode=pl.Buffered(1))


def _mem_kv_kernel(mem_ref, nw_ref, w_ref, k_ref, v_ref, kb_ref, vb_ref):
    ns = mem_ref.shape[0]
    xn = _rms(mem_ref[...].reshape(ns * N_MEM, D_MODEL), nw_ref[...]).astype(BF16)
    kv = _dot(xn, w_ref[...])
    for hd in range(MEM_HEADS):
        lo = hd * MEM_HD
        k_ref[:, :, hd, :] = kv[:, lo:lo + MEM_HD].reshape(ns, N_MEM, MEM_HD)
        v_ref[:, :, hd, :] = kv[:, D_MODEL + lo:D_MODEL + lo + MEM_HD].reshape(ns, N_MEM, MEM_HD)
    kb_ref[...] = kv[:, :D_MODEL].astype(BF16).reshape(ns, N_MEM, D_MODEL)
    vb_ref[...] = kv[:, D_MODEL:].astype(BF16).reshape(ns, N_MEM, D_MODEL)


def _mem_kv(mem, mem_norm_w, w_kv_bf):
    b = mem.shape[0]
    out_spec = pl.BlockSpec((KV_STREAMS, N_MEM, MEM_HEADS, MEM_HD), lambda i: (i, 0, 0, 0))
    flat_spec = pl.BlockSpec((KV_STREAMS, N_MEM, D_MODEL), lambda i: (i, 0, 0))
    return pl.pallas_call(
        _mem_kv_kernel,
        grid=(b // KV_STREAMS,),
        in_specs=[
            pl.BlockSpec((KV_STREAMS, N_MEM, D_MODEL), lambda i: (i, 0, 0)),
            _const_spec((1, D_MODEL)),
            _const_spec((D_MODEL, 2 * D_MODEL)),
        ],
        out_specs=[out_spec, out_spec, flat_spec, flat_spec],
        out_shape=[jax.ShapeDtypeStruct((b, N_MEM, MEM_HEADS, MEM_HD), F32)] * 2
        + [jax.ShapeDtypeStruct((b, N_MEM, D_MODEL), BF16)] * 2,
        compiler_params=pltpu.CompilerParams(
            dimension_semantics=("arbitrary",), vmem_limit_bytes=VMEM_LIMIT),
        name="mem_kv",
    )(mem, mem_norm_w, w_kv_bf)


def _kv_flat_kernel(k_ref, v_ref, kb_ref, vb_ref):
    for hd in range(MEM_HEADS):
        kb_ref[:, :, hd * MEM_HD:(hd + 1) * MEM_HD] = k_ref[:, :, hd, :].astype(BF16)
        vb_ref[:, :, hd * MEM_HD:(hd + 1) * MEM_HD] = v_ref[:, :, hd, :].astype(BF16)


def _kv_flat(mem_k, mem_v):
    b = mem_k.shape[0]
    in_spec = pl.BlockSpec((KV_STREAMS, N_MEM, MEM_HEADS, MEM_HD), lambda i: (i, 0, 0, 0))
    out_spec = pl.BlockSpec((KV_STREAMS, N_MEM, D_MODEL), lambda i: (i, 0, 0))
    return pl.pallas_call(
        _kv_flat_kernel,
        grid=(b // KV_STREAMS,),
        in_specs=[in_spec, in_spec],
        out_specs=[out_spec, out_spec],
        out_shape=[jax.ShapeDtypeStruct((b, N_MEM, D_MODEL), BF16)] * 2,
        compiler_params=pltpu.CompilerParams(
            dimension_semantics=("arbitrary",), vmem_limit_bytes=VMEM_LIMIT),
        name="kv_flat",
    )(mem_k, mem_v)


def _layer_kernel(x_ref, cos_ref, sin_ref, dmat_ref, qdec_ref, kdec_ref, state0_ref, hist0_ref,
                  mk_ref, mv_ref, nmix_ref, win_ref, gnw_ref, wpool_ref, pscale_ref, wout_ref,
                  nmem_ref, wq_ref, wo_ref, nffn_ref, rw_ref, rb_ref, earlier_ref, cnt0_ref,
                  x2_ref, h_ref, route_ref, rstate_ref, pstate_ref, cnt_ref,
                  s_scr, ext_scr, cnt_scr, *, tl, chain, pos0, cdec):
    t = pl.program_id(1)

    @pl.when((pl.program_id(0) == 0) & (t == 0))
    def _():
        cnt_scr[...] = cnt0_ref[...]

    @pl.when(t == 0)
    def _():
        s_scr[...] = state0_ref[...]
        ext_scr[:, 0:HIST_ROWS, :] = hist0_ref[...]

    for c0 in range(0, x_ref.shape[0], chain):
        _layer_chain(c0, t, x_ref, cos_ref, sin_ref, dmat_ref, qdec_ref, kdec_ref, nmix_ref, win_ref, gnw_ref,
                     wpool_ref, pscale_ref, wout_ref, nmem_ref, wq_ref, wo_ref, nffn_ref, rw_ref, rb_ref,
                     earlier_ref, x2_ref, h_ref, route_ref, pstate_ref, s_scr, ext_scr, mk_ref, mv_ref, cnt_scr,
                     tl=tl, chain=chain, pos0=pos0, cdec=cdec)
    rstate_ref[...] = s_scr[...]
    cnt_ref[...] = cnt_scr[...]


def _layer_chain(c0, t, x_ref, cos_ref, sin_ref, dmat_ref, qdec_ref, kdec_ref, nmix_ref, win_ref, gnw_ref,
                 wpool_ref, pscale_ref, wout_ref, nmem_ref, wq_ref, wo_ref, nffn_ref, rw_ref, rb_ref,
                 earlier_ref, x2_ref, h_ref, route_ref, pstate_ref, s_scr, ext_scr, mk_ref, mv_ref, cnt_scr,
                 *, tl, chain, pos0, cdec):
    rows = chain * tl
    x = x_ref[c0:c0 + chain].reshape(rows, D_MODEL)
    proj = _rms_dot(x, nmix_ref[...], win_ref[...])
    cos = cos_ref[...]
    sin = sin_ref[...]
    pos = (pos0 + t * tl + lax.broadcasted_iota(jnp.int32, (tl, POOL_C), 0)).astype(F32)

    mixes = []
    for sj in range(chain):
        si = c0 + sj
        pj = proj[sj * tl:(sj + 1) * tl]
        outs = []
        for hd in range(RET_HEADS):
            lo = hd * RET_DK
            q = pj[:, lo:lo + RET_DK]
            k = pj[:, RET_QK + lo:RET_QK + lo + RET_DK]
            v = pj[:, 2 * RET_QK + lo:2 * RET_QK + lo + RET_DV]
            g = pj[:, 2 * RET_QK + RET_VW + lo:2 * RET_QK + RET_VW + lo + RET_DV]
            qr = (q * cos + pltpu.roll(q, RET_DK // 2, 1) * sin) * (RET_DK ** -0.5)
            kr = k * cos + pltpu.roll(k, RET_DK // 2, 1) * sin
            vb = v.astype(BF16)
            s = _dot_nt(qr.astype(BF16), kr.astype(BF16)) * dmat_ref[hd]
            o = _dot(s.astype(BF16), vb)
            state = s_scr[si, hd]
            o = o + _dot((qr * qdec_ref[hd]).astype(BF16), state.astype(BF16))
            kd_t = jnp.transpose(kr * kdec_ref[hd]).astype(BF16)
            s_scr[si, hd] = cdec[hd] * state + _dot(kd_t, vb)
            mu = jnp.mean(o, axis=-1, keepdims=True)
            oc = o - mu
            var = jnp.mean(oc * oc, axis=-1, keepdims=True)
            on = oc * lax.rsqrt(var + EPS)
            outs.append(on * gnw_ref[:, lo:lo + RET_DV] * (g * jax.nn.sigmoid(g)))

        pin = pj[:, 2 * RET_QK + 2 * RET_VW:]
        ext_scr[si, HIST_ROWS:HIST_ROWS + tl, :] = pin
        pstate_ref[si] = pin[tl - POOL_HIST:, :]
        for gi, w in enumerate(POOL_WINDOWS):
            lo = gi * POOL_C
            wsum = ext_scr[si, :, lo:lo + POOL_C]
            shift = 1
            while shift < w:
                wsum = wsum + pltpu.roll(wsum, shift, 0)
                shift *= 2
            cnt = jnp.minimum(float(w), pos + 1.0)
            d = wsum[HIST_ROWS:, :] / cnt - pin[:, lo:lo + POOL_C]
            y = _dot(d.astype(BF16), wpool_ref[gi])
            outs.append(y * pscale_ref[:, lo:lo + POOL_C])
        ext_scr[si, 0:HIST_ROWS, :] = ext_scr[si, tl:tl + HIST_ROWS, :]
        mixes.append(jnp.concatenate(outs, axis=-1).astype(BF16))

    x1 = x + _dot(jnp.concatenate(mixes, axis=0), wout_ref[...])

    qm = _rms_dot(x1, nmem_ref[...], wq_ref[...])
    atts = []
    for sj in range(chain):
        si = c0 + sj
        aouts = []
        for hd in range(MEM_HEADS):
            lo = hd * MEM_HD
            qh = qm[sj * tl:(sj + 1) * tl, lo:lo + MEM_HD].astype(BF16)
            s = _dot_nt(qh, mk_ref[si, :, lo:lo + MEM_HD]) * (MEM_HD ** -0.5)
            e = jnp.exp(s - jnp.max(s, axis=-1, keepdims=True))
            p = e / jnp.sum(e, axis=-1, keepdims=True)
            aouts.append(_dot(p.astype(BF16), mv_ref[si, :, lo:lo + MEM_HD]))
        atts.append(jnp.concatenate(aouts, axis=-1).astype(BF16))
    x2 = x1 + _dot(jnp.concatenate(atts, axis=0), wo_ref[...])
    x2_ref[c0:c0 + chain] = x2.reshape(chain, tl, D_MODEL)

    hn = _rms(x2, nffn_ref[...])
    h_ref[c0:c0 + chain] = _pack_bf16(hn).reshape(chain, tl, D_MODEL // 2)
    lane_tile = lambda a: a[:, :rows] if rows <= LANES else jnp.concatenate([a] * (rows // LANES), axis=1)
    logits = _dot_nt(rw_ref[...], hn.astype(BF16)) + lane_tile(rb_ref[...])
    eiota = lax.broadcasted_iota(jnp.int32, (N_EXPERTS, rows), 0)
    neg = jnp.finfo(F32).min
    vals, idxs = [], []
    for _k in range(TOP_K):
        m = jnp.max(logits, axis=0, keepdims=True)
        idx = jnp.min(jnp.where(logits == m, eiota, N_EXPERTS), axis=0, keepdims=True)
        vals.append(m)
        idxs.append(idx)
        logits = jnp.where(eiota == idx, neg, logits)
    exps = [jnp.exp(vk - vals[0]) for vk in vals]
    den = exps[0] + exps[1] + exps[2] + exps[3]

    onehots = [(eiota == idx).astype(F32) for idx in idxs]
    picked = onehots[0] + onehots[1] + onehots[2] + onehots[3]
    before = lane_tile(cnt_scr[...]) + _dot(picked.astype(BF16), earlier_ref[...])
    ranks = [jnp.sum(oh * before, axis=0, keepdims=True) for oh in onehots]
    cnt_scr[...] = cnt_scr[...] + jnp.sum(picked, axis=1, keepdims=True)

    rowi = lax.broadcasted_iota(jnp.int32, (ROUTE_ROWS, rows), 0)
    route = jnp.zeros((ROUTE_ROWS, rows), F32)
    for kk in range(TOP_K):
        route = jnp.where(rowi == kk, exps[kk] / den, route)
        route = jnp.where(rowi == TOP_K + kk, idxs[kk].astype(F32), route)
        route = jnp.where(rowi == 2 * TOP_K + kk, ranks[kk], route)
    route_ref[0, :, c0 * tl:c0 * tl + rows] = route


def _decay_tables(tl):
    hh = np.arange(RET_HEADS, dtype=np.float64)
    log_g = np.log1p(-np.exp2(-5.0 - hh))
    idx = np.arange(tl, dtype=np.float64)
    dist = np.abs(idx[:, None] - idx[None, :])
    visible = (idx[None, :] // CHUNK) <= (idx[:, None] // CHUNK)
    dmat = np.where(visible[None], np.exp(log_g[:, None, None] * dist[None]), 0.0)
    qdec = np.exp(log_g[:, None] * (idx[None, :] + 1.0))
    kdec = np.exp(log_g[:, None] * (tl - 1.0 - idx[None, :]))
    cdec = tuple(float(c) for c in np.exp(log_g * tl).astype(np.float32))
    bcast = lambda a: np.ascontiguousarray(np.broadcast_to(a[:, :, None], (RET_HEADS, tl, RET_DK)))
    return (jnp.asarray(dmat, F32), jnp.asarray(bcast(qdec), F32), jnp.asarray(bcast(kdec), F32), cdec)


def _rotary_tables(pos0, length):
    half = RET_DK // 2
    inv_freq = jnp.power(ROPE_BASE, -jnp.arange(half, dtype=F32) / half)
    ang = (pos0 + jnp.arange(length, dtype=jnp.int32)).astype(F32)[:, None] * inv_freq[None, :]
    cos, sin = jnp.cos(ang), jnp.sin(ang)
    return jnp.concatenate([cos, cos], axis=-1), jnp.concatenate([-sin, sin], axis=-1)


def _layer(x, stream0, b, pos0, tl, ns, chain, state0, hist0, mk, mv, wts, cnt0):
    length = x.shape[1]
    assert b % ns == 0 and stream0 % ns == 0 and ns % chain == 0, (b, stream0, ns, chain)
    assert length % tl == 0 and tl % CHUNK == 0 and tl > POOL_HIST, (length, tl)
    nt = length // tl
    g0 = stream0 // ns
    rows = ns * tl
    cos, sin = _rotary_tables(pos0, length)
    dmat, qdec, kdec, cdec = _decay_tables(tl)
    crows = chain * tl
    earlier = jnp.asarray(np.triu(np.ones((crows, crows), np.float32), 1), BF16)
    kern = functools.partial(_layer_kernel, tl=tl, chain=chain, pos0=pos0, cdec=cdec)
    tok = lambda width: pl.BlockSpec((ns, tl, width), lambda i, j: (i, j, 0))
    per_stream = lambda *shape: pl.BlockSpec((ns,) + shape, lambda i, j: (i,) + (0,) * len(shape))
    per_stream_in = lambda off, *shape: pl.BlockSpec(
        (ns,) + shape, lambda i, j: (i + off,) + (0,) * len(shape), pipeline_mode=pl.Buffered(1))
    in_specs = [
        pl.BlockSpec((ns, tl, D_MODEL), lambda i, j: (i + g0, j, 0)),
        pl.BlockSpec((tl, RET_DK), lambda i, j: (j, 0)),
        pl.BlockSpec((tl, RET_DK), lambda i, j: (j, 0)),
        _const_spec((RET_HEADS, tl, tl)),
        _const_spec((RET_HEADS, tl, RET_DK)),
        _const_spec((RET_HEADS, tl, RET_DK)),
        per_stream_in(0, RET_HEADS, RET_DK, RET_DV),
        per_stream_in(0, HIST_ROWS, POOL_WIDTH),
        per_stream_in(g0, N_MEM, D_MODEL),
        per_stream_in(g0, N_MEM, D_MODEL),
    ] + [_const_spec(w.shape) for w in wts] + [_const_spec((crows, crows)), _const_spec((N_EXPERTS, LANES))]
    out_specs = [
        tok(D_MODEL), tok(D_MODEL // 2),
        pl.BlockSpec((1, ROUTE_ROWS, rows), lambda i, j: (i * nt + j, 0, 0)),
        per_stream(RET_HEADS, RET_DK, RET_DV),
        per_stream(POOL_HIST, POOL_WIDTH),
        pl.BlockSpec((N_EXPERTS, LANES), lambda i, j: (0, 0)),
    ]
    out_shape = [
        jax.ShapeDtypeStruct((b, length, D_MODEL), F32),
        jax.ShapeDtypeStruct((b, length, D_MODEL // 2), jnp.int32),
        jax.ShapeDtypeStruct((b // ns * nt, ROUTE_ROWS, rows), F32),
        jax.ShapeDtypeStruct((b, RET_HEADS, RET_DK, RET_DV), F32),
        jax.ShapeDtypeStruct((b, POOL_HIST, POOL_WIDTH), F32),
        jax.ShapeDtypeStruct((N_EXPERTS, LANES), F32),
    ]
    scratch = [
        pltpu.VMEM((ns, RET_HEADS, RET_DK, RET_DV), F32),
        pltpu.VMEM((ns, HIST_ROWS + tl, POOL_WIDTH), F32),
        pltpu.VMEM((N_EXPERTS, LANES), F32),
    ]
    return pl.pallas_call(
        kern,
        grid=(b // ns, nt),
        in_specs=in_specs,
        out_specs=out_specs,
        out_shape=out_shape,
        scratch_shapes=scratch,
        compiler_params=pltpu.CompilerParams(
            dimension_semantics=("arbitrary", "arbitrary"), vmem_limit_bytes=VMEM_LIMIT),
        name="layer_tl%d" % tl,
    )(x, cos, sin, dmat, qdec, kdec, state0, hist0, mk, mv, *wts, earlier, cnt0)


SPLIT_COLS = 2 * LANES


def _expert_block(x_ref, wgu, bgu_ref, wd, bd_ref, y_ref):
    x_lo, x_hi = _unpack_bf16(x_ref[...])
    xb = jnp.concatenate([x_lo.astype(BF16), x_hi.astype(BF16)], axis=1)
    gu = _dot(xb, wgu[...]) + bgu_ref[0]
    gate = jnp.minimum(gu[:, :D_FF], SWIGLU_LIMIT)
    up = jnp.clip(gu[:, D_FF:], -SWIGLU_LIMIT, SWIGLU_LIMIT)
    act = (up + 1.0) * gate * jax.nn.sigmoid(SWIGLU_ALPHA * gate)
    y_ref[...] = _pack_bf16(_dot(act.astype(BF16), wd[...]) + bd_ref[0])


def _expert_convert_kernel(be_ref, used_ref, x_ref, wgu_ref, bgu_ref, wd_ref, bd_ref, perm_ref,
                           y_ref, wgu_out_ref, wd_out_ref):
    i = pl.program_id(0)
    in_use = i < used_ref[0]

    @pl.when(in_use & ((i == 0) | (be_ref[i] != be_ref[jnp.maximum(i - 1, 0)])))
    def _():
        perm = perm_ref[...]
        for c in range(2 * D_FF // SPLIT_COLS):
            wc = wgu_ref[0, :, c * SPLIT_COLS:(c + 1) * SPLIT_COLS].astype(BF16)
            pc = _dot(wc, perm).astype(BF16)
            wgu_out_ref[0, :, c * LANES:(c + 1) * LANES] = pc[:, :LANES]
            wgu_out_ref[0, :, D_FF + c * LANES:D_FF + (c + 1) * LANES] = pc[:, LANES:]
        wd_out_ref[0] = wd_ref[0].astype(BF16)

    @pl.when(in_use)
    def _():
        _expert_block(x_ref, wgu_out_ref.at[0], bgu_ref, wd_out_ref.at[0], bd_ref, y_ref)


def _expert_pair_kernel(be_ref, used_ref, x_ref, wgu_a, bgu_a, wd_a, bd_a, wgu_b, bgu_b, wd_b, bd_b, y_ref):
    i = pl.program_id(0)
    use0 = 2 * i < used_ref[0]
    use1 = 2 * i + 1 < used_ref[0]
    joint = use1 & (be_ref[2 * i] == be_ref[2 * i + 1])
    first = pl.ds(0, EXPERT_ROWS)
    second = pl.ds(EXPERT_ROWS, EXPERT_ROWS)

    @pl.when(joint)
    def _():
        _expert_block(x_ref, wgu_a.at[0], bgu_a, wd_a.at[0], bd_a, y_ref)

    @pl.when(use0 & jnp.logical_not(joint))
    def _():
        _expert_block(x_ref.at[first], wgu_a.at[0], bgu_a, wd_a.at[0], bd_a, y_ref.at[first])

    @pl.when(use1 & jnp.logical_not(joint))
    def _():
        _expert_block(x_ref.at[second], wgu_b.at[0], bgu_b, wd_b.at[0], bd_b, y_ref.at[second])


def _expert_ffn_convert(block_e, n_used, xs, w_gate_up, bgu, w_down, bd, n_blocks):
    perm = np.zeros((SPLIT_COLS, SPLIT_COLS), np.float32)
    j = np.arange(LANES)
    perm[2 * j, j] = 1.0
    perm[2 * j + 1, LANES + j] = 1.0
    blk = lambda i, be, used: (jnp.minimum(i, used[0] - 1), 0)
    per_expert = lambda i, be, used: (be[i], 0, 0)
    wgu_spec = pl.BlockSpec((1, D_MODEL, 2 * D_FF), per_expert)
    wd_spec = pl.BlockSpec((1, D_FF, D_MODEL), per_expert)
    grid_spec = pltpu.PrefetchScalarGridSpec(
        num_scalar_prefetch=2, grid=(n_blocks,),
        in_specs=[pl.BlockSpec((EXPERT_ROWS, D_MODEL // 2), blk), wgu_spec,
                  pl.BlockSpec((1, 1, 2 * D_FF), per_expert), wd_spec, pl.BlockSpec((1, 1, D_MODEL), per_expert),
                  _const_spec((SPLIT_COLS, SPLIT_COLS))],
        out_specs=[pl.BlockSpec((EXPERT_ROWS, D_MODEL // 2), blk), wgu_spec, wd_spec])
    return pl.pallas_call(
        _expert_convert_kernel,
        grid_spec=grid_spec,
        out_shape=[jax.ShapeDtypeStruct(xs.shape, jnp.int32), jax.ShapeDtypeStruct(w_gate_up.shape, BF16),
                   jax.ShapeDtypeStruct(w_down.shape, BF16)],
        compiler_params=pltpu.CompilerParams(
            dimension_semantics=("arbitrary",), vmem_limit_bytes=VMEM_LIMIT),
        name="expert_ffn_convert",
    )(block_e, n_used, xs, w_gate_up, bgu, w_down, bd, jnp.asarray(perm, BF16))


def _expert_ffn_pairs(block_e, n_used, xs, wgu, bgu, wd, bd, n_blocks):
    pair = lambda i, be, used: (jnp.minimum(i, (used[0] - 1) // 2), 0)
    specs = []
    for half in range(2):
        per_expert = lambda i, be, used, half=half: (be[2 * i + half], 0, 0)
        specs += [pl.BlockSpec((1, D_MODEL, 2 * D_FF), per_expert), pl.BlockSpec((1, 1, 2 * D_FF), per_expert),
                  pl.BlockSpec((1, D_FF, D_MODEL), per_expert), pl.BlockSpec((1, 1, D_MODEL), per_expert)]
    grid_spec = pltpu.PrefetchScalarGridSpec(
        num_scalar_prefetch=2, grid=(n_blocks // 2,),
        in_specs=[pl.BlockSpec((2 * EXPERT_ROWS, D_MODEL // 2), pair)] + specs,
        out_specs=[pl.BlockSpec((2 * EXPERT_ROWS, D_MODEL // 2), pair)])
    return pl.pallas_call(
        _expert_pair_kernel,
        grid_spec=grid_spec,
        out_shape=[jax.ShapeDtypeStruct(xs.shape, jnp.int32)],
        compiler_params=pltpu.CompilerParams(
            dimension_semantics=("arbitrary",), vmem_limit_bytes=VMEM_LIMIT),
        name="expert_ffn",
    )(block_e, n_used, xs, wgu, bgu, wd, bd, wgu, bgu, wd, bd)


def _expert_layout(counts, n_blocks, every_expert):
    pcounts = (counts + EXPERT_ROWS - 1) // EXPERT_ROWS * EXPERT_ROWS
    if every_expert:
        pcounts = jnp.maximum(pcounts, EXPERT_ROWS)
    pend = jnp.cumsum(pcounts)
    n_used = pend[-1:] // EXPERT_ROWS
    block_start = jnp.minimum(jnp.arange(n_blocks, dtype=jnp.int32), n_used[0] - 1) * EXPERT_ROWS
    block_e = jnp.sum((pend[None, :] <= block_start[:, None]).astype(jnp.int32), axis=1)
    return pend - pcounts, jnp.minimum(block_e, N_EXPERTS - 1), n_used


def _slot_rows(route, pstart, ns, nt, tl):
    n = route.shape[0] * ns * tl
    fields = lambda lo: route[:, lo:lo + TOP_K, :].reshape(-1, nt, TOP_K, ns, tl)
    per_slot = lambda lo: jnp.transpose(fields(lo), (2, 0, 3, 1, 4)).reshape(TOP_K, n)
    e = per_slot(TOP_K).astype(jnp.int32)
    rank = per_slot(2 * TOP_K).astype(jnp.int32)
    hit = e[:, :, None] == jnp.arange(N_EXPERTS, dtype=jnp.int32)[None, None, :]
    rows = rank + jnp.sum(jnp.where(hit, pstart[None, None, :], 0), axis=-1)
    gates = jnp.transpose(fields(0), (0, 3, 1, 4, 2)).reshape(n, TOP_K)
    return rows, gates


SC_WINDOW = 128
SC_COLS = 256


def _sc_mesh():
    return plsc.VectorSubcoreMesh(core_axis_name="c", subcore_axis_name="s")


def _dispatch(sources, m_pad):
    width = sources[0][0].shape[1]
    n_src = len(sources)
    assert width % SC_COLS == 0 and all(h.shape[0] % SC_WINDOW == 0 for h, _ in sources)

    @functools.partial(pl.kernel, mesh=_sc_mesh(),
                       out_type=jax.ShapeDtypeStruct((m_pad, width), sources[0][0].dtype), scratch_types=[])
    def k(*refs):
        xs_hbm = refs[2 * n_src]

        def body(x_vmem, i_vmem):
            j = pl.program_id(1)
            for kk in range(TOP_K):
                pltpu.sync_copy(x_vmem, xs_hbm.at[i_vmem.at[kk], pl.ds(j * SC_COLS, SC_COLS)])

        for si in range(n_src):
            src, rows = refs[2 * si], refs[2 * si + 1]
            pltpu.emit_pipeline(
                body,
                grid=(src.shape[0] // SC_WINDOW, width // SC_COLS),
                in_specs=[pl.BlockSpec((SC_WINDOW, SC_COLS), lambda i, j: (i, j)),
                          pl.BlockSpec((TOP_K, SC_WINDOW), lambda i, j: (0, i))],
                out_specs=[],
                core_axis_name=("c", "s"),
                dimension_semantics=(pltpu.PARALLEL, pltpu.ARBITRARY),
            )(src, rows)

    return k(*[a for pair in sources for a in pair])


def _collect(yb, rows):
    width = yb.shape[1]
    assert width % SC_COLS == 0 and rows.shape[1] % SC_WINDOW == 0

    @functools.partial(pl.kernel, mesh=_sc_mesh(),
                       out_type=jax.ShapeDtypeStruct((rows.shape[1], width), yb.dtype), scratch_types=[])
    def k(yb_hbm, r_hbm, o_hbm):
        def body(i_vmem, o_vmem):
            j = pl.program_id(1)
            pltpu.sync_copy(yb_hbm.at[i_vmem.at[0], pl.ds(j * SC_COLS, SC_COLS)], o_vmem)

        pltpu.emit_pipeline(
            body,
            grid=(rows.shape[1] // SC_WINDOW, width // SC_COLS),
            in_specs=[pl.BlockSpec((1, SC_WINDOW), lambda i, j: (0, i))],
            out_specs=[pl.BlockSpec((SC_WINDOW, SC_COLS), lambda i, j: (i, j))],
            core_axis_name=("c", "s"),
            dimension_semantics=(pltpu.PARALLEL, pltpu.ARBITRARY),
        )(r_hbm, o_hbm)

    return k(yb, rows)


def _combine_kernel(x_ref, yg_ref, route_ref, fw_ref, y_ref):
    half = D_MODEL // 2
    acc_lo = x_ref[:, :half]
    acc_hi = x_ref[:, half:]
    route = route_ref[...]
    for kk in range(TOP_K):
        y_lo, y_hi = _unpack_bf16(yg_ref[kk])
        gate = route[:, kk:kk + 1]
        acc_lo = acc_lo + y_lo * gate
        acc_hi = acc_hi + y_hi * gate
    ms = (jnp.sum(acc_lo * acc_lo, axis=-1, keepdims=True)
          + jnp.sum(acc_hi * acc_hi, axis=-1, keepdims=True)) * (1.0 / D_MODEL)
    scale = lax.rsqrt(ms + EPS)
    y_ref[:, :half] = acc_lo * scale * fw_ref[:, :half]
    y_ref[:, half:] = acc_hi * scale * fw_ref[:, half:]


def _combine_next_kernel(x_ref, yg_ref, route_ref, fw_ref, prev_ref, y_ref):
    del prev_ref
    _combine_kernel(x_ref, yg_ref, route_ref, fw_ref, y_ref)


def _combine(x2, yg, gates, final_w, y_prev, first_row, out_rows, out_first_row):
    assert yg.shape[1] % COMBINE_ROWS == 0 and first_row % COMBINE_ROWS == 0 and out_first_row % COMBINE_ROWS == 0
    blk_in = first_row // COMBINE_ROWS
    blk_out = out_first_row // COMBINE_ROWS
    in_specs = [
        pl.BlockSpec((COMBINE_ROWS, D_MODEL), lambda i: (blk_in + i, 0)),
        pl.BlockSpec((TOP_K, COMBINE_ROWS, D_MODEL // 2), lambda i: (0, i, 0)),
        pl.BlockSpec((COMBINE_ROWS, TOP_K), lambda i: (blk_in + i, 0)),
        _const_spec((1, D_MODEL)),
    ]
    args = [x2, yg, gates, final_w]
    kern, aliases = _combine_kernel, {}
    if y_prev is not None:
        in_specs.append(pl.BlockSpec(memory_space=pl.ANY))
        args.append(y_prev)
        kern, aliases = _combine_next_kernel, {len(args) - 1: 0}
    return pl.pallas_call(
        kern,
        grid=(yg.shape[1] // COMBINE_ROWS,),
        in_specs=in_specs,
        out_specs=pl.BlockSpec((COMBINE_ROWS, D_MODEL), lambda i: (blk_out + i, 0)),
        out_shape=jax.ShapeDtypeStruct((out_rows, D_MODEL), F32),
        input_output_aliases=aliases,
        compiler_params=pltpu.CompilerParams(
            dimension_semantics=("arbitrary",), vmem_limit_bytes=VMEM_LIMIT),
        name="combine_%d_%d" % (out_rows, out_first_row),
    )(*args)


def _collect_combine(yb, x2, rows, gates, final_w, chunks, y, out_rows, out_first_row):
    nc = x2.shape[0] // chunks
    for c in range(chunks):
        yg = _collect(yb, rows[:, c * nc:(c + 1) * nc].reshape(1, TOP_K * nc))
        y = _combine(x2, yg.reshape(TOP_K, nc, D_MODEL // 2), gates, final_w, y, c * nc,
                     out_rows, out_first_row + c * nc)
    return y


def kernel(x_prompt, x_sample, cache_mem_k, cache_mem_v, state_ret, state_pool, mem_prompt,
           norm_mix_w, w_in, ret_gn_w, w_pool, pool_scale, w_out, norm_mem_w, mem_norm_w,
           w_q_mem, w_kv_mem, w_o_mem, norm_ffn_w, router_w, router_b, w_gate_up, b_gate_up,
           w_down, b_down, final_norm_w):
    assert norm_mix_w.shape[0] == 1, "one layer"
    b, seq, _ = x_prompt.shape
    db, dseq, _ = x_sample.shape
    row = lambda a: a.reshape(1, -1)

    mk_p, mv_p, mkb_p, mvb_p = _mem_kv(mem_prompt, row(mem_norm_w[0]), w_kv_mem[0].astype(BF16))
    mkb_s, mvb_s = _kv_flat(cache_mem_k[0], cache_mem_v[0])

    wts = (row(norm_mix_w[0]), w_in[0].astype(BF16), row(ret_gn_w[0]), w_pool[0].astype(BF16),
           row(pool_scale[0]), w_out[0].astype(BF16), row(norm_mem_w[0]), w_q_mem[0].astype(BF16),
           w_o_mem[0].astype(BF16), row(norm_ffn_w[0]), router_w[0].T.astype(BF16),
           jnp.broadcast_to(router_b[0][:, None], (N_EXPERTS, LANES)))

    assert sum(GROUP_STREAMS) == b
    n_s = db * dseq
    half = D_MODEL // 2
    final_w = row(final_norm_w)
    bgu = jnp.concatenate([b_gate_up[0][:, 0::2], b_gate_up[0][:, 1::2]], axis=-1).reshape(N_EXPERTS, 1, 2 * D_FF)
    bd = b_down[0].reshape(N_EXPERTS, 1, D_MODEL)
    no_counts = jnp.zeros((N_EXPERTS, LANES), F32)
    hist_s = jnp.concatenate([jnp.zeros((db, 1, POOL_WIDTH), F32), state_pool[0]], axis=1)

    y_p, rets, pools, stream0 = None, [], [], 0
    for g, per in enumerate(GROUP_STREAMS):
        n_g = per * seq
        x2_g, h_g, route_g, ret_g, pool_g, cnt = _layer(
            x_prompt, stream0, per, 0, PROMPT_TILE, PROMPT_STREAMS, PROMPT_CHAIN,
            jnp.zeros((per, RET_HEADS, RET_DK, RET_DV), F32), jnp.zeros((per, HIST_ROWS, POOL_WIDTH), F32),
            mkb_p, mvb_p, wts, no_counts)
        rets.append(ret_g)
        pools.append(pool_g)
        with_sample = g == len(GROUP_STREAMS) - 1
        n_slots = n_g * TOP_K
        if with_sample:
            x2_s, h_s, route_s, ret_s, pool_s, cnt = _layer(
                x_sample, 0, db, PAST_LEN, dseq, SAMPLE_STREAMS, SAMPLE_STREAMS, state_ret[0], hist_s,
                mkb_s, mvb_s, wts, cnt)
            n_slots += n_s * TOP_K
        first = g == 0
        n_blocks = -(-(n_slots + N_EXPERTS * (EXPERT_ROWS - 1)) // EXPERT_ROWS) + (N_EXPERTS if first else 0)
        n_blocks += n_blocks % 2
        pstart, block_e, n_used = _expert_layout(cnt[:, 0].astype(jnp.int32), n_blocks, first)
        rows_g, gates_g = _slot_rows(route_g, pstart, PROMPT_STREAMS, seq // PROMPT_TILE, PROMPT_TILE)
        sources = [(h_g.reshape(n_g, half), rows_g)]
        if with_sample:
            rows_s, gates_s = _slot_rows(route_s, pstart, SAMPLE_STREAMS, 1, dseq)
            sources.append((h_s.reshape(n_s, half), rows_s))
        xs = _dispatch(sources, n_blocks * EXPERT_ROWS)
        if first:
            yb, wgu_bf, wd_bf = _expert_ffn_convert(block_e, n_used, xs, w_gate_up[0], bgu, w_down[0], bd, n_blocks)
        else:
            (yb,) = _expert_ffn_pairs(block_e, n_used, xs, wgu_bf, bgu, wd_bf, bd, n_blocks)
        if with_sample:
            y_s = _collect_combine(yb, x2_s.reshape(n_s, D_MODEL), rows_s, gates_s, final_w, 1, None, n_s, 0)
        y_p = _collect_combine(yb, x2_g.reshape(n_g, D_MODEL), rows_g, gates_g, final_w, n_g // COMBINE_TOKENS,
                               y_p, b * seq, stream0 * seq)
        stream0 += per
    y_p = y_p.reshape(b, seq, D_MODEL)
    y_s = y_s.reshape(db, dseq, D_MODEL)
    ret_p = jnp.concatenate(rets, axis=0)
    pool_p = jnp.concatenate(pools, axis=0)
    return (y_p, y_s, mk_p[None], mv_p[None], ret_p[None], pool_p[None], ret_s[None], pool_s[None])
```

```python
import functools

import numpy as np
import jax
import jax.numpy as jnp
from jax import lax
from jax.experimental import pallas as pl
from jax.experimental.pallas import tpu as pltpu
from jax.experimental.pallas import tpu_sc as plsc

D_MODEL = 1024
CHUNK = 64
PAST_LEN = 4096
RET_HEADS = 4
RET_DK = 128
RET_DV = 128
RET_QK = RET_HEADS * RET_DK
RET_VW = RET_HEADS * RET_DV
ROPE_BASE = 10000.0
POOL_WINDOWS = (2, 4, 8, 16)
POOL_GROUPS = 4
POOL_WIDTH = D_MODEL // 2
POOL_C = POOL_WIDTH // POOL_GROUPS
POOL_HIST = max(POOL_WINDOWS) - 1
HIST_ROWS = POOL_HIST + 1
IN_WIDTH = 2 * RET_QK + 2 * RET_VW + POOL_WIDTH
N_MEM = 256
MEM_HEADS = 4
MEM_HD = D_MODEL // MEM_HEADS
N_EXPERTS = 32
TOP_K = 4
D_FF = D_MODEL
SWIGLU_LIMIT = 7.0
SWIGLU_ALPHA = 1.702
EPS = 1e-5

LANES = 128
ROUTE_ROWS = 16
PROMPT_TILE = 256
PROMPT_STREAMS = 4
PROMPT_CHAIN = 2
SAMPLE_STREAMS = 8
EXPERT_ROWS = 512
GROUP_STREAMS = (16, 16)
KV_STREAMS = 4
COMBINE_ROWS = 1024
COMBINE_TOKENS = 16384
VMEM_LIMIT = 56 * 1024 * 1024

BF16 = jnp.bfloat16
F32 = jnp.float32


def _rms(x, w):
    return x * lax.rsqrt(jnp.mean(x * x, axis=-1, keepdims=True) + EPS) * w


def _dot(a, b):
    return jnp.dot(a, b, preferred_element_type=F32)


def _rms_dot(x, w_norm, w_mat):
    inv = lax.rsqrt(jnp.mean(x * x, axis=-1, keepdims=True) + EPS)
    return _dot((x * w_norm).astype(BF16), w_mat) * inv


def _dot_nt(a, b):
    return lax.dot_general(a, b, (((1,), (1,)), ((), ())), preferred_element_type=F32)


BF16_BITS = 16
HIGH_HALF = -(1 << BF16_BITS)


def _pack_bf16(x):
    bits = lax.bitcast_convert_type(x.astype(BF16).astype(F32), jnp.int32)
    w = x.shape[1] // 2
    return lax.shift_right_logical(bits[:, :w], BF16_BITS) | (bits[:, w:] & HIGH_HALF)


def _unpack_bf16(p):
    lo = lax.bitcast_convert_type(lax.shift_left(p, BF16_BITS), F32)
    hi = lax.bitcast_convert_type(p & HIGH_HALF, F32)
    return lo, hi


def _const_spec(shape):
    nd = len(shape)
    return pl.BlockSpec(shape, lambda *_: (0,) * nd, pipeline_mode=pl.Buffered(1))


def _mem_kv_kernel(mem_ref, nw_ref, w_ref, k_ref, v_ref, kb_ref, vb_ref):
    ns = mem_ref.shape[0]
    xn = _rms(mem_ref[...].reshape(ns * N_MEM, D_MODEL), nw_ref[...]).astype(BF16)
    kv = _dot(xn, w_ref[...])
    for hd in range(MEM_HEADS):
        lo = hd * MEM_HD
        k_ref[:, :, hd, :] = kv[:, lo:lo + MEM_HD].reshape(ns, N_MEM, MEM_HD)
        v_ref[:, :, hd, :] = kv[:, D_MODEL + lo:D_MODEL + lo + MEM_HD].reshape(ns, N_MEM, MEM_HD)
    kb_ref[...] = kv[:, :D_MODEL].astype(BF16).reshape(ns, N_MEM, D_MODEL)
    vb_ref[...] = kv[:, D_MODEL:].astype(BF16).reshape(ns, N_MEM, D_MODEL)


def _mem_kv(mem, mem_norm_w, w_kv_bf):
    b = mem.shape[0]
    out_spec = pl.BlockSpec((KV_STREAMS, N_MEM, MEM_HEADS, MEM_HD), lambda i: (i, 0, 0, 0))
    flat_spec = pl.BlockSpec((KV_STREAMS, N_MEM, D_MODEL), lambda i: (i, 0, 0))
    return pl.pallas_call(
        _mem_kv_kernel,
        grid=(b // KV_STREAMS,),
        in_specs=[
            pl.BlockSpec((KV_STREAMS, N_MEM, D_MODEL), lambda i: (i, 0, 0)),
            _const_spec((1, D_MODEL)),
            _const_spec((D_MODEL, 2 * D_MODEL)),
        ],
        out_specs=[out_spec, out_spec, flat_spec, flat_spec],
        out_shape=[jax.ShapeDtypeStruct((b, N_MEM, MEM_HEADS, MEM_HD), F32)] * 2
        + [jax.ShapeDtypeStruct((b, N_MEM, D_MODEL), BF16)] * 2,
        compiler_params=pltpu.CompilerParams(
            dimension_semantics=("arbitrary",), vmem_limit_bytes=VMEM_LIMIT),
        name="mem_kv",
    )(mem, mem_norm_w, w_kv_bf)


def _kv_flat_kernel(k_ref, v_ref, kb_ref, vb_ref):
    for hd in range(MEM_HEADS):
        kb_ref[:, :, hd * MEM_HD:(hd + 1) * MEM_HD] = k_ref[:, :, hd, :].astype(BF16)
        vb_ref[:, :, hd * MEM_HD:(hd + 1) * MEM_HD] = v_ref[:, :, hd, :].astype(BF16)


def _kv_flat(mem_k, mem_v):
    b = mem_k.shape[0]
    in_spec = pl.BlockSpec((KV_STREAMS, N_MEM, MEM_HEADS, MEM_HD), lambda i: (i, 0, 0, 0))
    out_spec = pl.BlockSpec((KV_STREAMS, N_MEM, D_MODEL), lambda i: (i, 0, 0))
    return pl.pallas_call(
        _kv_flat_kernel,
        grid=(b // KV_STREAMS,),
        in_specs=[in_spec, in_spec],
        out_specs=[out_spec, out_spec],
        out_shape=[jax.ShapeDtypeStruct((b, N_MEM, D_MODEL), BF16)] * 2,
        compiler_params=pltpu.CompilerParams(
            dimension_semantics=("arbitrary",), vmem_limit_bytes=VMEM_LIMIT),
        name="kv_flat",
    )(mem_k, mem_v)


def _layer_kernel(x_ref, cos_ref, sin_ref, dmat_ref, qdec_ref, kdec_ref, state0_ref, hist0_ref,
                  mk_ref, mv_ref, nmix_ref, win_ref, gnw_ref, wpool_ref, pscale_ref, wout_ref,
                  nmem_ref, wq_ref, wo_ref, nffn_ref, rw_ref, rb_ref, earlier_ref, cnt0_ref,
                  x2_ref, h_ref, route_ref, rstate_ref, pstate_ref, cnt_ref,
                  s_scr, ext_scr, cnt_scr, *, tl, chain, pos0, cdec):
    t = pl.program_id(1)

    @pl.when((pl.program_id(0) == 0) & (t == 0))
    def _():
        cnt_scr[...] = cnt0_ref[...]

    @pl.when(t == 0)
    def _():
        s_scr[...] = state0_ref[...]
        ext_scr[:, 0:HIST_ROWS, :] = hist0_ref[...]

    for c0 in range(0, x_ref.shape[0], chain):
        _layer_chain(c0, t, x_ref, cos_ref, sin_ref, dmat_ref, qdec_ref, kdec_ref, nmix_ref, win_ref, gnw_ref,
                     wpool_ref, pscale_ref, wout_ref, nmem_ref, wq_ref, wo_ref, nffn_ref, rw_ref, rb_ref,
                     earlier_ref, x2_ref, h_ref, route_ref, pstate_ref, s_scr, ext_scr, mk_ref, mv_ref, cnt_scr,
                     tl=tl, chain=chain, pos0=pos0, cdec=cdec)
    rstate_ref[...] = s_scr[...]
    cnt_ref[...] = cnt_scr[...]


def _layer_chain(c0, t, x_ref, cos_ref, sin_ref, dmat_ref, qdec_ref, kdec_ref, nmix_ref, win_ref, gnw_ref,
                 wpool_ref, pscale_ref, wout_ref, nmem_ref, wq_ref, wo_ref, nffn_ref, rw_ref, rb_ref,
                 earlier_ref, x2_ref, h_ref, route_ref, pstate_ref, s_scr, ext_scr, mk_ref, mv_ref, cnt_scr,
                 *, tl, chain, pos0, cdec):
    rows = chain * tl
    x = x_ref[c0:c0 + chain].reshape(rows, D_MODEL)
    proj = _rms_dot(x, nmix_ref[...], win_ref[...])
    cos = cos_ref[...]
    sin = sin_ref[...]
    pos = (pos0 + t * tl + lax.broadcasted_iota(jnp.int32, (tl, POOL_C), 0)).astype(F32)

    mixes = []
    for sj in range(chain):
        si = c0 + sj
        pj = proj[sj * tl:(sj + 1) * tl]
        outs = []
        for hd in range(RET_HEADS):
            lo = hd * RET_DK
            q = pj[:, lo:lo + RET_DK]
            k = pj[:, RET_QK + lo:RET_QK + lo + RET_DK]
            v = pj[:, 2 * RET_QK + lo:2 * RET_QK + lo + RET_DV]
            g = pj[:, 2 * RET_QK + RET_VW + lo:2 * RET_QK + RET_VW + lo + RET_DV]
            qr = (q * cos + pltpu.roll(q, RET_DK // 2, 1) * sin) * (RET_DK ** -0.5)
            kr = k * cos + pltpu.roll(k, RET_DK // 2, 1) * sin
            vb = v.astype(BF16)
            s = _dot_nt(qr.astype(BF16), kr.astype(BF16)) * dmat_ref[hd]
            o = _dot(s.astype(BF16), vb)
            state = s_scr[si, hd]
            o = o + _dot((qr * qdec_ref[hd]).astype(BF16), state.astype(BF16))
            kd_t = jnp.transpose(kr * kdec_ref[hd]).astype(BF16)
            s_scr[si, hd] = cdec[hd] * state + _dot(kd_t, vb)
            mu = jnp.mean(o, axis=-1, keepdims=True)
            oc = o - mu
            var = jnp.mean(oc * oc, axis=-1, keepdims=True)
            on = oc * lax.rsqrt(var + EPS)
            outs.append(on * gnw_ref[:, lo:lo + RET_DV] * (g * jax.nn.sigmoid(g)))

        pin = pj[:, 2 * RET_QK + 2 * RET_VW:]
        ext_scr[si, HIST_ROWS:HIST_ROWS + tl, :] = pin
        pstate_ref[si] = pin[tl - POOL_HIST:, :]
        for gi, w in enumerate(POOL_WINDOWS):
            lo = gi * POOL_C
            wsum = ext_scr[si, :, lo:lo + POOL_C]
            shift = 1
            while shift < w:
                wsum = wsum + pltpu.roll(wsum, shift, 0)
                shift *= 2
            cnt = jnp.minimum(float(w), pos + 1.0)
            d = wsum[HIST_ROWS:, :] / cnt - pin[:, lo:lo + POOL_C]
            y = _dot(d.astype(BF16), wpool_ref[gi])
            outs.append(y * pscale_ref[:, lo:lo + POOL_C])
        ext_scr[si, 0:HIST_ROWS, :] = ext_scr[si, tl:tl + HIST_ROWS, :]
        mixes.append(jnp.concatenate(outs, axis=-1).astype(BF16))

    x1 = x + _dot(jnp.concatenate(mixes, axis=0), wout_ref[...])

    qm = _rms_dot(x1, nmem_ref[...], wq_ref[...])
    atts = []
    for sj in range(chain):
        si = c0 + sj
        aouts = []
        for hd in range(MEM_HEADS):
            lo = hd * MEM_HD
            qh = qm[sj * tl:(sj + 1) * tl, lo:lo + MEM_HD].astype(BF16)
            s = _dot_nt(qh, mk_ref[si, :, lo:lo + MEM_HD]) * (MEM_HD ** -0.5)
            e = jnp.exp(s - jnp.max(s, axis=-1, keepdims=True))
            p = e / jnp.sum(e, axis=-1, keepdims=True)
            aouts.append(_dot(p.astype(BF16), mv_ref[si, :, lo:lo + MEM_HD]))
        atts.append(jnp.concatenate(aouts, axis=-1).astype(BF16))
    x2 = x1 + _dot(jnp.concatenate(atts, axis=0), wo_ref[...])
    x2_ref[c0:c0 + chain] = x2.reshape(chain, tl, D_MODEL)

    hn = _rms(x2, nffn_ref[...])
    h_ref[c0:c0 + chain] = _pack_bf16(hn).reshape(chain, tl, D_MODEL // 2)
    lane_tile = lambda a: a[:, :rows] if rows <= LANES else jnp.concatenate([a] * (rows // LANES), axis=1)
    logits = _dot_nt(rw_ref[...], hn.astype(BF16)) + lane_tile(rb_ref[...])
    eiota = lax.broadcasted_iota(jnp.int32, (N_EXPERTS, rows), 0)
    neg = jnp.finfo(F32).min
    vals, idxs = [], []
    for _k in range(TOP_K):
        m = jnp.max(logits, axis=0, keepdims=True)
        idx = jnp.min(jnp.where(logits == m, eiota, N_EXPERTS), axis=0, keepdims=True)
        vals.append(m)
        idxs.append(idx)
        logits = jnp.where(eiota == idx, neg, logits)
    exps = [jnp.exp(vk - vals[0]) for vk in vals]
    den = exps[0] + exps[1] + exps[2] + exps[3]

    onehots = [(eiota == idx).astype(F32) for idx in idxs]
    picked = onehots[0] + onehots[1] + onehots[2] + onehots[3]
    before = lane_tile(cnt_scr[...]) + _dot(picked.astype(BF16), earlier_ref[...])
    ranks = [jnp.sum(oh * before, axis=0, keepdims=True) for oh in onehots]
    cnt_scr[...] = cnt_scr[...] + jnp.sum(picked, axis=1, keepdims=True)

    rowi = lax.broadcasted_iota(jnp.int32, (ROUTE_ROWS, rows), 0)
    route = jnp.zeros((ROUTE_ROWS, rows), F32)
    for kk in range(TOP_K):
        route = jnp.where(rowi == kk, exps[kk] / den, route)
        route = jnp.where(rowi == TOP_K + kk, idxs[kk].astype(F32), route)
        route = jnp.where(rowi == 2 * TOP_K + kk, ranks[kk], route)
    route_ref[0, :, c0 * tl:c0 * tl + rows] = route


def _decay_tables(tl):
    hh = np.arange(RET_HEADS, dtype=np.float64)
    log_g = np.log1p(-np.exp2(-5.0 - hh))
    idx = np.arange(tl, dtype=np.float64)
    dist = np.abs(idx[:, None] - idx[None, :])
    visible = (idx[None, :] // CHUNK) <= (idx[:, None] // CHUNK)
    dmat = np.where(visible[None], np.exp(log_g[:, None, None] * dist[None]), 0.0)
    qdec = np.exp(log_g[:, None] * (idx[None, :] + 1.0))
    kdec = np.exp(log_g[:, None] * (tl - 1.0 - idx[None, :]))
    cdec = tuple(float(c) for c in np.exp(log_g * tl).astype(np.float32))
    bcast = lambda a: np.ascontiguousarray(np.broadcast_to(a[:, :, None], (RET_HEADS, tl, RET_DK)))
    return (jnp.asarray(dmat, F32), jnp.asarray(bcast(qdec), F32), jnp.asarray(bcast(kdec), F32), cdec)


def _rotary_tables(pos0, length):
    half = RET_DK // 2
    inv_freq = jnp.power(ROPE_BASE, -jnp.arange(half, dtype=F32) / half)
    ang = (pos0 + jnp.arange(length, dtype=jnp.int32)).astype(F32)[:, None] * inv_freq[None, :]
    cos, sin = jnp.cos(ang), jnp.sin(ang)
    return jnp.concatenate([cos, cos], axis=-1), jnp.concatenate([-sin, sin], axis=-1)


def _layer(x, stream0, b, pos0, tl, ns, chain, state0, hist0, mk, mv, wts, cnt0):
    length = x.shape[1]
    assert b % ns == 0 and stream0 % ns == 0 and ns % chain == 0, (b, stream0, ns, chain)
    assert length % tl == 0 and tl % CHUNK == 0 and tl > POOL_HIST, (length, tl)
    nt = length // tl
    g0 = stream0 // ns
    rows = ns * tl
    cos, sin = _rotary_tables(pos0, length)
    dmat, qdec, kdec, cdec = _decay_tables(tl)
    crows = chain * tl
    earlier = jnp.asarray(np.triu(np.ones((crows, crows), np.float32), 1), BF16)
    kern = functools.partial(_layer_kernel, tl=tl, chain=chain, pos0=pos0, cdec=cdec)
    tok = lambda width: pl.BlockSpec((ns, tl, width), lambda i, j: (i, j, 0))
    per_stream = lambda *shape: pl.BlockSpec((ns,) + shape, lambda i, j: (i,) + (0,) * len(shape))
    per_stream_in = lambda off, *shape: pl.BlockSpec(
        (ns,) + shape, lambda i, j: (i + off,) + (0,) * len(shape), pipeline_mode=pl.Buffered(1))
    in_specs = [
        pl.BlockSpec((ns, tl, D_MODEL), lambda i, j: (i + g0, j, 0)),
        pl.BlockSpec((tl, RET_DK), lambda i, j: (j, 0)),
        pl.BlockSpec((tl, RET_DK), lambda i, j: (j, 0)),
        _const_spec((RET_HEADS, tl, tl)),
        _const_spec((RET_HEADS, tl, RET_DK)),
        _const_spec((RET_HEADS, tl, RET_DK)),
        per_stream_in(0, RET_HEADS, RET_DK, RET_DV),
        per_stream_in(0, HIST_ROWS, POOL_WIDTH),
        per_stream_in(g0, N_MEM, D_MODEL),
        per_stream_in(g0, N_MEM, D_MODEL),
    ] + [_const_spec(w.shape) for w in wts] + [_const_spec((crows, crows)), _const_spec((N_EXPERTS, LANES))]
    out_specs = [
        tok(D_MODEL), tok(D_MODEL // 2),
        pl.BlockSpec((1, ROUTE_ROWS, rows), lambda i, j: (i * nt + j, 0, 0)),
        per_stream(RET_HEADS, RET_DK, RET_DV),
        per_stream(POOL_HIST, POOL_WIDTH),
        pl.BlockSpec((N_EXPERTS, LANES), lambda i, j: (0, 0)),
    ]
    out_shape = [
        jax.ShapeDtypeStruct((b, length, D_MODEL), F32),
        jax.ShapeDtypeStruct((b, length, D_MODEL // 2), jnp.int32),
        jax.ShapeDtypeStruct((b // ns * nt, ROUTE_ROWS, rows), F32),
        jax.ShapeDtypeStruct((b, RET_HEADS, RET_DK, RET_DV), F32),
        jax.ShapeDtypeStruct((b, POOL_HIST, POOL_WIDTH), F32),
        jax.ShapeDtypeStruct((N_EXPERTS, LANES), F32),
    ]
    scratch = [
        pltpu.VMEM((ns, RET_HEADS, RET_DK, RET_DV), F32),
        pltpu.VMEM((ns, HIST_ROWS + tl, POOL_WIDTH), F32),
        pltpu.VMEM((N_EXPERTS, LANES), F32),
    ]
    return pl.pallas_call(
        kern,
        grid=(b // ns, nt),
        in_specs=in_specs,
        out_specs=out_specs,
        out_shape=out_shape,
        scratch_shapes=scratch,
        compiler_params=pltpu.CompilerParams(
            dimension_semantics=("arbitrary", "arbitrary"), vmem_limit_bytes=VMEM_LIMIT),
        name="layer_tl%d" % tl,
    )(x, cos, sin, dmat, qdec, kdec, state0, hist0, mk, mv, *wts, earlier, cnt0)


SPLIT_COLS = 2 * LANES


def _expert_block(x_ref, wgu, bgu_ref, wd, bd_ref, y_ref):
    x_lo, x_hi = _unpack_bf16(x_ref[...])
    xb = jnp.concatenate([x_lo.astype(BF16), x_hi.astype(BF16)], axis=1)
    gu = _dot(xb, wgu[...]) + bgu_ref[0]
    gate = jnp.minimum(gu[:, :D_FF], SWIGLU_LIMIT)
    up = jnp.clip(gu[:, D_FF:], -SWIGLU_LIMIT, SWIGLU_LIMIT)
    act = (up + 1.0) * gate * jax.nn.sigmoid(SWIGLU_ALPHA * gate)
    y_ref[...] = _pack_bf16(_dot(act.astype(BF16), wd[...]) + bd_ref[0])


def _expert_convert_kernel(be_ref, used_ref, x_ref, wgu_ref, bgu_ref, wd_ref, bd_ref, perm_ref,
                           y_ref, wgu_out_ref, wd_out_ref):
    i = pl.program_id(0)
    in_use = i < used_ref[0]

    @pl.when(in_use & ((i == 0) | (be_ref[i] != be_ref[jnp.maximum(i - 1, 0)])))
    def _():
        perm = perm_ref[...]
        for c in range(2 * D_FF // SPLIT_COLS):
            wc = wgu_ref[0, :, c * SPLIT_COLS:(c + 1) * SPLIT_COLS].astype(BF16)
            pc = _dot(wc, perm).astype(BF16)
            wgu_out_ref[0, :, c * LANES:(c + 1) * LANES] = pc[:, :LANES]
            wgu_out_ref[0, :, D_FF + c * LANES:D_FF + (c + 1) * LANES] = pc[:, LANES:]
        wd_out_ref[0] = wd_ref[0].astype(BF16)

    @pl.when(in_use)
    def _():
        _expert_block(x_ref, wgu_out_ref.at[0], bgu_ref, wd_out_ref.at[0], bd_ref, y_ref)


def _expert_pair_kernel(be_ref, used_ref, x_ref, wgu_a, bgu_a, wd_a, bd_a, wgu_b, bgu_b, wd_b, bd_b, y_ref):
    i = pl.program_id(0)
    use0 = 2 * i < used_ref[0]
    use1 = 2 * i + 1 < used_ref[0]
    joint = use1 & (be_ref[2 * i] == be_ref[2 * i + 1])
    first = pl.ds(0, EXPERT_ROWS)
    second = pl.ds(EXPERT_ROWS, EXPERT_ROWS)

    @pl.when(joint)
    def _():
        _expert_block(x_ref, wgu_a.at[0], bgu_a, wd_a.at[0], bd_a, y_ref)

    @pl.when(use0 & jnp.logical_not(joint))
    def _():
        _expert_block(x_ref.at[first], wgu_a.at[0], bgu_a, wd_a.at[0], bd_a, y_ref.at[first])

    @pl.when(use1 & jnp.logical_not(joint))
    def _():
        _expert_block(x_ref.at[second], wgu_b.at[0], bgu_b, wd_b.at[0], bd_b, y_ref.at[second])


def _expert_ffn_convert(block_e, n_used, xs, w_gate_up, bgu, w_down, bd, n_blocks):
    perm = np.zeros((SPLIT_COLS, SPLIT_COLS), np.float32)
    j = np.arange(LANES)
    perm[2 * j, j] = 1.0
    perm[2 * j + 1, LANES + j] = 1.0
    blk = lambda i, be, used: (jnp.minimum(i, used[0] - 1), 0)
    per_expert = lambda i, be, used: (be[i], 0, 0)
    wgu_spec = pl.BlockSpec((1, D_MODEL, 2 * D_FF), per_expert)
    wd_spec = pl.BlockSpec((1, D_FF, D_MODEL), per_expert)
    grid_spec = pltpu.PrefetchScalarGridSpec(
        num_scalar_prefetch=2, grid=(n_blocks,),
        in_specs=[pl.BlockSpec((EXPERT_ROWS, D_MODEL // 2), blk), wgu_spec,
                  pl.BlockSpec((1, 1, 2 * D_FF), per_expert), wd_spec, pl.BlockSpec((1, 1, D_MODEL), per_expert),
                  _const_spec((SPLIT_COLS, SPLIT_COLS))],
        out_specs=[pl.BlockSpec((EXPERT_ROWS, D_MODEL // 2), blk), wgu_spec, wd_spec])
    return pl.pallas_call(
        _expert_convert_kernel,
        grid_spec=grid_spec,
        out_shape=[jax.ShapeDtypeStruct(xs.shape, jnp.int32), jax.ShapeDtypeStruct(w_gate_up.shape, BF16),
                   jax.ShapeDtypeStruct(w_down.shape, BF16)],
        compiler_params=pltpu.CompilerParams(
            dimension_semantics=("arbitrary",), vmem_limit_bytes=VMEM_LIMIT),
        name="expert_ffn_convert",
    )(block_e, n_used, xs, w_gate_up, bgu, w_down, bd, jnp.asarray(perm, BF16))


def _expert_ffn_pairs(block_e, n_used, xs, wgu, bgu, wd, bd, n_blocks):
    pair = lambda i, be, used: (jnp.minimum(i, (used[0] - 1) // 2), 0)
    specs = []
    for half in range(2):
        per_expert = lambda i, be, used, half=half: (be[2 * i + half], 0, 0)
        specs += [pl.BlockSpec((1, D_MODEL, 2 * D_FF), per_expert), pl.BlockSpec((1, 1, 2 * D_FF), per_expert),
                  pl.BlockSpec((1, D_FF, D_MODEL), per_expert), pl.BlockSpec((1, 1, D_MODEL), per_expert)]
    grid_spec = pltpu.PrefetchScalarGridSpec(
        num_scalar_prefetch=2, grid=(n_blocks // 2,),
        in_specs=[pl.BlockSpec((2 * EXPERT_ROWS, D_MODEL // 2), pair)] + specs,
        out_specs=[pl.BlockSpec((2 * EXPERT_ROWS, D_MODEL // 2), pair)])
    return pl.pallas_call(
        _expert_pair_kernel,
        grid_spec=grid_spec,
        out_shape=[jax.ShapeDtypeStruct(xs.shape, jnp.int32)],
        compiler_params=pltpu.CompilerParams(
            dimension_semantics=("arbitrary",), vmem_limit_bytes=VMEM_LIMIT),
        name="expert_ffn",
    )(block_e, n_used, xs, wgu, bgu, wd, bd, wgu, bgu, wd, bd)


def _expert_layout(counts, n_blocks, every_expert):
    pcounts = (counts + EXPERT_ROWS - 1) // EXPERT_ROWS * EXPERT_ROWS
    if every_expert:
        pcounts = jnp.maximum(pcounts, EXPERT_ROWS)
    pend = jnp.cumsum(pcounts)
    n_used = pend[-1:] // EXPERT_ROWS
    block_start = jnp.minimum(jnp.arange(n_blocks, dtype=jnp.int32), n_used[0] - 1) * EXPERT_ROWS
    block_e = jnp.sum((pend[None, :] <= block_start[:, None]).astype(jnp.int32), axis=1)
    return pend - pcounts, jnp.minimum(block_e, N_EXPERTS - 1), n_used


def _slot_rows(route, pstart, ns, nt, tl):
    n = route.shape[0] * ns * tl
    fields = lambda lo: route[:, lo:lo + TOP_K, :].reshape(-1, nt, TOP_K, ns, tl)
    per_slot = lambda lo: jnp.transpose(fields(lo), (2, 0, 3, 1, 4)).reshape(TOP_K, n)
    e = per_slot(TOP_K).astype(jnp.int32)
    rank = per_slot(2 * TOP_K).astype(jnp.int32)
    hit = e[:, :, None] == jnp.arange(N_EXPERTS, dtype=jnp.int32)[None, None, :]
    rows = rank + jnp.sum(jnp.where(hit, pstart[None, None, :], 0), axis=-1)
    gates = jnp.transpose(fields(0), (0, 3, 1, 4, 2)).reshape(n, TOP_K)
    return rows, gates


SC_WINDOW = 128
SC_COLS = 256


def _sc_mesh():
    return plsc.VectorSubcoreMesh(core_axis_name="c", subcore_axis_name="s")


def _dispatch(sources, m_pad):
    width = sources[0][0].shape[1]
    n_src = len(sources)
    assert width % SC_COLS == 0 and all(h.shape[0] % SC_WINDOW == 0 for h, _ in sources)

    @functools.partial(pl.kernel, mesh=_sc_mesh(),
                       out_type=jax.ShapeDtypeStruct((m_pad, width), sources[0][0].dtype), scratch_types=[])
    def k(*refs):
        xs_hbm = refs[2 * n_src]

        def body(x_vmem, i_vmem):
            j = pl.program_id(1)
            for kk in range(TOP_K):
                pltpu.sync_copy(x_vmem, xs_hbm.at[i_vmem.at[kk], pl.ds(j * SC_COLS, SC_COLS)])

        for si in range(n_src):
            src, rows = refs[2 * si], refs[2 * si + 1]
            pltpu.emit_pipeline(
                body,
                grid=(src.shape[0] // SC_WINDOW, width // SC_COLS),
                in_specs=[pl.BlockSpec((SC_WINDOW, SC_COLS), lambda i, j: (i, j)),
                          pl.BlockSpec((TOP_K, SC_WINDOW), lambda i, j: (0, i))],
                out_specs=[],
                core_axis_name=("c", "s"),
                dimension_semantics=(pltpu.PARALLEL, pltpu.ARBITRARY),
            )(src, rows)

    return k(*[a for pair in sources for a in pair])


def _collect(yb, rows):
    width = yb.shape[1]
    assert width % SC_COLS == 0 and rows.shape[1] % SC_WINDOW == 0

    @functools.partial(pl.kernel, mesh=_sc_mesh(),
                       out_type=jax.ShapeDtypeStruct((rows.shape[1], width), yb.dtype), scratch_types=[])
    def k(yb_hbm, r_hbm, o_hbm):
        def body(i_vmem, o_vmem):
            j = pl.program_id(1)
            pltpu.sync_copy(yb_hbm.at[i_vmem.at[0], pl.ds(j * SC_COLS, SC_COLS)], o_vmem)

        pltpu.emit_pipeline(
            body,
            grid=(rows.shape[1] // SC_WINDOW, width // SC_COLS),
            in_specs=[pl.BlockSpec((1, SC_WINDOW), lambda i, j: (0, i))],
            out_specs=[pl.BlockSpec((SC_WINDOW, SC_COLS), lambda i, j: (i, j))],
            core_axis_name=("c", "s"),
            dimension_semantics=(pltpu.PARALLEL, pltpu.ARBITRARY),
        )(r_hbm, o_hbm)

    return k(yb, rows)


def _combine_kernel(x_ref, yg_ref, route_ref, fw_ref, y_ref):
    half = D_MODEL // 2
    acc_lo = x_ref[:, :half]
    acc_hi = x_ref[:, half:]
    route = route_ref[...]
    for kk in range(TOP_K):
        y_lo, y_hi = _unpack_bf16(yg_ref[kk])
        gate = route[:, kk:kk + 1]
        acc_lo = acc_lo + y_lo * gate
        acc_hi = acc_hi + y_hi * gate
    ms = (jnp.sum(acc_lo * acc_lo, axis=-1, keepdims=True)
          + jnp.sum(acc_hi * acc_hi, axis=-1, keepdims=True)) * (1.0 / D_MODEL)
    scale = lax.rsqrt(ms + EPS)
    y_ref[:, :half] = acc_lo * scale * fw_ref[:, :half]
    y_ref[:, half:] = acc_hi * scale * fw_ref[:, half:]


def _combine_next_kernel(x_ref, yg_ref, route_ref, fw_ref, prev_ref, y_ref):
    del prev_ref
    _combine_kernel(x_ref, yg_ref, route_ref, fw_ref, y_ref)


def _combine(x2, yg, gates, final_w, y_prev, first_row, out_rows, out_first_row):
    assert yg.shape[1] % COMBINE_ROWS == 0 and first_row % COMBINE_ROWS == 0 and out_first_row % COMBINE_ROWS == 0
    blk_in = first_row // COMBINE_ROWS
    blk_out = out_first_row // COMBINE_ROWS
    in_specs = [
        pl.BlockSpec((COMBINE_ROWS, D_MODEL), lambda i: (blk_in + i, 0)),
        pl.BlockSpec((TOP_K, COMBINE_ROWS, D_MODEL // 2), lambda i: (0, i, 0)),
        pl.BlockSpec((COMBINE_ROWS, TOP_K), lambda i: (blk_in + i, 0)),
        _const_spec((1, D_MODEL)),
    ]
    args = [x2, yg, gates, final_w]
    kern, aliases = _combine_kernel, {}
    if y_prev is not None:
        in_specs.append(pl.BlockSpec(memory_space=pl.ANY))
        args.append(y_prev)
        kern, aliases = _combine_next_kernel, {len(args) - 1: 0}
    return pl.pallas_call(
        kern,
        grid=(yg.shape[1] // COMBINE_ROWS,),
        in_specs=in_specs,
        out_specs=pl.BlockSpec((COMBINE_ROWS, D_MODEL), lambda i: (blk_out + i, 0)),
        out_shape=jax.ShapeDtypeStruct((out_rows, D_MODEL), F32),
        input_output_aliases=aliases,
        compiler_params=pltpu.CompilerParams(
            dimension_semantics=("arbitrary",), vmem_limit_bytes=VMEM_LIMIT),
        name="combine_%d_%d" % (out_rows, out_first_row),
    )(*args)


def _collect_combine(yb, x2, rows, gates, final_w, chunks, y, out_rows, out_first_row):
    nc = x2.shape[0] // chunks
    for c in range(chunks):
        yg = _collect(yb, rows[:, c * nc:(c + 1) * nc].reshape(1, TOP_K * nc))
        y = _combine(x2, yg.reshape(TOP_K, nc, D_MODEL // 2), gates, final_w, y, c * nc,
                     out_rows, out_first_row + c * nc)
    return y


def kernel(x_prompt, x_sample, cache_mem_k, cache_mem_v, state_ret, state_pool, mem_prompt,
           norm_mix_w, w_in, ret_gn_w, w_pool, pool_scale, w_out, norm_mem_w, mem_norm_w,
           w_q_mem, w_kv_mem, w_o_mem, norm_ffn_w, router_w, router_b, w_gate_up, b_gate_up,
           w_down, b_down, final_norm_w):
    assert norm_mix_w.shape[0] == 1, "one layer"
    b, seq, _ = x_prompt.shape
    db, dseq, _ = x_sample.shape
    row = lambda a: a.reshape(1, -1)

    mk_p, mv_p, mkb_p, mvb_p = _mem_kv(mem_prompt, row(mem_norm_w[0]), w_kv_mem[0].astype(BF16))
    mkb_s, mvb_s = _kv_flat(cache_mem_k[0], cache_mem_v[0])

    wts = (row(norm_mix_w[0]), w_in[0].astype(BF16), row(ret_gn_w[0]), w_pool[0].astype(BF16),
           row(pool_scale[0]), w_out[0].astype(BF16), row(norm_mem_w[0]), w_q_mem[0].astype(BF16),
           w_o_mem[0].astype(BF16), row(norm_ffn_w[0]), router_w[0].T.astype(BF16),
           jnp.broadcast_to(router_b[0][:, None], (N_EXPERTS, LANES)))

    assert sum(GROUP_STREAMS) == b
    n_s = db * dseq
    half = D_MODEL // 2
    final_w = row(final_norm_w)
    bgu = jnp.concatenate([b_gate_up[0][:, 0::2], b_gate_up[0][:, 1::2]], axis=-1).reshape(N_EXPERTS, 1, 2 * D_FF)
    bd = b_down[0].reshape(N_EXPERTS, 1, D_MODEL)
    no_counts = jnp.zeros((N_EXPERTS, LANES), F32)
    hist_s = jnp.concatenate([jnp.zeros((db, 1, POOL_WIDTH), F32), state_pool[0]], axis=1)

    y_p, rets, pools, stream0 = None, [], [], 0
    for g, per in enumerate(GROUP_STREAMS):
        n_g = per * seq
        x2_g, h_g, route_g, ret_g, pool_g, cnt = _layer(
            x_prompt, stream0, per, 0, PROMPT_TILE, PROMPT_STREAMS, PROMPT_CHAIN,
            jnp.zeros((per, RET_HEADS, RET_DK, RET_DV), F32), jnp.zeros((per, HIST_ROWS, POOL_WIDTH), F32),
            mkb_p, mvb_p, wts, no_counts)
        rets.append(ret_g)
        pools.append(pool_g)
        with_sample = g == len(GROUP_STREAMS) - 1
        n_slots = n_g * TOP_K
        if with_sample:
            x2_s, h_s, route_s, ret_s, pool_s, cnt = _layer(
                x_sample, 0, db, PAST_LEN, dseq, SAMPLE_STREAMS, SAMPLE_STREAMS, state_ret[0], hist_s,
                mkb_s, mvb_s, wts, cnt)
            n_slots += n_s * TOP_K
        first = g == 0
        n_blocks = -(-(n_slots + N_EXPERTS * (EXPERT_ROWS - 1)) // EXPERT_ROWS) + (N_EXPERTS if first else 0)
        n_blocks += n_blocks % 2
        pstart, block_e, n_used = _expert_layout(cnt[:, 0].astype(jnp.int32), n_blocks, first)
        rows_g, gates_g = _slot_rows(route_g, pstart, PROMPT_STREAMS, seq // PROMPT_TILE, PROMPT_TILE)
        sources = [(h_g.reshape(n_g, half), rows_g)]
        if with_sample:
            rows_s, gates_s = _slot_rows(route_s, pstart, SAMPLE_STREAMS, 1, dseq)
            sources.append((h_s.reshape(n_s, half), rows_s))
        xs = _dispatch(sources, n_blocks * EXPERT_ROWS)
        if first:
            yb, wgu_bf, wd_bf = _expert_ffn_convert(block_e, n_used, xs, w_gate_up[0], bgu, w_down[0], bd, n_blocks)
        else:
            (yb,) = _expert_ffn_pairs(block_e, n_used, xs, wgu_bf, bgu, wd_bf, bd, n_blocks)
        if with_sample:
            y_s = _collect_combine(yb, x2_s.reshape(n_s, D_MODEL), rows_s, gates_s, final_w, 1, None, n_s, 0)
        y_p = _collect_combine(yb, x2_g.reshape(n_g, D_MODEL), rows_g, gates_g, final_w, n_g // COMBINE_TOKENS,
                               y_p, b * seq, stream0 * seq)
        stream0 += per
    y_p = y_p.reshape(b, seq, D_MODEL)
    y_s = y_s.reshape(db, dseq, D_MODEL)
    ret_p = jnp.concatenate(rets, axis=0)
    pool_p = jnp.concatenate(pools, axis=0)
    return (y_p, y_s, mk_p[None], mv_p[None], ret_p[None], pool_p[None], ret_s[None], pool_s[None])
```
